```python
import jax, jax.numpy as jnp
from jax import lax
import numpy as np

D_MODEL = 2048
BATCH = 2
SEQ = 8192
DEPTH = 2

GRID_W = 64
CTX_LEN = 256
N_MIXERS = 2
N_HGRN_LAYERS = (DEPTH + N_MIXERS - 1) // N_MIXERS
N_ATTN_LAYERS = DEPTH // N_MIXERS
HGRN_HEADS = 16
HGRN_HEAD_K = D_MODEL // HGRN_HEADS
HGRN_HEAD_V = D_MODEL // HGRN_HEADS
HGRN_CHUNK = 64
ATTN_HEADS = 16
ATTN_KV_HEADS = 4
ATTN_GROUP = ATTN_HEADS // ATTN_KV_HEADS
HEAD_DIM = D_MODEL // ATTN_HEADS
AXIS_DIM = HEAD_DIM // 2
ROPE_THETA = 10000.0
Q_BLOCK = 128
D_FF = ((8 * D_MODEL // 3 + 255) // 256) * 256
CONV_W = 3
N_MOD = 6
EPS = 1e-6

kernel_name = 'hybrid_hgrn2_gqa_convffn_dit'


def rms_norm(x, gain):
    x32 = x.astype(jnp.float32)
    y = x32 * lax.rsqrt(jnp.mean(x32 * x32, axis=-1, keepdims=True) + EPS)
    return (y * gain.astype(jnp.float32)).astype(x.dtype)


def modulate(x, shift, scale):
    return x * (1 + scale) + shift


def dwconv_centred(h, w, b):
    T = h.shape[1]
    hp = jnp.pad(h, ((0, 0), (CONV_W // 2, CONV_W // 2), (0, 0)))
    return sum(hp[:, d:d + T] * w[d] for d in range(CONV_W)) + b


def conv_ffn(h, w_in, conv_w, conv_b, w_out):
    gate, up = jnp.split(h @ w_in, 2, axis=-1)
    return (jax.nn.silu(dwconv_centred(gate, conv_w, conv_b)) * up) @ w_out


def gla_chunk_scan(q, k, v, log_f, s0):
    B, H, T, _ = q.shape
    n_chunks = T // HGRN_CHUNK

    def chunks(a):
        return jnp.moveaxis(a.reshape(B, H, n_chunks, HGRN_CHUNK, a.shape[-1]), 2, 0)

    within = jnp.tril(jnp.ones((HGRN_CHUNK, HGRN_CHUNK), dtype=bool))[:, :, None]

    def step(S, inp):
        qc, kc, vc, lfc = inp
        b = jnp.cumsum(lfc, axis=2)
        rel = b[:, :, :, None, :] - b[:, :, None, :, :]
        decay = jnp.exp(jnp.where(within, rel, -jnp.inf))
        scores = jnp.einsum('bhtk,bhsk,bhtsk->bhts', qc, kc, decay)
        o = (jnp.einsum('bhts,bhsv->bhtv', scores, vc)
             + jnp.einsum('bhtk,bhkv->bhtv', qc * jnp.exp(b), S))
        b_end = b[:, :, -1:, :]
        S = (jnp.exp(b_end[:, :, 0, :, None]) * S
             + jnp.einsum('bhsk,bhsv->bhkv', kc * jnp.exp(b_end - b), vc))
        return S, o

    S, o = lax.scan(step, s0, (chunks(q), chunks(k), chunks(v), chunks(log_f)))
    return jnp.moveaxis(o, 0, 2).reshape(B, H, T, v.shape[-1]), S


def hgrn2_mixer(h_lat, h_ctx, w_in, lower_bound, o_gain, w_out, need_ctx):
    def project(h):
        B, T, _ = h.shape
        q, f_fw, f_bw, v, g = jnp.split(h @ w_in, 5, axis=-1)

        def heads(a):
            return a.reshape(B, T, HGRN_HEADS, -1).transpose(0, 2, 1, 3).astype(jnp.float32)

        dirs = []
        for f_raw, lb in ((f_fw, lower_bound[0]), (f_bw, lower_bound[1])):
            f = lb + (1 - lb) * jax.nn.sigmoid(f_raw.astype(jnp.float32))
            dirs.append((heads(1 - f), heads(jnp.log(f))))
        return heads(jax.nn.silu(q)) * HGRN_HEAD_K ** -0.5, dirs, heads(v), g

    def flip(a):
        return a[:, :, ::-1]

    def readout(o, g):
        B, H, T, V = o.shape
        o = rms_norm(o, o_gain).transpose(0, 2, 1, 3).reshape(B, T, H * V).astype(g.dtype)
        return (o * jax.nn.silu(g)) @ w_out

    q_c, ((k_cf, lf_cf), (k_cb, lf_cb)), v_c, g_c = project(h_ctx)
    q_l, ((k_lf, lf_lf), (k_lb, lf_lb)), v_l, g_l = project(h_lat)
    s0 = jnp.zeros((h_lat.shape[0], HGRN_HEADS, HGRN_HEAD_K, HGRN_HEAD_V), jnp.float32)
    o_cf, s_fw = gla_chunk_scan(q_c, k_cf, v_c, lf_cf, s0)
    o_cb, s_bw = gla_chunk_scan(flip(q_c), flip(k_cb), flip(v_c), flip(lf_cb), s0)
    o_lf, _ = gla_chunk_scan(q_l, k_lf, v_l, lf_lf, s_fw)
    o_lb, _ = gla_chunk_scan(flip(q_l), flip(k_lb), flip(v_l), flip(lf_lb), s_bw)
    y_lat = readout(o_lf + flip(o_lb), g_l)
    y_ctx = readout(o_cf + flip(o_cb), g_c) if need_ctx else None
    return y_lat, y_ctx


def axial_rope_tables(rows, cols):
    inv_freq = ROPE_THETA ** (-jnp.arange(0, AXIS_DIM, 2, dtype=jnp.float32) / AXIS_DIM)
    ang = jnp.concatenate([rows[:, None] * inv_freq, cols[:, None] * inv_freq], axis=-1)
    return jnp.cos(ang), jnp.sin(ang)


def apply_axial_rope(x, cos, sin):
    x32 = x.astype(jnp.float32)
    x1, x2 = x32[..., 0::2], x32[..., 1::2]
    c, s = cos[:, None, :], sin[:, None, :]
    return jnp.stack([x1 * c - x2 * s, x1 * s + x2 * c], axis=-1).reshape(x.shape).astype(x.dtype)


def gqa_mixer(h_lat, h_ctx, w_qkv, q_gain, k_gain, w_out, cos, sin, need_ctx):
    def project(h):
        B, T, _ = h.shape
        p = (h @ w_qkv).reshape(B, T, ATTN_HEADS + 2 * ATTN_KV_HEADS, HEAD_DIM)
        k = rms_norm(p[:, :, ATTN_HEADS:ATTN_HEADS + ATTN_KV_HEADS], k_gain)
        return p[:, :, :ATTN_HEADS], k, p[:, :, ATTN_HEADS + ATTN_KV_HEADS:]

    def attend(q, k, v):
        s = jnp.einsum('bqkgd,bskd->bkgqs', q, k, preferred_element_type=jnp.float32) * HEAD_DIM ** -0.5
        p = jax.nn.softmax(s, axis=-1).astype(v.dtype)
        return jnp.einsum('bkgqs,bskd->bqkgd', p, v)

    B, T, D = h_lat.shape
    q_l, k_l, v_l = project(h_lat)
    q_c, k_c, v_c = project(h_ctx)
    q_l = apply_axial_rope(rms_norm(q_l, q_gain), cos, sin)
    q_blocks = jnp.moveaxis(
        q_l.reshape(B, T // Q_BLOCK, Q_BLOCK, ATTN_KV_HEADS, ATTN_GROUP, HEAD_DIM), 1, 0)
    k_all = jnp.concatenate([k_c, apply_axial_rope(k_l, cos, sin)], axis=1)
    v_all = jnp.concatenate([v_c, v_l], axis=1)
    o_l = lax.map(lambda qb: attend(qb, k_all, v_all), q_blocks)
    y_lat = jnp.moveaxis(o_l, 0, 1).reshape(B, T, D) @ w_out
    if need_ctx:
        Tc = h_ctx.shape[1]
        q_c = rms_norm(q_c, q_gain).reshape(B, Tc, ATTN_KV_HEADS, ATTN_GROUP, HEAD_DIM)
        y_ctx = attend(q_c, k_c, v_c).reshape(B, Tc, D) @ w_out
    else:
        y_ctx = None
    return y_lat, y_ctx


def setup_inputs(seed: int = 0) -> dict:
    key = jax.random.key(seed)
    ks = iter(jax.random.split(key, 24))

    def normal(shape, scale):
        return jax.random.normal(next(ks), shape, jnp.float32) * scale

    def gain(shape):
        return 1.0 + normal(shape, 0.05)

    D = D_MODEL
    return {
        'x': normal((BATCH, SEQ, D), 1.0),
        'c': normal((BATCH, D), 1.0),
        'ctx': normal((BATCH, CTX_LEN, D), 1.0),
        'c_ctx': normal((D,), 1.0),
        'ada_w': normal((DEPTH, D, N_MOD * D), D ** -0.5),
        'ada_b': normal((DEPTH, N_MOD * D), 0.02),
        'norm_mix_pre': gain((DEPTH, D)),
        'norm_mix_post': gain((DEPTH, D)),
        'norm_ffn_pre': gain((DEPTH, D)),
        'norm_ffn_post': gain((DEPTH, D)),
        'hgrn_w_in': normal((N_HGRN_LAYERS, D, 5 * D), D ** -0.5),
        'hgrn_lb_logits': normal((2, N_HGRN_LAYERS + 1, HGRN_HEADS * HGRN_HEAD_K), 0.1),
        'hgrn_o_norm': gain((N_HGRN_LAYERS, HGRN_HEAD_V)),
        'hgrn_w_out': normal((N_HGRN_LAYERS, HGRN_HEADS * HGRN_HEAD_V, D), D ** -0.5),
        'attn_w_qkv': normal((N_ATTN_LAYERS, D, (ATTN_HEADS + 2 * ATTN_KV_HEADS) * HEAD_DIM), D ** -0.5),
        'attn_q_norm': gain((N_ATTN_LAYERS, HEAD_DIM)),
        'attn_k_norm': gain((N_ATTN_LAYERS, HEAD_DIM)),
        'attn_w_out': normal((N_ATTN_LAYERS, ATTN_HEADS * HEAD_DIM, D), D ** -0.5),
        'ffn_w_in': normal((DEPTH, D, 2 * D_FF), D ** -0.5),
        'ffn_conv_w': normal((DEPTH, CONV_W, D_FF), CONV_W ** -0.5),
        'ffn_conv_b': normal((DEPTH, D_FF), 0.02),
        'ffn_w_out': normal((DEPTH, D_FF, D), D_FF ** -0.5),
    }


def reference(x, c, ctx, c_ctx, ada_w, ada_b, norm_mix_pre, norm_mix_post, norm_ffn_pre, norm_ffn_post,
              hgrn_w_in, hgrn_lb_logits, hgrn_o_norm, hgrn_w_out,
              attn_w_qkv, attn_q_norm, attn_k_norm, attn_w_out,
              ffn_w_in, ffn_conv_w, ffn_conv_b, ffn_w_out):
    T = x.shape[1]
    ROWS = T // GRID_W
    rows = jnp.repeat(jnp.arange(ROWS, dtype=jnp.float32), GRID_W)
    cols = jnp.tile(jnp.arange(GRID_W, dtype=jnp.float32), ROWS)
    cos, sin = axial_rope_tables(rows, cols)
    lower_bounds = jnp.cumsum(jax.nn.softmax(hgrn_lb_logits.astype(jnp.float32), axis=1), axis=1)
    silu_c = jax.nn.silu(c)
    silu_cc = jax.nn.silu(c_ctx)
    x_lat, x_ctx = x, ctx
    for i in range(DEPTH):
        j = i // N_MIXERS
        last = i == DEPTH - 1
        sh_ml, sc_ml, gt_ml, sh_fl, sc_fl, gt_fl = jnp.split(
            (silu_c @ ada_w[i] + ada_b[i])[:, None, :], N_MOD, axis=-1)
        sh_mc, sc_mc, gt_mc, sh_fc, sc_fc, gt_fc = jnp.split(
            silu_cc @ ada_w[i] + ada_b[i], N_MOD, axis=-1)
        h_l = modulate(rms_norm(x_lat, norm_mix_pre[i]), sh_ml, sc_ml)
        h_c = modulate(rms_norm(x_ctx, norm_mix_pre[i]), sh_mc, sc_mc)
        if i % N_MIXERS == 0:
            y_l, y_c = hgrn2_mixer(h_l, h_c, hgrn_w_in[j], lower_bounds[:, j], hgrn_o_norm[j],
                                   hgrn_w_out[j], not last)
        else:
            y_l, y_c = gqa_mixer(h_l, h_c, attn_w_qkv[j], attn_q_norm[j], attn_k_norm[j],
                                 attn_w_out[j], cos, sin, not last)
        x_lat = x_lat + gt_ml * rms_norm(y_l, norm_mix_post[i])
        h_l = modulate(rms_norm(x_lat, norm_ffn_pre[i]), sh_fl, sc_fl)
        f_l = conv_ffn(h_l, ffn_w_in[i], ffn_conv_w[i], ffn_conv_b[i], ffn_w_out[i])
        x_lat = x_lat + gt_fl * rms_norm(f_l, norm_ffn_post[i])
        if not last:
            x_ctx = x_ctx + gt_mc * rms_norm(y_c, norm_mix_post[i])
            h_c = modulate(rms_norm(x_ctx, norm_ffn_pre[i]), sh_fc, sc_fc)
            f_c = conv_ffn(h_c, ffn_w_in[i], ffn_conv_w[i], ffn_conv_b[i], ffn_w_out[i])
            x_ctx = x_ctx + gt_fc * rms_norm(f_c, norm_ffn_post[i])
    return x_lat
```

```python
import functools
import math

import jax
import jax.numpy as jnp
from jax import lax
from jax.experimental import pallas as pl
from jax.experimental.pallas import tpu as pltpu

F32 = jnp.float32
BF16 = jnp.bfloat16

EPS = 1e-6
GRID_W = 64
ROPE_THETA = 10000.0
N_MOD = 6
HEAD = 128
GLA_CHUNK = 128
BF16_SUBLANES = 16
ADA_ROWS = 8
V7X_VMEM_BYTES = 64 * 1024 * 1024
VMEM_LIMIT = V7X_VMEM_BYTES - 12 * 1024 * 1024


def _pick(n, target, align):
    if n <= target:
        return n
    for t in range(target - target % align, 0, -align):
        if n % t == 0:
            return t
    return n


def _params(sem):
    return pltpu.CompilerParams(dimension_semantics=sem, vmem_limit_bytes=VMEM_LIMIT)


def _silu(x):
    return x * jax.nn.sigmoid(x)


def _dot(a, b):
    return jnp.dot(a, b, preferred_element_type=F32)


def _dot_nt(a, b):
    return lax.dot_general(a, b, (((1,), (1,)), ((), ())), preferred_element_type=F32)


def _dot_tn(a, b):
    return lax.dot_general(a, b, (((0,), (0,)), ((), ())), preferred_element_type=F32)


def _ada_kernel(c_ref, w_ref, b_ref, o_ref):
    sc = _silu(c_ref[...]).astype(BF16)
    o_ref[...] = _dot(sc, w_ref[...].astype(BF16)) + b_ref[...]


def _ada_call(cond, ada_w, ada_b):
    depth, d, n = ada_w.shape
    tn = _pick(n, 1024, HEAD)
    return pl.pallas_call(
        _ada_kernel,
        grid=(depth, n // tn),
        in_specs=[
            pl.BlockSpec((ADA_ROWS, d), lambda l, j: (0, 0)),
            pl.BlockSpec((None, d, tn), lambda l, j: (l, 0, j)),
            pl.BlockSpec((None, 1, tn), lambda l, j: (l, 0, j)),
        ],
        out_specs=pl.BlockSpec((None, ADA_ROWS, tn), lambda l, j: (l, 0, j)),
        out_shape=jax.ShapeDtypeStruct((depth, ADA_ROWS, n), F32),
        compiler_params=_params(("arbitrary", "arbitrary")),
        name="ada_ln",
    )(cond, ada_w, ada_b.reshape(depth, 1, n))


def _norm_mod(x, gain, shift, scale):
    ms = jnp.mean(x * x, axis=-1, keepdims=True)
    return (x * lax.rsqrt(ms + EPS) * gain) * (1.0 + scale) + shift


def _nmm_kernel(x_ref, gain_ref, shift_ref, scale_ref, w_ref, o_ref, h_ref):
    @pl.when(pl.program_id(1) == 0)
    def _():
        h_ref[...] = _norm_mod(x_ref[...], gain_ref[...], shift_ref[...], scale_ref[...]).astype(BF16)

    o_ref[...] = _dot(h_ref[...], w_ref[...]).astype(o_ref.dtype)


def _nmm_rope_kernel(x_ref, gain_ref, shift_ref, scale_ref, w_ref, hg_ref, cos_ref, sin_ref,
                     o_ref, h_ref, *, n_normed_blocks):
    j = pl.program_id(1)

    @pl.when(j == 0)
    def _():
        h_ref[...] = _norm_mod(x_ref[...], gain_ref[...], shift_ref[...], scale_ref[...]).astype(BF16)

    acc = _dot(h_ref[...], w_ref[...])

    @pl.when(j < n_normed_blocks)
    def _():
        cos = cos_ref[...]
        sin = sin_ref[...]
        for hh in range(acc.shape[1] // HEAD):
            cols = slice(hh * HEAD, (hh + 1) * HEAD)
            a = acc[:, cols]
            ms = jnp.mean(a * a, axis=-1, keepdims=True)
            a = a * lax.rsqrt(ms + EPS) * hg_ref[:, cols]
            a = a * cos + pltpu.roll(a, HEAD // 2, axis=1) * sin
            o_ref[:, cols] = a.astype(o_ref.dtype)

    @pl.when(j >= n_normed_blocks)
    def _():
        o_ref[...] = acc.astype(o_ref.dtype)


def _nmm_call(x, gain, shift, scale, w, *, rope=None, name):
    m, d = x.shape
    n = w.shape[1]
    bm = shift.shape[0]
    rows_per_mod = m // bm
    row_period = rows_per_mod if rope is None else math.gcd(rows_per_mod, rope[1].shape[0])
    tm = _pick(row_period, 1024, BF16_SUBLANES)
    tn = _pick(n, 1024, HEAD) if rope is None else _pick(math.gcd(n, rope[3]), 512, HEAD)
    mod_blocks = rows_per_mod // tm
    in_specs = [
        pl.BlockSpec((tm, d), lambda i, j: (i, 0)),
        pl.BlockSpec((1, d), lambda i, j: (0, 0)),
        pl.BlockSpec((None, 1, d), lambda i, j: (i // mod_blocks, 0, 0)),
        pl.BlockSpec((None, 1, d), lambda i, j: (i // mod_blocks, 0, 0)),
        pl.BlockSpec((d, tn), lambda i, j: (0, j)),
    ]
    args = [x, gain.reshape(1, d), shift, scale, w]
    if rope is None:
        body = _nmm_kernel
    else:
        head_gain, cos, sin, n_normed_cols = rope
        assert n_normed_cols % tn == 0 and cos.shape[0] % tm == 0
        table_blocks = cos.shape[0] // tm
        in_specs += [
            pl.BlockSpec((1, tn), lambda i, j: (0, j)),
            pl.BlockSpec((tm, HEAD), lambda i, j: (i % table_blocks, 0)),
            pl.BlockSpec((tm, HEAD), lambda i, j: (i % table_blocks, 0)),
        ]
        args += [head_gain, cos, sin]
        body = functools.partial(_nmm_rope_kernel, n_normed_blocks=n_normed_cols // tn)
    return pl.pallas_call(
        body,
        grid=(m // tm, n // tn),
        in_specs=in_specs,
        out_specs=pl.BlockSpec((tm, tn), lambda i, j: (i, j)),
        out_shape=jax.ShapeDtypeStruct((m, n), BF16),
        scratch_shapes=[pltpu.VMEM((tm, d), BF16)],
        compiler_params=_params(("arbitrary", "arbitrary")),
        name=name,
    )(*args)


def _residual(x, y, gain, gate):
    ms = jnp.mean(y * y, axis=-1, keepdims=True)
    return x + gate * (y * lax.rsqrt(ms + EPS) * gain)


def _proj_res_kernel(a_ref, w_ref, x_ref, gain_ref, gate_ref, o_ref):
    y = _dot(a_ref[...], w_ref[...])
    o_ref[...] = _residual(x_ref[...], y, gain_ref[...], gate_ref[...])


def _proj_res_call(a, w, x, gain, gate, *, name):
    m, k = a.shape
    d = w.shape[1]
    bm = gate.shape[0]
    rows_per_mod = m // bm
    tm = _pick(rows_per_mod, 512, BF16_SUBLANES)
    mod_blocks = rows_per_mod // tm
    return pl.pallas_call(
        _proj_res_kernel,
        grid=(m // tm,),
        in_specs=[
            pl.BlockSpec((tm, k), lambda i: (i, 0)),
            pl.BlockSpec((k, d), lambda i: (0, 0)),
            pl.BlockSpec((tm, d), lambda i: (i, 0)),
            pl.BlockSpec((1, d), lambda i: (0, 0)),
            pl.BlockSpec((None, 1, d), lambda i: (i // mod_blocks, 0, 0)),
        ],
        out_specs=pl.BlockSpec((tm, d), lambda i: (i, 0)),
        out_shape=jax.ShapeDtypeStruct((m, d), F32),
        compiler_params=_params(("arbitrary",)),
        name=name,
    )(a, w, x, gain.reshape(1, d), gate)


def _ffn_out_kernel(g_ref, gprev_ref, gnext_ref, u_ref, cw_ref, cb_ref, w_ref, x_ref, gain_ref, gate_ref,
                    o_ref, acc_ref, *, blocks_per_seq):
    i = pl.program_id(0)
    k = pl.program_id(1)
    tm = g_ref.shape[0]
    g = g_ref[...].astype(F32)
    seq_block = i % blocks_per_seq
    prev_row = jnp.where(seq_block > 0, gprev_ref[BF16_SUBLANES - 1:BF16_SUBLANES, :].astype(F32), 0.0)
    next_row = jnp.where(seq_block < blocks_per_seq - 1, gnext_ref[0:1, :].astype(F32), 0.0)
    row = lax.broadcasted_iota(jnp.int32, g.shape, 0)
    g_prev = jnp.where(row == 0, prev_row, pltpu.roll(g, 1, axis=0))
    g_next = jnp.where(row == tm - 1, next_row, pltpu.roll(g, tm - 1, axis=0))
    conv = g_prev * cw_ref[0:1, :] + g * cw_ref[1:2, :] + g_next * cw_ref[2:3, :] + cb_ref[...]
    a = (_silu(conv) * u_ref[...].astype(F32)).astype(BF16)
    part = _dot(a, w_ref[...])

    @pl.when(k == 0)
    def _():
        acc_ref[...] = part

    @pl.when(k > 0)
    def _():
        acc_ref[...] += part

    @pl.when(k == pl.num_programs(1) - 1)
    def _():
        o_ref[...] = _residual(x_ref[...], acc_ref[...], gain_ref[...], gate_ref[...])


def _ffn_out_call(gu, conv_w, conv_b, w, x, gain, gate, *, seq_len, name):
    m, f2 = gu.shape
    f = f2 // 2
    d = w.shape[1]
    bm = gate.shape[0]
    rows_per_mod = m // bm
    tm = _pick(seq_len, 512, BF16_SUBLANES)
    tk = _pick(f, 512, HEAD)
    assert rows_per_mod % tm == 0 and tm % BF16_SUBLANES == 0
    mod_blocks = rows_per_mod // tm
    kb = f // tk
    halo_per_tm = tm // BF16_SUBLANES
    last_halo = m // BF16_SUBLANES - 1
    body = functools.partial(_ffn_out_kernel, blocks_per_seq=seq_len // tm)
    return pl.pallas_call(
        body,
        grid=(m // tm, kb),
        in_specs=[
            pl.BlockSpec((tm, tk), lambda i, k: (i, k)),
            pl.BlockSpec((BF16_SUBLANES, tk), lambda i, k: (jnp.maximum(i * halo_per_tm - 1, 0), k)),
            pl.BlockSpec((BF16_SUBLANES, tk), lambda i, k: (jnp.minimum((i + 1) * halo_per_tm, last_halo), k)),
            pl.BlockSpec((tm, tk), lambda i, k: (i, kb + k)),
            pl.BlockSpec((3, tk), lambda i, k: (0, k)),
            pl.BlockSpec((1, tk), lambda i, k: (0, k)),
            pl.BlockSpec((tk, d), lambda i, k: (k, 0)),
            pl.BlockSpec((tm, d), lambda i, k: (i, 0)),
            pl.BlockSpec((1, d), lambda i, k: (0, 0)),
            pl.BlockSpec((None, 1, d), lambda i, k: (i // mod_blocks, 0, 0)),
        ],
        out_specs=pl.BlockSpec((tm, d), lambda i, k: (i, 0)),
        out_shape=jax.ShapeDtypeStruct((m, d), F32),
        scratch_shapes=[pltpu.VMEM((tm, d), F32)],
        compiler_params=_params(("arbitrary", "arbitrary")),
        name=name,
    )(gu, gu, gu, gu, conv_w, conv_b.reshape(1, f), w, x, gain.reshape(1, d), gate)


def _gla_chunk(q_raw, f_raw, v_raw, lb, st_ref, forward):
    c = q_raw.shape[0]
    qf = q_raw.astype(F32)
    qs = _silu(qf) * (HEAD ** -0.5)
    f = lb + (1.0 - lb) * jax.nn.sigmoid(f_raw.astype(F32))
    kk = 1.0 - f
    lf = jnp.log(f)
    row = lax.broadcasted_iota(jnp.int32, (c, c), 0)
    col = lax.broadcasted_iota(jnp.int32, (c, c), 1)
    within = (col <= row) if forward else (col >= row)
    ones = jnp.where(within, 1.0, 0.0).astype(BF16)
    lf_hi = lf.astype(BF16)
    lf_lo = (lf - lf_hi.astype(F32)).astype(BF16)
    b = _dot(ones, lf_hi) + _dot(ones, lf_lo)
    a_row = c // 2 - 1 if forward else c // 2
    e_row = c - 1 if forward else 0
    anchor = b[a_row:a_row + 1, :]
    b_end = b[e_row:e_row + 1, :]
    qa = (qs * jnp.exp(b - anchor)).astype(BF16)
    ka = (kk * jnp.exp(anchor - b)).astype(BF16)
    scores = jnp.where(within, _dot_nt(qa, ka), 0.0)
    st = st_ref[...]
    qb = (qs * jnp.exp(b)).astype(BF16)
    o = _dot(scores.astype(BF16), v_raw) + _dot_nt(qb, st.astype(BF16))
    k_end = (kk * jnp.exp(b_end - b)).astype(BF16)
    st_ref[...] = st * jnp.exp(b_end) + _dot_tn(v_raw, k_end)
    return o


def _gla_readout(o, g_raw, o_gain):
    ms = jnp.mean(o * o, axis=-1, keepdims=True)
    y = o * lax.rsqrt(ms + EPS) * o_gain
    return (y * _silu(g_raw.astype(F32))).astype(BF16)


def _gla_kernel(qc, ffc, fbc, vc, gc, ql, ffl, fbl, vl, gl, lb_ref, og_ref, yc_ref, yl_ref,
                ofc, obc, ofl, obl, sf_ref, sb_ref):
    c = GLA_CHUNK
    n_ctx = qc.shape[0] // c
    n_lat = ql.shape[0] // c
    lb_f = lb_ref[0]
    lb_b = lb_ref[1]
    sf_ref[...] = jnp.zeros_like(sf_ref)
    sb_ref[...] = jnp.zeros_like(sb_ref)

    for i in range(n_ctx):
        rf = slice(i * c, (i + 1) * c)
        rb = slice((n_ctx - 1 - i) * c, (n_ctx - i) * c)
        ofc[rf, :] = _gla_chunk(qc[rf, :], ffc[rf, :], vc[rf, :], lb_f, sf_ref, True)
        obc[rb, :] = _gla_chunk(qc[rb, :], fbc[rb, :], vc[rb, :], lb_b, sb_ref, False)

    def lat_body(i, carry):
        rf = pl.ds(pl.multiple_of(i * c, c), c)
        rb = pl.ds(pl.multiple_of((n_lat - 1 - i) * c, c), c)
        ofl[rf, :] = _gla_chunk(ql[rf, :], ffl[rf, :], vl[rf, :], lb_f, sf_ref, True)
        obl[rb, :] = _gla_chunk(ql[rb, :], fbl[rb, :], vl[rb, :], lb_b, sb_ref, False)
        return carry

    lax.fori_loop(0, n_lat, lat_body, 0)

    o_gain = og_ref[...]
    for i in range(n_ctx):
        r = slice(i * c, (i + 1) * c)
        yc_ref[r, :] = _gla_readout(ofc[r, :] + obc[r, :], gc[r, :], o_gain)

    def out_body(i, carry):
        r = pl.ds(pl.multiple_of(i * c, c), c)
        yl_ref[r, :] = _gla_readout(ofl[r, :] + obl[r, :], gl[r, :], o_gain)
        return carry

    lax.fori_loop(0, n_lat, out_body, 0)


def _gla_call(p_ctx, p_lat, lower_bound, o_gain):
    b, tc, d5 = p_ctx.shape
    t = p_lat.shape[1]
    d = d5 // 5
    heads = d // HEAD
    assert tc % GLA_CHUNK == 0 and t % GLA_CHUNK == 0

    def col(tt, part):
        return pl.BlockSpec((None, tt, HEAD), lambda bi, h: (bi, 0, part * heads + h))

    out_spec = lambda tt: pl.BlockSpec((None, tt, HEAD), lambda bi, h: (bi, 0, h))
    return pl.pallas_call(
        _gla_kernel,
        grid=(b, heads),
        in_specs=[col(tc, p) for p in range(5)] + [col(t, p) for p in range(5)] + [
            pl.BlockSpec((2, None, 1, HEAD), lambda bi, h: (0, h, 0, 0)),
            pl.BlockSpec((1, HEAD), lambda bi, h: (0, 0)),
        ],
        out_specs=[out_spec(tc), out_spec(t)],
        out_shape=[jax.ShapeDtypeStruct((b, tc, d), BF16), jax.ShapeDtypeStruct((b, t, d), BF16)],
        scratch_shapes=[
            pltpu.VMEM((tc, HEAD), F32), pltpu.VMEM((tc, HEAD), F32),
            pltpu.VMEM((t, HEAD), F32), pltpu.VMEM((t, HEAD), F32),
            pltpu.VMEM((HEAD, HEAD), F32), pltpu.VMEM((HEAD, HEAD), F32),
        ],
        compiler_params=_params(("arbitrary", "arbitrary")),
        name="hgrn_scan",
    )(*([p_ctx] * 5), *([p_lat] * 5), lower_bound.reshape(2, heads, 1, HEAD), o_gain.reshape(1, HEAD))


def _flash_kernel(q_ref, k_ref, v_ref, o_ref, *, tk):
    tq = q_ref.shape[0]
    group = q_ref.shape[1] // HEAD
    q = jnp.concatenate([q_ref[:, g * HEAD:(g + 1) * HEAD] for g in range(group)], axis=0)
    rows = group * tq

    def body(j, carry):
        m, l, acc = carry
        r = pl.ds(pl.multiple_of(j * tk, tk), tk)
        s = _dot_nt(q, k_ref[r, :])
        m_new = jnp.maximum(m, jnp.max(s, axis=-1, keepdims=True))
        alpha = jnp.exp(m - m_new)
        p = jnp.exp(s - m_new)
        l = alpha * l + jnp.sum(p, axis=-1, keepdims=True)
        acc = alpha * acc + _dot(p.astype(BF16), v_ref[r, :])
        return m_new, l, acc

    init = (jnp.full((rows, 1), -jnp.inf, F32), jnp.zeros((rows, 1), F32), jnp.zeros((rows, HEAD), F32))
    _, l, acc = lax.fori_loop(0, k_ref.shape[0] // tk, body, init)
    out = acc / l
    for g in range(group):
        o_ref[:, g * HEAD:(g + 1) * HEAD] = out[g * tq:(g + 1) * tq, :].astype(o_ref.dtype)


def _flash_call(qkv, k_all, v_all, *, heads):
    b, t, _ = qkv.shape
    tk_total = k_all.shape[1]
    kvh = k_all.shape[2] // HEAD
    group = heads // kvh
    tq = _pick(t, 256, BF16_SUBLANES)
    tk = _pick(tk_total, 768, HEAD)
    body = functools.partial(_flash_kernel, tk=tk)
    return pl.pallas_call(
        body,
        grid=(b, kvh, t // tq),
        in_specs=[
            pl.BlockSpec((None, tq, group * HEAD), lambda bi, h, i: (bi, i, h)),
            pl.BlockSpec((None, tk_total, HEAD), lambda bi, h, i: (bi, 0, h)),
            pl.BlockSpec((None, tk_total, HEAD), lambda bi, h, i: (bi, 0, h)),
        ],
        out_specs=pl.BlockSpec((None, tq, group * HEAD), lambda bi, h, i: (bi, i, h)),
        out_shape=jax.ShapeDtypeStruct((b, t, heads * HEAD), BF16),
        compiler_params=_params(("arbitrary", "arbitrary", "arbitrary")),
        name="gqa_flash",
    )(qkv, k_all, v_all)


def _rope_tables(t):
    pos = jnp.arange(t, dtype=jnp.int32)
    rows = (pos // GRID_W).astype(F32)
    cols = (pos % GRID_W).astype(F32)
    axis_dim = HEAD // 2
    inv_freq = ROPE_THETA ** (-jnp.arange(0, axis_dim, 2, dtype=F32) / axis_dim)
    ang = jnp.concatenate([rows[:, None] * inv_freq, cols[:, None] * inv_freq], axis=-1)
    cos, sin = jnp.cos(ang), jnp.sin(ang)
    return jnp.concatenate([cos, cos], axis=-1), jnp.concatenate([-sin, sin], axis=-1)


def _split_mods(mods, b):
    d = mods.shape[1] // N_MOD
    lat = [mods[:b, i * d:(i + 1) * d].reshape(b, 1, d) for i in range(N_MOD)]
    ctx = [mods[b:b + 1, i * d:(i + 1) * d].reshape(1, 1, d) for i in range(N_MOD)]
    return lat, ctx


def kernel(x, c, ctx, c_ctx, ada_w, ada_b, norm_mix_pre, norm_mix_post, norm_ffn_pre, norm_ffn_post, hgrn_w_in, hgrn_lb_logits, hgrn_o_norm, hgrn_w_out, attn_w_qkv, attn_q_norm, attn_k_norm, attn_w_out, ffn_w_in, ffn_conv_w, ffn_conv_b, ffn_w_out):
    b, t, d = x.shape
    tc = ctx.shape[1]
    heads = d // HEAD
    assert ada_w.shape[0] == 2 and b + 1 <= ADA_ROWS

    cond = jnp.zeros((ADA_ROWS, d), F32).at[:b].set(c).at[b].set(c_ctx)
    mods = _ada_call(cond, ada_w, ada_b)
    x_lat = x.reshape(b * t, d)
    x_ctx = ctx.reshape(b * tc, d)

    (sh_ml, sc_ml, gt_ml, sh_fl, sc_fl, gt_fl), (sh_mc, sc_mc, gt_mc, sh_fc, sc_fc, gt_fc) = _split_mods(mods[0], b)
    lower_bound = jnp.cumsum(jax.nn.softmax(hgrn_lb_logits.astype(F32), axis=1), axis=1)[:, 0]
    w_in = hgrn_w_in[0].astype(BF16)
    p_lat = _nmm_call(x_lat, norm_mix_pre[0], sh_ml, sc_ml, w_in, name="hgrn_in_lat")
    p_ctx = _nmm_call(x_ctx, norm_mix_pre[0], sh_mc, sc_mc, w_in, name="hgrn_in_ctx")
    y_ctx, y_lat = _gla_call(p_ctx.reshape(b, tc, 5 * d), p_lat.reshape(b, t, 5 * d), lower_bound, hgrn_o_norm[0])
    w_out = hgrn_w_out[0].astype(BF16)
    x_lat = _proj_res_call(y_lat.reshape(b * t, d), w_out, x_lat, norm_mix_post[0], gt_ml, name="hgrn_out_lat")
    x_ctx = _proj_res_call(y_ctx.reshape(b * tc, d), w_out, x_ctx, norm_mix_post[0], gt_mc, name="hgrn_out_ctx")
    w_in = ffn_w_in[0].astype(BF16)
    w_out = ffn_w_out[0].astype(BF16)
    gu = _nmm_call(x_lat, norm_ffn_pre[0], sh_fl, sc_fl, w_in, name="ffn0_in_lat")
    x_lat = _ffn_out_call(gu, ffn_conv_w[0], ffn_conv_b[0], w_out, x_lat, norm_ffn_post[0], gt_fl,
                          seq_len=t, name="ffn0_out_lat")
    gu = _nmm_call(x_ctx, norm_ffn_pre[0], sh_fc, sc_fc, w_in, name="ffn0_in_ctx")
    x_ctx = _ffn_out_call(gu, ffn_conv_w[0], ffn_conv_b[0], w_out, x_ctx, norm_ffn_post[0], gt_fc,
                          seq_len=tc, name="ffn0_out_ctx")

    (sh_ml, sc_ml, gt_ml, sh_fl, sc_fl, gt_fl), (sh_mc, sc_mc, _, _, _, _) = _split_mods(mods[1], b)
    n_qkv = attn_w_qkv.shape[2]
    kvh = (n_qkv // HEAD - heads) // 2
    perm = jnp.concatenate([jnp.arange(0, HEAD, 2), jnp.arange(1, HEAD, 2)])
    head_perm = (jnp.arange(heads + kvh)[:, None] * HEAD + perm[None, :]).reshape(-1)
    col_perm = jnp.concatenate([head_perm, jnp.arange((heads + kvh) * HEAD, n_qkv)])
    w_qkv = attn_w_qkv[0][:, col_perm].astype(BF16)
    head_gain = jnp.concatenate([
        jnp.tile(attn_q_norm[0][perm] * (HEAD ** -0.5), heads),
        jnp.tile(attn_k_norm[0][perm], kvh),
        jnp.ones((kvh * HEAD,), F32),
    ]).reshape(1, n_qkv)
    cos, sin = _rope_tables(t)
    qk_cols = (heads + kvh) * HEAD
    qkv = _nmm_call(x_lat, norm_mix_pre[1], sh_ml, sc_ml, w_qkv,
                    rope=(head_gain, cos, sin, qk_cols), name="attn_qkv_lat")
    q_cols = heads * HEAD
    kv_ctx = _nmm_call(x_ctx, norm_mix_pre[1], sh_mc, sc_mc, w_qkv[:, q_cols:],
                       rope=(head_gain[:, q_cols:], jnp.ones((tc, HEAD), F32), jnp.zeros((tc, HEAD), F32),
                             kvh * HEAD), name="attn_kv_ctx")
    qkv = qkv.reshape(b, t, n_qkv)
    kv_ctx = kv_ctx.reshape(b, tc, 2 * kvh * HEAD)
    k_all = jnp.concatenate([kv_ctx[:, :, :kvh * HEAD], qkv[:, :, q_cols:qk_cols]], axis=1)
    v_all = jnp.concatenate([kv_ctx[:, :, kvh * HEAD:], qkv[:, :, qk_cols:]], axis=1)
    o = _flash_call(qkv, k_all, v_all, heads=heads)
    x_lat = _proj_res_call(o.reshape(b * t, d), attn_w_out[0].astype(BF16), x_lat, norm_mix_post[1], gt_ml,
                           name="attn_out_lat")
    gu = _nmm_call(x_lat, norm_ffn_pre[1], sh_fl, sc_fl, ffn_w_in[1].astype(BF16), name="ffn1_in_lat")
    x_lat = _ffn_out_call(gu, ffn_conv_w[1], ffn_conv_b[1], ffn_w_out[1].astype(BF16), x_lat, norm_ffn_post[1],
                          gt_fl, seq_len=t, name="ffn1_out_lat")
    return x_lat.reshape(b, t, d)
```

```python
import functools
import math

import jax
import jax.numpy as jnp
from jax import lax
from jax.experimental import pallas as pl
from jax.experimental.pallas import tpu as pltpu

F32 = jnp.float32
BF16 = jnp.bfloat16

EPS = 1e-6
GRID_W = 64
ROPE_THETA = 10000.0
LOG2_E = math.log2(math.e)
N_MOD = 6
HEAD = 128
GLA_CHUNK = 128
BF16_SUBLANES = 16
ADA_ROWS = 8
V7X_VMEM_BYTES = 64 * 1024 * 1024
VMEM_LIMIT = V7X_VMEM_BYTES - 12 * 1024 * 1024


def _pick(n, target, align):
    if n <= target:
        return n
    for t in range(target - target % align, 0, -align):
        if n % t == 0:
            return t
    return n


def _params(sem):
    return pltpu.CompilerParams(dimension_semantics=sem, vmem_limit_bytes=VMEM_LIMIT)


def _silu(x):
    return x * jax.nn.sigmoid(x)


def _dot(a, b):
    return jnp.dot(a, b, preferred_element_type=F32)


def _dot_nt(a, b):
    return lax.dot_general(a, b, (((1,), (1,)), ((), ())), preferred_element_type=F32)


def _dot_tn(a, b):
    return lax.dot_general(a, b, (((0,), (0,)), ((), ())), preferred_element_type=F32)


def _ada_kernel(c_ref, w_ref, b_ref, o_ref):
    sc = _silu(c_ref[...]).astype(BF16)
    o_ref[...] = _dot(sc, w_ref[...].astype(BF16)) + b_ref[...]


def _ada_call(cond, ada_w, ada_b):
    depth, d, n = ada_w.shape
    tn = _pick(n, 1024, HEAD)
    return pl.pallas_call(
        _ada_kernel,
        grid=(depth, n // tn),
        in_specs=[
            pl.BlockSpec((ADA_ROWS, d), lambda l, j: (0, 0)),
            pl.BlockSpec((None, d, tn), lambda l, j: (l, 0, j)),
            pl.BlockSpec((None, 1, tn), lambda l, j: (l, 0, j)),
        ],
        out_specs=pl.BlockSpec((None, ADA_ROWS, tn), lambda l, j: (l, 0, j)),
        out_shape=jax.ShapeDtypeStruct((depth, ADA_ROWS, n), F32),
        compiler_params=_params(("arbitrary", "arbitrary")),
        name="ada_ln",
    )(cond, ada_w, ada_b.reshape(depth, 1, n))


def _norm_mod(x, gain, shift, scale):
    ms = jnp.mean(x * x, axis=-1, keepdims=True)
    return (x * lax.rsqrt(ms + EPS) * gain) * (1.0 + scale) + shift


def _nmm_kernel(x_ref, gain_ref, shift_ref, scale_ref, w_ref, o_ref, h_ref):
    @pl.when(pl.program_id(1) == 0)
    def _():
        h_ref[...] = _norm_mod(x_ref[...], gain_ref[...], shift_ref[...], scale_ref[...]).astype(BF16)

    o_ref[...] = _dot(h_ref[...], w_ref[...]).astype(o_ref.dtype)


def _nmm_rope_kernel(x_ref, gain_ref, shift_ref, scale_ref, w_ref, hg_ref, cos_ref, sin_ref,
                     o_ref, h_ref, *, n_normed_blocks):
    j = pl.program_id(1)

    @pl.when(j == 0)
    def _():
        h_ref[...] = _norm_mod(x_ref[...], gain_ref[...], shift_ref[...], scale_ref[...]).astype(BF16)

    acc = _dot(h_ref[...], w_ref[...])

    @pl.when(j < n_normed_blocks)
    def _():
        cos = cos_ref[...]
        sin = sin_ref[...]
        for hh in range(acc.shape[1] // HEAD):
            cols = slice(hh * HEAD, (hh + 1) * HEAD)
            a = acc[:, cols]
            ms = jnp.mean(a * a, axis=-1, keepdims=True)
            a = a * lax.rsqrt(ms + EPS) * hg_ref[:, cols]
            a = a * cos + pltpu.roll(a, HEAD // 2, axis=1) * sin
            o_ref[:, cols] = a.astype(o_ref.dtype)

    @pl.when(j >= n_normed_blocks)
    def _():
        o_ref[...] = acc.astype(o_ref.dtype)


def _nmm_call(x, gain, shift, scale, w, *, rope=None, name):
    m, d = x.shape
    n = w.shape[1]
    bm = shift.shape[0]
    rows_per_mod = m // bm
    row_period = rows_per_mod if rope is None else math.gcd(rows_per_mod, rope[1].shape[0])
    tm = _pick(row_period, 1024, BF16_SUBLANES)
    tn = _pick(n, 1024, HEAD) if rope is None else _pick(math.gcd(n, rope[3]), 512, HEAD)
    mod_blocks = rows_per_mod // tm
    in_specs = [
        pl.BlockSpec((tm, d), lambda i, j: (i, 0)),
        pl.BlockSpec((1, d), lambda i, j: (0, 0)),
        pl.BlockSpec((None, 1, d), lambda i, j: (i // mod_blocks, 0, 0)),
        pl.BlockSpec((None, 1, d), lambda i, j: (i // mod_blocks, 0, 0)),
        pl.BlockSpec((d, tn), lambda i, j: (0, j)),
    ]
    args = [x, gain.reshape(1, d), shift, scale, w]
    if rope is None:
        body = _nmm_kernel
    else:
        head_gain, cos, sin, n_normed_cols = rope
        assert n_normed_cols % tn == 0 and cos.shape[0] % tm == 0
        table_blocks = cos.shape[0] // tm
        in_specs += [
            pl.BlockSpec((1, tn), lambda i, j: (0, j)),
            pl.BlockSpec((tm, HEAD), lambda i, j: (i % table_blocks, 0)),
            pl.BlockSpec((tm, HEAD), lambda i, j: (i % table_blocks, 0)),
        ]
        args += [head_gain, cos, sin]
        body = functools.partial(_nmm_rope_kernel, n_normed_blocks=n_normed_cols // tn)
    return pl.pallas_call(
        body,
        grid=(m // tm, n // tn),
        in_specs=in_specs,
        out_specs=pl.BlockSpec((tm, tn), lambda i, j: (i, j)),
        out_shape=jax.ShapeDtypeStruct((m, n), BF16),
        scratch_shapes=[pltpu.VMEM((tm, d), BF16)],
        compiler_params=_params(("arbitrary", "arbitrary")),
        name=name,
    )(*args)


def _residual(x, y, gain, gate):
    ms = jnp.mean(y * y, axis=-1, keepdims=True)
    return x + gate * (y * lax.rsqrt(ms + EPS) * gain)


def _proj_res_kernel(a_ref, w_ref, x_ref, gain_ref, gate_ref, o_ref):
    y = _dot(a_ref[...], w_ref[...])
    o_ref[...] = _residual(x_ref[...], y, gain_ref[...], gate_ref[...])


def _proj_res_call(a, w, x, gain, gate, *, name):
    m, k = a.shape
    d = w.shape[1]
    bm = gate.shape[0]
    rows_per_mod = m // bm
    tm = _pick(rows_per_mod, 512, BF16_SUBLANES)
    mod_blocks = rows_per_mod // tm
    return pl.pallas_call(
        _proj_res_kernel,
        grid=(m // tm,),
        in_specs=[
            pl.BlockSpec((tm, k), lambda i: (i, 0)),
            pl.BlockSpec((k, d), lambda i: (0, 0)),
            pl.BlockSpec((tm, d), lambda i: (i, 0)),
            pl.BlockSpec((1, d), lambda i: (0, 0)),
            pl.BlockSpec((None, 1, d), lambda i: (i // mod_blocks, 0, 0)),
        ],
        out_specs=pl.BlockSpec((tm, d), lambda i: (i, 0)),
        out_shape=jax.ShapeDtypeStruct((m, d), F32),
        compiler_params=_params(("arbitrary",)),
        name=name,
    )(a, w, x, gain.reshape(1, d), gate)


def _ffn_out_kernel(g_ref, gprev_ref, gnext_ref, u_ref, cw_ref, cb_ref, w_ref, x_ref, gain_ref, gate_ref,
                    o_ref, acc_ref, *, blocks_per_seq):
    i = pl.program_id(0)
    k = pl.program_id(1)
    tm = g_ref.shape[0]
    g = g_ref[...].astype(F32)
    seq_block = i % blocks_per_seq
    prev_row = jnp.where(seq_block > 0, gprev_ref[BF16_SUBLANES - 1:BF16_SUBLANES, :].astype(F32), 0.0)
    next_row = jnp.where(seq_block < blocks_per_seq - 1, gnext_ref[0:1, :].astype(F32), 0.0)
    row = lax.broadcasted_iota(jnp.int32, g.shape, 0)
    g_prev = jnp.where(row == 0, prev_row, pltpu.roll(g, 1, axis=0))
    g_next = jnp.where(row == tm - 1, next_row, pltpu.roll(g, tm - 1, axis=0))
    conv = g_prev * cw_ref[0:1, :] + g * cw_ref[1:2, :] + g_next * cw_ref[2:3, :] + cb_ref[...]
    a = (_silu(conv) * u_ref[...].astype(F32)).astype(BF16)
    part = _dot(a, w_ref[...])

    @pl.when(k == 0)
    def _():
        acc_ref[...] = part

    @pl.when(k > 0)
    def _():
        acc_ref[...] += part

    @pl.when(k == pl.num_programs(1) - 1)
    def _():
        o_ref[...] = _residual(x_ref[...], acc_ref[...], gain_ref[...], gate_ref[...])


def _ffn_out_call(gu, conv_w, conv_b, w, x, gain, gate, *, seq_len, name):
    m, f2 = gu.shape
    f = f2 // 2
    d = w.shape[1]
    bm = gate.shape[0]
    rows_per_mod = m // bm
    tm = _pick(seq_len, 512, BF16_SUBLANES)
    tk = _pick(f, 512, HEAD)
    assert rows_per_mod % tm == 0 and tm % BF16_SUBLANES == 0
    mod_blocks = rows_per_mod // tm
    kb = f // tk
    halo_per_tm = tm // BF16_SUBLANES
    last_halo = m // BF16_SUBLANES - 1
    body = functools.partial(_ffn_out_kernel, blocks_per_seq=seq_len // tm)
    return pl.pallas_call(
        body,
        grid=(m // tm, kb),
        in_specs=[
            pl.BlockSpec((tm, tk), lambda i, k: (i, k)),
            pl.BlockSpec((BF16_SUBLANES, tk), lambda i, k: (jnp.maximum(i * halo_per_tm - 1, 0), k)),
            pl.BlockSpec((BF16_SUBLANES, tk), lambda i, k: (jnp.minimum((i + 1) * halo_per_tm, last_halo), k)),
            pl.BlockSpec((tm, tk), lambda i, k: (i, kb + k)),
            pl.BlockSpec((3, tk), lambda i, k: (0, k)),
            pl.BlockSpec((1, tk), lambda i, k: (0, k)),
            pl.BlockSpec((tk, d), lambda i, k: (k, 0)),
            pl.BlockSpec((tm, d), lambda i, k: (i, 0)),
            pl.BlockSpec((1, d), lambda i, k: (0, 0)),
            pl.BlockSpec((None, 1, d), lambda i, k: (i // mod_blocks, 0, 0)),
        ],
        out_specs=pl.BlockSpec((tm, d), lambda i, k: (i, 0)),
        out_shape=jax.ShapeDtypeStruct((m, d), F32),
        scratch_shapes=[pltpu.VMEM((tm, d), F32)],
        compiler_params=_params(("arbitrary", "arbitrary")),
        name=name,
    )(gu, gu, gu, gu, conv_w, conv_b.reshape(1, f), w, x, gain.reshape(1, d), gate)


def _gla_chunk(q_raw, f_raw, v_raw, lb, st_ref, forward):
    c = q_raw.shape[0]
    qf = q_raw.astype(F32)
    qs = _silu(qf) * (HEAD ** -0.5)
    f = lb + (1.0 - lb) * jax.nn.sigmoid(f_raw.astype(F32))
    kk = 1.0 - f
    lf = jnp.log(f)
    row = lax.broadcasted_iota(jnp.int32, (c, c), 0)
    col = lax.broadcasted_iota(jnp.int32, (c, c), 1)
    within = (col <= row) if forward else (col >= row)
    ones = jnp.where(within, 1.0, 0.0).astype(BF16)
    lf_hi = lf.astype(BF16)
    lf_lo = (lf - lf_hi.astype(F32)).astype(BF16)
    b = _dot(ones, lf_hi) + _dot(ones, lf_lo)
    a_row = c // 2 - 1 if forward else c // 2
    e_row = c - 1 if forward else 0
    anchor = b[a_row:a_row + 1, :]
    b_end = b[e_row:e_row + 1, :]
    qa = (qs * jnp.exp(b - anchor)).astype(BF16)
    ka = (kk * jnp.exp(anchor - b)).astype(BF16)
    scores = jnp.where(within, _dot_nt(qa, ka), 0.0)
    st = st_ref[...]
    qb = (qs * jnp.exp(b)).astype(BF16)
    o = _dot(scores.astype(BF16), v_raw) + _dot_nt(qb, st.astype(BF16))
    k_end = (kk * jnp.exp(b_end - b)).astype(BF16)
    st_ref[...] = st * jnp.exp(b_end) + _dot_tn(v_raw, k_end)
    return o


def _gla_readout(o, g_raw, o_gain):
    ms = jnp.mean(o * o, axis=-1, keepdims=True)
    y = o * lax.rsqrt(ms + EPS) * o_gain
    return (y * _silu(g_raw.astype(F32))).astype(BF16)


def _gla_kernel(qc, ffc, fbc, vc, gc, ql, ffl, fbl, vl, gl, lb_ref, og_ref, yc_ref, yl_ref,
                ofc, obc, ofl, obl, sf_ref, sb_ref):
    c = GLA_CHUNK
    n_ctx = qc.shape[0] // c
    n_lat = ql.shape[0] // c
    lb_f = lb_ref[0]
    lb_b = lb_ref[1]
    sf_ref[...] = jnp.zeros_like(sf_ref)
    sb_ref[...] = jnp.zeros_like(sb_ref)

    for i in range(n_ctx):
        rf = slice(i * c, (i + 1) * c)
        rb = slice((n_ctx - 1 - i) * c, (n_ctx - i) * c)
        ofc[rf, :] = _gla_chunk(qc[rf, :], ffc[rf, :], vc[rf, :], lb_f, sf_ref, True)
        obc[rb, :] = _gla_chunk(qc[rb, :], fbc[rb, :], vc[rb, :], lb_b, sb_ref, False)

    def lat_body(i, carry):
        rf = pl.ds(pl.multiple_of(i * c, c), c)
        rb = pl.ds(pl.multiple_of((n_lat - 1 - i) * c, c), c)
        ofl[rf, :] = _gla_chunk(ql[rf, :], ffl[rf, :], vl[rf, :], lb_f, sf_ref, True)
        obl[rb, :] = _gla_chunk(ql[rb, :], fbl[rb, :], vl[rb, :], lb_b, sb_ref, False)
        return carry

    lax.fori_loop(0, n_lat, lat_body, 0)

    o_gain = og_ref[...]
    for i in range(n_ctx):
        r = slice(i * c, (i + 1) * c)
        yc_ref[r, :] = _gla_readout(ofc[r, :] + obc[r, :], gc[r, :], o_gain)

    def out_body(i, carry):
        r = pl.ds(pl.multiple_of(i * c, c), c)
        yl_ref[r, :] = _gla_readout(ofl[r, :] + obl[r, :], gl[r, :], o_gain)
        return carry

    lax.fori_loop(0, n_lat, out_body, 0)


def _gla_call(p_ctx, p_lat, lower_bound, o_gain):
    b, tc, d5 = p_ctx.shape
    t = p_lat.shape[1]
    d = d5 // 5
    heads = d // HEAD
    assert tc % GLA_CHUNK == 0 and t % GLA_CHUNK == 0

    def col(tt, part):
        return pl.BlockSpec((None, tt, HEAD), lambda bi, h: (bi, 0, part * heads + h))

    out_spec = lambda tt: pl.BlockSpec((None, tt, HEAD), lambda bi, h: (bi, 0, h))
    return pl.pallas_call(
        _gla_kernel,
        grid=(b, heads),
        in_specs=[col(tc, p) for p in range(5)] + [col(t, p) for p in range(5)] + [
            pl.BlockSpec((2, None, 1, HEAD), lambda bi, h: (0, h, 0, 0)),
            pl.BlockSpec((1, HEAD), lambda bi, h: (0, 0)),
        ],
        out_specs=[out_spec(tc), out_spec(t)],
        out_shape=[jax.ShapeDtypeStruct((b, tc, d), BF16), jax.ShapeDtypeStruct((b, t, d), BF16)],
        scratch_shapes=[
            pltpu.VMEM((tc, HEAD), F32), pltpu.VMEM((tc, HEAD), F32),
            pltpu.VMEM((t, HEAD), F32), pltpu.VMEM((t, HEAD), F32),
            pltpu.VMEM((HEAD, HEAD), F32), pltpu.VMEM((HEAD, HEAD), F32),
        ],
        compiler_params=_params(("arbitrary", "arbitrary")),
        name="hgrn_scan",
    )(*([p_ctx] * 5), *([p_lat] * 5), lower_bound.reshape(2, heads, 1, HEAD), o_gain.reshape(1, HEAD))


def _flash_kernel(q_ref, kc_ref, vc_ref, kl_ref, vl_ref, o_ref, *, tk, unroll):
    tq = q_ref.shape[0]
    group = q_ref.shape[1] // HEAD
    qs = [q_ref[:, g * HEAD:(g + 1) * HEAD] for g in range(group)]

    def attend(carry, k, v):
        v_ext = jnp.concatenate([v, jnp.ones_like(v)], axis=1)
        out = []
        for g in range(group):
            m, acc = carry[g]
            s = _dot_nt(qs[g], k)
            m_new = jnp.maximum(m, jnp.max(s, axis=-1, keepdims=True))
            p = jnp.exp2(s - m_new).astype(BF16)
            out.append((m_new, jnp.exp2(m - m_new) * acc + _dot(p, v_ext)))
        return tuple(out)

    def body(j, carry):
        r = pl.ds(pl.multiple_of(j * tk, tk), tk)
        return attend(carry, kl_ref[r, :], vl_ref[r, :])

    init = tuple((jnp.full((tq, 1), -jnp.inf, F32), jnp.zeros((tq, 2 * HEAD), F32)) for _ in range(group))
    carry = attend(init, kc_ref[...], vc_ref[...])
    final = lax.fori_loop(0, kl_ref.shape[0] // tk, body, carry, unroll=unroll)
    for g in range(group):
        acc = final[g][1]
        o_ref[:, g * HEAD:(g + 1) * HEAD] = (acc[:, :HEAD] / acc[:, HEAD:]).astype(o_ref.dtype)


def _flash_call(qkv, kv_ctx, *, heads):
    b, t, n = qkv.shape
    tc = kv_ctx.shape[1]
    kvh = (n // HEAD - heads) // 2
    group = heads // kvh
    tq = _pick(t, 512, BF16_SUBLANES)
    tk = _pick(t, 1024, 2 * HEAD)
    body = functools.partial(_flash_kernel, tk=tk, unroll=2 if (t // tk) % 2 == 0 else 1)
    return pl.pallas_call(
        body,
        grid=(b, kvh, t // tq),
        in_specs=[
            pl.BlockSpec((None, tq, group * HEAD), lambda bi, h, i: (bi, i, h)),
            pl.BlockSpec((None, tc, HEAD), lambda bi, h, i: (bi, 0, h)),
            pl.BlockSpec((None, tc, HEAD), lambda bi, h, i: (bi, 0, kvh + h)),
            pl.BlockSpec((None, t, HEAD), lambda bi, h, i: (bi, 0, heads + h)),
            pl.BlockSpec((None, t, HEAD), lambda bi, h, i: (bi, 0, heads + kvh + h)),
        ],
        out_specs=pl.BlockSpec((None, tq, group * HEAD), lambda bi, h, i: (bi, i, h)),
        out_shape=jax.ShapeDtypeStruct((b, t, heads * HEAD), BF16),
        compiler_params=_params(("arbitrary", "arbitrary", "arbitrary")),
        name="gqa_flash",
    )(qkv, kv_ctx, kv_ctx, qkv, qkv)


def _rope_tables(t):
    pos = jnp.arange(t, dtype=jnp.int32)
    rows = (pos // GRID_W).astype(F32)
    cols = (pos % GRID_W).astype(F32)
    axis_dim = HEAD // 2
    inv_freq = ROPE_THETA ** (-jnp.arange(0, axis_dim, 2, dtype=F32) / axis_dim)
    ang = jnp.concatenate([rows[:, None] * inv_freq, cols[:, None] * inv_freq], axis=-1)
    cos, sin = jnp.cos(ang), jnp.sin(ang)
    return jnp.concatenate([cos, cos], axis=-1), jnp.concatenate([-sin, sin], axis=-1)


def _split_mods(mods, b):
    d = mods.shape[1] // N_MOD
    lat = [mods[:b, i * d:(i + 1) * d].reshape(b, 1, d) for i in range(N_MOD)]
    ctx = [mods[b:b + 1, i * d:(i + 1) * d].reshape(1, 1, d) for i in range(N_MOD)]
    return lat, ctx


def kernel(x, c, ctx, c_ctx, ada_w, ada_b, norm_mix_pre, norm_mix_post, norm_ffn_pre, norm_ffn_post, hgrn_w_in, hgrn_lb_logits, hgrn_o_norm, hgrn_w_out, attn_w_qkv, attn_q_norm, attn_k_norm, attn_w_out, ffn_w_in, ffn_conv_w, ffn_conv_b, ffn_w_out):
    b, t, d = x.shape
    tc = ctx.shape[1]
    heads = d // HEAD
    assert ada_w.shape[0] == 2 and b + 1 <= ADA_ROWS

    cond = jnp.zeros((ADA_ROWS, d), F32).at[:b].set(c).at[b].set(c_ctx)
    mods = _ada_call(cond, ada_w, ada_b)
    x_lat = x.reshape(b * t, d)
    x_ctx = ctx.reshape(b * tc, d)

    (sh_ml, sc_ml, gt_ml, sh_fl, sc_fl, gt_fl), (sh_mc, sc_mc, gt_mc, sh_fc, sc_fc, gt_fc) = _split_mods(mods[0], b)
    lower_bound = jnp.cumsum(jax.nn.softmax(hgrn_lb_logits.astype(F32), axis=1), axis=1)[:, 0]
    w_in = hgrn_w_in[0].astype(BF16)
    p_lat = _nmm_call(x_lat, norm_mix_pre[0], sh_ml, sc_ml, w_in, name="hgrn_in_lat")
    p_ctx = _nmm_call(x_ctx, norm_mix_pre[0], sh_mc, sc_mc, w_in, name="hgrn_in_ctx")
    y_ctx, y_lat = _gla_call(p_ctx.reshape(b, tc, 5 * d), p_lat.reshape(b, t, 5 * d), lower_bound, hgrn_o_norm[0])
    w_out = hgrn_w_out[0].astype(BF16)
    x_lat = _proj_res_call(y_lat.reshape(b * t, d), w_out, x_lat, norm_mix_post[0], gt_ml, name="hgrn_out_lat")
    x_ctx = _proj_res_call(y_ctx.reshape(b * tc, d), w_out, x_ctx, norm_mix_post[0], gt_mc, name="hgrn_out_ctx")
    w_in = ffn_w_in[0].astype(BF16)
    w_out = ffn_w_out[0].astype(BF16)
    gu = _nmm_call(x_lat, norm_ffn_pre[0], sh_fl, sc_fl, w_in, name="ffn0_in_lat")
    x_lat = _ffn_out_call(gu, ffn_conv_w[0], ffn_conv_b[0], w_out, x_lat, norm_ffn_post[0], gt_fl,
                          seq_len=t, name="ffn0_out_lat")
    gu = _nmm_call(x_ctx, norm_ffn_pre[0], sh_fc, sc_fc, w_in, name="ffn0_in_ctx")
    x_ctx = _ffn_out_call(gu, ffn_conv_w[0], ffn_conv_b[0], w_out, x_ctx, norm_ffn_post[0], gt_fc,
                          seq_len=tc, name="ffn0_out_ctx")

    (sh_ml, sc_ml, gt_ml, sh_fl, sc_fl, gt_fl), (sh_mc, sc_mc, _, _, _, _) = _split_mods(mods[1], b)
    n_qkv = attn_w_qkv.shape[2]
    kvh = (n_qkv // HEAD - heads) // 2
    perm = jnp.concatenate([jnp.arange(0, HEAD, 2), jnp.arange(1, HEAD, 2)])
    head_perm = (jnp.arange(heads + kvh)[:, None] * HEAD + perm[None, :]).reshape(-1)
    col_perm = jnp.concatenate([head_perm, jnp.arange((heads + kvh) * HEAD, n_qkv)])
    w_qkv = attn_w_qkv[0][:, col_perm].astype(BF16)
    head_gain = jnp.concatenate([
        jnp.tile(attn_q_norm[0][perm] * (LOG2_E * HEAD ** -0.5), heads),
        jnp.tile(attn_k_norm[0][perm], kvh),
        jnp.ones((kvh * HEAD,), F32),
    ]).reshape(1, n_qkv)
    cos, sin = _rope_tables(t)
    qk_cols = (heads + kvh) * HEAD
    qkv = _nmm_call(x_lat, norm_mix_pre[1], sh_ml, sc_ml, w_qkv,
                    rope=(head_gain, cos, sin, qk_cols), name="attn_qkv_lat")
    q_cols = heads * HEAD
    kv_ctx = _nmm_call(x_ctx, norm_mix_pre[1], sh_mc, sc_mc, w_qkv[:, q_cols:],
                       rope=(head_gain[:, q_cols:], jnp.ones((tc, HEAD), F32), jnp.zeros((tc, HEAD), F32),
                             kvh * HEAD), name="attn_kv_ctx")
    qkv = qkv.reshape(b, t, n_qkv)
    kv_ctx = kv_ctx.reshape(b, tc, 2 * kvh * HEAD)
    o = _flash_call(qkv, kv_ctx, heads=heads)
    x_lat = _proj_res_call(o.reshape(b * t, d), attn_w_out[0].astype(BF16), x_lat, norm_mix_post[1], gt_ml,
                           name="attn_out_lat")
    gu = _nmm_call(x_lat, norm_ffn_pre[1], sh_fl, sc_fl, ffn_w_in[1].astype(BF16), name="ffn1_in_lat")
    x_lat = _ffn_out_call(gu, ffn_conv_w[1], ffn_conv_b[1], ffn_w_out[1].astype(BF16), x_lat, norm_ffn_post[1],
                          gt_fl, seq_len=t, name="ffn1_out_lat")
    return x_lat.reshape(b, t, d)
```

```python
import functools
import math

import jax
import jax.numpy as jnp
from jax import lax
from jax.experimental import pallas as pl
from jax.experimental.pallas import tpu as pltpu

F32 = jnp.float32
BF16 = jnp.bfloat16

EPS = 1e-6
GRID_W = 64
ROPE_THETA = 10000.0
LOG2_E = math.log2(math.e)
N_MOD = 6
HEAD = 128
GLA_CHUNK = 128
BF16_SUBLANES = 16
ADA_ROWS = 8
V7X_VMEM_BYTES = 64 * 1024 * 1024
VMEM_LIMIT = V7X_VMEM_BYTES - 6 * 1024 * 1024


def _pick(n, target, align):
    if n <= target:
        return n
    for t in range(target - target % align, 0, -align):
        if n % t == 0:
            return t
    return n


def _params(sem):
    return pltpu.CompilerParams(dimension_semantics=sem, vmem_limit_bytes=VMEM_LIMIT)


def _sigmoid(x):
    return 0.5 * jnp.tanh(0.5 * x) + 0.5


def _silu(x):
    return x * _sigmoid(x)


def _dot(a, b):
    return jnp.dot(a, b, preferred_element_type=F32)


def _dot_nt(a, b):
    return lax.dot_general(a, b, (((1,), (1,)), ((), ())), preferred_element_type=F32)


def _dot_tn(a, b):
    return lax.dot_general(a, b, (((0,), (0,)), ((), ())), preferred_element_type=F32)


def _ada_kernel(c_ref, w_ref, b_ref, o_ref):
    sc = _silu(c_ref[...]).astype(BF16)
    o_ref[...] = _dot(sc, w_ref[...].astype(BF16)) + b_ref[...]


def _ada_call(cond, ada_w, ada_b):
    depth, d, n = ada_w.shape
    tn = _pick(n, 1024, HEAD)
    return pl.pallas_call(
        _ada_kernel,
        grid=(depth, n // tn),
        in_specs=[
            pl.BlockSpec((ADA_ROWS, d), lambda l, j: (0, 0)),
            pl.BlockSpec((None, d, tn), lambda l, j: (l, 0, j)),
            pl.BlockSpec((None, 1, tn), lambda l, j: (l, 0, j)),
        ],
        out_specs=pl.BlockSpec((None, ADA_ROWS, tn), lambda l, j: (l, 0, j)),
        out_shape=jax.ShapeDtypeStruct((depth, ADA_ROWS, n), F32),
        compiler_params=_params(("arbitrary", "arbitrary")),
        name="ada_ln",
    )(cond, ada_w, ada_b.reshape(depth, 1, n))


def _norm_mod(x, gain, shift, scale):
    ms = jnp.mean(x * x, axis=-1, keepdims=True)
    return (x * lax.rsqrt(ms + EPS) * gain) * (1.0 + scale) + shift


def _nmm_kernel(x_ref, gain_ref, shift_ref, scale_ref, w_ref, o_ref, h_ref):
    @pl.when(pl.program_id(1) == 0)
    def _():
        h_ref[...] = _norm_mod(x_ref[...], gain_ref[...], shift_ref[...], scale_ref[...]).astype(BF16)

    o_ref[...] = _dot(h_ref[...], w_ref[...]).astype(o_ref.dtype)


def _nmm_rope_kernel(x_ref, gain_ref, shift_ref, scale_ref, w_ref, hg_ref, cos_ref, sin_ref,
                     o_ref, h_ref, *, n_normed_blocks):
    j = pl.program_id(1)

    @pl.when(j == 0)
    def _():
        h_ref[...] = _norm_mod(x_ref[...], gain_ref[...], shift_ref[...], scale_ref[...]).astype(BF16)

    acc = _dot(h_ref[...], w_ref[...])

    @pl.when(j < n_normed_blocks)
    def _():
        cos = cos_ref[...]
        sin = sin_ref[...]
        for hh in range(acc.shape[1] // HEAD):
            cols = slice(hh * HEAD, (hh + 1) * HEAD)
            a = acc[:, cols]
            ms = jnp.mean(a * a, axis=-1, keepdims=True)
            a = a * lax.rsqrt(ms + EPS) * hg_ref[:, cols]
            a = a * cos + pltpu.roll(a, HEAD // 2, axis=1) * sin
            o_ref[:, cols] = a.astype(o_ref.dtype)

    @pl.when(j >= n_normed_blocks)
    def _():
        o_ref[...] = acc.astype(o_ref.dtype)


def _nmm_call(x, gain, shift, scale, w, *, rope=None, name):
    m, d = x.shape
    n = w.shape[1]
    bm = shift.shape[0]
    rows_per_mod = m // bm
    row_period = rows_per_mod if rope is None else math.gcd(rows_per_mod, rope[1].shape[0])
    tm = _pick(row_period, 1024, BF16_SUBLANES)
    tn = _pick(n, 1024, HEAD) if rope is None else _pick(math.gcd(n, rope[3]), 512, HEAD)
    mod_blocks = rows_per_mod // tm
    in_specs = [
        pl.BlockSpec((tm, d), lambda i, j: (i, 0)),
        pl.BlockSpec((1, d), lambda i, j: (0, 0)),
        pl.BlockSpec((None, 1, d), lambda i, j: (i // mod_blocks, 0, 0)),
        pl.BlockSpec((None, 1, d), lambda i, j: (i // mod_blocks, 0, 0)),
        pl.BlockSpec((d, tn), lambda i, j: (0, j)),
    ]
    args = [x, gain.reshape(1, d), shift, scale, w]
    if rope is None:
        body = _nmm_kernel
    else:
        head_gain, cos, sin, n_normed_cols = rope
        assert n_normed_cols % tn == 0 and cos.shape[0] % tm == 0
        table_blocks = cos.shape[0] // tm
        in_specs += [
            pl.BlockSpec((1, tn), lambda i, j: (0, j)),
            pl.BlockSpec((tm, HEAD), lambda i, j: (i % table_blocks, 0)),
            pl.BlockSpec((tm, HEAD), lambda i, j: (i % table_blocks, 0)),
        ]
        args += [head_gain, cos, sin]
        body = functools.partial(_nmm_rope_kernel, n_normed_blocks=n_normed_cols // tn)
    return pl.pallas_call(
        body,
        grid=(m // tm, n // tn),
        in_specs=in_specs,
        out_specs=pl.BlockSpec((tm, tn), lambda i, j: (i, j)),
        out_shape=jax.ShapeDtypeStruct((m, n), BF16),
        scratch_shapes=[pltpu.VMEM((tm, d), BF16)],
        compiler_params=_params(("arbitrary", "arbitrary")),
        name=name,
    )(*args)


def _residual(x, y, gain, gate):
    ms = jnp.mean(y * y, axis=-1, keepdims=True)
    return x + gate * (y * lax.rsqrt(ms + EPS) * gain)


def _proj_res_kernel(a_ref, w_ref, x_ref, gain_ref, gate_ref, o_ref):
    y = _dot(a_ref[...], w_ref[...])
    o_ref[...] = _residual(x_ref[...], y, gain_ref[...], gate_ref[...])


def _proj_res_ktiled_kernel(a_ref, w_ref, x_ref, gain_ref, gate_ref, o_ref):
    k = pl.program_id(1)

    @pl.when(k == 0)
    def _():
        o_ref[...] = jnp.zeros_like(o_ref)

    o_ref[...] += _dot(a_ref[...], w_ref[...])

    @pl.when(k == pl.num_programs(1) - 1)
    def _():
        o_ref[...] = _residual(x_ref[...], o_ref[...], gain_ref[...], gate_ref[...])


def _proj_res_call(a, w, x, gain, gate, *, name):
    m, k = a.shape
    d = w.shape[1]
    bm = gate.shape[0]
    rows_per_mod = m // bm
    tk = k if k <= d else _pick(k, 512, 2 * HEAD)
    kb = k // tk
    tm = _pick(rows_per_mod, 512 if kb == 1 else 1024, BF16_SUBLANES)
    mod_blocks = rows_per_mod // tm
    return pl.pallas_call(
        _proj_res_kernel if kb == 1 else _proj_res_ktiled_kernel,
        grid=(m // tm, kb),
        in_specs=[
            pl.BlockSpec((tm, tk), lambda i, kk: (i, kk)),
            pl.BlockSpec((tk, d), lambda i, kk: (kk, 0)),
            pl.BlockSpec((tm, d), lambda i, kk: (i, 0)),
            pl.BlockSpec((1, d), lambda i, kk: (0, 0)),
            pl.BlockSpec((None, 1, d), lambda i, kk: (i // mod_blocks, 0, 0)),
        ],
        out_specs=pl.BlockSpec((tm, d), lambda i, kk: (i, 0)),
        out_shape=jax.ShapeDtypeStruct((m, d), F32),
        compiler_params=_params(("arbitrary", "arbitrary")),
        name=name,
    )(a, w, x, gain.reshape(1, d), gate)


def _ffn_in_kernel(x_ref, xprev_ref, xnext_ref, gain_ref, shift_ref, scale_ref, wg_ref, wu_ref, cw_ref, cb_ref,
                   o_ref, h_ref, *, blocks_per_seq):
    i = pl.program_id(0)
    tm = x_ref.shape[0]
    halo = BF16_SUBLANES

    @pl.when(pl.program_id(1) == 0)
    def _():
        gain, shift, scale = gain_ref[...], shift_ref[...], scale_ref[...]
        seq_block = i % blocks_per_seq
        h_prev = jnp.where(seq_block > 0, _norm_mod(xprev_ref[...], gain, shift, scale), 0.0)
        h_next = jnp.where(seq_block < blocks_per_seq - 1, _norm_mod(xnext_ref[...], gain, shift, scale), 0.0)
        zeros = jnp.zeros((halo - h_prev.shape[0], h_prev.shape[1]), F32)
        h_ref[0:halo, :] = jnp.concatenate([zeros, h_prev], axis=0).astype(BF16)
        h_ref[halo:halo + tm, :] = _norm_mod(x_ref[...], gain, shift, scale).astype(BF16)
        h_ref[halo + tm:, :] = jnp.concatenate([h_next, zeros], axis=0).astype(BF16)

    gate = _dot(h_ref[...], wg_ref[...])
    up = _dot(h_ref[halo:halo + tm, :], wu_ref[...])
    rows = gate.shape[0]
    g_prev = pltpu.roll(gate, 1, axis=0)[halo:halo + tm, :]
    g_next = pltpu.roll(gate, rows - 1, axis=0)[halo:halo + tm, :]
    conv = g_prev * cw_ref[0:1, :] + gate[halo:halo + tm, :] * cw_ref[1:2, :] + g_next * cw_ref[2:3, :] + cb_ref[...]
    o_ref[...] = (_silu(conv) * up).astype(o_ref.dtype)


def _ffn_in_call(x, gain, shift, scale, w, conv_w, conv_b, *, seq_len, name):
    m, d = x.shape
    f = w.shape[1] // 2
    bm = shift.shape[0]
    rows_per_mod = m // bm
    tm = _pick(seq_len, 1024, BF16_SUBLANES)
    tn = _pick(f, 512, HEAD)
    assert rows_per_mod % tm == 0
    mod_blocks = rows_per_mod // tm
    nb = f // tn
    xh = 8
    tiles_per_tm = tm // xh
    last_tile = m // xh - 1
    body = functools.partial(_ffn_in_kernel, blocks_per_seq=seq_len // tm)
    return pl.pallas_call(
        body,
        grid=(m // tm, nb),
        in_specs=[
            pl.BlockSpec((tm, d), lambda i, j: (i, 0)),
            pl.BlockSpec((xh, d), lambda i, j: (jnp.maximum(i * tiles_per_tm - 1, 0), 0)),
            pl.BlockSpec((xh, d), lambda i, j: (jnp.minimum((i + 1) * tiles_per_tm, last_tile), 0)),
            pl.BlockSpec((1, d), lambda i, j: (0, 0)),
            pl.BlockSpec((None, 1, d), lambda i, j: (i // mod_blocks, 0, 0)),
            pl.BlockSpec((None, 1, d), lambda i, j: (i // mod_blocks, 0, 0)),
            pl.BlockSpec((d, tn), lambda i, j: (0, j)),
            pl.BlockSpec((d, tn), lambda i, j: (0, nb + j)),
            pl.BlockSpec((3, tn), lambda i, j: (0, j)),
            pl.BlockSpec((1, tn), lambda i, j: (0, j)),
        ],
        out_specs=pl.BlockSpec((tm, tn), lambda i, j: (i, j)),
        out_shape=jax.ShapeDtypeStruct((m, f), BF16),
        scratch_shapes=[pltpu.VMEM((tm + 2 * BF16_SUBLANES, d), BF16)],
        compiler_params=_params(("arbitrary", "arbitrary")),
        name=name,
    )(x, x, x, gain.reshape(1, d), shift, scale, w, w, conv_w, conv_b.reshape(1, f))


def _gla_local(q_raw, ff_raw, fb_raw, v_raw, lb_f, lb_b):
    c = q_raw.shape[0]
    qs = _silu(q_raw.astype(F32)) * (HEAD ** -0.5)
    row = lax.broadcasted_iota(jnp.int32, (c, c), 0)
    col = lax.broadcasted_iota(jnp.int32, (c, c), 1)
    o = None
    carried = []
    for forward, f_raw, lb in ((True, ff_raw, lb_f), (False, fb_raw, lb_b)):
        f = lb + (1.0 - lb) * _sigmoid(f_raw.astype(F32))
        kk = 1.0 - f
        lf = jnp.log(f)
        within = (col <= row) if forward else (col >= row)
        ones = jnp.where(within, 1.0, 0.0).astype(BF16)
        lf_hi = lf.astype(BF16)
        lf_lo = (lf - lf_hi.astype(F32)).astype(BF16)
        b = _dot(ones, lf_hi) + _dot(ones, lf_lo)
        a_row = c // 2 - 1 if forward else c // 2
        e_row = c - 1 if forward else 0
        anchor = b[a_row:a_row + 1, :]
        b_end = b[e_row:e_row + 1, :]
        qa = qs * jnp.exp(b - anchor)
        ka = kk * jnp.exp(anchor - b)
        scores = jnp.where(within, _dot_nt(qa.astype(BF16), ka.astype(BF16)), 0.0).astype(BF16)
        oi = _dot(scores, v_raw)
        o = oi if o is None else o + oi
        carried.append(((qa * jnp.exp(anchor)).astype(BF16), (ka * jnp.exp(b_end - anchor)).astype(BF16),
                        jnp.exp(b_end)))
    return o, carried


def _gla_readout(o, g_raw, o_gain):
    ms = jnp.mean(o * o, axis=-1, keepdims=True)
    y = o * lax.rsqrt(ms + EPS) * o_gain
    return (y * _silu(g_raw.astype(F32))).astype(BF16)


def _gla_kernel(qc, ffc, fbc, vc, gc, ql, ffl, fbl, vl, gl, lb_ref, og_ref, yc_ref, yl_ref,
                o_ref, qb_ref, ke_ref, dec_ref, st_ref):
    c = GLA_CHUNK
    tc = qc.shape[0]
    n_ctx = tc // c
    n_lat = ql.shape[0] // c
    lb_f = lb_ref[0]
    lb_b = lb_ref[1]

    def local(refs, src_rows, dst_rows, chunk):
        q, ff, fb, v = refs
        o, carried = _gla_local(q[src_rows, :], ff[src_rows, :], fb[src_rows, :], v[src_rows, :], lb_f, lb_b)
        o_ref[dst_rows, :] = o
        for d, (qb, ke, dec) in enumerate(carried):
            qb_ref[d, dst_rows, :] = qb
            ke_ref[d, dst_rows, :] = ke
            dec_ref[d, chunk] = dec

    for i in range(n_ctx):
        r = slice(i * c, (i + 1) * c)
        local((qc, ffc, fbc, vc), r, r, i)

    def local_body(i, carry):
        src = pl.ds(pl.multiple_of(i * c, c), c)
        dst = pl.ds(pl.multiple_of(tc + i * c, c), c)
        local((ql, ffl, fbl, vl), src, dst, n_ctx + i)
        return carry

    lax.fori_loop(0, n_lat, local_body, 0, unroll=math.gcd(n_lat, 8))

    st_ref[...] = jnp.zeros_like(st_ref)

    def step(d, v, src_rows, dst_rows, chunk):
        st = st_ref[d]
        o_ref[dst_rows, :] += _dot_nt(qb_ref[d, dst_rows, :], st.astype(BF16))
        st_ref[d] = st * dec_ref[d, chunk] + _dot_tn(v[src_rows, :], ke_ref[d, dst_rows, :])

    for i in range(n_ctx):
        for d, j in ((0, i), (1, n_ctx - 1 - i)):
            r = slice(j * c, (j + 1) * c)
            step(d, vc, r, r, j)

    def step_body(i, carry):
        for d, j in ((0, i), (1, n_lat - 1 - i)):
            src = pl.ds(pl.multiple_of(j * c, c), c)
            dst = pl.ds(pl.multiple_of(tc + j * c, c), c)
            step(d, vl, src, dst, n_ctx + j)
        return carry

    lax.fori_loop(0, n_lat, step_body, 0, unroll=math.gcd(n_lat, 8))

    o_gain = og_ref[...]
    for i in range(n_ctx):
        r = slice(i * c, (i + 1) * c)
        yc_ref[r, :] = _gla_readout(o_ref[r, :], gc[r, :], o_gain)

    def out_body(i, carry):
        src = pl.ds(pl.multiple_of(i * c, c), c)
        dst = pl.ds(pl.multiple_of(tc + i * c, c), c)
        yl_ref[src, :] = _gla_readout(o_ref[dst, :], gl[src, :], o_gain)
        return carry

    lax.fori_loop(0, n_lat, out_body, 0, unroll=math.gcd(n_lat, 4))


def _gla_call(p_ctx, p_lat, lower_bound, o_gain):
    b, tc, d5 = p_ctx.shape
    t = p_lat.shape[1]
    d = d5 // 5
    heads = d // HEAD
    assert tc % GLA_CHUNK == 0 and t % GLA_CHUNK == 0

    def col(tt, part):
        return pl.BlockSpec((None, tt, HEAD), lambda bi, h: (bi, 0, part * heads + h))

    out_spec = lambda tt: pl.BlockSpec((None, tt, HEAD), lambda bi, h: (bi, 0, h))
    return pl.pallas_call(
        _gla_kernel,
        grid=(b, heads),
        in_specs=[col(tc, p) for p in range(5)] + [col(t, p) for p in range(5)] + [
            pl.BlockSpec((2, None, 1, HEAD), lambda bi, h: (0, h, 0, 0)),
            pl.BlockSpec((1, HEAD), lambda bi, h: (0, 0)),
        ],
        out_specs=[out_spec(tc), out_spec(t)],
        out_shape=[jax.ShapeDtypeStruct((b, tc, d), BF16), jax.ShapeDtypeStruct((b, t, d), BF16)],
        scratch_shapes=[
            pltpu.VMEM((tc + t, HEAD), F32),
            pltpu.VMEM((2, tc + t, HEAD), BF16),
            pltpu.VMEM((2, tc + t, HEAD), BF16),
            pltpu.VMEM((2, (tc + t) // GLA_CHUNK, 1, HEAD), F32),
            pltpu.VMEM((2, HEAD, HEAD), F32),
        ],
        compiler_params=_params(("arbitrary", "arbitrary")),
        name="hgrn_scan",
    )(*([p_ctx] * 5), *([p_lat] * 5), lower_bound.reshape(2, heads, 1, HEAD), o_gain.reshape(1, HEAD))


def _flash_kernel(q_ref, kc_ref, vc_ref, kl_ref, vl_ref, o_ref, *, tk, unroll):
    tq = q_ref.shape[0]
    group = q_ref.shape[1] // HEAD
    qs = [q_ref[:, g * HEAD:(g + 1) * HEAD] for g in range(group)]

    def attend(carry, k, v):
        v_ext = jnp.concatenate([v, jnp.ones_like(v)], axis=1)
        out = []
        for g in range(group):
            m, acc = carry[g]
            s = _dot_nt(qs[g], k)
            m_new = jnp.maximum(m, jnp.max(s, axis=-1, keepdims=True))
            p = jnp.exp2(s - m_new).astype(BF16)
            out.append((m_new, jnp.exp2(m - m_new) * acc + _dot(p, v_ext)))
        return tuple(out)

    def body(j, carry):
        r = pl.ds(pl.multiple_of(j * tk, tk), tk)
        return attend(carry, kl_ref[r, :], vl_ref[r, :])

    init = tuple((jnp.full((tq, 1), -jnp.inf, F32), jnp.zeros((tq, 2 * HEAD), F32)) for _ in range(group))
    carry = attend(init, kc_ref[...], vc_ref[...])
    final = lax.fori_loop(0, kl_ref.shape[0] // tk, body, carry, unroll=unroll)
    for g in range(group):
        acc = final[g][1]
        o_ref[:, g * HEAD:(g + 1) * HEAD] = (acc[:, :HEAD] / acc[:, HEAD:]).astype(o_ref.dtype)


def _flash_call(qkv, kv_ctx, *, heads):
    b, t, n = qkv.shape
    tc = kv_ctx.shape[1]
    kvh = (n // HEAD - heads) // 2
    group = heads // kvh
    tq = _pick(t, 512, BF16_SUBLANES)
    tk = _pick(t, 1024, 2 * HEAD)
    body = functools.partial(_flash_kernel, tk=tk, unroll=2 if (t // tk) % 2 == 0 else 1)
    return pl.pallas_call(
        body,
        grid=(b, kvh, t // tq),
        in_specs=[
            pl.BlockSpec((None, tq, group * HEAD), lambda bi, h, i: (bi, i, h)),
            pl.BlockSpec((None, tc, HEAD), lambda bi, h, i: (bi, 0, h)),
            pl.BlockSpec((None, tc, HEAD), lambda bi, h, i: (bi, 0, kvh + h)),
            pl.BlockSpec((None, t, HEAD), lambda bi, h, i: (bi, 0, heads + h)),
            pl.BlockSpec((None, t, HEAD), lambda bi, h, i: (bi, 0, heads + kvh + h)),
        ],
        out_specs=pl.BlockSpec((None, tq, group * HEAD), lambda bi, h, i: (bi, i, h)),
        out_shape=jax.ShapeDtypeStruct((b, t, heads * HEAD), BF16),
        compiler_params=_params(("arbitrary", "arbitrary", "arbitrary")),
        name="gqa_flash",
    )(qkv, kv_ctx, kv_ctx, qkv, qkv)


def _rope_tables(t):
    pos = jnp.arange(t, dtype=jnp.int32)
    rows = (pos // GRID_W).astype(F32)
    cols = (pos % GRID_W).astype(F32)
    axis_dim = HEAD // 2
    inv_freq = ROPE_THETA ** (-jnp.arange(0, axis_dim, 2, dtype=F32) / axis_dim)
    ang = jnp.concatenate([rows[:, None] * inv_freq, cols[:, None] * inv_freq], axis=-1)
    cos, sin = jnp.cos(ang), jnp.sin(ang)
    return jnp.concatenate([cos, cos], axis=-1), jnp.concatenate([-sin, sin], axis=-1)


def _split_mods(mods, b):
    d = mods.shape[1] // N_MOD
    lat = [mods[:b, i * d:(i + 1) * d].reshape(b, 1, d) for i in range(N_MOD)]
    ctx = [mods[b:b + 1, i * d:(i + 1) * d].reshape(1, 1, d) for i in range(N_MOD)]
    return lat, ctx


def kernel(x, c, ctx, c_ctx, ada_w, ada_b, norm_mix_pre, norm_mix_post, norm_ffn_pre, norm_ffn_post, hgrn_w_in, hgrn_lb_logits, hgrn_o_norm, hgrn_w_out, attn_w_qkv, attn_q_norm, attn_k_norm, attn_w_out, ffn_w_in, ffn_conv_w, ffn_conv_b, ffn_w_out):
    b, t, d = x.shape
    tc = ctx.shape[1]
    heads = d // HEAD
    assert ada_w.shape[0] == 2 and b + 1 <= ADA_ROWS

    cond = jnp.zeros((ADA_ROWS, d), F32).at[:b].set(c).at[b].set(c_ctx)
    mods = _ada_call(cond, ada_w, ada_b)
    x_lat = x.reshape(b * t, d)
    x_ctx = ctx.reshape(b * tc, d)

    (sh_ml, sc_ml, gt_ml, sh_fl, sc_fl, gt_fl), (sh_mc, sc_mc, gt_mc, sh_fc, sc_fc, gt_fc) = _split_mods(mods[0], b)
    lower_bound = jnp.cumsum(jax.nn.softmax(hgrn_lb_logits.astype(F32), axis=1), axis=1)[:, 0]
    w_in = hgrn_w_in[0].astype(BF16)
    p_lat = _nmm_call(x_lat, norm_mix_pre[0], sh_ml, sc_ml, w_in, name="hgrn_in_lat")
    p_ctx = _nmm_call(x_ctx, norm_mix_pre[0], sh_mc, sc_mc, w_in, name="hgrn_in_ctx")
    y_ctx, y_lat = _gla_call(p_ctx.reshape(b, tc, 5 * d), p_lat.reshape(b, t, 5 * d), lower_bound, hgrn_o_norm[0])
    w_out = hgrn_w_out[0].astype(BF16)
    x_lat = _proj_res_call(y_lat.reshape(b * t, d), w_out, x_lat, norm_mix_post[0], gt_ml, name="hgrn_out_lat")
    x_ctx = _proj_res_call(y_ctx.reshape(b * tc, d), w_out, x_ctx, norm_mix_post[0], gt_mc, name="hgrn_out_ctx")
    w_in = ffn_w_in[0].astype(BF16)
    w_out = ffn_w_out[0].astype(BF16)
    a = _ffn_in_call(x_lat, norm_ffn_pre[0], sh_fl, sc_fl, w_in, ffn_conv_w[0], ffn_conv_b[0],
                     seq_len=t, name="ffn0_in_lat")
    x_lat = _proj_res_call(a, w_out, x_lat, norm_ffn_post[0], gt_fl, name="ffn0_out_lat")
    a = _ffn_in_call(x_ctx, norm_ffn_pre[0], sh_fc, sc_fc, w_in, ffn_conv_w[0], ffn_conv_b[0],
                     seq_len=tc, name="ffn0_in_ctx")
    x_ctx = _proj_res_call(a, w_out, x_ctx, norm_ffn_post[0], gt_fc, name="ffn0_out_ctx")

    (sh_ml, sc_ml, gt_ml, sh_fl, sc_fl, gt_fl), (sh_mc, sc_mc, _, _, _, _) = _split_mods(mods[1], b)
    n_qkv = attn_w_qkv.shape[2]
    kvh = (n_qkv // HEAD - heads) // 2
    perm = jnp.concatenate([jnp.arange(0, HEAD, 2), jnp.arange(1, HEAD, 2)])
    head_perm = (jnp.arange(heads + kvh)[:, None] * HEAD + perm[None, :]).reshape(-1)
    col_perm = jnp.concatenate([head_perm, jnp.arange((heads + kvh) * HEAD, n_qkv)])
    w_qkv = attn_w_qkv[0][:, col_perm].astype(BF16)
    head_gain = jnp.concatenate([
        jnp.tile(attn_q_norm[0][perm] * (LOG2_E * HEAD ** -0.5), heads),
        jnp.tile(attn_k_norm[0][perm], kvh),
        jnp.ones((kvh * HEAD,), F32),
    ]).reshape(1, n_qkv)
    cos, sin = _rope_tables(t)
    qk_cols = (heads + kvh) * HEAD
    qkv = _nmm_call(x_lat, norm_mix_pre[1], sh_ml, sc_ml, w_qkv,
                    rope=(head_gain, cos, sin, qk_cols), name="attn_qkv_lat")
    q_cols = heads * HEAD
    kv_ctx = _nmm_call(x_ctx, norm_mix_pre[1], sh_mc, sc_mc, w_qkv[:, q_cols:],
                       rope=(head_gain[:, q_cols:], jnp.ones((tc, HEAD), F32), jnp.zeros((tc, HEAD), F32),
                             kvh * HEAD), name="attn_kv_ctx")
    qkv = qkv.reshape(b, t, n_qkv)
    kv_ctx = kv_ctx.reshape(b, tc, 2 * kvh * HEAD)
    o = _flash_call(qkv, kv_ctx, heads=heads)
    x_lat = _proj_res_call(o.reshape(b * t, d), attn_w_out[0].astype(BF16), x_lat, norm_mix_post[1], gt_ml,
                           name="attn_out_lat")
    a = _ffn_in_call(x_lat, norm_ffn_pre[1], sh_fl, sc_fl, ffn_w_in[1].astype(BF16), ffn_conv_w[1], ffn_conv_b[1],
                     seq_len=t, name="ffn1_in_lat")
    x_lat = _proj_res_call(a, ffn_w_out[1].astype(BF16), x_lat, norm_ffn_post[1], gt_fl, name="ffn1_out_lat")
    return x_lat.reshape(b, t, d)
```

```python
import functools
import math

import jax
import jax.numpy as jnp
from jax import lax
from jax.experimental import pallas as pl
from jax.experimental.pallas import tpu as pltpu

F32 = jnp.float32
BF16 = jnp.bfloat16

EPS = 1e-6
GRID_W = 64
ROPE_THETA = 10000.0
LOG2_E = math.log2(math.e)
N_MOD = 6
HEAD = 128
GLA_CHUNK = 128
BF16_SUBLANES = 16
ADA_ROWS = 8
V7X_VMEM_BYTES = 64 * 1024 * 1024
VMEM_LIMIT = V7X_VMEM_BYTES - 6 * 1024 * 1024


def _pick(n, target, align):
    if n <= target:
        return n
    for t in range(target - target % align, 0, -align):
        if n % t == 0:
            return t
    return n


def _params(sem):
    return pltpu.CompilerParams(dimension_semantics=sem, vmem_limit_bytes=VMEM_LIMIT)


def _sigmoid(x):
    return 0.5 * jnp.tanh(0.5 * x) + 0.5


def _silu(x):
    return x * _sigmoid(x)


def _dot(a, b):
    return jnp.dot(a, b, preferred_element_type=F32)


def _dot_nt(a, b):
    return lax.dot_general(a, b, (((1,), (1,)), ((), ())), preferred_element_type=F32)


def _dot_tn(a, b):
    return lax.dot_general(a, b, (((0,), (0,)), ((), ())), preferred_element_type=F32)


def _ada_kernel(c_ref, w_ref, b_ref, o_ref):
    sc = _silu(c_ref[...]).astype(BF16)
    o_ref[...] = _dot(sc, w_ref[...].astype(BF16)) + b_ref[...]


def _ada_call(cond, ada_w, ada_b):
    depth, d, n = ada_w.shape
    tn = _pick(n, 1024, HEAD)
    return pl.pallas_call(
        _ada_kernel,
        grid=(depth, n // tn),
        in_specs=[
            pl.BlockSpec((ADA_ROWS, d), lambda l, j: (0, 0)),
            pl.BlockSpec((None, d, tn), lambda l, j: (l, 0, j)),
            pl.BlockSpec((None, 1, tn), lambda l, j: (l, 0, j)),
        ],
        out_specs=pl.BlockSpec((None, ADA_ROWS, tn), lambda l, j: (l, 0, j)),
        out_shape=jax.ShapeDtypeStruct((depth, ADA_ROWS, n), F32),
        compiler_params=_params(("arbitrary", "arbitrary")),
        name="ada_ln",
    )(cond, ada_w, ada_b.reshape(depth, 1, n))


def _norm_mod(x, gain, shift, scale):
    ms = jnp.mean(x * x, axis=-1, keepdims=True)
    return (x * lax.rsqrt(ms + EPS) * gain) * (1.0 + scale) + shift


def _nmm_kernel(x_ref, gain_ref, shift_ref, scale_ref, w_ref, o_ref, h_ref):
    @pl.when(pl.program_id(1) == 0)
    def _():
        h_ref[...] = _norm_mod(x_ref[...], gain_ref[...], shift_ref[...], scale_ref[...]).astype(BF16)

    o_ref[...] = _dot(h_ref[...], w_ref[...]).astype(o_ref.dtype)


def _nmm_rope_kernel(x_ref, gain_ref, shift_ref, scale_ref, w_ref, hg_ref, cos_ref, sin_ref,
                     o_ref, h_ref, *, n_normed_blocks):
    j = pl.program_id(1)

    @pl.when(j == 0)
    def _():
        h_ref[...] = _norm_mod(x_ref[...], gain_ref[...], shift_ref[...], scale_ref[...]).astype(BF16)

    acc = _dot(h_ref[...], w_ref[...])

    @pl.when(j < n_normed_blocks)
    def _():
        cos = cos_ref[...]
        sin = sin_ref[...]
        for hh in range(acc.shape[1] // HEAD):
            cols = slice(hh * HEAD, (hh + 1) * HEAD)
            a = acc[:, cols]
            ms = jnp.mean(a * a, axis=-1, keepdims=True)
            a = a * lax.rsqrt(ms + EPS) * hg_ref[:, cols]
            a = a * cos + pltpu.roll(a, HEAD // 2, axis=1) * sin
            o_ref[:, cols] = a.astype(o_ref.dtype)

    @pl.when(j >= n_normed_blocks)
    def _():
        o_ref[...] = acc.astype(o_ref.dtype)


def _nmm_call(x, gain, shift, scale, w, *, rope=None, name):
    m, d = x.shape
    n = w.shape[1]
    bm = shift.shape[0]
    rows_per_mod = m // bm
    row_period = rows_per_mod if rope is None else math.gcd(rows_per_mod, rope[1].shape[0])
    tm = _pick(row_period, 1024, BF16_SUBLANES)
    tn = _pick(n, 1024, HEAD) if rope is None else _pick(math.gcd(n, rope[3]), 512, HEAD)
    mod_blocks = rows_per_mod // tm
    in_specs = [
        pl.BlockSpec((tm, d), lambda i, j: (i, 0)),
        pl.BlockSpec((1, d), lambda i, j: (0, 0)),
        pl.BlockSpec((None, 1, d), lambda i, j: (i // mod_blocks, 0, 0)),
        pl.BlockSpec((None, 1, d), lambda i, j: (i // mod_blocks, 0, 0)),
        pl.BlockSpec((d, tn), lambda i, j: (0, j)),
    ]
    args = [x, gain.reshape(1, d), shift, scale, w]
    if rope is None:
        body = _nmm_kernel
    else:
        head_gain, cos, sin, n_normed_cols = rope
        assert n_normed_cols % tn == 0 and cos.shape[0] % tm == 0
        table_blocks = cos.shape[0] // tm
        in_specs += [
            pl.BlockSpec((1, tn), lambda i, j: (0, j)),
            pl.BlockSpec((tm, HEAD), lambda i, j: (i % table_blocks, 0)),
            pl.BlockSpec((tm, HEAD), lambda i, j: (i % table_blocks, 0)),
        ]
        args += [head_gain, cos, sin]
        body = functools.partial(_nmm_rope_kernel, n_normed_blocks=n_normed_cols // tn)
    return pl.pallas_call(
        body,
        grid=(m // tm, n // tn),
        in_specs=in_specs,
        out_specs=pl.BlockSpec((tm, tn), lambda i, j: (i, j)),
        out_shape=jax.ShapeDtypeStruct((m, n), BF16),
        scratch_shapes=[pltpu.VMEM((tm, d), BF16)],
        compiler_params=_params(("arbitrary", "arbitrary")),
        name=name,
    )(*args)


def _residual(x, y, gain, gate):
    ms = jnp.mean(y * y, axis=-1, keepdims=True)
    return x + gate * (y * lax.rsqrt(ms + EPS) * gain)


def _proj_res_kernel(a_ref, w_ref, x_ref, gain_ref, gate_ref, o_ref):
    y = _dot(a_ref[...], w_ref[...])
    o_ref[...] = _residual(x_ref[...], y, gain_ref[...], gate_ref[...])


def _proj_res_ktiled_kernel(a_ref, w_ref, x_ref, gain_ref, gate_ref, o_ref):
    k = pl.program_id(1)

    @pl.when(k == 0)
    def _():
        o_ref[...] = jnp.zeros_like(o_ref)

    o_ref[...] += _dot(a_ref[...], w_ref[...])

    @pl.when(k == pl.num_programs(1) - 1)
    def _():
        o_ref[...] = _residual(x_ref[...], o_ref[...], gain_ref[...], gate_ref[...])


def _proj_res_call(a, w, x, gain, gate, *, name):
    m, k = a.shape
    d = w.shape[1]
    bm = gate.shape[0]
    rows_per_mod = m // bm
    tk = k if k <= d else _pick(k, 512, 2 * HEAD)
    kb = k // tk
    tm = _pick(rows_per_mod, 512 if kb == 1 else 1024, BF16_SUBLANES)
    mod_blocks = rows_per_mod // tm
    return pl.pallas_call(
        _proj_res_kernel if kb == 1 else _proj_res_ktiled_kernel,
        grid=(m // tm, kb),
        in_specs=[
            pl.BlockSpec((tm, tk), lambda i, kk: (i, kk)),
            pl.BlockSpec((tk, d), lambda i, kk: (kk, 0)),
            pl.BlockSpec((tm, d), lambda i, kk: (i, 0)),
            pl.BlockSpec((1, d), lambda i, kk: (0, 0)),
            pl.BlockSpec((None, 1, d), lambda i, kk: (i // mod_blocks, 0, 0)),
        ],
        out_specs=pl.BlockSpec((tm, d), lambda i, kk: (i, 0)),
        out_shape=jax.ShapeDtypeStruct((m, d), F32),
        compiler_params=_params(("arbitrary", "arbitrary")),
        name=name,
    )(a, w, x, gain.reshape(1, d), gate)


def _ffn_in_kernel(x_ref, xprev_ref, xnext_ref, gain_ref, shift_ref, scale_ref, wg_ref, wu_ref, cw_ref, cb_ref,
                   o_ref, h_ref, *, blocks_per_seq):
    i = pl.program_id(0)
    tm = x_ref.shape[0]
    halo = BF16_SUBLANES

    @pl.when(pl.program_id(1) == 0)
    def _():
        gain, shift, scale = gain_ref[...], shift_ref[...], scale_ref[...]
        seq_block = i % blocks_per_seq
        h_prev = jnp.where(seq_block > 0, _norm_mod(xprev_ref[...], gain, shift, scale), 0.0)
        h_next = jnp.where(seq_block < blocks_per_seq - 1, _norm_mod(xnext_ref[...], gain, shift, scale), 0.0)
        zeros = jnp.zeros((halo - h_prev.shape[0], h_prev.shape[1]), F32)
        h_ref[0:halo, :] = jnp.concatenate([zeros, h_prev], axis=0).astype(BF16)
        h_ref[halo:halo + tm, :] = _norm_mod(x_ref[...], gain, shift, scale).astype(BF16)
        h_ref[halo + tm:, :] = jnp.concatenate([h_next, zeros], axis=0).astype(BF16)

    gate = _dot(h_ref[...], wg_ref[...])
    up = _dot(h_ref[halo:halo + tm, :], wu_ref[...])
    rows = gate.shape[0]
    g_prev = pltpu.roll(gate, 1, axis=0)[halo:halo + tm, :]
    g_next = pltpu.roll(gate, rows - 1, axis=0)[halo:halo + tm, :]
    conv = g_prev * cw_ref[0:1, :] + gate[halo:halo + tm, :] * cw_ref[1:2, :] + g_next * cw_ref[2:3, :] + cb_ref[...]
    o_ref[...] = (_silu(conv) * up).astype(o_ref.dtype)


def _ffn_in_call(x, gain, shift, scale, w, conv_w, conv_b, *, seq_len, name):
    m, d = x.shape
    f = w.shape[1] // 2
    bm = shift.shape[0]
    rows_per_mod = m // bm
    tm = _pick(seq_len, 1024, BF16_SUBLANES)
    tn = _pick(f, 512, HEAD)
    assert rows_per_mod % tm == 0
    mod_blocks = rows_per_mod // tm
    nb = f // tn
    xh = 8
    tiles_per_tm = tm // xh
    last_tile = m // xh - 1
    body = functools.partial(_ffn_in_kernel, blocks_per_seq=seq_len // tm)
    return pl.pallas_call(
        body,
        grid=(m // tm, nb),
        in_specs=[
            pl.BlockSpec((tm, d), lambda i, j: (i, 0)),
            pl.BlockSpec((xh, d), lambda i, j: (jnp.maximum(i * tiles_per_tm - 1, 0), 0)),
            pl.BlockSpec((xh, d), lambda i, j: (jnp.minimum((i + 1) * tiles_per_tm, last_tile), 0)),
            pl.BlockSpec((1, d), lambda i, j: (0, 0)),
            pl.BlockSpec((None, 1, d), lambda i, j: (i // mod_blocks, 0, 0)),
            pl.BlockSpec((None, 1, d), lambda i, j: (i // mod_blocks, 0, 0)),
            pl.BlockSpec((d, tn), lambda i, j: (0, j)),
            pl.BlockSpec((d, tn), lambda i, j: (0, nb + j)),
            pl.BlockSpec((3, tn), lambda i, j: (0, j)),
            pl.BlockSpec((1, tn), lambda i, j: (0, j)),
        ],
        out_specs=pl.BlockSpec((tm, tn), lambda i, j: (i, j)),
        out_shape=jax.ShapeDtypeStruct((m, f), BF16),
        scratch_shapes=[pltpu.VMEM((tm + 2 * BF16_SUBLANES, d), BF16)],
        compiler_params=_params(("arbitrary", "arbitrary")),
        name=name,
    )(x, x, x, gain.reshape(1, d), shift, scale, w, w, conv_w, conv_b.reshape(1, f))


def _gla_local(q_raw, ff_raw, fb_raw, v_raw, lb_f, lb_b):
    c = q_raw.shape[0]
    qs = _silu(q_raw.astype(F32)) * (HEAD ** -0.5)
    row = lax.broadcasted_iota(jnp.int32, (c, c), 0)
    col = lax.broadcasted_iota(jnp.int32, (c, c), 1)
    o = None
    carried = []
    for forward, f_raw, lb in ((True, ff_raw, lb_f), (False, fb_raw, lb_b)):
        f = lb + (1.0 - lb) * _sigmoid(f_raw.astype(F32))
        kk = 1.0 - f
        lf = jnp.log(f)
        within = (col <= row) if forward else (col >= row)
        ones = jnp.where(within, 1.0, 0.0).astype(BF16)
        lf_hi = lf.astype(BF16)
        lf_lo = (lf - lf_hi.astype(F32)).astype(BF16)
        b = _dot(ones, lf_hi) + _dot(ones, lf_lo)
        a_row = c // 2 - 1 if forward else c // 2
        e_row = c - 1 if forward else 0
        anchor = b[a_row:a_row + 1, :]
        b_end = b[e_row:e_row + 1, :]
        qa = qs * jnp.exp(b - anchor)
        ka = kk * jnp.exp(anchor - b)
        scores = jnp.where(within, _dot_nt(qa.astype(BF16), ka.astype(BF16)), 0.0).astype(BF16)
        oi = _dot(scores, v_raw)
        o = oi if o is None else o + oi
        carried.append(((qa * jnp.exp(anchor)).astype(BF16), (ka * jnp.exp(b_end - anchor)).astype(BF16),
                        jnp.exp(b_end)))
    return o, carried


def _gla_readout(o, g_raw, o_gain):
    ms = jnp.mean(o * o, axis=-1, keepdims=True)
    y = o * lax.rsqrt(ms + EPS) * o_gain
    return (y * _silu(g_raw.astype(F32))).astype(BF16)


def _gla_kernel(qc, ffc, fbc, vc, gc, ql, ffl, fbl, vl, gl, lb_ref, og_ref, yc_ref, yl_ref,
                o_ref, qb_ref, ke_ref, dec_ref, st_ref):
    c = GLA_CHUNK
    tc = qc.shape[0]
    n_ctx = tc // c
    n_lat = ql.shape[0] // c
    lb_f = lb_ref[0]
    lb_b = lb_ref[1]

    def local(refs, src_rows, dst_rows, chunk):
        q, ff, fb, v = refs
        o, carried = _gla_local(q[src_rows, :], ff[src_rows, :], fb[src_rows, :], v[src_rows, :], lb_f, lb_b)
        o_ref[dst_rows, :] = o
        for d, (qb, ke, dec) in enumerate(carried):
            qb_ref[d, dst_rows, :] = qb
            ke_ref[d, dst_rows, :] = ke
            dec_ref[d, chunk] = dec

    for i in range(n_ctx):
        r = slice(i * c, (i + 1) * c)
        local((qc, ffc, fbc, vc), r, r, i)

    def local_body(i, carry):
        src = pl.ds(pl.multiple_of(i * c, c), c)
        dst = pl.ds(pl.multiple_of(tc + i * c, c), c)
        local((ql, ffl, fbl, vl), src, dst, n_ctx + i)
        return carry

    lax.fori_loop(0, n_lat, local_body, 0, unroll=math.gcd(n_lat, 8))

    st_ref[...] = jnp.zeros_like(st_ref)

    def step(d, v, src_rows, dst_rows, chunk):
        st = st_ref[d]
        o_ref[dst_rows, :] += _dot_nt(qb_ref[d, dst_rows, :], st.astype(BF16))
        st_ref[d] = st * dec_ref[d, chunk] + _dot_tn(v[src_rows, :], ke_ref[d, dst_rows, :])

    for i in range(n_ctx):
        for d, j in ((0, i), (1, n_ctx - 1 - i)):
            r = slice(j * c, (j + 1) * c)
            step(d, vc, r, r, j)

    def step_body(i, carry):
        for d, j in ((0, i), (1, n_lat - 1 - i)):
            src = pl.ds(pl.multiple_of(j * c, c), c)
            dst = pl.ds(pl.multiple_of(tc + j * c, c), c)
            step(d, vl, src, dst, n_ctx + j)
        return carry

    lax.fori_loop(0, n_lat, step_body, 0, unroll=math.gcd(n_lat, 8))

    o_gain = og_ref[...]
    for i in range(n_ctx):
        r = slice(i * c, (i + 1) * c)
        yc_ref[r, :] = _gla_readout(o_ref[r, :], gc[r, :], o_gain)

    def out_body(i, carry):
        src = pl.ds(pl.multiple_of(i * c, c), c)
        dst = pl.ds(pl.multiple_of(tc + i * c, c), c)
        yl_ref[src, :] = _gla_readout(o_ref[dst, :], gl[src, :], o_gain)
        return carry

    lax.fori_loop(0, n_lat, out_body, 0, unroll=math.gcd(n_lat, 4))


def _gla_call(p_ctx, p_lat, lower_bound, o_gain):
    b, tc, d5 = p_ctx.shape
    t = p_lat.shape[1]
    d = d5 // 5
    heads = d // HEAD
    assert tc % GLA_CHUNK == 0 and t % GLA_CHUNK == 0

    def col(tt, part):
        return pl.BlockSpec((None, tt, HEAD), lambda bi, h: (bi, 0, part * heads + h))

    out_spec = lambda tt: pl.BlockSpec((None, tt, HEAD), lambda bi, h: (bi, 0, h))
    return pl.pallas_call(
        _gla_kernel,
        grid=(b, heads),
        in_specs=[col(tc, p) for p in range(5)] + [col(t, p) for p in range(5)] + [
            pl.BlockSpec((2, None, 1, HEAD), lambda bi, h: (0, h, 0, 0)),
            pl.BlockSpec((1, HEAD), lambda bi, h: (0, 0)),
        ],
        out_specs=[out_spec(tc), out_spec(t)],
        out_shape=[jax.ShapeDtypeStruct((b, tc, d), BF16), jax.ShapeDtypeStruct((b, t, d), BF16)],
        scratch_shapes=[
            pltpu.VMEM((tc + t, HEAD), F32),
            pltpu.VMEM((2, tc + t, HEAD), BF16),
            pltpu.VMEM((2, tc + t, HEAD), BF16),
            pltpu.VMEM((2, (tc + t) // GLA_CHUNK, 1, HEAD), F32),
            pltpu.VMEM((2, HEAD, HEAD), F32),
        ],
        compiler_params=_params(("arbitrary", "arbitrary")),
        name="hgrn_scan",
    )(*([p_ctx] * 5), *([p_lat] * 5), lower_bound.reshape(2, heads, 1, HEAD), o_gain.reshape(1, HEAD))


def _flash_kernel(q_ref, kc_ref, vc_ref, kl_ref, vl_ref, o_ref, *, tk, unroll):
    tq = q_ref.shape[0]
    group = q_ref.shape[1] // HEAD
    qs = [q_ref[:, g * HEAD:(g + 1) * HEAD] for g in range(group)]

    def attend(carry, k, v):
        v_ext = jnp.concatenate([v, jnp.ones_like(v)], axis=1)
        out = []
        for g in range(group):
            m, acc = carry[g]
            s = _dot_nt(qs[g], k)
            m_new = jnp.maximum(m, jnp.max(s, axis=-1, keepdims=True))
            p = jnp.exp2(s - m_new).astype(BF16)
            out.append((m_new, jnp.exp2(m - m_new) * acc + _dot(p, v_ext)))
        return tuple(out)

    def body(j, carry):
        r = pl.ds(pl.multiple_of(j * tk, tk), tk)
        return attend(carry, kl_ref[r, :], vl_ref[r, :])

    init = tuple((jnp.full((tq, 1), -jnp.inf, F32), jnp.zeros((tq, 2 * HEAD), F32)) for _ in range(group))
    carry = attend(init, kc_ref[...], vc_ref[...])
    final = lax.fori_loop(0, kl_ref.shape[0] // tk, body, carry, unroll=unroll)
    for g in range(group):
        acc = final[g][1]
        o_ref[:, g * HEAD:(g + 1) * HEAD] = (acc[:, :HEAD] / acc[:, HEAD:]).astype(o_ref.dtype)


def _flash_call(qkv, kv_ctx, *, heads):
    b, t, n = qkv.shape
    tc = kv_ctx.shape[1]
    kvh = (n // HEAD - heads) // 2
    group = heads // kvh
    tq = _pick(t, 512, BF16_SUBLANES)
    tk = _pick(t, 1024, 2 * HEAD)
    body = functools.partial(_flash_kernel, tk=tk, unroll=math.gcd(t // tk, 8))
    return pl.pallas_call(
        body,
        grid=(b, kvh, t // tq),
        in_specs=[
            pl.BlockSpec((None, tq, group * HEAD), lambda bi, h, i: (bi, i, h)),
            pl.BlockSpec((None, tc, HEAD), lambda bi, h, i: (bi, 0, h)),
            pl.BlockSpec((None, tc, HEAD), lambda bi, h, i: (bi, 0, kvh + h)),
            pl.BlockSpec((None, t, HEAD), lambda bi, h, i: (bi, 0, heads + h)),
            pl.BlockSpec((None, t, HEAD), lambda bi, h, i: (bi, 0, heads + kvh + h)),
        ],
        out_specs=pl.BlockSpec((None, tq, group * HEAD), lambda bi, h, i: (bi, i, h)),
        out_shape=jax.ShapeDtypeStruct((b, t, heads * HEAD), BF16),
        compiler_params=_params(("arbitrary", "arbitrary", "arbitrary")),
        name="gqa_flash",
    )(qkv, kv_ctx, kv_ctx, qkv, qkv)


def _rope_tables(t):
    pos = jnp.arange(t, dtype=jnp.int32)
    rows = (pos // GRID_W).astype(F32)
    cols = (pos % GRID_W).astype(F32)
    axis_dim = HEAD // 2
    inv_freq = ROPE_THETA ** (-jnp.arange(0, axis_dim, 2, dtype=F32) / axis_dim)
    ang = jnp.concatenate([rows[:, None] * inv_freq, cols[:, None] * inv_freq], axis=-1)
    cos, sin = jnp.cos(ang), jnp.sin(ang)
    return jnp.concatenate([cos, cos], axis=-1), jnp.concatenate([-sin, sin], axis=-1)


def _split_mods(mods, b):
    d = mods.shape[1] // N_MOD
    lat = [mods[:b, i * d:(i + 1) * d].reshape(b, 1, d) for i in range(N_MOD)]
    ctx = [mods[b:b + 1, i * d:(i + 1) * d].reshape(1, 1, d) for i in range(N_MOD)]
    return lat, ctx


def kernel(x, c, ctx, c_ctx, ada_w, ada_b, norm_mix_pre, norm_mix_post, norm_ffn_pre, norm_ffn_post, hgrn_w_in, hgrn_lb_logits, hgrn_o_norm, hgrn_w_out, attn_w_qkv, attn_q_norm, attn_k_norm, attn_w_out, ffn_w_in, ffn_conv_w, ffn_conv_b, ffn_w_out):
    b, t, d = x.shape
    tc = ctx.shape[1]
    heads = d // HEAD
    assert ada_w.shape[0] == 2 and b + 1 <= ADA_ROWS

    cond = jnp.zeros((ADA_ROWS, d), F32).at[:b].set(c).at[b].set(c_ctx)
    mods = _ada_call(cond, ada_w, ada_b)
    x_lat = x.reshape(b * t, d)
    x_ctx = ctx.reshape(b * tc, d)

    (sh_ml, sc_ml, gt_ml, sh_fl, sc_fl, gt_fl), (sh_mc, sc_mc, gt_mc, sh_fc, sc_fc, gt_fc) = _split_mods(mods[0], b)
    lower_bound = jnp.cumsum(jax.nn.softmax(hgrn_lb_logits.astype(F32), axis=1), axis=1)[:, 0]
    w_in = hgrn_w_in[0].astype(BF16)
    p_lat = _nmm_call(x_lat, norm_mix_pre[0], sh_ml, sc_ml, w_in, name="hgrn_in_lat")
    p_ctx = _nmm_call(x_ctx, norm_mix_pre[0], sh_mc, sc_mc, w_in, name="hgrn_in_ctx")
    y_ctx, y_lat = _gla_call(p_ctx.reshape(b, tc, 5 * d), p_lat.reshape(b, t, 5 * d), lower_bound, hgrn_o_norm[0])
    w_out = hgrn_w_out[0].astype(BF16)
    x_lat = _proj_res_call(y_lat.reshape(b * t, d), w_out, x_lat, norm_mix_post[0], gt_ml, name="hgrn_out_lat")
    x_ctx = _proj_res_call(y_ctx.reshape(b * tc, d), w_out, x_ctx, norm_mix_post[0], gt_mc, name="hgrn_out_ctx")
    w_in = ffn_w_in[0].astype(BF16)
    w_out = ffn_w_out[0].astype(BF16)
    a = _ffn_in_call(x_lat, norm_ffn_pre[0], sh_fl, sc_fl, w_in, ffn_conv_w[0], ffn_conv_b[0],
                     seq_len=t, name="ffn0_in_lat")
    x_lat = _proj_res_call(a, w_out, x_lat, norm_ffn_post[0], gt_fl, name="ffn0_out_lat")
    a = _ffn_in_call(x_ctx, norm_ffn_pre[0], sh_fc, sc_fc, w_in, ffn_conv_w[0], ffn_conv_b[0],
                     seq_len=tc, name="ffn0_in_ctx")
    x_ctx = _proj_res_call(a, w_out, x_ctx, norm_ffn_post[0], gt_fc, name="ffn0_out_ctx")

    (sh_ml, sc_ml, gt_ml, sh_fl, sc_fl, gt_fl), (sh_mc, sc_mc, _, _, _, _) = _split_mods(mods[1], b)
    n_qkv = attn_w_qkv.shape[2]
    kvh = (n_qkv // HEAD - heads) // 2
    perm = jnp.concatenate([jnp.arange(0, HEAD, 2), jnp.arange(1, HEAD, 2)])
    head_perm = (jnp.arange(heads + kvh)[:, None] * HEAD + perm[None, :]).reshape(-1)
    col_perm = jnp.concatenate([head_perm, jnp.arange((heads + kvh) * HEAD, n_qkv)])
    w_qkv = attn_w_qkv[0][:, col_perm].astype(BF16)
    head_gain = jnp.concatenate([
        jnp.tile(attn_q_norm[0][perm] * (LOG2_E * HEAD ** -0.5), heads),
        jnp.tile(attn_k_norm[0][perm], kvh),
        jnp.ones((kvh * HEAD,), F32),
    ]).reshape(1, n_qkv)
    cos, sin = _rope_tables(t)
    qk_cols = (heads + kvh) * HEAD
    qkv = _nmm_call(x_lat, norm_mix_pre[1], sh_ml, sc_ml, w_qkv,
                    rope=(head_gain, cos, sin, qk_cols), name="attn_qkv_lat")
    q_cols = heads * HEAD
    kv_ctx = _nmm_call(x_ctx, norm_mix_pre[1], sh_mc, sc_mc, w_qkv[:, q_cols:],
                       rope=(head_gain[:, q_cols:], jnp.ones((tc, HEAD), F32), jnp.zeros((tc, HEAD), F32),
                             kvh * HEAD), name="attn_kv_ctx")
    qkv = qkv.reshape(b, t, n_qkv)
    kv_ctx = kv_ctx.reshape(b, tc, 2 * kvh * HEAD)
    o = _flash_call(qkv, kv_ctx, heads=heads)
    x_lat = _proj_res_call(o.reshape(b * t, d), attn_w_out[0].astype(BF16), x_lat, norm_mix_post[1], gt_ml,
                           name="attn_out_lat")
    a = _ffn_in_call(x_lat, norm_ffn_pre[1], sh_fl, sc_fl, ffn_w_in[1].astype(BF16), ffn_conv_w[1], ffn_conv_b[1],
                     seq_len=t, name="ffn1_in_lat")
    x_lat = _proj_res_call(a, ffn_w_out[1].astype(BF16), x_lat, norm_ffn_post[1], gt_fl, name="ffn1_out_lat")
    return x_lat.reshape(b, t, d)
```

```python
import functools
import math

import jax
import jax.numpy as jnp
from jax import lax
from jax.experimental import pallas as pl
from jax.experimental.pallas import tpu as pltpu

F32 = jnp.float32
BF16 = jnp.bfloat16

EPS = 1e-6
GRID_W = 64
ROPE_THETA = 10000.0
LOG2_E = math.log2(math.e)
N_MOD = 6
HEAD = 128
GLA_CHUNK = 128
BF16_SUBLANES = 16
ADA_ROWS = 8
V7X_VMEM_BYTES = 64 * 1024 * 1024
VMEM_LIMIT = V7X_VMEM_BYTES - 6 * 1024 * 1024


def _pick(n, target, align):
    if n <= target:
        return n
    for t in range(target - target % align, 0, -align):
        if n % t == 0:
            return t
    return n


def _params(sem):
    return pltpu.CompilerParams(dimension_semantics=sem, vmem_limit_bytes=VMEM_LIMIT)


def _sigmoid(x):
    return 0.5 * jnp.tanh(0.5 * x) + 0.5


def _silu(x):
    return x * _sigmoid(x)


def _dot(a, b):
    return jnp.dot(a, b, preferred_element_type=F32)


def _dot_nt(a, b):
    return lax.dot_general(a, b, (((1,), (1,)), ((), ())), preferred_element_type=F32)


def _dot_tn(a, b):
    return lax.dot_general(a, b, (((0,), (0,)), ((), ())), preferred_element_type=F32)


def _ada_kernel(c_ref, w_ref, b_ref, o_ref):
    sc = _silu(c_ref[...]).astype(BF16)
    o_ref[...] = _dot(sc, w_ref[...].astype(BF16)) + b_ref[...]


def _ada_call(cond, ada_w, ada_b):
    depth, d, n = ada_w.shape
    tn = _pick(n, 1024, HEAD)
    return pl.pallas_call(
        _ada_kernel,
        grid=(depth, n // tn),
        in_specs=[
            pl.BlockSpec((ADA_ROWS, d), lambda l, j: (0, 0)),
            pl.BlockSpec((None, d, tn), lambda l, j: (l, 0, j)),
            pl.BlockSpec((None, 1, tn), lambda l, j: (l, 0, j)),
        ],
        out_specs=pl.BlockSpec((None, ADA_ROWS, tn), lambda l, j: (l, 0, j)),
        out_shape=jax.ShapeDtypeStruct((depth, ADA_ROWS, n), F32),
        compiler_params=_params(("arbitrary", "arbitrary")),
        name="ada_ln",
    )(cond, ada_w, ada_b.reshape(depth, 1, n))


def _norm_mod(x, gain, shift, scale):
    ms = jnp.mean(x * x, axis=-1, keepdims=True)
    return (x * lax.rsqrt(ms + EPS) * gain) * (1.0 + scale) + shift


def _nmm_kernel(x_ref, gain_ref, shift_ref, scale_ref, w_ref, o_ref, h_ref):
    @pl.when(pl.program_id(1) == 0)
    def _():
        h_ref[...] = _norm_mod(x_ref[...], gain_ref[...], shift_ref[...], scale_ref[...]).astype(BF16)

    o_ref[...] = _dot(h_ref[...], w_ref[...]).astype(o_ref.dtype)


def _nmm_rope_kernel(x_ref, gain_ref, shift_ref, scale_ref, w_ref, hg_ref, cos_ref, sin_ref,
                     o_ref, h_ref, *, n_normed_blocks):
    j = pl.program_id(1)

    @pl.when(j == 0)
    def _():
        h_ref[...] = _norm_mod(x_ref[...], gain_ref[...], shift_ref[...], scale_ref[...]).astype(BF16)

    acc = _dot(h_ref[...], w_ref[...])
    normed = j < n_normed_blocks
    cos = cos_ref[...]
    sin = sin_ref[...]
    for hh in range(acc.shape[1] // HEAD):
        cols = slice(hh * HEAD, (hh + 1) * HEAD)
        a = acc[:, cols]
        ms = jnp.mean(a * a, axis=-1, keepdims=True)
        r = a * lax.rsqrt(ms + EPS) * hg_ref[:, cols]
        r = r * cos + pltpu.roll(r, HEAD // 2, axis=1) * sin
        o_ref[:, cols] = jnp.where(normed, r, a).astype(o_ref.dtype)


def _nmm_call(x, gain, shift, scale, w, *, layer=0, col0=0, rope=None, name):
    m, d = x.shape
    n = w.shape[2] - col0
    bm = shift.shape[0]
    rows_per_mod = m // bm
    row_period = rows_per_mod if rope is None else math.gcd(rows_per_mod, rope[1].shape[0])
    tm = _pick(row_period, 1024, BF16_SUBLANES)
    tn = _pick(n, 1024, HEAD) if rope is None else _pick(math.gcd(n, rope[3]), 512, HEAD)
    mod_blocks = rows_per_mod // tm
    assert col0 % tn == 0
    col_block0 = col0 // tn
    in_specs = [
        pl.BlockSpec((tm, d), lambda i, j: (i, 0)),
        pl.BlockSpec((1, d), lambda i, j: (0, 0)),
        pl.BlockSpec((None, 1, d), lambda i, j: (i // mod_blocks, 0, 0)),
        pl.BlockSpec((None, 1, d), lambda i, j: (i // mod_blocks, 0, 0)),
        pl.BlockSpec((None, d, tn), lambda i, j: (layer, 0, col_block0 + j)),
    ]
    args = [x, gain.reshape(1, d), shift, scale, w]
    if rope is None:
        body = _nmm_kernel
    else:
        head_gain, cos, sin, n_normed_cols = rope
        assert n_normed_cols % tn == 0 and cos.shape[0] % tm == 0
        table_blocks = cos.shape[0] // tm
        in_specs += [
            pl.BlockSpec((1, tn), lambda i, j: (0, j)),
            pl.BlockSpec((tm, HEAD), lambda i, j: (i % table_blocks, 0)),
            pl.BlockSpec((tm, HEAD), lambda i, j: (i % table_blocks, 0)),
        ]
        args += [head_gain, cos, sin]
        body = functools.partial(_nmm_rope_kernel, n_normed_blocks=n_normed_cols // tn)
    return pl.pallas_call(
        body,
        grid=(m // tm, n // tn),
        in_specs=in_specs,
        out_specs=pl.BlockSpec((tm, tn), lambda i, j: (i, j)),
        out_shape=jax.ShapeDtypeStruct((m, n), BF16),
        scratch_shapes=[pltpu.VMEM((tm, d), BF16)],
        compiler_params=_params(("arbitrary", "arbitrary")),
        name=name,
    )(*args)


def _residual(x, y, gain, gate):
    ms = jnp.mean(y * y, axis=-1, keepdims=True)
    return x + gate * (y * lax.rsqrt(ms + EPS) * gain)


def _proj_res_kernel(a_ref, w_ref, x_ref, gain_ref, gate_ref, o_ref):
    y = _dot(a_ref[...], w_ref[...])
    o_ref[...] = _residual(x_ref[...], y, gain_ref[...], gate_ref[...])


def _proj_res_ktiled_kernel(a_ref, w_ref, x_ref, gain_ref, gate_ref, o_ref):
    k = pl.program_id(1)

    @pl.when(k == 0)
    def _():
        o_ref[...] = jnp.zeros_like(o_ref)

    o_ref[...] += _dot(a_ref[...], w_ref[...])

    @pl.when(k == pl.num_programs(1) - 1)
    def _():
        o_ref[...] = _residual(x_ref[...], o_ref[...], gain_ref[...], gate_ref[...])


def _proj_res_call(a, w, x, gain, gate, *, layer=0, name):
    m, k = a.shape
    d = w.shape[2]
    bm = gate.shape[0]
    rows_per_mod = m // bm
    tk = k if k <= d else _pick(k, 512, 2 * HEAD)
    kb = k // tk
    tm = _pick(rows_per_mod, 512 if kb == 1 else 1024, BF16_SUBLANES)
    mod_blocks = rows_per_mod // tm
    return pl.pallas_call(
        _proj_res_kernel if kb == 1 else _proj_res_ktiled_kernel,
        grid=(m // tm, kb),
        in_specs=[
            pl.BlockSpec((tm, tk), lambda i, kk: (i, kk)),
            pl.BlockSpec((None, tk, d), lambda i, kk: (layer, kk, 0)),
            pl.BlockSpec((tm, d), lambda i, kk: (i, 0)),
            pl.BlockSpec((1, d), lambda i, kk: (0, 0)),
            pl.BlockSpec((None, 1, d), lambda i, kk: (i // mod_blocks, 0, 0)),
        ],
        out_specs=pl.BlockSpec((tm, d), lambda i, kk: (i, 0)),
        out_shape=jax.ShapeDtypeStruct((m, d), F32),
        compiler_params=_params(("arbitrary", "arbitrary")),
        name=name,
    )(a, w, x, gain.reshape(1, d), gate)


def _ffn_in_kernel(x_ref, xprev_ref, xnext_ref, gain_ref, shift_ref, scale_ref, wg_ref, wu_ref, cw_ref, cb_ref,
                   o_ref, h_ref, *, blocks_per_seq):
    i = pl.program_id(0)
    tm = x_ref.shape[0]
    halo = BF16_SUBLANES

    @pl.when(pl.program_id(1) == 0)
    def _():
        gain, shift, scale = gain_ref[...], shift_ref[...], scale_ref[...]
        seq_block = i % blocks_per_seq
        h_prev = jnp.where(seq_block > 0, _norm_mod(xprev_ref[...], gain, shift, scale), 0.0)
        h_next = jnp.where(seq_block < blocks_per_seq - 1, _norm_mod(xnext_ref[...], gain, shift, scale), 0.0)
        zeros = jnp.zeros((halo - h_prev.shape[0], h_prev.shape[1]), F32)
        h_ref[0:halo, :] = jnp.concatenate([zeros, h_prev], axis=0).astype(BF16)
        h_ref[halo:halo + tm, :] = _norm_mod(x_ref[...], gain, shift, scale).astype(BF16)
        h_ref[halo + tm:, :] = jnp.concatenate([h_next, zeros], axis=0).astype(BF16)

    gate = _dot(h_ref[...], wg_ref[...])
    up = _dot(h_ref[halo:halo + tm, :], wu_ref[...])
    rows = gate.shape[0]
    g_prev = pltpu.roll(gate, 1, axis=0)[halo:halo + tm, :]
    g_next = pltpu.roll(gate, rows - 1, axis=0)[halo:halo + tm, :]
    conv = g_prev * cw_ref[0:1, :] + gate[halo:halo + tm, :] * cw_ref[1:2, :] + g_next * cw_ref[2:3, :] + cb_ref[...]
    o_ref[...] = (_silu(conv) * up).astype(o_ref.dtype)


def _ffn_in_call(x, gain, shift, scale, w, conv_w, conv_b, *, layer, seq_len, name):
    m, d = x.shape
    f = w.shape[2] // 2
    bm = shift.shape[0]
    rows_per_mod = m // bm
    tm = _pick(seq_len, 1024, BF16_SUBLANES)
    tn = _pick(f, 512, HEAD)
    assert rows_per_mod % tm == 0
    mod_blocks = rows_per_mod // tm
    nb = f // tn
    xh = 8
    tiles_per_tm = tm // xh
    last_tile = m // xh - 1
    body = functools.partial(_ffn_in_kernel, blocks_per_seq=seq_len // tm)
    return pl.pallas_call(
        body,
        grid=(m // tm, nb),
        in_specs=[
            pl.BlockSpec((tm, d), lambda i, j: (i, 0)),
            pl.BlockSpec((xh, d), lambda i, j: (jnp.maximum(i * tiles_per_tm - 1, 0), 0)),
            pl.BlockSpec((xh, d), lambda i, j: (jnp.minimum((i + 1) * tiles_per_tm, last_tile), 0)),
            pl.BlockSpec((1, d), lambda i, j: (0, 0)),
            pl.BlockSpec((None, 1, d), lambda i, j: (i // mod_blocks, 0, 0)),
            pl.BlockSpec((None, 1, d), lambda i, j: (i // mod_blocks, 0, 0)),
            pl.BlockSpec((None, d, tn), lambda i, j: (layer, 0, j)),
            pl.BlockSpec((None, d, tn), lambda i, j: (layer, 0, nb + j)),
            pl.BlockSpec((3, tn), lambda i, j: (0, j)),
            pl.BlockSpec((1, tn), lambda i, j: (0, j)),
        ],
        out_specs=pl.BlockSpec((tm, tn), lambda i, j: (i, j)),
        out_shape=jax.ShapeDtypeStruct((m, f), BF16),
        scratch_shapes=[pltpu.VMEM((tm + 2 * BF16_SUBLANES, d), BF16)],
        compiler_params=_params(("arbitrary", "arbitrary")),
        name=name,
    )(x, x, x, gain.reshape(1, d), shift, scale, w, w, conv_w, conv_b.reshape(1, f))


def _gla_local(q_raw, ff_raw, fb_raw, v_raw, lb_f, lb_b):
    c = q_raw.shape[0]
    qs = _silu(q_raw.astype(F32)) * (HEAD ** -0.5)
    row = lax.broadcasted_iota(jnp.int32, (c, c), 0)
    col = lax.broadcasted_iota(jnp.int32, (c, c), 1)
    o = None
    carried = []
    for forward, f_raw, lb in ((True, ff_raw, lb_f), (False, fb_raw, lb_b)):
        f = lb + (1.0 - lb) * _sigmoid(f_raw.astype(F32))
        kk = 1.0 - f
        lf = jnp.log(f)
        within = (col <= row) if forward else (col >= row)
        ones = jnp.where(within, 1.0, 0.0).astype(BF16)
        lf_hi = lf.astype(BF16)
        lf_lo = (lf - lf_hi.astype(F32)).astype(BF16)
        b = _dot(ones, lf_hi) + _dot(ones, lf_lo)
        a_row = c // 2 - 1 if forward else c // 2
        e_row = c - 1 if forward else 0
        anchor = b[a_row:a_row + 1, :]
        b_end = b[e_row:e_row + 1, :]
        qa = qs * jnp.exp(b - anchor)
        ka = kk * jnp.exp(anchor - b)
        scores = jnp.where(within, _dot_nt(qa.astype(BF16), ka.astype(BF16)), 0.0).astype(BF16)
        oi = _dot(scores, v_raw)
        o = oi if o is None else o + oi
        carried.append(((qa * jnp.exp(anchor)).astype(BF16), (ka * jnp.exp(b_end - anchor)).astype(BF16),
                        jnp.exp(b_end)))
    return o, carried


def _gla_readout(o, g_raw, o_gain):
    ms = jnp.mean(o * o, axis=-1, keepdims=True)
    y = o * lax.rsqrt(ms + EPS) * o_gain
    return (y * _silu(g_raw.astype(F32))).astype(BF16)


def _gla_kernel(qc, ffc, fbc, vc, gc, ql, ffl, fbl, vl, gl, lb_ref, og_ref, yc_ref, yl_ref,
                o_ref, qb_ref, ke_ref, dec_ref, st_ref):
    c = GLA_CHUNK
    tc = qc.shape[0]
    n_ctx = tc // c
    n_lat = ql.shape[0] // c
    lb_f = lb_ref[0]
    lb_b = lb_ref[1]

    def local(refs, src_rows, dst_rows, chunk):
        q, ff, fb, v = refs
        o, carried = _gla_local(q[src_rows, :], ff[src_rows, :], fb[src_rows, :], v[src_rows, :], lb_f, lb_b)
        o_ref[dst_rows, :] = o
        for d, (qb, ke, dec) in enumerate(carried):
            qb_ref[d, dst_rows, :] = qb
            ke_ref[d, dst_rows, :] = ke
            dec_ref[d, chunk] = dec

    for i in range(n_ctx):
        r = slice(i * c, (i + 1) * c)
        local((qc, ffc, fbc, vc), r, r, i)

    def local_body(i, carry):
        src = pl.ds(pl.multiple_of(i * c, c), c)
        dst = pl.ds(pl.multiple_of(tc + i * c, c), c)
        local((ql, ffl, fbl, vl), src, dst, n_ctx + i)
        return carry

    lax.fori_loop(0, n_lat, local_body, 0, unroll=math.gcd(n_lat, 8))

    st_ref[...] = jnp.zeros_like(st_ref)

    def step(d, v, src_rows, dst_rows, chunk):
        st = st_ref[d]
        o_ref[dst_rows, :] += _dot_nt(qb_ref[d, dst_rows, :], st.astype(BF16))
        st_ref[d] = st * dec_ref[d, chunk] + _dot_tn(v[src_rows, :], ke_ref[d, dst_rows, :])

    for i in range(n_ctx):
        for d, j in ((0, i), (1, n_ctx - 1 - i)):
            r = slice(j * c, (j + 1) * c)
            step(d, vc, r, r, j)

    def step_body(i, carry):
        for d, j in ((0, i), (1, n_lat - 1 - i)):
            src = pl.ds(pl.multiple_of(j * c, c), c)
            dst = pl.ds(pl.multiple_of(tc + j * c, c), c)
            step(d, vl, src, dst, n_ctx + j)
        return carry

    lax.fori_loop(0, n_lat, step_body, 0, unroll=math.gcd(n_lat, 8))

    o_gain = og_ref[...]
    for i in range(n_ctx):
        r = slice(i * c, (i + 1) * c)
        yc_ref[r, :] = _gla_readout(o_ref[r, :], gc[r, :], o_gain)

    def out_body(i, carry):
        src = pl.ds(pl.multiple_of(i * c, c), c)
        dst = pl.ds(pl.multiple_of(tc + i * c, c), c)
        yl_ref[src, :] = _gla_readout(o_ref[dst, :], gl[src, :], o_gain)
        return carry

    lax.fori_loop(0, n_lat, out_body, 0, unroll=math.gcd(n_lat, 4))


def _gla_call(p_ctx, p_lat, lower_bound, o_gain):
    b, tc, d5 = p_ctx.shape
    t = p_lat.shape[1]
    d = d5 // 5
    heads = d // HEAD
    assert tc % GLA_CHUNK == 0 and t % GLA_CHUNK == 0

    def col(tt, part):
        return pl.BlockSpec((None, tt, HEAD), lambda bi, h: (bi, 0, part * heads + h))

    out_spec = lambda tt: pl.BlockSpec((None, tt, HEAD), lambda bi, h: (bi, 0, h))
    return pl.pallas_call(
        _gla_kernel,
        grid=(b, heads),
        in_specs=[col(tc, p) for p in range(5)] + [col(t, p) for p in range(5)] + [
            pl.BlockSpec((2, None, 1, HEAD), lambda bi, h: (0, h, 0, 0)),
            pl.BlockSpec((1, HEAD), lambda bi, h: (0, 0)),
        ],
        out_specs=[out_spec(tc), out_spec(t)],
        out_shape=[jax.ShapeDtypeStruct((b, tc, d), BF16), jax.ShapeDtypeStruct((b, t, d), BF16)],
        scratch_shapes=[
            pltpu.VMEM((tc + t, HEAD), F32),
            pltpu.VMEM((2, tc + t, HEAD), BF16),
            pltpu.VMEM((2, tc + t, HEAD), BF16),
            pltpu.VMEM((2, (tc + t) // GLA_CHUNK, 1, HEAD), F32),
            pltpu.VMEM((2, HEAD, HEAD), F32),
        ],
        compiler_params=_params(("arbitrary", "arbitrary")),
        name="hgrn_scan",
    )(*([p_ctx] * 5), *([p_lat] * 5), lower_bound.reshape(2, heads, 1, HEAD), o_gain.reshape(1, HEAD))


def _flash_kernel(q_ref, kc_ref, vc_ref, kl_ref, vl_ref, o_ref, *, tk, unroll):
    tq = q_ref.shape[0]
    group = q_ref.shape[1] // HEAD
    qs = [q_ref[:, g * HEAD:(g + 1) * HEAD] for g in range(group)]

    def attend(carry, k, v):
        v_ext = jnp.concatenate([v, jnp.ones_like(v)], axis=1)
        out = []
        for g in range(group):
            m, acc = carry[g]
            s = _dot_nt(qs[g], k)
            m_new = jnp.maximum(m, jnp.max(s, axis=-1, keepdims=True))
            p = jnp.exp2(s - m_new).astype(BF16)
            out.append((m_new, jnp.exp2(m - m_new) * acc + _dot(p, v_ext)))
        return tuple(out)

    def body(j, carry):
        r = pl.ds(pl.multiple_of(j * tk, tk), tk)
        return attend(carry, kl_ref[r, :], vl_ref[r, :])

    init = tuple((jnp.full((tq, 1), -jnp.inf, F32), jnp.zeros((tq, 2 * HEAD), F32)) for _ in range(group))
    carry = attend(init, kc_ref[...], vc_ref[...])
    final = lax.fori_loop(0, kl_ref.shape[0] // tk, body, carry, unroll=unroll)
    for g in range(group):
        acc = final[g][1]
        o_ref[:, g * HEAD:(g + 1) * HEAD] = (acc[:, :HEAD] / acc[:, HEAD:]).astype(o_ref.dtype)


def _flash_call(qkv, kv_ctx, *, heads):
    b, t, n = qkv.shape
    tc = kv_ctx.shape[1]
    kvh = (n // HEAD - heads) // 2
    group = heads // kvh
    tq = _pick(t, 1024, BF16_SUBLANES)
    tk = _pick(t, 1024, 2 * HEAD)
    body = functools.partial(_flash_kernel, tk=tk, unroll=math.gcd(t // tk, 8))
    return pl.pallas_call(
        body,
        grid=(b, kvh, t // tq),
        in_specs=[
            pl.BlockSpec((None, tq, group * HEAD), lambda bi, h, i: (bi, i, h)),
            pl.BlockSpec((None, tc, HEAD), lambda bi, h, i: (bi, 0, h)),
            pl.BlockSpec((None, tc, HEAD), lambda bi, h, i: (bi, 0, kvh + h)),
            pl.BlockSpec((None, t, HEAD), lambda bi, h, i: (bi, 0, heads + h)),
            pl.BlockSpec((None, t, HEAD), lambda bi, h, i: (bi, 0, heads + kvh + h)),
        ],
        out_specs=pl.BlockSpec((None, tq, group * HEAD), lambda bi, h, i: (bi, i, h)),
        out_shape=jax.ShapeDtypeStruct((b, t, heads * HEAD), BF16),
        compiler_params=_params(("arbitrary", "arbitrary", "arbitrary")),
        name="gqa_flash",
    )(qkv, kv_ctx, kv_ctx, qkv, qkv)


def _rope_tables(t):
    pos = jnp.arange(t, dtype=jnp.int32)
    rows = (pos // GRID_W).astype(F32)
    cols = (pos % GRID_W).astype(F32)
    axis_dim = HEAD // 2
    inv_freq = ROPE_THETA ** (-jnp.arange(0, axis_dim, 2, dtype=F32) / axis_dim)
    ang = jnp.concatenate([rows[:, None] * inv_freq, cols[:, None] * inv_freq], axis=-1)
    cos, sin = jnp.cos(ang), jnp.sin(ang)
    return jnp.concatenate([cos, cos], axis=-1), jnp.concatenate([-sin, sin], axis=-1)


def _split_mods(mods, b):
    d = mods.shape[1] // N_MOD
    lat = [mods[:b, i * d:(i + 1) * d].reshape(b, 1, d) for i in range(N_MOD)]
    ctx = [mods[b:b + 1, i * d:(i + 1) * d].reshape(1, 1, d) for i in range(N_MOD)]
    return lat, ctx


def kernel(x, c, ctx, c_ctx, ada_w, ada_b, norm_mix_pre, norm_mix_post, norm_ffn_pre, norm_ffn_post, hgrn_w_in, hgrn_lb_logits, hgrn_o_norm, hgrn_w_out, attn_w_qkv, attn_q_norm, attn_k_norm, attn_w_out, ffn_w_in, ffn_conv_w, ffn_conv_b, ffn_w_out):
    b, t, d = x.shape
    tc = ctx.shape[1]
    heads = d // HEAD
    assert ada_w.shape[0] == 2 and b + 1 <= ADA_ROWS

    cond = jnp.zeros((ADA_ROWS, d), F32).at[:b].set(c).at[b].set(c_ctx)
    mods = _ada_call(cond, ada_w, ada_b)
    x_lat = x.reshape(b * t, d)
    x_ctx = ctx.reshape(b * tc, d)

    (sh_ml, sc_ml, gt_ml, sh_fl, sc_fl, gt_fl), (sh_mc, sc_mc, gt_mc, sh_fc, sc_fc, gt_fc) = _split_mods(mods[0], b)
    lower_bound = jnp.cumsum(jax.nn.softmax(hgrn_lb_logits.astype(F32), axis=1), axis=1)[:, 0]
    hgrn_w_in, hgrn_w_out, attn_w_out = hgrn_w_in.astype(BF16), hgrn_w_out.astype(BF16), attn_w_out.astype(BF16)
    ffn_w_in, ffn_w_out = ffn_w_in.astype(BF16), ffn_w_out.astype(BF16)
    p_lat = _nmm_call(x_lat, norm_mix_pre[0], sh_ml, sc_ml, hgrn_w_in, name="hgrn_in_lat")
    p_ctx = _nmm_call(x_ctx, norm_mix_pre[0], sh_mc, sc_mc, hgrn_w_in, name="hgrn_in_ctx")
    y_ctx, y_lat = _gla_call(p_ctx.reshape(b, tc, 5 * d), p_lat.reshape(b, t, 5 * d), lower_bound, hgrn_o_norm[0])
    x_lat = _proj_res_call(y_lat.reshape(b * t, d), hgrn_w_out, x_lat, norm_mix_post[0], gt_ml, name="hgrn_out_lat")
    x_ctx = _proj_res_call(y_ctx.reshape(b * tc, d), hgrn_w_out, x_ctx, norm_mix_post[0], gt_mc, name="hgrn_out_ctx")
    a = _ffn_in_call(x_lat, norm_ffn_pre[0], sh_fl, sc_fl, ffn_w_in, ffn_conv_w[0], ffn_conv_b[0],
                     layer=0, seq_len=t, name="ffn0_in_lat")
    x_lat = _proj_res_call(a, ffn_w_out, x_lat, norm_ffn_post[0], gt_fl, layer=0, name="ffn0_out_lat")
    a = _ffn_in_call(x_ctx, norm_ffn_pre[0], sh_fc, sc_fc, ffn_w_in, ffn_conv_w[0], ffn_conv_b[0],
                     layer=0, seq_len=tc, name="ffn0_in_ctx")
    x_ctx = _proj_res_call(a, ffn_w_out, x_ctx, norm_ffn_post[0], gt_fc, layer=0, name="ffn0_out_ctx")

    (sh_ml, sc_ml, gt_ml, sh_fl, sc_fl, gt_fl), (sh_mc, sc_mc, _, _, _, _) = _split_mods(mods[1], b)
    n_qkv = attn_w_qkv.shape[2]
    kvh = (n_qkv // HEAD - heads) // 2
    perm = jnp.concatenate([jnp.arange(0, HEAD, 2), jnp.arange(1, HEAD, 2)])
    head_perm = (jnp.arange(heads + kvh)[:, None] * HEAD + perm[None, :]).reshape(-1)
    col_perm = jnp.concatenate([head_perm, jnp.arange((heads + kvh) * HEAD, n_qkv)])
    w_qkv = attn_w_qkv[:, :, col_perm].astype(BF16)
    head_gain = jnp.concatenate([
        jnp.tile(attn_q_norm[0][perm] * (LOG2_E * HEAD ** -0.5), heads),
        jnp.tile(attn_k_norm[0][perm], kvh),
        jnp.ones((kvh * HEAD,), F32),
    ]).reshape(1, n_qkv)
    cos, sin = _rope_tables(t)
    qk_cols = (heads + kvh) * HEAD
    qkv = _nmm_call(x_lat, norm_mix_pre[1], sh_ml, sc_ml, w_qkv,
                    rope=(head_gain, cos, sin, qk_cols), name="attn_qkv_lat")
    q_cols = heads * HEAD
    kv_ctx = _nmm_call(x_ctx, norm_mix_pre[1], sh_mc, sc_mc, w_qkv, col0=q_cols,
                       rope=(head_gain[:, q_cols:], jnp.ones((tc, HEAD), F32), jnp.zeros((tc, HEAD), F32),
                             kvh * HEAD), name="attn_kv_ctx")
    qkv = qkv.reshape(b, t, n_qkv)
    kv_ctx = kv_ctx.reshape(b, tc, 2 * kvh * HEAD)
    o = _flash_call(qkv, kv_ctx, heads=heads)
    x_lat = _proj_res_call(o.reshape(b * t, d), attn_w_out, x_lat, norm_mix_post[1], gt_ml, name="attn_out_lat")
    a = _ffn_in_call(x_lat, norm_ffn_pre[1], sh_fl, sc_fl, ffn_w_in, ffn_conv_w[1], ffn_conv_b[1],
                     layer=1, seq_len=t, name="ffn1_in_lat")
    x_lat = _proj_res_call(a, ffn_w_out, x_lat, norm_ffn_post[1], gt_fl, layer=1, name="ffn1_out_lat")
    return x_lat.reshape(b, t, d)
```

```python
import functools
import math

import jax
import jax.numpy as jnp
from jax import lax
from jax.experimental import pallas as pl
from jax.experimental.pallas import tpu as pltpu

F32 = jnp.float32
BF16 = jnp.bfloat16

EPS = 1e-6
GRID_W = 64
ROPE_THETA = 10000.0
LOG2_E = math.log2(math.e)
N_MOD = 6
HEAD = 128
GLA_CHUNK = 128
GLA_RUN = 8
GLA_SAFE_LB = math.exp(-80.0 / (GLA_CHUNK // 2))
BF16_SUBLANES = 16
ADA_ROWS = 8
V7X_VMEM_BYTES = 64 * 1024 * 1024
VMEM_LIMIT = V7X_VMEM_BYTES - 6 * 1024 * 1024


def _pick(n, target, align):
    if n <= target:
        return n
    for t in range(target - target % align, 0, -align):
        if n % t == 0:
            return t
    return n


def _params(sem):
    return pltpu.CompilerParams(dimension_semantics=sem, vmem_limit_bytes=VMEM_LIMIT)


def _sigmoid(x):
    return 0.5 * jnp.tanh(0.5 * x) + 0.5


def _silu(x):
    return x * _sigmoid(x)


def _dot(a, b):
    return jnp.dot(a, b, preferred_element_type=F32)


def _dot_nt(a, b):
    return lax.dot_general(a, b, (((1,), (1,)), ((), ())), preferred_element_type=F32)


def _dot_tn(a, b):
    return lax.dot_general(a, b, (((0,), (0,)), ((), ())), preferred_element_type=F32)


def _ada_kernel(c_ref, w_ref, b_ref, o_ref):
    sc = _silu(c_ref[...]).astype(BF16)
    o_ref[...] = _dot(sc, w_ref[...].astype(BF16)) + b_ref[...]


def _ada_call(cond, ada_w, ada_b):
    depth, d, n = ada_w.shape
    tn = _pick(n, 1024, HEAD)
    return pl.pallas_call(
        _ada_kernel,
        grid=(depth, n // tn),
        in_specs=[
            pl.BlockSpec((ADA_ROWS, d), lambda l, j: (0, 0)),
            pl.BlockSpec((None, d, tn), lambda l, j: (l, 0, j)),
            pl.BlockSpec((None, 1, tn), lambda l, j: (l, 0, j)),
        ],
        out_specs=pl.BlockSpec((None, ADA_ROWS, tn), lambda l, j: (l, 0, j)),
        out_shape=jax.ShapeDtypeStruct((depth, ADA_ROWS, n), F32),
        compiler_params=_params(("arbitrary", "arbitrary")),
        name="ada_ln",
    )(cond, ada_w, ada_b.reshape(depth, 1, n))


def _norm_mod(x, gain, shift, scale):
    ms = jnp.mean(x * x, axis=-1, keepdims=True)
    return (x * lax.rsqrt(ms + EPS) * gain) * (1.0 + scale) + shift


def _nmm_kernel(x_ref, gain_ref, shift_ref, scale_ref, w_ref, o_ref, h_ref):
    @pl.when(pl.program_id(1) == 0)
    def _():
        h_ref[...] = _norm_mod(x_ref[...], gain_ref[...], shift_ref[...], scale_ref[...]).astype(BF16)

    o_ref[...] = _dot(h_ref[...], w_ref[...]).astype(o_ref.dtype)


def _nmm_rope_kernel(x_ref, gain_ref, shift_ref, scale_ref, w_ref, hg_ref, cos_ref, sin_ref,
                     o_ref, h_ref, *, n_normed_blocks):
    j = pl.program_id(1)

    @pl.when(j == 0)
    def _():
        h_ref[...] = _norm_mod(x_ref[...], gain_ref[...], shift_ref[...], scale_ref[...]).astype(BF16)

    acc = _dot(h_ref[...], w_ref[...])

    @pl.when(j < n_normed_blocks)
    def _():
        cos = cos_ref[...]
        sin = sin_ref[...]
        for hh in range(acc.shape[1] // HEAD):
            cols = slice(hh * HEAD, (hh + 1) * HEAD)
            a = acc[:, cols]
            ms = jnp.mean(a * a, axis=-1, keepdims=True)
            a = a * lax.rsqrt(ms + EPS) * hg_ref[:, cols]
            a = a * cos + pltpu.roll(a, HEAD // 2, axis=1) * sin
            o_ref[:, cols] = a.astype(o_ref.dtype)

    @pl.when(j >= n_normed_blocks)
    def _():
        o_ref[...] = acc.astype(o_ref.dtype)


def _nmm_call(x, gain, shift, scale, w, *, layer=0, col0=0, rope=None, name):
    m, d = x.shape
    n = w.shape[2] - col0
    bm = shift.shape[0]
    rows_per_mod = m // bm
    row_period = rows_per_mod if rope is None else math.gcd(rows_per_mod, rope[1].shape[0])
    tm = _pick(row_period, 1024, BF16_SUBLANES)
    tn = _pick(n, 1024, HEAD) if rope is None else _pick(math.gcd(n, rope[3]), 512, HEAD)
    mod_blocks = rows_per_mod // tm
    assert col0 % tn == 0
    col_block0 = col0 // tn
    in_specs = [
        pl.BlockSpec((tm, d), lambda i, j: (i, 0)),
        pl.BlockSpec((1, d), lambda i, j: (0, 0)),
        pl.BlockSpec((None, 1, d), lambda i, j: (i // mod_blocks, 0, 0)),
        pl.BlockSpec((None, 1, d), lambda i, j: (i // mod_blocks, 0, 0)),
        pl.BlockSpec((None, d, tn), lambda i, j: (layer, 0, col_block0 + j)),
    ]
    args = [x, gain.reshape(1, d), shift, scale, w]
    if rope is None:
        body = _nmm_kernel
    else:
        head_gain, cos, sin, n_normed_cols = rope
        assert n_normed_cols % tn == 0 and cos.shape[0] % tm == 0
        table_blocks = cos.shape[0] // tm
        in_specs += [
            pl.BlockSpec((1, tn), lambda i, j: (0, j)),
            pl.BlockSpec((tm, HEAD), lambda i, j: (i % table_blocks, 0)),
            pl.BlockSpec((tm, HEAD), lambda i, j: (i % table_blocks, 0)),
        ]
        args += [head_gain, cos, sin]
        body = functools.partial(_nmm_rope_kernel, n_normed_blocks=n_normed_cols // tn)
    return pl.pallas_call(
        body,
        grid=(m // tm, n // tn),
        in_specs=in_specs,
        out_specs=pl.BlockSpec((tm, tn), lambda i, j: (i, j)),
        out_shape=jax.ShapeDtypeStruct((m, n), BF16),
        scratch_shapes=[pltpu.VMEM((tm, d), BF16)],
        compiler_params=_params(("arbitrary", "arbitrary")),
        name=name,
    )(*args)


def _residual(x, y, gain, gate):
    ms = jnp.mean(y * y, axis=-1, keepdims=True)
    return x + gate * (y * lax.rsqrt(ms + EPS) * gain)


def _proj_res_kernel(a_ref, w_ref, x_ref, gain_ref, gate_ref, o_ref):
    y = _dot(a_ref[...], w_ref[...])
    o_ref[...] = _residual(x_ref[...], y, gain_ref[...], gate_ref[...])


def _proj_res_ktiled_kernel(a_ref, w_ref, x_ref, gain_ref, gate_ref, o_ref):
    k = pl.program_id(1)

    @pl.when(k == 0)
    def _():
        o_ref[...] = jnp.zeros_like(o_ref)

    o_ref[...] += _dot(a_ref[...], w_ref[...])

    @pl.when(k == pl.num_programs(1) - 1)
    def _():
        o_ref[...] = _residual(x_ref[...], o_ref[...], gain_ref[...], gate_ref[...])


def _proj_res_call(a, w, x, gain, gate, *, layer=0, name):
    m, k = a.shape
    d = w.shape[2]
    bm = gate.shape[0]
    rows_per_mod = m // bm
    tk = k if k <= d else _pick(k, 512, 2 * HEAD)
    kb = k // tk
    tm = _pick(rows_per_mod, 512 if kb == 1 else 1024, BF16_SUBLANES)
    mod_blocks = rows_per_mod // tm
    return pl.pallas_call(
        _proj_res_kernel if kb == 1 else _proj_res_ktiled_kernel,
        grid=(m // tm, kb),
        in_specs=[
            pl.BlockSpec((tm, tk), lambda i, kk: (i, kk)),
            pl.BlockSpec((None, tk, d), lambda i, kk: (layer, kk, 0)),
            pl.BlockSpec((tm, d), lambda i, kk: (i, 0)),
            pl.BlockSpec((1, d), lambda i, kk: (0, 0)),
            pl.BlockSpec((None, 1, d), lambda i, kk: (i // mod_blocks, 0, 0)),
        ],
        out_specs=pl.BlockSpec((tm, d), lambda i, kk: (i, 0)),
        out_shape=jax.ShapeDtypeStruct((m, d), F32),
        compiler_params=_params(("arbitrary", "arbitrary")),
        name=name,
    )(a, w, x, gain.reshape(1, d), gate)


def _ffn_in_kernel(x_ref, xprev_ref, xnext_ref, gain_ref, shift_ref, scale_ref, wg_ref, wu_ref, cw_ref, cb_ref,
                   o_ref, h_ref, *, blocks_per_seq):
    i = pl.program_id(0)
    tm = x_ref.shape[0]
    halo = BF16_SUBLANES

    @pl.when(pl.program_id(1) == 0)
    def _():
        gain, shift, scale = gain_ref[...], shift_ref[...], scale_ref[...]
        seq_block = i % blocks_per_seq
        h_prev = jnp.where(seq_block > 0, _norm_mod(xprev_ref[...], gain, shift, scale), 0.0)
        h_next = jnp.where(seq_block < blocks_per_seq - 1, _norm_mod(xnext_ref[...], gain, shift, scale), 0.0)
        zeros = jnp.zeros((halo - h_prev.shape[0], h_prev.shape[1]), F32)
        h_ref[0:halo, :] = jnp.concatenate([zeros, h_prev], axis=0).astype(BF16)
        h_ref[halo:halo + tm, :] = _norm_mod(x_ref[...], gain, shift, scale).astype(BF16)
        h_ref[halo + tm:, :] = jnp.concatenate([h_next, zeros], axis=0).astype(BF16)

    gate = _dot(h_ref[...], wg_ref[...])
    up = _dot(h_ref[halo:halo + tm, :], wu_ref[...])
    rows = gate.shape[0]
    g_prev = pltpu.roll(gate, 1, axis=0)[halo:halo + tm, :]
    g_next = pltpu.roll(gate, rows - 1, axis=0)[halo:halo + tm, :]
    conv = g_prev * cw_ref[0:1, :] + gate[halo:halo + tm, :] * cw_ref[1:2, :] + g_next * cw_ref[2:3, :] + cb_ref[...]
    o_ref[...] = (_silu(conv) * up).astype(o_ref.dtype)


def _ffn_in_call(x, gain, shift, scale, w, conv_w, conv_b, *, layer, seq_len, name):
    m, d = x.shape
    f = w.shape[2] // 2
    bm = shift.shape[0]
    rows_per_mod = m // bm
    tm = _pick(seq_len, 1024, BF16_SUBLANES)
    tn = _pick(f, 512, HEAD)
    assert rows_per_mod % tm == 0
    mod_blocks = rows_per_mod // tm
    nb = f // tn
    xh = 8
    tiles_per_tm = tm // xh
    last_tile = m // xh - 1
    body = functools.partial(_ffn_in_kernel, blocks_per_seq=seq_len // tm)
    return pl.pallas_call(
        body,
        grid=(m // tm, nb),
        in_specs=[
            pl.BlockSpec((tm, d), lambda i, j: (i, 0)),
            pl.BlockSpec((xh, d), lambda i, j: (jnp.maximum(i * tiles_per_tm - 1, 0), 0)),
            pl.BlockSpec((xh, d), lambda i, j: (jnp.minimum((i + 1) * tiles_per_tm, last_tile), 0)),
            pl.BlockSpec((1, d), lambda i, j: (0, 0)),
            pl.BlockSpec((None, 1, d), lambda i, j: (i // mod_blocks, 0, 0)),
            pl.BlockSpec((None, 1, d), lambda i, j: (i // mod_blocks, 0, 0)),
            pl.BlockSpec((None, d, tn), lambda i, j: (layer, 0, j)),
            pl.BlockSpec((None, d, tn), lambda i, j: (layer, 0, nb + j)),
            pl.BlockSpec((3, tn), lambda i, j: (0, j)),
            pl.BlockSpec((1, tn), lambda i, j: (0, j)),
        ],
        out_specs=pl.BlockSpec((tm, tn), lambda i, j: (i, j)),
        out_shape=jax.ShapeDtypeStruct((m, f), BF16),
        scratch_shapes=[pltpu.VMEM((tm + 2 * BF16_SUBLANES, d), BF16)],
        compiler_params=_params(("arbitrary", "arbitrary")),
        name=name,
    )(x, x, x, gain.reshape(1, d), shift, scale, w, w, conv_w, conv_b.reshape(1, f))


def _gla_gates(f_raw, lb, within):
    half = 0.5 * (1.0 - lb)
    ht = half * jnp.tanh(0.5 * f_raw.astype(F32))
    lf = jnp.log((1.0 - half) + ht)
    ones = jnp.where(within, 1.0, 0.0).astype(BF16)
    lf_hi = lf.astype(BF16)
    lf_lo = (lf - lf_hi.astype(F32)).astype(BF16)
    return half - ht, _dot(ones, lf_hi) + _dot(ones, lf_lo)


def _gla_pairs(qs, kk, b, forward, within):
    c = qs.shape[0]
    a_row = c // 2 - 1 if forward else c // 2
    e_row = c - 1 if forward else 0
    anchor = b[a_row:a_row + 1, :]
    b_end = b[e_row:e_row + 1, :]
    qa = qs * jnp.exp(b - anchor)
    ka = kk * jnp.exp(anchor - b)
    scores = jnp.where(within, _dot_nt(qa.astype(BF16), ka.astype(BF16)), 0.0).astype(BF16)
    return (scores, (qa * jnp.exp(anchor)).astype(BF16), (ka * jnp.exp(b_end - anchor)).astype(BF16),
            jnp.exp(b_end))


def _gla_pairs_exact(qs, kk, b, forward, within, tmp):
    tb_ref, tq_ref, ts_ref = tmp
    c = qs.shape[0]
    b_end = b[c - 1:c, :] if forward else b[0:1, :]
    tb_ref[...] = b
    tq_ref[...] = qs
    ones = jnp.ones((8, qs.shape[1]), BF16)

    def row(t, carry):
        rel = jnp.minimum(tb_ref[pl.ds(t, 1), :] - b, 0.0)
        e = (tq_ref[pl.ds(t, 1), :] * kk) * jnp.exp(rel)
        ts_ref[pl.ds(t, 1), :] = _dot_nt(ones, e.astype(BF16))[0:1, :]
        return carry

    lax.fori_loop(0, c, row, 0)
    scores = jnp.where(within, ts_ref[...], 0.0).astype(BF16)
    return scores, (qs * jnp.exp(b)).astype(BF16), (kk * jnp.exp(b_end - b)).astype(BF16), jnp.exp(b_end)


def _gla_readout(o, g_raw, o_gain):
    ms = jnp.mean(o * o, axis=-1, keepdims=True)
    y = o * lax.rsqrt(ms + EPS) * o_gain
    return (y * _silu(g_raw.astype(F32))).astype(BF16)


def _gla_kernel(qc, ffc, fbc, vc, gc, ql, ffl, fbl, vl, gl, lb_ref, og_ref, yc_ref, yl_ref,
                o_ref, qb_ref, ke_ref, dec_ref, st_ref, tb_ref, tq_ref, ts_ref):
    c = GLA_CHUNK
    tc = qc.shape[0]
    n_ctx = tc // c
    n_lat = ql.shape[0] // c
    lb = (lb_ref[0], lb_ref[1])
    row = lax.broadcasted_iota(jnp.int32, (c, c), 0)
    col = lax.broadcasted_iota(jnp.int32, (c, c), 1)
    within = (col <= row, col >= row)

    def rows(start):
        return pl.ds(start if isinstance(start, int) else pl.multiple_of(start, c), c)

    def local(refs, src0, dst0, chunk0, n, exact):
        q, ff, fb, v = refs
        src = [rows(src0 + j * c) for j in range(n)]
        dst = [rows(dst0 + j * c) for j in range(n)]
        qs = [_silu(q[src[j], :].astype(F32)) * (HEAD ** -0.5) for j in range(n)]
        gates = [[_gla_gates(f[src[j], :], lb[d], within[d]) for d, f in enumerate((ff, fb))] for j in range(n)]
        scores = []
        for j in range(n):
            per_dir = []
            for d in range(2):
                kk, b = gates[j][d]
                if exact:
                    sc, qb, ke, dec = _gla_pairs_exact(qs[j], kk, b, d == 0, within[d], (tb_ref, tq_ref, ts_ref))
                else:
                    sc, qb, ke, dec = _gla_pairs(qs[j], kk, b, d == 0, within[d])
                qb_ref[d, dst[j], :] = qb
                ke_ref[d, dst[j], :] = ke
                dec_ref[d, chunk0 + j] = dec
                per_dir.append(sc)
            scores.append(per_dir)
        for j in range(n):
            vj = v[src[j], :]
            o_ref[dst[j], :] = _dot(scores[j][0], vj) + _dot(scores[j][1], vj)

    def phase1(exact):
        run = 1 if exact else math.gcd(n_lat, GLA_RUN)
        run_ctx = math.gcd(n_ctx, run)
        for i in range(0, n_ctx, run_ctx):
            local((qc, ffc, fbc, vc), i * c, i * c, i, run_ctx, exact)

        def body(i, carry):
            src0 = pl.multiple_of(i * (run * c), c)
            local((ql, ffl, fbl, vl), src0, tc + src0, n_ctx + i * run, run, exact)
            return carry

        lax.fori_loop(0, n_lat // run, body, 0)

    safe = jnp.min(jnp.minimum(lb[0], lb[1])) >= GLA_SAFE_LB

    @pl.when(safe)
    def _():
        phase1(False)

    @pl.when(jnp.logical_not(safe))
    def _():
        phase1(True)

    st_ref[...] = jnp.zeros_like(st_ref)

    def step(d, v, src_rows, dst_rows, chunk):
        st = st_ref[d]
        o_ref[dst_rows, :] += _dot_nt(qb_ref[d, dst_rows, :], st.astype(BF16))
        st_ref[d] = st * dec_ref[d, chunk] + _dot_tn(v[src_rows, :], ke_ref[d, dst_rows, :])

    for i in range(n_ctx):
        for d, j in ((0, i), (1, n_ctx - 1 - i)):
            r = slice(j * c, (j + 1) * c)
            step(d, vc, r, r, j)

    def step_body(i, carry):
        for d, j in ((0, i), (1, n_lat - 1 - i)):
            src = pl.ds(pl.multiple_of(j * c, c), c)
            dst = pl.ds(pl.multiple_of(tc + j * c, c), c)
            step(d, vl, src, dst, n_ctx + j)
        return carry

    lax.fori_loop(0, n_lat, step_body, 0, unroll=math.gcd(n_lat, 8))

    o_gain = og_ref[...]
    for i in range(n_ctx):
        r = slice(i * c, (i + 1) * c)
        yc_ref[r, :] = _gla_readout(o_ref[r, :], gc[r, :], o_gain)

    def out_body(i, carry):
        src = pl.ds(pl.multiple_of(i * c, c), c)
        dst = pl.ds(pl.multiple_of(tc + i * c, c), c)
        yl_ref[src, :] = _gla_readout(o_ref[dst, :], gl[src, :], o_gain)
        return carry

    lax.fori_loop(0, n_lat, out_body, 0, unroll=math.gcd(n_lat, 4))


def _gla_call(p_ctx, p_lat, lower_bound, o_gain):
    b, tc, d5 = p_ctx.shape
    t = p_lat.shape[1]
    d = d5 // 5
    heads = d // HEAD
    assert tc % GLA_CHUNK == 0 and t % GLA_CHUNK == 0

    def col(tt, part):
        return pl.BlockSpec((None, tt, HEAD), lambda bi, h: (bi, 0, part * heads + h))

    out_spec = lambda tt: pl.BlockSpec((None, tt, HEAD), lambda bi, h: (bi, 0, h))
    return pl.pallas_call(
        _gla_kernel,
        grid=(b, heads),
        in_specs=[col(tc, p) for p in range(5)] + [col(t, p) for p in range(5)] + [
            pl.BlockSpec((2, None, 1, HEAD), lambda bi, h: (0, h, 0, 0)),
            pl.BlockSpec((1, HEAD), lambda bi, h: (0, 0)),
        ],
        out_specs=[out_spec(tc), out_spec(t)],
        out_shape=[jax.ShapeDtypeStruct((b, tc, d), BF16), jax.ShapeDtypeStruct((b, t, d), BF16)],
        scratch_shapes=[
            pltpu.VMEM((tc + t, HEAD), F32),
            pltpu.VMEM((2, tc + t, HEAD), BF16),
            pltpu.VMEM((2, tc + t, HEAD), BF16),
            pltpu.VMEM((2, (tc + t) // GLA_CHUNK, 1, HEAD), F32),
            pltpu.VMEM((2, HEAD, HEAD), F32),
            pltpu.VMEM((GLA_CHUNK, HEAD), F32),
            pltpu.VMEM((GLA_CHUNK, HEAD), F32),
            pltpu.VMEM((GLA_CHUNK, GLA_CHUNK), F32),
        ],
        compiler_params=_params(("arbitrary", "arbitrary")),
        name="hgrn_scan",
    )(*([p_ctx] * 5), *([p_lat] * 5), lower_bound.reshape(2, heads, 1, HEAD), o_gain.reshape(1, HEAD))


def _flash_kernel(q_ref, kc_ref, vc_ref, kl_ref, vl_ref, o_ref, *, tk, unroll):
    tq = q_ref.shape[0]
    group = q_ref.shape[1] // HEAD
    qs = [q_ref[:, g * HEAD:(g + 1) * HEAD] for g in range(group)]

    def attend(carry, k, v):
        v_ext = jnp.concatenate([v, jnp.ones_like(v)], axis=1)
        out = []
        for g in range(group):
            m, acc = carry[g]
            s = _dot_nt(qs[g], k)
            m_new = jnp.maximum(m, jnp.max(s, axis=-1, keepdims=True))
            p = jnp.exp2(s - m_new).astype(BF16)
            out.append((m_new, jnp.exp2(m - m_new) * acc + _dot(p, v_ext)))
        return tuple(out)

    def body(j, carry):
        r = pl.ds(pl.multiple_of(j * tk, tk), tk)
        return attend(carry, kl_ref[r, :], vl_ref[r, :])

    init = tuple((jnp.full((tq, 1), -jnp.inf, F32), jnp.zeros((tq, 2 * HEAD), F32)) for _ in range(group))
    carry = attend(init, kc_ref[...], vc_ref[...])
    final = lax.fori_loop(0, kl_ref.shape[0] // tk, body, carry, unroll=unroll)
    for g in range(group):
        acc = final[g][1]
        o_ref[:, g * HEAD:(g + 1) * HEAD] = (acc[:, :HEAD] / acc[:, HEAD:]).astype(o_ref.dtype)


def _flash_call(qkv, kv_ctx, *, heads):
    b, t, n = qkv.shape
    tc = kv_ctx.shape[1]
    kvh = (n // HEAD - heads) // 2
    group = heads // kvh
    tq = _pick(t, 512, BF16_SUBLANES)
    tk = _pick(t, 1024, 2 * HEAD)
    body = functools.partial(_flash_kernel, tk=tk, unroll=math.gcd(t // tk, 8))
    return pl.pallas_call(
        body,
        grid=(b, kvh, t // tq),
        in_specs=[
            pl.BlockSpec((None, tq, group * HEAD), lambda bi, h, i: (bi, i, h)),
            pl.BlockSpec((None, tc, HEAD), lambda bi, h, i: (bi, 0, h)),
            pl.BlockSpec((None, tc, HEAD), lambda bi, h, i: (bi, 0, kvh + h)),
            pl.BlockSpec((None, t, HEAD), lambda bi, h, i: (bi, 0, heads + h)),
            pl.BlockSpec((None, t, HEAD), lambda bi, h, i: (bi, 0, heads + kvh + h)),
        ],
        out_specs=pl.BlockSpec((None, tq, group * HEAD), lambda bi, h, i: (bi, i, h)),
        out_shape=jax.ShapeDtypeStruct((b, t, heads * HEAD), BF16),
        compiler_params=_params(("arbitrary", "arbitrary", "arbitrary")),
        name="gqa_flash",
    )(qkv, kv_ctx, kv_ctx, qkv, qkv)


def _rope_tables(t):
    pos = jnp.arange(t, dtype=jnp.int32)
    rows = (pos // GRID_W).astype(F32)
    cols = (pos % GRID_W).astype(F32)
    axis_dim = HEAD // 2
    inv_freq = ROPE_THETA ** (-jnp.arange(0, axis_dim, 2, dtype=F32) / axis_dim)
    ang = jnp.concatenate([rows[:, None] * inv_freq, cols[:, None] * inv_freq], axis=-1)
    cos, sin = jnp.cos(ang), jnp.sin(ang)
    return jnp.concatenate([cos, cos], axis=-1), jnp.concatenate([-sin, sin], axis=-1)


def _split_mods(mods, b):
    d = mods.shape[1] // N_MOD
    lat = [mods[:b, i * d:(i + 1) * d].reshape(b, 1, d) for i in range(N_MOD)]
    ctx = [mods[b:b + 1, i * d:(i + 1) * d].reshape(1, 1, d) for i in range(N_MOD)]
    return lat, ctx


def kernel(x, c, ctx, c_ctx, ada_w, ada_b, norm_mix_pre, norm_mix_post, norm_ffn_pre, norm_ffn_post, hgrn_w_in, hgrn_lb_logits, hgrn_o_norm, hgrn_w_out, attn_w_qkv, attn_q_norm, attn_k_norm, attn_w_out, ffn_w_in, ffn_conv_w, ffn_conv_b, ffn_w_out):
    b, t, d = x.shape
    tc = ctx.shape[1]
    heads = d // HEAD
    assert ada_w.shape[0] == 2 and b + 1 <= ADA_ROWS

    cond = jnp.zeros((ADA_ROWS, d), F32).at[:b].set(c).at[b].set(c_ctx)
    mods = _ada_call(cond, ada_w, ada_b)
    x_lat = x.reshape(b * t, d)
    x_ctx = ctx.reshape(b * tc, d)

    (sh_ml, sc_ml, gt_ml, sh_fl, sc_fl, gt_fl), (sh_mc, sc_mc, gt_mc, sh_fc, sc_fc, gt_fc) = _split_mods(mods[0], b)
    lower_bound = jnp.cumsum(jax.nn.softmax(hgrn_lb_logits.astype(F32), axis=1), axis=1)[:, 0]
    hgrn_w_in, hgrn_w_out, attn_w_out = hgrn_w_in.astype(BF16), hgrn_w_out.astype(BF16), attn_w_out.astype(BF16)
    ffn_w_in, ffn_w_out = ffn_w_in.astype(BF16), ffn_w_out.astype(BF16)
    p_lat = _nmm_call(x_lat, norm_mix_pre[0], sh_ml, sc_ml, hgrn_w_in, name="hgrn_in_lat")
    p_ctx = _nmm_call(x_ctx, norm_mix_pre[0], sh_mc, sc_mc, hgrn_w_in, name="hgrn_in_ctx")
    y_ctx, y_lat = _gla_call(p_ctx.reshape(b, tc, 5 * d), p_lat.reshape(b, t, 5 * d), lower_bound, hgrn_o_norm[0])
    x_lat = _proj_res_call(y_lat.reshape(b * t, d), hgrn_w_out, x_lat, norm_mix_post[0], gt_ml, name="hgrn_out_lat")
    x_ctx = _proj_res_call(y_ctx.reshape(b * tc, d), hgrn_w_out, x_ctx, norm_mix_post[0], gt_mc, name="hgrn_out_ctx")
    a = _ffn_in_call(x_lat, norm_ffn_pre[0], sh_fl, sc_fl, ffn_w_in, ffn_conv_w[0], ffn_conv_b[0],
                     layer=0, seq_len=t, name="ffn0_in_lat")
    x_lat = _proj_res_call(a, ffn_w_out, x_lat, norm_ffn_post[0], gt_fl, layer=0, name="ffn0_out_lat")
    a = _ffn_in_call(x_ctx, norm_ffn_pre[0], sh_fc, sc_fc, ffn_w_in, ffn_conv_w[0], ffn_conv_b[0],
                     layer=0, seq_len=tc, name="ffn0_in_ctx")
    x_ctx = _proj_res_call(a, ffn_w_out, x_ctx, norm_ffn_post[0], gt_fc, layer=0, name="ffn0_out_ctx")

    (sh_ml, sc_ml, gt_ml, sh_fl, sc_fl, gt_fl), (sh_mc, sc_mc, _, _, _, _) = _split_mods(mods[1], b)
    n_qkv = attn_w_qkv.shape[2]
    kvh = (n_qkv // HEAD - heads) // 2
    perm = jnp.concatenate([jnp.arange(0, HEAD, 2), jnp.arange(1, HEAD, 2)])
    head_perm = (jnp.arange(heads + kvh)[:, None] * HEAD + perm[None, :]).reshape(-1)
    col_perm = jnp.concatenate([head_perm, jnp.arange((heads + kvh) * HEAD, n_qkv)])
    w_qkv = attn_w_qkv[:, :, col_perm].astype(BF16)
    head_gain = jnp.concatenate([
        jnp.tile(attn_q_norm[0][perm] * (LOG2_E * HEAD ** -0.5), heads),
        jnp.tile(attn_k_norm[0][perm], kvh),
        jnp.ones((kvh * HEAD,), F32),
    ]).reshape(1, n_qkv)
    cos, sin = _rope_tables(t)
    qk_cols = (heads + kvh) * HEAD
    qkv = _nmm_call(x_lat, norm_mix_pre[1], sh_ml, sc_ml, w_qkv,
                    rope=(head_gain, cos, sin, qk_cols), name="attn_qkv_lat")
    q_cols = heads * HEAD
    kv_ctx = _nmm_call(x_ctx, norm_mix_pre[1], sh_mc, sc_mc, w_qkv, col0=q_cols,
                       rope=(head_gain[:, q_cols:], jnp.ones((tc, HEAD), F32), jnp.zeros((tc, HEAD), F32),
                             kvh * HEAD), name="attn_kv_ctx")
    qkv = qkv.reshape(b, t, n_qkv)
    kv_ctx = kv_ctx.reshape(b, tc, 2 * kvh * HEAD)
    o = _flash_call(qkv, kv_ctx, heads=heads)
    x_lat = _proj_res_call(o.reshape(b * t, d), attn_w_out, x_lat, norm_mix_post[1], gt_ml, name="attn_out_lat")
    a = _ffn_in_call(x_lat, norm_ffn_pre[1], sh_fl, sc_fl, ffn_w_in, ffn_conv_w[1], ffn_conv_b[1],
                     layer=1, seq_len=t, name="ffn1_in_lat")
    x_lat = _proj_res_call(a, ffn_w_out, x_lat, norm_ffn_post[1], gt_fl, layer=1, name="ffn1_out_lat")
    return x_lat.reshape(b, t, d)
```

```python
import functools
import math

import jax
import jax.numpy as jnp
from jax import lax
from jax.experimental import pallas as pl
from jax.experimental.pallas import tpu as pltpu

F32 = jnp.float32
BF16 = jnp.bfloat16

EPS = 1e-6
GRID_W = 64
ROPE_THETA = 10000.0
LOG2_E = math.log2(math.e)
N_MOD = 6
HEAD = 128
ROW_GROUP = 16
GLA_CHUNK = 128
GLA_RUN = 8
GLA_SAFE_LB = math.exp(-80.0 / (GLA_CHUNK // 2))
BF16_SUBLANES = 16
ADA_ROWS = 8
V7X_VMEM_BYTES = 64 * 1024 * 1024
VMEM_LIMIT = V7X_VMEM_BYTES - 6 * 1024 * 1024


def _pick(n, target, align):
    if n <= target:
        return n
    for t in range(target - target % align, 0, -align):
        if n % t == 0:
            return t
    return n


def _params(sem):
    return pltpu.CompilerParams(dimension_semantics=sem, vmem_limit_bytes=VMEM_LIMIT)


def _sigmoid(x):
    return 0.5 * jnp.tanh(0.5 * x) + 0.5


def _silu(x):
    return x * _sigmoid(x)


def _dot(a, b):
    return jnp.dot(a, b, preferred_element_type=F32)


def _dot_nt(a, b):
    return lax.dot_general(a, b, (((1,), (1,)), ((), ())), preferred_element_type=F32)


def _dot_tn(a, b):
    return lax.dot_general(a, b, (((0,), (0,)), ((), ())), preferred_element_type=F32)


def _ada_kernel(c_ref, w_ref, b_ref, o_ref):
    sc = _silu(c_ref[...]).astype(BF16)
    o_ref[...] = _dot(sc, w_ref[...].astype(BF16)) + b_ref[...]


def _ada_call(cond, ada_w, ada_b):
    depth, d, n = ada_w.shape
    tn = _pick(n, 1024, HEAD)
    return pl.pallas_call(
        _ada_kernel,
        grid=(depth, n // tn),
        in_specs=[
            pl.BlockSpec((ADA_ROWS, d), lambda l, j: (0, 0)),
            pl.BlockSpec((None, d, tn), lambda l, j: (l, 0, j)),
            pl.BlockSpec((None, 1, tn), lambda l, j: (l, 0, j)),
        ],
        out_specs=pl.BlockSpec((None, ADA_ROWS, tn), lambda l, j: (l, 0, j)),
        out_shape=jax.ShapeDtypeStruct((depth, ADA_ROWS, n), F32),
        compiler_params=_params(("arbitrary", "arbitrary")),
        name="ada_ln",
    )(cond, ada_w, ada_b.reshape(depth, 1, n))


def _norm_mod(x, gain, shift, scale):
    ms = jnp.mean(x * x, axis=-1, keepdims=True)
    return (x * lax.rsqrt(ms + EPS) * gain) * (1.0 + scale) + shift


def _norm_mod_rows(x_ref, h_ref, h_row0, gain, shift, scale):
    mult = gain * (1.0 + scale)

    def body(r, carry):
        start = pl.multiple_of(r * ROW_GROUP, ROW_GROUP)
        x = x_ref[pl.ds(start, ROW_GROUP), :]
        ms = jnp.mean(x * x, axis=-1, keepdims=True)
        dst = pl.ds(pl.multiple_of(h_row0 + start, ROW_GROUP), ROW_GROUP)
        h_ref[dst, :] = (x * lax.rsqrt(ms + EPS) * mult + shift).astype(BF16)
        return carry

    groups = x_ref.shape[0] // ROW_GROUP
    lax.fori_loop(0, groups, body, 0, unroll=math.gcd(groups, 4))


def _nmm_kernel(x_ref, gain_ref, shift_ref, scale_ref, w_ref, o_ref, h_ref):
    @pl.when(pl.program_id(1) == 0)
    def _():
        _norm_mod_rows(x_ref, h_ref, 0, gain_ref[...], shift_ref[...], scale_ref[...])

    o_ref[...] = _dot(h_ref[...], w_ref[...]).astype(o_ref.dtype)


def _nmm_rope_kernel(x_ref, gain_ref, shift_ref, scale_ref, w_ref, hg_ref, cos_ref, sin_ref,
                     o_ref, h_ref, *, n_normed_blocks):
    j = pl.program_id(1)

    @pl.when(j == 0)
    def _():
        _norm_mod_rows(x_ref, h_ref, 0, gain_ref[...], shift_ref[...], scale_ref[...])

    acc = _dot(h_ref[...], w_ref[...])

    @pl.when(j < n_normed_blocks)
    def _():
        cos = cos_ref[...]
        sin = sin_ref[...]
        for hh in range(acc.shape[1] // HEAD):
            cols = slice(hh * HEAD, (hh + 1) * HEAD)
            a = acc[:, cols]
            ms = jnp.mean(a * a, axis=-1, keepdims=True)
            a = a * lax.rsqrt(ms + EPS) * hg_ref[:, cols]
            a = a * cos + pltpu.roll(a, HEAD // 2, axis=1) * sin
            o_ref[:, cols] = a.astype(o_ref.dtype)

    @pl.when(j >= n_normed_blocks)
    def _():
        o_ref[...] = acc.astype(o_ref.dtype)


def _nmm_call(x, gain, shift, scale, w, *, layer=0, col0=0, rope=None, name):
    m, d = x.shape
    n = w.shape[2] - col0
    bm = shift.shape[0]
    rows_per_mod = m // bm
    row_period = rows_per_mod if rope is None else math.gcd(rows_per_mod, rope[1].shape[0])
    tm = _pick(row_period, 1024, BF16_SUBLANES)
    tn = _pick(n, 1024, HEAD) if rope is None else _pick(math.gcd(n, rope[3]), 512, HEAD)
    mod_blocks = rows_per_mod // tm
    assert col0 % tn == 0
    col_block0 = col0 // tn
    in_specs = [
        pl.BlockSpec((tm, d), lambda i, j: (i, 0)),
        pl.BlockSpec((1, d), lambda i, j: (0, 0)),
        pl.BlockSpec((None, 1, d), lambda i, j: (i // mod_blocks, 0, 0)),
        pl.BlockSpec((None, 1, d), lambda i, j: (i // mod_blocks, 0, 0)),
        pl.BlockSpec((None, d, tn), lambda i, j: (layer, 0, col_block0 + j)),
    ]
    args = [x, gain.reshape(1, d), shift, scale, w]
    if rope is None:
        body = _nmm_kernel
    else:
        head_gain, cos, sin, n_normed_cols = rope
        assert n_normed_cols % tn == 0 and cos.shape[0] % tm == 0
        table_blocks = cos.shape[0] // tm
        in_specs += [
            pl.BlockSpec((1, tn), lambda i, j: (0, j)),
            pl.BlockSpec((tm, HEAD), lambda i, j: (i % table_blocks, 0)),
            pl.BlockSpec((tm, HEAD), lambda i, j: (i % table_blocks, 0)),
        ]
        args += [head_gain, cos, sin]
        body = functools.partial(_nmm_rope_kernel, n_normed_blocks=n_normed_cols // tn)
    return pl.pallas_call(
        body,
        grid=(m // tm, n // tn),
        in_specs=in_specs,
        out_specs=pl.BlockSpec((tm, tn), lambda i, j: (i, j)),
        out_shape=jax.ShapeDtypeStruct((m, n), BF16),
        scratch_shapes=[pltpu.VMEM((tm, d), BF16)],
        compiler_params=_params(("arbitrary", "arbitrary")),
        name=name,
    )(*args)


def _residual_rows(x_ref, o_ref, gain, gate):
    mult = gain * gate

    def body(r, carry):
        rows = pl.ds(pl.multiple_of(r * ROW_GROUP, ROW_GROUP), ROW_GROUP)
        y = o_ref[rows, :]
        ms = jnp.mean(y * y, axis=-1, keepdims=True)
        o_ref[rows, :] = x_ref[rows, :] + y * lax.rsqrt(ms + EPS) * mult
        return carry

    groups = o_ref.shape[0] // ROW_GROUP
    lax.fori_loop(0, groups, body, 0, unroll=math.gcd(groups, 4))


def _proj_res_kernel(a_ref, w_ref, x_ref, gain_ref, gate_ref, o_ref):
    o_ref[...] = _dot(a_ref[...], w_ref[...])
    _residual_rows(x_ref, o_ref, gain_ref[...], gate_ref[...])


def _proj_res_ktiled_kernel(a_ref, w_ref, x_ref, gain_ref, gate_ref, o_ref):
    k = pl.program_id(1)

    @pl.when(k == 0)
    def _():
        o_ref[...] = jnp.zeros_like(o_ref)

    o_ref[...] += _dot(a_ref[...], w_ref[...])

    @pl.when(k == pl.num_programs(1) - 1)
    def _():
        _residual_rows(x_ref, o_ref, gain_ref[...], gate_ref[...])


def _proj_res_call(a, w, x, gain, gate, *, layer=0, name):
    m, k = a.shape
    d = w.shape[2]
    bm = gate.shape[0]
    rows_per_mod = m // bm
    tk = k if k <= d else _pick(k, 512, 2 * HEAD)
    kb = k // tk
    tm = _pick(rows_per_mod, 512 if kb == 1 else 1024, BF16_SUBLANES)
    mod_blocks = rows_per_mod // tm
    return pl.pallas_call(
        _proj_res_kernel if kb == 1 else _proj_res_ktiled_kernel,
        grid=(m // tm, kb),
        in_specs=[
            pl.BlockSpec((tm, tk), lambda i, kk: (i, kk)),
            pl.BlockSpec((None, tk, d), lambda i, kk: (layer, kk, 0)),
            pl.BlockSpec((tm, d), lambda i, kk: (i, 0)),
            pl.BlockSpec((1, d), lambda i, kk: (0, 0)),
            pl.BlockSpec((None, 1, d), lambda i, kk: (i // mod_blocks, 0, 0)),
        ],
        out_specs=pl.BlockSpec((tm, d), lambda i, kk: (i, 0)),
        out_shape=jax.ShapeDtypeStruct((m, d), F32),
        compiler_params=_params(("arbitrary", "arbitrary")),
        name=name,
    )(a, w, x, gain.reshape(1, d), gate)


def _ffn_in_kernel(x_ref, xprev_ref, xnext_ref, gain_ref, shift_ref, scale_ref, wg_ref, wu_ref, cw_ref, cb_ref,
                   o_ref, h_ref, *, blocks_per_seq):
    i = pl.program_id(0)
    tm = x_ref.shape[0]
    halo = BF16_SUBLANES

    @pl.when(pl.program_id(1) == 0)
    def _():
        gain, shift, scale = gain_ref[...], shift_ref[...], scale_ref[...]
        seq_block = i % blocks_per_seq
        h_prev = jnp.where(seq_block > 0, _norm_mod(xprev_ref[...], gain, shift, scale), 0.0)
        h_next = jnp.where(seq_block < blocks_per_seq - 1, _norm_mod(xnext_ref[...], gain, shift, scale), 0.0)
        zeros = jnp.zeros((halo - h_prev.shape[0], h_prev.shape[1]), F32)
        h_ref[0:halo, :] = jnp.concatenate([zeros, h_prev], axis=0).astype(BF16)
        _norm_mod_rows(x_ref, h_ref, halo, gain, shift, scale)
        h_ref[halo + tm:, :] = jnp.concatenate([h_next, zeros], axis=0).astype(BF16)

    gate = _dot(h_ref[...], wg_ref[...])
    up = _dot(h_ref[halo:halo + tm, :], wu_ref[...])
    rows = gate.shape[0]
    g_prev = pltpu.roll(gate, 1, axis=0)[halo:halo + tm, :]
    g_next = pltpu.roll(gate, rows - 1, axis=0)[halo:halo + tm, :]
    conv = g_prev * cw_ref[0:1, :] + gate[halo:halo + tm, :] * cw_ref[1:2, :] + g_next * cw_ref[2:3, :] + cb_ref[...]
    o_ref[...] = (_silu(conv) * up).astype(o_ref.dtype)


def _ffn_in_call(x, gain, shift, scale, w, conv_w, conv_b, *, layer, seq_len, name):
    m, d = x.shape
    f = w.shape[2] // 2
    bm = shift.shape[0]
    rows_per_mod = m // bm
    tm = _pick(seq_len, 1024, BF16_SUBLANES)
    tn = _pick(f, 512, HEAD)
    assert rows_per_mod % tm == 0
    mod_blocks = rows_per_mod // tm
    nb = f // tn
    xh = 8
    tiles_per_tm = tm // xh
    last_tile = m // xh - 1
    body = functools.partial(_ffn_in_kernel, blocks_per_seq=seq_len // tm)
    return pl.pallas_call(
        body,
        grid=(m // tm, nb),
        in_specs=[
            pl.BlockSpec((tm, d), lambda i, j: (i, 0)),
            pl.BlockSpec((xh, d), lambda i, j: (jnp.maximum(i * tiles_per_tm - 1, 0), 0)),
            pl.BlockSpec((xh, d), lambda i, j: (jnp.minimum((i + 1) * tiles_per_tm, last_tile), 0)),
            pl.BlockSpec((1, d), lambda i, j: (0, 0)),
            pl.BlockSpec((None, 1, d), lambda i, j: (i // mod_blocks, 0, 0)),
            pl.BlockSpec((None, 1, d), lambda i, j: (i // mod_blocks, 0, 0)),
            pl.BlockSpec((None, d, tn), lambda i, j: (layer, 0, j)),
            pl.BlockSpec((None, d, tn), lambda i, j: (layer, 0, nb + j)),
            pl.BlockSpec((3, tn), lambda i, j: (0, j)),
            pl.BlockSpec((1, tn), lambda i, j: (0, j)),
        ],
        out_specs=pl.BlockSpec((tm, tn), lambda i, j: (i, j)),
        out_shape=jax.ShapeDtypeStruct((m, f), BF16),
        scratch_shapes=[pltpu.VMEM((tm + 2 * BF16_SUBLANES, d), BF16)],
        compiler_params=_params(("arbitrary", "arbitrary")),
        name=name,
    )(x, x, x, gain.reshape(1, d), shift, scale, w, w, conv_w, conv_b.reshape(1, f))


def _gla_gates(f_raw, lb, within):
    half = 0.5 * (1.0 - lb)
    ht = half * jnp.tanh(0.5 * f_raw.astype(F32))
    lf = jnp.log((1.0 - half) + ht)
    ones = jnp.where(within, 1.0, 0.0).astype(BF16)
    lf_hi = lf.astype(BF16)
    lf_lo = (lf - lf_hi.astype(F32)).astype(BF16)
    return half - ht, _dot(ones, lf_hi) + _dot(ones, lf_lo)


def _gla_pairs(qs, kk, b, forward, within):
    c = qs.shape[0]
    a_row = c // 2 - 1 if forward else c // 2
    e_row = c - 1 if forward else 0
    anchor = b[a_row:a_row + 1, :]
    b_end = b[e_row:e_row + 1, :]
    qa = qs * jnp.exp(b - anchor)
    ka = kk * jnp.exp(anchor - b)
    scores = jnp.where(within, _dot_nt(qa.astype(BF16), ka.astype(BF16)), 0.0).astype(BF16)
    return (scores, (qa * jnp.exp(anchor)).astype(BF16), (ka * jnp.exp(b_end - anchor)).astype(BF16),
            jnp.exp(b_end))


def _gla_pairs_exact(qs, kk, b, forward, within, tmp):
    tb_ref, tq_ref, ts_ref = tmp
    c = qs.shape[0]
    b_end = b[c - 1:c, :] if forward else b[0:1, :]
    tb_ref[...] = b
    tq_ref[...] = qs
    ones = jnp.ones((8, qs.shape[1]), BF16)

    def row(t, carry):
        rel = jnp.minimum(tb_ref[pl.ds(t, 1), :] - b, 0.0)
        e = (tq_ref[pl.ds(t, 1), :] * kk) * jnp.exp(rel)
        ts_ref[pl.ds(t, 1), :] = _dot_nt(ones, e.astype(BF16))[0:1, :]
        return carry

    lax.fori_loop(0, c, row, 0)
    scores = jnp.where(within, ts_ref[...], 0.0).astype(BF16)
    return scores, (qs * jnp.exp(b)).astype(BF16), (kk * jnp.exp(b_end - b)).astype(BF16), jnp.exp(b_end)


def _gla_readout(o, g_raw, o_gain):
    ms = jnp.mean(o * o, axis=-1, keepdims=True)
    y = o * lax.rsqrt(ms + EPS) * o_gain
    return (y * _silu(g_raw.astype(F32))).astype(BF16)


def _gla_kernel(qc, ffc, fbc, vc, gc, ql, ffl, fbl, vl, gl, lb_ref, og_ref, yc_ref, yl_ref,
                o_ref, qb_ref, ke_ref, dec_ref, st_ref, tb_ref, tq_ref, ts_ref):
    c = GLA_CHUNK
    tc = qc.shape[0]
    n_ctx = tc // c
    n_lat = ql.shape[0] // c
    lb = (lb_ref[0], lb_ref[1])
    row = lax.broadcasted_iota(jnp.int32, (c, c), 0)
    col = lax.broadcasted_iota(jnp.int32, (c, c), 1)
    within = (col <= row, col >= row)

    def rows(start):
        return pl.ds(start if isinstance(start, int) else pl.multiple_of(start, c), c)

    def local(refs, src0, dst0, chunk0, n, exact):
        q, ff, fb, v = refs
        src = [rows(src0 + j * c) for j in range(n)]
        dst = [rows(dst0 + j * c) for j in range(n)]
        qs = [_silu(q[src[j], :].astype(F32)) * (HEAD ** -0.5) for j in range(n)]
        gates = [[_gla_gates(f[src[j], :], lb[d], within[d]) for d, f in enumerate((ff, fb))] for j in range(n)]
        scores = []
        for j in range(n):
            per_dir = []
            for d in range(2):
                kk, b = gates[j][d]
                if exact:
                    sc, qb, ke, dec = _gla_pairs_exact(qs[j], kk, b, d == 0, within[d], (tb_ref, tq_ref, ts_ref))
                else:
                    sc, qb, ke, dec = _gla_pairs(qs[j], kk, b, d == 0, within[d])
                qb_ref[d, dst[j], :] = qb
                ke_ref[d, dst[j], :] = ke
                dec_ref[d, chunk0 + j] = dec
                per_dir.append(sc)
            scores.append(per_dir)
        for j in range(n):
            vj = v[src[j], :]
            o_ref[dst[j], :] = _dot(scores[j][0], vj) + _dot(scores[j][1], vj)

    def phase1(exact):
        run = 1 if exact else math.gcd(n_lat, GLA_RUN)
        run_ctx = math.gcd(n_ctx, run)
        for i in range(0, n_ctx, run_ctx):
            local((qc, ffc, fbc, vc), i * c, i * c, i, run_ctx, exact)

        def body(i, carry):
            src0 = pl.multiple_of(i * (run * c), c)
            local((ql, ffl, fbl, vl), src0, tc + src0, n_ctx + i * run, run, exact)
            return carry

        lax.fori_loop(0, n_lat // run, body, 0)

    safe = jnp.min(jnp.minimum(lb[0], lb[1])) >= GLA_SAFE_LB

    @pl.when(safe)
    def _():
        phase1(False)

    @pl.when(jnp.logical_not(safe))
    def _():
        phase1(True)

    st_ref[...] = jnp.zeros_like(st_ref)

    def step(d, v, src_rows, dst_rows, chunk):
        st = st_ref[d]
        o_ref[dst_rows, :] += _dot_nt(qb_ref[d, dst_rows, :], st.astype(BF16))
        st_ref[d] = st * dec_ref[d, chunk] + _dot_tn(v[src_rows, :], ke_ref[d, dst_rows, :])

    for i in range(n_ctx):
        for d, j in ((0, i), (1, n_ctx - 1 - i)):
            r = slice(j * c, (j + 1) * c)
            step(d, vc, r, r, j)

    def step_body(i, carry):
        for d, j in ((0, i), (1, n_lat - 1 - i)):
            src = pl.ds(pl.multiple_of(j * c, c), c)
            dst = pl.ds(pl.multiple_of(tc + j * c, c), c)
            step(d, vl, src, dst, n_ctx + j)
        return carry

    lax.fori_loop(0, n_lat, step_body, 0, unroll=math.gcd(n_lat, 8))

    o_gain = og_ref[...]
    for i in range(n_ctx):
        r = slice(i * c, (i + 1) * c)
        yc_ref[r, :] = _gla_readout(o_ref[r, :], gc[r, :], o_gain)

    def out_body(i, carry):
        src = pl.ds(pl.multiple_of(i * c, c), c)
        dst = pl.ds(pl.multiple_of(tc + i * c, c), c)
        yl_ref[src, :] = _gla_readout(o_ref[dst, :], gl[src, :], o_gain)
        return carry

    lax.fori_loop(0, n_lat, out_body, 0, unroll=math.gcd(n_lat, 4))


def _gla_call(p_ctx, p_lat, lower_bound, o_gain):
    b, tc, d5 = p_ctx.shape
    t = p_lat.shape[1]
    d = d5 // 5
    heads = d // HEAD
    assert tc % GLA_CHUNK == 0 and t % GLA_CHUNK == 0

    def col(tt, part):
        return pl.BlockSpec((None, tt, HEAD), lambda bi, h: (bi, 0, part * heads + h))

    out_spec = lambda tt: pl.BlockSpec((None, tt, HEAD), lambda bi, h: (bi, 0, h))
    return pl.pallas_call(
        _gla_kernel,
        grid=(b, heads),
        in_specs=[col(tc, p) for p in range(5)] + [col(t, p) for p in range(5)] + [
            pl.BlockSpec((2, None, 1, HEAD), lambda bi, h: (0, h, 0, 0)),
            pl.BlockSpec((1, HEAD), lambda bi, h: (0, 0)),
        ],
        out_specs=[out_spec(tc), out_spec(t)],
        out_shape=[jax.ShapeDtypeStruct((b, tc, d), BF16), jax.ShapeDtypeStruct((b, t, d), BF16)],
        scratch_shapes=[
            pltpu.VMEM((tc + t, HEAD), F32),
            pltpu.VMEM((2, tc + t, HEAD), BF16),
            pltpu.VMEM((2, tc + t, HEAD), BF16),
            pltpu.VMEM((2, (tc + t) // GLA_CHUNK, 1, HEAD), F32),
            pltpu.VMEM((2, HEAD, HEAD), F32),
            pltpu.VMEM((GLA_CHUNK, HEAD), F32),
            pltpu.VMEM((GLA_CHUNK, HEAD), F32),
            pltpu.VMEM((GLA_CHUNK, GLA_CHUNK), F32),
        ],
        compiler_params=_params(("arbitrary", "arbitrary")),
        name="hgrn_scan",
    )(*([p_ctx] * 5), *([p_lat] * 5), lower_bound.reshape(2, heads, 1, HEAD), o_gain.reshape(1, HEAD))


def _flash_kernel(q_ref, kc_ref, vc_ref, kl_ref, vl_ref, o_ref, *, tk, unroll):
    tq = q_ref.shape[0]
    group = q_ref.shape[1] // HEAD
    qs = [q_ref[:, g * HEAD:(g + 1) * HEAD] for g in range(group)]

    def attend(carry, k, v):
        v_ext = jnp.concatenate([v, jnp.ones_like(v)], axis=1)
        out = []
        for g in range(group):
            m, acc = carry[g]
            s = _dot_nt(qs[g], k)
            m_new = jnp.maximum(m, jnp.max(s, axis=-1, keepdims=True))
            p = jnp.exp2(s - m_new).astype(BF16)
            out.append((m_new, jnp.exp2(m - m_new) * acc + _dot(p, v_ext)))
        return tuple(out)

    def body(j, carry):
        r = pl.ds(pl.multiple_of(j * tk, tk), tk)
        return attend(carry, kl_ref[r, :], vl_ref[r, :])

    init = tuple((jnp.full((tq, 1), -jnp.inf, F32), jnp.zeros((tq, 2 * HEAD), F32)) for _ in range(group))
    carry = attend(init, kc_ref[...], vc_ref[...])
    final = lax.fori_loop(0, kl_ref.shape[0] // tk, body, carry, unroll=unroll)
    for g in range(group):
        acc = final[g][1]
        o_ref[:, g * HEAD:(g + 1) * HEAD] = (acc[:, :HEAD] / acc[:, HEAD:]).astype(o_ref.dtype)


def _flash_call(qkv, kv_ctx, *, heads):
    b, t, n = qkv.shape
    tc = kv_ctx.shape[1]
    kvh = (n // HEAD - heads) // 2
    group = heads // kvh
    tq = _pick(t, 512, BF16_SUBLANES)
    tk = _pick(t, 1024, 2 * HEAD)
    body = functools.partial(_flash_kernel, tk=tk, unroll=math.gcd(t // tk, 8))
    return pl.pallas_call(
        body,
        grid=(b, kvh, t // tq),
        in_specs=[
            pl.BlockSpec((None, tq, group * HEAD), lambda bi, h, i: (bi, i, h)),
            pl.BlockSpec((None, tc, HEAD), lambda bi, h, i: (bi, 0, h)),
            pl.BlockSpec((None, tc, HEAD), lambda bi, h, i: (bi, 0, kvh + h)),
            pl.BlockSpec((None, t, HEAD), lambda bi, h, i: (bi, 0, heads + h)),
            pl.BlockSpec((None, t, HEAD), lambda bi, h, i: (bi, 0, heads + kvh + h)),
        ],
        out_specs=pl.BlockSpec((None, tq, group * HEAD), lambda bi, h, i: (bi, i, h)),
        out_shape=jax.ShapeDtypeStruct((b, t, heads * HEAD), BF16),
        compiler_params=_params(("arbitrary", "arbitrary", "arbitrary")),
        name="gqa_flash",
    )(qkv, kv_ctx, kv_ctx, qkv, qkv)


def _rope_tables(t):
    pos = jnp.arange(t, dtype=jnp.int32)
    rows = (pos // GRID_W).astype(F32)
    cols = (pos % GRID_W).astype(F32)
    axis_dim = HEAD // 2
    inv_freq = ROPE_THETA ** (-jnp.arange(0, axis_dim, 2, dtype=F32) / axis_dim)
    ang = jnp.concatenate([rows[:, None] * inv_freq, cols[:, None] * inv_freq], axis=-1)
    cos, sin = jnp.cos(ang), jnp.sin(ang)
    return jnp.concatenate([cos, cos], axis=-1), jnp.concatenate([-sin, sin], axis=-1)


def _split_mods(mods, b):
    d = mods.shape[1] // N_MOD
    lat = [mods[:b, i * d:(i + 1) * d].reshape(b, 1, d) for i in range(N_MOD)]
    ctx = [mods[b:b + 1, i * d:(i + 1) * d].reshape(1, 1, d) for i in range(N_MOD)]
    return lat, ctx


def kernel(x, c, ctx, c_ctx, ada_w, ada_b, norm_mix_pre, norm_mix_post, norm_ffn_pre, norm_ffn_post, hgrn_w_in, hgrn_lb_logits, hgrn_o_norm, hgrn_w_out, attn_w_qkv, attn_q_norm, attn_k_norm, attn_w_out, ffn_w_in, ffn_conv_w, ffn_conv_b, ffn_w_out):
    b, t, d = x.shape
    tc = ctx.shape[1]
    heads = d // HEAD
    assert ada_w.shape[0] == 2 and b + 1 <= ADA_ROWS

    cond = jnp.zeros((ADA_ROWS, d), F32).at[:b].set(c).at[b].set(c_ctx)
    mods = _ada_call(cond, ada_w, ada_b)
    x_lat = x.reshape(b * t, d)
    x_ctx = ctx.reshape(b * tc, d)

    (sh_ml, sc_ml, gt_ml, sh_fl, sc_fl, gt_fl), (sh_mc, sc_mc, gt_mc, sh_fc, sc_fc, gt_fc) = _split_mods(mods[0], b)
    lower_bound = jnp.cumsum(jax.nn.softmax(hgrn_lb_logits.astype(F32), axis=1), axis=1)[:, 0]
    hgrn_w_in, hgrn_w_out, attn_w_out = hgrn_w_in.astype(BF16), hgrn_w_out.astype(BF16), attn_w_out.astype(BF16)
    ffn_w_in, ffn_w_out = ffn_w_in.astype(BF16), ffn_w_out.astype(BF16)
    p_lat = _nmm_call(x_lat, norm_mix_pre[0], sh_ml, sc_ml, hgrn_w_in, name="hgrn_in_lat")
    p_ctx = _nmm_call(x_ctx, norm_mix_pre[0], sh_mc, sc_mc, hgrn_w_in, name="hgrn_in_ctx")
    y_ctx, y_lat = _gla_call(p_ctx.reshape(b, tc, 5 * d), p_lat.reshape(b, t, 5 * d), lower_bound, hgrn_o_norm[0])
    x_lat = _proj_res_call(y_lat.reshape(b * t, d), hgrn_w_out, x_lat, norm_mix_post[0], gt_ml, name="hgrn_out_lat")
    x_ctx = _proj_res_call(y_ctx.reshape(b * tc, d), hgrn_w_out, x_ctx, norm_mix_post[0], gt_mc, name="hgrn_out_ctx")
    a = _ffn_in_call(x_lat, norm_ffn_pre[0], sh_fl, sc_fl, ffn_w_in, ffn_conv_w[0], ffn_conv_b[0],
                     layer=0, seq_len=t, name="ffn0_in_lat")
    x_lat = _proj_res_call(a, ffn_w_out, x_lat, norm_ffn_post[0], gt_fl, layer=0, name="ffn0_out_lat")
    a = _ffn_in_call(x_ctx, norm_ffn_pre[0], sh_fc, sc_fc, ffn_w_in, ffn_conv_w[0], ffn_conv_b[0],
                     layer=0, seq_len=tc, name="ffn0_in_ctx")
    x_ctx = _proj_res_call(a, ffn_w_out, x_ctx, norm_ffn_post[0], gt_fc, layer=0, name="ffn0_out_ctx")

    (sh_ml, sc_ml, gt_ml, sh_fl, sc_fl, gt_fl), (sh_mc, sc_mc, _, _, _, _) = _split_mods(mods[1], b)
    n_qkv = attn_w_qkv.shape[2]
    kvh = (n_qkv // HEAD - heads) // 2
    perm = jnp.concatenate([jnp.arange(0, HEAD, 2), jnp.arange(1, HEAD, 2)])
    head_perm = (jnp.arange(heads + kvh)[:, None] * HEAD + perm[None, :]).reshape(-1)
    col_perm = jnp.concatenate([head_perm, jnp.arange((heads + kvh) * HEAD, n_qkv)])
    w_qkv = attn_w_qkv[:, :, col_perm].astype(BF16)
    head_gain = jnp.concatenate([
        jnp.tile(attn_q_norm[0][perm] * (LOG2_E * HEAD ** -0.5), heads),
        jnp.tile(attn_k_norm[0][perm], kvh),
        jnp.ones((kvh * HEAD,), F32),
    ]).reshape(1, n_qkv)
    cos, sin = _rope_tables(t)
    qk_cols = (heads + kvh) * HEAD
    qkv = _nmm_call(x_lat, norm_mix_pre[1], sh_ml, sc_ml, w_qkv,
                    rope=(head_gain, cos, sin, qk_cols), name="attn_qkv_lat")
    q_cols = heads * HEAD
    kv_ctx = _nmm_call(x_ctx, norm_mix_pre[1], sh_mc, sc_mc, w_qkv, col0=q_cols,
                       rope=(head_gain[:, q_cols:], jnp.ones((tc, HEAD), F32), jnp.zeros((tc, HEAD), F32),
                             kvh * HEAD), name="attn_kv_ctx")
    qkv = qkv.reshape(b, t, n_qkv)
    kv_ctx = kv_ctx.reshape(b, tc, 2 * kvh * HEAD)
    o = _flash_call(qkv, kv_ctx, heads=heads)
    x_lat = _proj_res_call(o.reshape(b * t, d), attn_w_out, x_lat, norm_mix_post[1], gt_ml, name="attn_out_lat")
    a = _ffn_in_call(x_lat, norm_ffn_pre[1], sh_fl, sc_fl, ffn_w_in, ffn_conv_w[1], ffn_conv_b[1],
                     layer=1, seq_len=t, name="ffn1_in_lat")
    x_lat = _proj_res_call(a, ffn_w_out, x_lat, norm_ffn_post[1], gt_fl, layer=1, name="ffn1_out_lat")
    return x_lat.reshape(b, t, d)
```

```python
import functools
import math

import jax
import jax.numpy as jnp
from jax import lax
from jax.experimental import pallas as pl
from jax.experimental.pallas import tpu as pltpu

F32 = jnp.float32
BF16 = jnp.bfloat16

EPS = 1e-6
GRID_W = 64
ROPE_THETA = 10000.0
LOG2_E = math.log2(math.e)
N_MOD = 6
HEAD = 128
ROW_GROUP = 16
GLA_CHUNK = 128
GLA_RUN = 8
GLA_SAFE_LB = math.exp(-80.0 / (GLA_CHUNK // 2))
BF16_SUBLANES = 16
ADA_ROWS = 8
V7X_VMEM_BYTES = 64 * 1024 * 1024
VMEM_LIMIT = V7X_VMEM_BYTES - 6 * 1024 * 1024


def _pick(n, target, align):
    if n <= target:
        return n
    for t in range(target - target % align, 0, -align):
        if n % t == 0:
            return t
    return n


def _params(sem):
    return pltpu.CompilerParams(dimension_semantics=sem, vmem_limit_bytes=VMEM_LIMIT)


def _sigmoid(x):
    return 0.5 * jnp.tanh(0.5 * x) + 0.5


def _silu(x):
    return x * _sigmoid(x)


def _dot(a, b):
    return jnp.dot(a, b, preferred_element_type=F32)


def _bf16(w):
    return w if w.dtype == BF16 else w.astype(BF16)


def _dot_nt(a, b):
    return lax.dot_general(a, b, (((1,), (1,)), ((), ())), preferred_element_type=F32)


def _dot_tn(a, b):
    return lax.dot_general(a, b, (((0,), (0,)), ((), ())), preferred_element_type=F32)


def _ada_kernel(c_ref, w_ref, b_ref, o_ref):
    sc = _silu(c_ref[...]).astype(BF16)
    o_ref[...] = _dot(sc, w_ref[...].astype(BF16)) + b_ref[...]


def _ada_call(cond, ada_w, ada_b):
    depth, d, n = ada_w.shape
    tn = _pick(n, 1024, HEAD)
    return pl.pallas_call(
        _ada_kernel,
        grid=(depth, n // tn),
        in_specs=[
            pl.BlockSpec((ADA_ROWS, d), lambda l, j: (0, 0)),
            pl.BlockSpec((None, d, tn), lambda l, j: (l, 0, j)),
            pl.BlockSpec((None, 1, tn), lambda l, j: (l, 0, j)),
        ],
        out_specs=pl.BlockSpec((None, ADA_ROWS, tn), lambda l, j: (l, 0, j)),
        out_shape=jax.ShapeDtypeStruct((depth, ADA_ROWS, n), F32),
        compiler_params=_params(("arbitrary", "arbitrary")),
        name="ada_ln",
    )(cond, ada_w, ada_b.reshape(depth, 1, n))


def _norm_mod(x, gain, shift, scale):
    ms = jnp.mean(x * x, axis=-1, keepdims=True)
    return (x * lax.rsqrt(ms + EPS) * gain) * (1.0 + scale) + shift


def _norm_mod_rows(x_ref, h_ref, h_row0, gain, shift, scale):
    mult = gain * (1.0 + scale)

    def body(r, carry):
        start = pl.multiple_of(r * ROW_GROUP, ROW_GROUP)
        x = x_ref[pl.ds(start, ROW_GROUP), :]
        ms = jnp.mean(x * x, axis=-1, keepdims=True)
        dst = pl.ds(pl.multiple_of(h_row0 + start, ROW_GROUP), ROW_GROUP)
        h_ref[dst, :] = (x * lax.rsqrt(ms + EPS) * mult + shift).astype(BF16)
        return carry

    groups = x_ref.shape[0] // ROW_GROUP
    lax.fori_loop(0, groups, body, 0, unroll=math.gcd(groups, 4))


def _nmm_kernel(x_ref, gain_ref, shift_ref, scale_ref, w_ref, o_ref, h_ref):
    @pl.when(pl.program_id(1) == 0)
    def _():
        _norm_mod_rows(x_ref, h_ref, 0, gain_ref[...], shift_ref[...], scale_ref[...])

    o_ref[...] = _dot(h_ref[...], _bf16(w_ref[...])).astype(o_ref.dtype)


def _nmm_rope_kernel(x_ref, gain_ref, shift_ref, scale_ref, w_ref, hg_ref, cos_ref, sin_ref,
                     o_ref, h_ref, *, n_normed_blocks):
    j = pl.program_id(1)

    @pl.when(j == 0)
    def _():
        _norm_mod_rows(x_ref, h_ref, 0, gain_ref[...], shift_ref[...], scale_ref[...])

    acc = _dot(h_ref[...], w_ref[...])

    @pl.when(j < n_normed_blocks)
    def _():
        cos = cos_ref[...]
        sin = sin_ref[...]
        for hh in range(acc.shape[1] // HEAD):
            cols = slice(hh * HEAD, (hh + 1) * HEAD)
            a = acc[:, cols]
            ms = jnp.mean(a * a, axis=-1, keepdims=True)
            a = a * lax.rsqrt(ms + EPS) * hg_ref[:, cols]
            a = a * cos + pltpu.roll(a, HEAD // 2, axis=1) * sin
            o_ref[:, cols] = a.astype(o_ref.dtype)

    @pl.when(j >= n_normed_blocks)
    def _():
        o_ref[...] = acc.astype(o_ref.dtype)


def _nmm_call(x, gain, shift, scale, w, *, layer=0, col0=0, rope=None, name):
    m, d = x.shape
    n = w.shape[2] - col0
    bm = shift.shape[0]
    rows_per_mod = m // bm
    row_period = rows_per_mod if rope is None else math.gcd(rows_per_mod, rope[1].shape[0])
    tm = _pick(row_period, 1024, BF16_SUBLANES)
    tn = _pick(n, 1024, HEAD) if rope is None else _pick(math.gcd(n, rope[3]), 512, HEAD)
    mod_blocks = rows_per_mod // tm
    assert col0 % tn == 0
    col_block0 = col0 // tn
    in_specs = [
        pl.BlockSpec((tm, d), lambda i, j: (i, 0)),
        pl.BlockSpec((1, d), lambda i, j: (0, 0)),
        pl.BlockSpec((None, 1, d), lambda i, j: (i // mod_blocks, 0, 0)),
        pl.BlockSpec((None, 1, d), lambda i, j: (i // mod_blocks, 0, 0)),
        pl.BlockSpec((None, d, tn), lambda i, j: (layer, 0, col_block0 + j)),
    ]
    args = [x, gain.reshape(1, d), shift, scale, w]
    if rope is None:
        body = _nmm_kernel
    else:
        head_gain, cos, sin, n_normed_cols = rope
        assert n_normed_cols % tn == 0 and cos.shape[0] % tm == 0
        table_blocks = cos.shape[0] // tm
        in_specs += [
            pl.BlockSpec((1, tn), lambda i, j: (0, j)),
            pl.BlockSpec((tm, HEAD), lambda i, j: (i % table_blocks, 0)),
            pl.BlockSpec((tm, HEAD), lambda i, j: (i % table_blocks, 0)),
        ]
        args += [head_gain, cos, sin]
        body = functools.partial(_nmm_rope_kernel, n_normed_blocks=n_normed_cols // tn)
    return pl.pallas_call(
        body,
        grid=(m // tm, n // tn),
        in_specs=in_specs,
        out_specs=pl.BlockSpec((tm, tn), lambda i, j: (i, j)),
        out_shape=jax.ShapeDtypeStruct((m, n), BF16),
        scratch_shapes=[pltpu.VMEM((tm, d), BF16)],
        compiler_params=_params(("arbitrary", "arbitrary")),
        name=name,
    )(*args)


def _residual(x, y, gain, gate):
    ms = jnp.mean(y * y, axis=-1, keepdims=True)
    return x + gate * (y * lax.rsqrt(ms + EPS) * gain)


def _proj_res_kernel(a_ref, w_ref, x_ref, gain_ref, gate_ref, o_ref):
    y = _dot(a_ref[...], w_ref[...])
    o_ref[...] = _residual(x_ref[...], y, gain_ref[...], gate_ref[...])


def _proj_res_ktiled_kernel(a_ref, w_ref, x_ref, gain_ref, gate_ref, o_ref):
    k = pl.program_id(1)

    @pl.when(k == 0)
    def _():
        o_ref[...] = _dot(a_ref[...], _bf16(w_ref[...]))

    @pl.when(k > 0)
    def _():
        o_ref[...] += _dot(a_ref[...], _bf16(w_ref[...]))

    @pl.when(k == pl.num_programs(1) - 1)
    def _():
        o_ref[...] = _residual(x_ref[...], o_ref[...], gain_ref[...], gate_ref[...])


def _proj_res_call(a, w, x, gain, gate, *, layer=0, name):
    m, k = a.shape
    d = w.shape[2]
    bm = gate.shape[0]
    rows_per_mod = m // bm
    tk = k if k <= d else _pick(k, 512, 2 * HEAD)
    kb = k // tk
    tm = _pick(rows_per_mod, 512 if kb == 1 else 1024, BF16_SUBLANES)
    mod_blocks = rows_per_mod // tm
    return pl.pallas_call(
        _proj_res_kernel if kb == 1 else _proj_res_ktiled_kernel,
        grid=(m // tm, kb),
        in_specs=[
            pl.BlockSpec((tm, tk), lambda i, kk: (i, kk)),
            pl.BlockSpec((None, tk, d), lambda i, kk: (layer, kk, 0)),
            pl.BlockSpec((tm, d), lambda i, kk: (i, 0)),
            pl.BlockSpec((1, d), lambda i, kk: (0, 0)),
            pl.BlockSpec((None, 1, d), lambda i, kk: (i // mod_blocks, 0, 0)),
        ],
        out_specs=pl.BlockSpec((tm, d), lambda i, kk: (i, 0)),
        out_shape=jax.ShapeDtypeStruct((m, d), F32),
        compiler_params=_params(("arbitrary", "arbitrary")),
        name=name,
    )(a, w, x, gain.reshape(1, d), gate)


def _ffn_in_kernel(x_ref, xprev_ref, xnext_ref, gain_ref, shift_ref, scale_ref, wg_ref, wu_ref, cw_ref, cb_ref,
                   o_ref, h_ref, *, blocks_per_seq):
    i = pl.program_id(0)
    tm = x_ref.shape[0]
    halo = BF16_SUBLANES

    @pl.when(pl.program_id(1) == 0)
    def _():
        gain, shift, scale = gain_ref[...], shift_ref[...], scale_ref[...]
        seq_block = i % blocks_per_seq
        h_prev = jnp.where(seq_block > 0, _norm_mod(xprev_ref[...], gain, shift, scale), 0.0)
        h_next = jnp.where(seq_block < blocks_per_seq - 1, _norm_mod(xnext_ref[...], gain, shift, scale), 0.0)
        zeros = jnp.zeros((halo - h_prev.shape[0], h_prev.shape[1]), F32)
        h_ref[0:halo, :] = jnp.concatenate([zeros, h_prev], axis=0).astype(BF16)
        _norm_mod_rows(x_ref, h_ref, halo, gain, shift, scale)
        h_ref[halo + tm:, :] = jnp.concatenate([h_next, zeros], axis=0).astype(BF16)

    gate = _dot(h_ref[...], _bf16(wg_ref[...]))
    up = _dot(h_ref[halo:halo + tm, :], _bf16(wu_ref[...]))
    rows = gate.shape[0]
    g_prev = pltpu.roll(gate, 1, axis=0)[halo:halo + tm, :]
    g_next = pltpu.roll(gate, rows - 1, axis=0)[halo:halo + tm, :]
    conv = g_prev * cw_ref[0:1, :] + gate[halo:halo + tm, :] * cw_ref[1:2, :] + g_next * cw_ref[2:3, :] + cb_ref[...]
    o_ref[...] = (_silu(conv) * up).astype(o_ref.dtype)


def _ffn_in_call(x, gain, shift, scale, w, conv_w, conv_b, *, layer, seq_len, name):
    m, d = x.shape
    f = w.shape[2] // 2
    bm = shift.shape[0]
    rows_per_mod = m // bm
    tm = _pick(seq_len, 1024, BF16_SUBLANES)
    tn = _pick(f, 512, HEAD)
    assert rows_per_mod % tm == 0
    mod_blocks = rows_per_mod // tm
    nb = f // tn
    xh = 8
    tiles_per_tm = tm // xh
    last_tile = m // xh - 1
    body = functools.partial(_ffn_in_kernel, blocks_per_seq=seq_len // tm)
    return pl.pallas_call(
        body,
        grid=(m // tm, nb),
        in_specs=[
            pl.BlockSpec((tm, d), lambda i, j: (i, 0)),
            pl.BlockSpec((xh, d), lambda i, j: (jnp.maximum(i * tiles_per_tm - 1, 0), 0)),
            pl.BlockSpec((xh, d), lambda i, j: (jnp.minimum((i + 1) * tiles_per_tm, last_tile), 0)),
            pl.BlockSpec((1, d), lambda i, j: (0, 0)),
            pl.BlockSpec((None, 1, d), lambda i, j: (i // mod_blocks, 0, 0)),
            pl.BlockSpec((None, 1, d), lambda i, j: (i // mod_blocks, 0, 0)),
            pl.BlockSpec((None, d, tn), lambda i, j: (layer, 0, j)),
            pl.BlockSpec((None, d, tn), lambda i, j: (layer, 0, nb + j)),
            pl.BlockSpec((3, tn), lambda i, j: (0, j)),
            pl.BlockSpec((1, tn), lambda i, j: (0, j)),
        ],
        out_specs=pl.BlockSpec((tm, tn), lambda i, j: (i, j)),
        out_shape=jax.ShapeDtypeStruct((m, f), BF16),
        scratch_shapes=[pltpu.VMEM((tm + 2 * BF16_SUBLANES, d), BF16)],
        compiler_params=_params(("arbitrary", "arbitrary")),
        name=name,
    )(x, x, x, gain.reshape(1, d), shift, scale, w, w, conv_w, conv_b.reshape(1, f))


def _gla_gates(f_raw, lb, within):
    half = 0.5 * (1.0 - lb)
    ht = half * jnp.tanh(0.5 * f_raw.astype(F32))
    lf = jnp.log((1.0 - half) + ht)
    ones = jnp.where(within, 1.0, 0.0).astype(BF16)
    lf_hi = lf.astype(BF16)
    lf_lo = (lf - lf_hi.astype(F32)).astype(BF16)
    return half - ht, _dot(ones, lf_hi) + _dot(ones, lf_lo)


def _gla_pairs(qs, kk, b, forward, within):
    c = qs.shape[0]
    a_row = c // 2 - 1 if forward else c // 2
    e_row = c - 1 if forward else 0
    anchor = b[a_row:a_row + 1, :]
    b_end = b[e_row:e_row + 1, :]
    qa = qs * jnp.exp(b - anchor)
    ka = kk * jnp.exp(anchor - b)
    scores = jnp.where(within, _dot_nt(qa.astype(BF16), ka.astype(BF16)), 0.0).astype(BF16)
    return (scores, (qa * jnp.exp(anchor)).astype(BF16), (ka * jnp.exp(b_end - anchor)).astype(BF16),
            jnp.exp(b_end))


def _gla_pairs_exact(qs, kk, b, forward, within, tmp):
    tb_ref, tq_ref, ts_ref = tmp
    c = qs.shape[0]
    b_end = b[c - 1:c, :] if forward else b[0:1, :]
    tb_ref[...] = b
    tq_ref[...] = qs
    ones = jnp.ones((8, qs.shape[1]), BF16)

    def row(t, carry):
        rel = jnp.minimum(tb_ref[pl.ds(t, 1), :] - b, 0.0)
        e = (tq_ref[pl.ds(t, 1), :] * kk) * jnp.exp(rel)
        ts_ref[pl.ds(t, 1), :] = _dot_nt(ones, e.astype(BF16))[0:1, :]
        return carry

    lax.fori_loop(0, c, row, 0)
    scores = jnp.where(within, ts_ref[...], 0.0).astype(BF16)
    return scores, (qs * jnp.exp(b)).astype(BF16), (kk * jnp.exp(b_end - b)).astype(BF16), jnp.exp(b_end)


def _gla_readout(o, g_raw, o_gain):
    ms = jnp.mean(o * o, axis=-1, keepdims=True)
    y = o * lax.rsqrt(ms + EPS) * o_gain
    return (y * _silu(g_raw.astype(F32))).astype(BF16)


def _gla_kernel(qc, ffc, fbc, vc, gc, ql, ffl, fbl, vl, gl, lb_ref, og_ref, yc_ref, yl_ref,
                o_ref, qb_ref, ke_ref, dec_ref, st_ref, tb_ref, tq_ref, ts_ref):
    c = GLA_CHUNK
    tc = qc.shape[0]
    n_ctx = tc // c
    n_lat = ql.shape[0] // c
    lb = (lb_ref[0], lb_ref[1])
    row = lax.broadcasted_iota(jnp.int32, (c, c), 0)
    col = lax.broadcasted_iota(jnp.int32, (c, c), 1)
    within = (col <= row, col >= row)

    def rows(start):
        return pl.ds(start if isinstance(start, int) else pl.multiple_of(start, c), c)

    def local(refs, src0, dst0, chunk0, n, exact):
        q, ff, fb, v = refs
        src = [rows(src0 + j * c) for j in range(n)]
        dst = [rows(dst0 + j * c) for j in range(n)]
        qs = [_silu(q[src[j], :].astype(F32)) * (HEAD ** -0.5) for j in range(n)]
        gates = [[_gla_gates(f[src[j], :], lb[d], within[d]) for d, f in enumerate((ff, fb))] for j in range(n)]
        scores = []
        for j in range(n):
            per_dir = []
            for d in range(2):
                kk, b = gates[j][d]
                if exact:
                    sc, qb, ke, dec = _gla_pairs_exact(qs[j], kk, b, d == 0, within[d], (tb_ref, tq_ref, ts_ref))
                else:
                    sc, qb, ke, dec = _gla_pairs(qs[j], kk, b, d == 0, within[d])
                qb_ref[d, dst[j], :] = qb
                ke_ref[d, dst[j], :] = ke
                dec_ref[d, chunk0 + j] = dec
                per_dir.append(sc)
            scores.append(per_dir)
        for j in range(n):
            vj = v[src[j], :]
            o_ref[dst[j], :] = _dot(scores[j][0], vj) + _dot(scores[j][1], vj)

    def phase1(exact):
        run = 1 if exact else math.gcd(n_lat, GLA_RUN)
        run_ctx = math.gcd(n_ctx, run)
        for i in range(0, n_ctx, run_ctx):
            local((qc, ffc, fbc, vc), i * c, i * c, i, run_ctx, exact)

        def body(i, carry):
            src0 = pl.multiple_of(i * (run * c), c)
            local((ql, ffl, fbl, vl), src0, tc + src0, n_ctx + i * run, run, exact)
            return carry

        lax.fori_loop(0, n_lat // run, body, 0)

    safe = jnp.min(jnp.minimum(lb[0], lb[1])) >= GLA_SAFE_LB

    @pl.when(safe)
    def _():
        phase1(False)

    @pl.when(jnp.logical_not(safe))
    def _():
        phase1(True)

    st_ref[...] = jnp.zeros_like(st_ref)

    def step(d, v, src_rows, dst_rows, chunk):
        st = st_ref[d]
        o_ref[dst_rows, :] += _dot_nt(qb_ref[d, dst_rows, :], st.astype(BF16))
        st_ref[d] = st * dec_ref[d, chunk] + _dot_tn(v[src_rows, :], ke_ref[d, dst_rows, :])

    for i in range(n_ctx):
        for d, j in ((0, i), (1, n_ctx - 1 - i)):
            r = slice(j * c, (j + 1) * c)
            step(d, vc, r, r, j)

    def step_body(i, carry):
        for d, j in ((0, i), (1, n_lat - 1 - i)):
            src = pl.ds(pl.multiple_of(j * c, c), c)
            dst = pl.ds(pl.multiple_of(tc + j * c, c), c)
            step(d, vl, src, dst, n_ctx + j)
        return carry

    lax.fori_loop(0, n_lat, step_body, 0, unroll=math.gcd(n_lat, 8))

    o_gain = og_ref[...]
    for i in range(n_ctx):
        r = slice(i * c, (i + 1) * c)
        yc_ref[r, :] = _gla_readout(o_ref[r, :], gc[r, :], o_gain)

    def out_body(i, carry):
        src = pl.ds(pl.multiple_of(i * c, c), c)
        dst = pl.ds(pl.multiple_of(tc + i * c, c), c)
        yl_ref[src, :] = _gla_readout(o_ref[dst, :], gl[src, :], o_gain)
        return carry

    lax.fori_loop(0, n_lat, out_body, 0, unroll=math.gcd(n_lat, 4))


def _gla_call(p_ctx, p_lat, lower_bound, o_gain):
    b, tc, d5 = p_ctx.shape
    t = p_lat.shape[1]
    d = d5 // 5
    heads = d // HEAD
    assert tc % GLA_CHUNK == 0 and t % GLA_CHUNK == 0

    def col(tt, part):
        return pl.BlockSpec((None, tt, HEAD), lambda bi, h: (bi, 0, part * heads + h))

    out_spec = lambda tt: pl.BlockSpec((None, tt, HEAD), lambda bi, h: (bi, 0, h))
    return pl.pallas_call(
        _gla_kernel,
        grid=(b, heads),
        in_specs=[col(tc, p) for p in range(5)] + [col(t, p) for p in range(5)] + [
            pl.BlockSpec((2, None, 1, HEAD), lambda bi, h: (0, h, 0, 0)),
            pl.BlockSpec((1, HEAD), lambda bi, h: (0, 0)),
        ],
        out_specs=[out_spec(tc), out_spec(t)],
        out_shape=[jax.ShapeDtypeStruct((b, tc, d), BF16), jax.ShapeDtypeStruct((b, t, d), BF16)],
        scratch_shapes=[
            pltpu.VMEM((tc + t, HEAD), F32),
            pltpu.VMEM((2, tc + t, HEAD), BF16),
            pltpu.VMEM((2, tc + t, HEAD), BF16),
            pltpu.VMEM((2, (tc + t) // GLA_CHUNK, 1, HEAD), F32),
            pltpu.VMEM((2, HEAD, HEAD), F32),
            pltpu.VMEM((GLA_CHUNK, HEAD), F32),
            pltpu.VMEM((GLA_CHUNK, HEAD), F32),
            pltpu.VMEM((GLA_CHUNK, GLA_CHUNK), F32),
        ],
        compiler_params=_params(("arbitrary", "arbitrary")),
        name="hgrn_scan",
    )(*([p_ctx] * 5), *([p_lat] * 5), lower_bound.reshape(2, heads, 1, HEAD), o_gain.reshape(1, HEAD))


def _flash_kernel(q_ref, kc_ref, vc_ref, kl_ref, vl_ref, o_ref, *, tk, unroll):
    tq = q_ref.shape[0]
    group = q_ref.shape[1] // HEAD
    qs = [q_ref[:, g * HEAD:(g + 1) * HEAD] for g in range(group)]

    def attend(carry, k, v):
        v_ext = jnp.concatenate([v, jnp.ones_like(v)], axis=1)
        out = []
        for g in range(group):
            m, acc = carry[g]
            s = _dot_nt(qs[g], k)
            m_new = jnp.maximum(m, jnp.max(s, axis=-1, keepdims=True))
            p = jnp.exp2(s - m_new).astype(BF16)
            out.append((m_new, jnp.exp2(m - m_new) * acc + _dot(p, v_ext)))
        return tuple(out)

    def body(j, carry):
        r = pl.ds(pl.multiple_of(j * tk, tk), tk)
        return attend(carry, kl_ref[r, :], vl_ref[r, :])

    init = tuple((jnp.full((tq, 1), -jnp.inf, F32), jnp.zeros((tq, 2 * HEAD), F32)) for _ in range(group))
    carry = attend(init, kc_ref[...], vc_ref[...])
    final = lax.fori_loop(0, kl_ref.shape[0] // tk, body, carry, unroll=unroll)
    for g in range(group):
        acc = final[g][1]
        o_ref[:, g * HEAD:(g + 1) * HEAD] = (acc[:, :HEAD] / acc[:, HEAD:]).astype(o_ref.dtype)


def _flash_call(qkv, kv_ctx, *, heads):
    b, t, n = qkv.shape
    tc = kv_ctx.shape[1]
    kvh = (n // HEAD - heads) // 2
    group = heads // kvh
    tq = _pick(t, 512, BF16_SUBLANES)
    tk = _pick(t, 1024, 2 * HEAD)
    body = functools.partial(_flash_kernel, tk=tk, unroll=math.gcd(t // tk, 8))
    return pl.pallas_call(
        body,
        grid=(b, kvh, t // tq),
        in_specs=[
            pl.BlockSpec((None, tq, group * HEAD), lambda bi, h, i: (bi, i, h)),
            pl.BlockSpec((None, tc, HEAD), lambda bi, h, i: (bi, 0, h)),
            pl.BlockSpec((None, tc, HEAD), lambda bi, h, i: (bi, 0, kvh + h)),
            pl.BlockSpec((None, t, HEAD), lambda bi, h, i: (bi, 0, heads + h)),
            pl.BlockSpec((None, t, HEAD), lambda bi, h, i: (bi, 0, heads + kvh + h)),
        ],
        out_specs=pl.BlockSpec((None, tq, group * HEAD), lambda bi, h, i: (bi, i, h)),
        out_shape=jax.ShapeDtypeStruct((b, t, heads * HEAD), BF16),
        compiler_params=_params(("arbitrary", "arbitrary", "arbitrary")),
        name="gqa_flash",
    )(qkv, kv_ctx, kv_ctx, qkv, qkv)


def _rope_tables(t):
    pos = jnp.arange(t, dtype=jnp.int32)
    rows = (pos // GRID_W).astype(F32)
    cols = (pos % GRID_W).astype(F32)
    axis_dim = HEAD // 2
    inv_freq = ROPE_THETA ** (-jnp.arange(0, axis_dim, 2, dtype=F32) / axis_dim)
    ang = jnp.concatenate([rows[:, None] * inv_freq, cols[:, None] * inv_freq], axis=-1)
    cos, sin = jnp.cos(ang), jnp.sin(ang)
    return jnp.concatenate([cos, cos], axis=-1), jnp.concatenate([-sin, sin], axis=-1)


def _split_mods(mods, b):
    d = mods.shape[1] // N_MOD
    lat = [mods[:b, i * d:(i + 1) * d].reshape(b, 1, d) for i in range(N_MOD)]
    ctx = [mods[b:b + 1, i * d:(i + 1) * d].reshape(1, 1, d) for i in range(N_MOD)]
    return lat, ctx


def kernel(x, c, ctx, c_ctx, ada_w, ada_b, norm_mix_pre, norm_mix_post, norm_ffn_pre, norm_ffn_post, hgrn_w_in, hgrn_lb_logits, hgrn_o_norm, hgrn_w_out, attn_w_qkv, attn_q_norm, attn_k_norm, attn_w_out, ffn_w_in, ffn_conv_w, ffn_conv_b, ffn_w_out):
    b, t, d = x.shape
    tc = ctx.shape[1]
    heads = d // HEAD
    assert ada_w.shape[0] == 2 and b + 1 <= ADA_ROWS

    cond = jnp.zeros((ADA_ROWS, d), F32).at[:b].set(c).at[b].set(c_ctx)
    mods = _ada_call(cond, ada_w, ada_b)
    x_lat = x.reshape(b * t, d)
    x_ctx = ctx.reshape(b * tc, d)

    (sh_ml, sc_ml, gt_ml, sh_fl, sc_fl, gt_fl), (sh_mc, sc_mc, gt_mc, sh_fc, sc_fc, gt_fc) = _split_mods(mods[0], b)
    lower_bound = jnp.cumsum(jax.nn.softmax(hgrn_lb_logits.astype(F32), axis=1), axis=1)[:, 0]
    hgrn_w_out, attn_w_out = hgrn_w_out.astype(BF16), attn_w_out.astype(BF16)
    p_lat = _nmm_call(x_lat, norm_mix_pre[0], sh_ml, sc_ml, hgrn_w_in, name="hgrn_in_lat")
    p_ctx = _nmm_call(x_ctx, norm_mix_pre[0], sh_mc, sc_mc, hgrn_w_in, name="hgrn_in_ctx")
    y_ctx, y_lat = _gla_call(p_ctx.reshape(b, tc, 5 * d), p_lat.reshape(b, t, 5 * d), lower_bound, hgrn_o_norm[0])
    x_lat = _proj_res_call(y_lat.reshape(b * t, d), hgrn_w_out, x_lat, norm_mix_post[0], gt_ml, name="hgrn_out_lat")
    x_ctx = _proj_res_call(y_ctx.reshape(b * tc, d), hgrn_w_out, x_ctx, norm_mix_post[0], gt_mc, name="hgrn_out_ctx")
    a = _ffn_in_call(x_lat, norm_ffn_pre[0], sh_fl, sc_fl, ffn_w_in, ffn_conv_w[0], ffn_conv_b[0],
                     layer=0, seq_len=t, name="ffn0_in_lat")
    x_lat = _proj_res_call(a, ffn_w_out, x_lat, norm_ffn_post[0], gt_fl, layer=0, name="ffn0_out_lat")
    a = _ffn_in_call(x_ctx, norm_ffn_pre[0], sh_fc, sc_fc, ffn_w_in, ffn_conv_w[0], ffn_conv_b[0],
                     layer=0, seq_len=tc, name="ffn0_in_ctx")
    x_ctx = _proj_res_call(a, ffn_w_out, x_ctx, norm_ffn_post[0], gt_fc, layer=0, name="ffn0_out_ctx")

    (sh_ml, sc_ml, gt_ml, sh_fl, sc_fl, gt_fl), (sh_mc, sc_mc, _, _, _, _) = _split_mods(mods[1], b)
    n_qkv = attn_w_qkv.shape[2]
    kvh = (n_qkv // HEAD - heads) // 2
    perm = jnp.concatenate([jnp.arange(0, HEAD, 2), jnp.arange(1, HEAD, 2)])
    head_perm = (jnp.arange(heads + kvh)[:, None] * HEAD + perm[None, :]).reshape(-1)
    col_perm = jnp.concatenate([head_perm, jnp.arange((heads + kvh) * HEAD, n_qkv)])
    w_qkv = attn_w_qkv[:, :, col_perm].astype(BF16)
    head_gain = jnp.concatenate([
        jnp.tile(attn_q_norm[0][perm] * (LOG2_E * HEAD ** -0.5), heads),
        jnp.tile(attn_k_norm[0][perm], kvh),
        jnp.ones((kvh * HEAD,), F32),
    ]).reshape(1, n_qkv)
    cos, sin = _rope_tables(t)
    qk_cols = (heads + kvh) * HEAD
    qkv = _nmm_call(x_lat, norm_mix_pre[1], sh_ml, sc_ml, w_qkv,
                    rope=(head_gain, cos, sin, qk_cols), name="attn_qkv_lat")
    q_cols = heads * HEAD
    kv_ctx = _nmm_call(x_ctx, norm_mix_pre[1], sh_mc, sc_mc, w_qkv, col0=q_cols,
                       rope=(head_gain[:, q_cols:], jnp.ones((tc, HEAD), F32), jnp.zeros((tc, HEAD), F32),
                             kvh * HEAD), name="attn_kv_ctx")
    qkv = qkv.reshape(b, t, n_qkv)
    kv_ctx = kv_ctx.reshape(b, tc, 2 * kvh * HEAD)
    o = _flash_call(qkv, kv_ctx, heads=heads)
    x_lat = _proj_res_call(o.reshape(b * t, d), attn_w_out, x_lat, norm_mix_post[1], gt_ml, name="attn_out_lat")
    a = _ffn_in_call(x_lat, norm_ffn_pre[1], sh_fl, sc_fl, ffn_w_in, ffn_conv_w[1], ffn_conv_b[1],
                     layer=1, seq_len=t, name="ffn1_in_lat")
    x_lat = _proj_res_call(a, ffn_w_out, x_lat, norm_ffn_post[1], gt_fl, layer=1, name="ffn1_out_lat")
    return x_lat.reshape(b, t, d)
```

```python
import functools
import math

import jax
import jax.numpy as jnp
from jax import lax
from jax.experimental import pallas as pl
from jax.experimental.pallas import tpu as pltpu

F32 = jnp.float32
BF16 = jnp.bfloat16

EPS = 1e-6
GRID_W = 64
ROPE_THETA = 10000.0
LOG2_E = math.log2(math.e)
N_MOD = 6
HEAD = 128
ROW_GROUP = 16
GLA_CHUNK = 128
GLA_RUN = 8
GLA_SAFE_LB = math.exp(-80.0 / (GLA_CHUNK // 2))
BF16_SUBLANES = 16
ADA_ROWS = 8
V7X_VMEM_BYTES = 64 * 1024 * 1024
VMEM_LIMIT = V7X_VMEM_BYTES - 6 * 1024 * 1024


def _pick(n, target, align):
    if n <= target:
        return n
    for t in range(target - target % align, 0, -align):
        if n % t == 0:
            return t
    return n


def _params(sem):
    return pltpu.CompilerParams(dimension_semantics=sem, vmem_limit_bytes=VMEM_LIMIT)


def _sigmoid(x):
    return 0.5 * jnp.tanh(0.5 * x) + 0.5


def _silu(x):
    return x * _sigmoid(x)


def _dot(a, b):
    return jnp.dot(a, b, preferred_element_type=F32)


def _dot_nt(a, b):
    return lax.dot_general(a, b, (((1,), (1,)), ((), ())), preferred_element_type=F32)


def _dot_tn(a, b):
    return lax.dot_general(a, b, (((0,), (0,)), ((), ())), preferred_element_type=F32)


def _ada_kernel(c_ref, w_ref, b_ref, o_ref):
    sc = _silu(c_ref[...]).astype(BF16)
    o_ref[...] = _dot(sc, w_ref[...].astype(BF16)) + b_ref[...]


def _ada_call(cond, ada_w, ada_b):
    depth, d, n = ada_w.shape
    tn = _pick(n, 1024, HEAD)
    return pl.pallas_call(
        _ada_kernel,
        grid=(depth, n // tn),
        in_specs=[
            pl.BlockSpec((ADA_ROWS, d), lambda l, j: (0, 0)),
            pl.BlockSpec((None, d, tn), lambda l, j: (l, 0, j)),
            pl.BlockSpec((None, 1, tn), lambda l, j: (l, 0, j)),
        ],
        out_specs=pl.BlockSpec((None, ADA_ROWS, tn), lambda l, j: (l, 0, j)),
        out_shape=jax.ShapeDtypeStruct((depth, ADA_ROWS, n), F32),
        compiler_params=_params(("arbitrary", "arbitrary")),
        name="ada_ln",
    )(cond, ada_w, ada_b.reshape(depth, 1, n))


def _norm_mod(x, gain, shift, scale):
    ms = jnp.mean(x * x, axis=-1, keepdims=True)
    return (x * lax.rsqrt(ms + EPS) * gain) * (1.0 + scale) + shift


def _norm_mod_rows(x_ref, h_ref, h_row0, gain, shift, scale):
    mult = gain * (1.0 + scale)

    def body(r, carry):
        start = pl.multiple_of(r * ROW_GROUP, ROW_GROUP)
        x = x_ref[pl.ds(start, ROW_GROUP), :]
        ms = jnp.mean(x * x, axis=-1, keepdims=True)
        dst = pl.ds(pl.multiple_of(h_row0 + start, ROW_GROUP), ROW_GROUP)
        h_ref[dst, :] = (x * lax.rsqrt(ms + EPS) * mult + shift).astype(BF16)
        return carry

    groups = x_ref.shape[0] // ROW_GROUP
    lax.fori_loop(0, groups, body, 0, unroll=math.gcd(groups, 4))


def _nmm_kernel(x_ref, gain_ref, shift_ref, scale_ref, w_ref, o_ref, h_ref):
    @pl.when(pl.program_id(1) == 0)
    def _():
        _norm_mod_rows(x_ref, h_ref, 0, gain_ref[...], shift_ref[...], scale_ref[...])

    o_ref[...] = _dot(h_ref[...], w_ref[...]).astype(o_ref.dtype)


def _nmm_rope_kernel(x_ref, gain_ref, shift_ref, scale_ref, w_ref, hg_ref, cos_ref, sin_ref,
                     o_ref, h_ref, *, n_normed_blocks):
    j = pl.program_id(1)

    @pl.when(j == 0)
    def _():
        _norm_mod_rows(x_ref, h_ref, 0, gain_ref[...], shift_ref[...], scale_ref[...])

    acc = _dot(h_ref[...], w_ref[...])

    @pl.when(j < n_normed_blocks)
    def _():
        cos = cos_ref[...]
        sin = sin_ref[...]
        for hh in range(acc.shape[1] // HEAD):
            cols = slice(hh * HEAD, (hh + 1) * HEAD)
            a = acc[:, cols]
            ms = jnp.mean(a * a, axis=-1, keepdims=True)
            a = a * lax.rsqrt(ms + EPS) * hg_ref[:, cols]
            a = a * cos + pltpu.roll(a, HEAD // 2, axis=1) * sin
            o_ref[:, cols] = a.astype(o_ref.dtype)

    @pl.when(j >= n_normed_blocks)
    def _():
        o_ref[...] = acc.astype(o_ref.dtype)


def _nmm_call(x, gain, shift, scale, w, *, layer=0, col0=0, rope=None, name):
    m, d = x.shape
    n = w.shape[2] - col0
    bm = shift.shape[0]
    rows_per_mod = m // bm
    row_period = rows_per_mod if rope is None else math.gcd(rows_per_mod, rope[1].shape[0])
    tm = _pick(row_period, 1024, BF16_SUBLANES)
    tn = _pick(n, 1024, HEAD) if rope is None else _pick(math.gcd(n, rope[3]), 512, HEAD)
    mod_blocks = rows_per_mod // tm
    assert col0 % tn == 0
    col_block0 = col0 // tn
    in_specs = [
        pl.BlockSpec((tm, d), lambda i, j: (i, 0)),
        pl.BlockSpec((1, d), lambda i, j: (0, 0)),
        pl.BlockSpec((None, 1, d), lambda i, j: (i // mod_blocks, 0, 0)),
        pl.BlockSpec((None, 1, d), lambda i, j: (i // mod_blocks, 0, 0)),
        pl.BlockSpec((None, d, tn), lambda i, j: (layer, 0, col_block0 + j)),
    ]
    args = [x, gain.reshape(1, d), shift, scale, w]
    if rope is None:
        body = _nmm_kernel
    else:
        head_gain, cos, sin, n_normed_cols = rope
        assert n_normed_cols % tn == 0 and cos.shape[0] % tm == 0
        table_blocks = cos.shape[0] // tm
        in_specs += [
            pl.BlockSpec((1, tn), lambda i, j: (0, j)),
            pl.BlockSpec((tm, HEAD), lambda i, j: (i % table_blocks, 0)),
            pl.BlockSpec((tm, HEAD), lambda i, j: (i % table_blocks, 0)),
        ]
        args += [head_gain, cos, sin]
        body = functools.partial(_nmm_rope_kernel, n_normed_blocks=n_normed_cols // tn)
    return pl.pallas_call(
        body,
        grid=(m // tm, n // tn),
        in_specs=in_specs,
        out_specs=pl.BlockSpec((tm, tn), lambda i, j: (i, j)),
        out_shape=jax.ShapeDtypeStruct((m, n), BF16),
        scratch_shapes=[pltpu.VMEM((tm, d), BF16)],
        compiler_params=_params(("arbitrary", "arbitrary")),
        name=name,
    )(*args)


def _residual(x, y, gain, gate):
    ms = jnp.mean(y * y, axis=-1, keepdims=True)
    return x + gate * (y * lax.rsqrt(ms + EPS) * gain)


def _proj_res_kernel(a_ref, w_ref, x_ref, gain_ref, gate_ref, o_ref):
    y = _dot(a_ref[...], w_ref[...])
    o_ref[...] = _residual(x_ref[...], y, gain_ref[...], gate_ref[...])


def _proj_res_ktiled_kernel(a_ref, w_ref, x_ref, gain_ref, gate_ref, o_ref):
    k = pl.program_id(1)

    @pl.when(k == 0)
    def _():
        o_ref[...] = _dot(a_ref[...], w_ref[...])

    @pl.when(k > 0)
    def _():
        o_ref[...] += _dot(a_ref[...], w_ref[...])

    @pl.when(k == pl.num_programs(1) - 1)
    def _():
        o_ref[...] = _residual(x_ref[...], o_ref[...], gain_ref[...], gate_ref[...])


def _proj_res_call(a, w, x, gain, gate, *, layer=0, name):
    m, k = a.shape
    d = w.shape[2]
    bm = gate.shape[0]
    rows_per_mod = m // bm
    tk = k if k <= d else _pick(k, 512, 2 * HEAD)
    kb = k // tk
    tm = _pick(rows_per_mod, 512 if kb == 1 else 1024, BF16_SUBLANES)
    mod_blocks = rows_per_mod // tm
    return pl.pallas_call(
        _proj_res_kernel if kb == 1 else _proj_res_ktiled_kernel,
        grid=(m // tm, kb),
        in_specs=[
            pl.BlockSpec((tm, tk), lambda i, kk: (i, kk)),
            pl.BlockSpec((None, tk, d), lambda i, kk: (layer, kk, 0)),
            pl.BlockSpec((tm, d), lambda i, kk: (i, 0)),
            pl.BlockSpec((1, d), lambda i, kk: (0, 0)),
            pl.BlockSpec((None, 1, d), lambda i, kk: (i // mod_blocks, 0, 0)),
        ],
        out_specs=pl.BlockSpec((tm, d), lambda i, kk: (i, 0)),
        out_shape=jax.ShapeDtypeStruct((m, d), F32),
        compiler_params=_params(("arbitrary", "arbitrary")),
        name=name,
    )(a, w, x, gain.reshape(1, d), gate)


def _ffn_in_kernel(x_ref, xprev_ref, xnext_ref, gain_ref, shift_ref, scale_ref, wg_ref, wu_ref, cw_ref, cb_ref,
                   o_ref, h_ref, *, blocks_per_seq):
    i = pl.program_id(0)
    tm = x_ref.shape[0]
    halo = BF16_SUBLANES

    @pl.when(pl.program_id(1) == 0)
    def _():
        gain, shift, scale = gain_ref[...], shift_ref[...], scale_ref[...]
        seq_block = i % blocks_per_seq
        h_prev = jnp.where(seq_block > 0, _norm_mod(xprev_ref[...], gain, shift, scale), 0.0)
        h_next = jnp.where(seq_block < blocks_per_seq - 1, _norm_mod(xnext_ref[...], gain, shift, scale), 0.0)
        zeros = jnp.zeros((halo - h_prev.shape[0], h_prev.shape[1]), F32)
        h_ref[0:halo, :] = jnp.concatenate([zeros, h_prev], axis=0).astype(BF16)
        _norm_mod_rows(x_ref, h_ref, halo, gain, shift, scale)
        h_ref[halo + tm:, :] = jnp.concatenate([h_next, zeros], axis=0).astype(BF16)

    gate = _dot(h_ref[...], wg_ref[...])
    up = _dot(h_ref[halo:halo + tm, :], wu_ref[...])
    rows = gate.shape[0]
    g_prev = pltpu.roll(gate, 1, axis=0)[halo:halo + tm, :]
    g_next = pltpu.roll(gate, rows - 1, axis=0)[halo:halo + tm, :]
    conv = g_prev * cw_ref[0:1, :] + gate[halo:halo + tm, :] * cw_ref[1:2, :] + g_next * cw_ref[2:3, :] + cb_ref[...]
    o_ref[...] = (_silu(conv) * up).astype(o_ref.dtype)


def _ffn_in_call(x, gain, shift, scale, w, conv_w, conv_b, *, layer, seq_len, name):
    m, d = x.shape
    f = w.shape[2] // 2
    bm = shift.shape[0]
    rows_per_mod = m // bm
    tm = _pick(seq_len, 1024, BF16_SUBLANES)
    tn = _pick(f, 512, HEAD)
    assert rows_per_mod % tm == 0
    mod_blocks = rows_per_mod // tm
    nb = f // tn
    xh = 8
    tiles_per_tm = tm // xh
    last_tile = m // xh - 1
    body = functools.partial(_ffn_in_kernel, blocks_per_seq=seq_len // tm)
    return pl.pallas_call(
        body,
        grid=(m // tm, nb),
        in_specs=[
            pl.BlockSpec((tm, d), lambda i, j: (i, 0)),
            pl.BlockSpec((xh, d), lambda i, j: (jnp.maximum(i * tiles_per_tm - 1, 0), 0)),
            pl.BlockSpec((xh, d), lambda i, j: (jnp.minimum((i + 1) * tiles_per_tm, last_tile), 0)),
            pl.BlockSpec((1, d), lambda i, j: (0, 0)),
            pl.BlockSpec((None, 1, d), lambda i, j: (i // mod_blocks, 0, 0)),
            pl.BlockSpec((None, 1, d), lambda i, j: (i // mod_blocks, 0, 0)),
            pl.BlockSpec((None, d, tn), lambda i, j: (layer, 0, j)),
            pl.BlockSpec((None, d, tn), lambda i, j: (layer, 0, nb + j)),
            pl.BlockSpec((3, tn), lambda i, j: (0, j)),
            pl.BlockSpec((1, tn), lambda i, j: (0, j)),
        ],
        out_specs=pl.BlockSpec((tm, tn), lambda i, j: (i, j)),
        out_shape=jax.ShapeDtypeStruct((m, f), BF16),
        scratch_shapes=[pltpu.VMEM((tm + 2 * BF16_SUBLANES, d), BF16)],
        compiler_params=_params(("arbitrary", "arbitrary")),
        name=name,
    )(x, x, x, gain.reshape(1, d), shift, scale, w, w, conv_w, conv_b.reshape(1, f))


def _gla_gates(f_raw, lb, within):
    half = 0.5 * (1.0 - lb)
    ht = half * jnp.tanh(0.5 * f_raw.astype(F32))
    lf = jnp.log((1.0 - half) + ht)
    ones = jnp.where(within, 1.0, 0.0).astype(BF16)
    lf_hi = lf.astype(BF16)
    lf_lo = (lf - lf_hi.astype(F32)).astype(BF16)
    return half - ht, _dot(ones, lf_hi) + _dot(ones, lf_lo)


def _gla_pairs(qs, kk, b, forward, within):
    c = qs.shape[0]
    a_row = c // 2 - 1 if forward else c // 2
    e_row = c - 1 if forward else 0
    anchor = b[a_row:a_row + 1, :]
    b_end = b[e_row:e_row + 1, :]
    qa = qs * jnp.exp(b - anchor)
    ka = kk * jnp.exp(anchor - b)
    scores = jnp.where(within, _dot_nt(qa.astype(BF16), ka.astype(BF16)), 0.0).astype(BF16)
    return (scores, (qa * jnp.exp(anchor)).astype(BF16), (ka * jnp.exp(b_end - anchor)).astype(BF16),
            jnp.exp(b_end))


def _gla_pairs_exact(qs, kk, b, forward, within, tmp):
    tb_ref, tq_ref, ts_ref = tmp
    c = qs.shape[0]
    b_end = b[c - 1:c, :] if forward else b[0:1, :]
    tb_ref[...] = b
    tq_ref[...] = qs
    ones = jnp.ones((8, qs.shape[1]), BF16)

    def row(t, carry):
        rel = jnp.minimum(tb_ref[pl.ds(t, 1), :] - b, 0.0)
        e = (tq_ref[pl.ds(t, 1), :] * kk) * jnp.exp(rel)
        ts_ref[pl.ds(t, 1), :] = _dot_nt(ones, e.astype(BF16))[0:1, :]
        return carry

    lax.fori_loop(0, c, row, 0)
    scores = jnp.where(within, ts_ref[...], 0.0).astype(BF16)
    return scores, (qs * jnp.exp(b)).astype(BF16), (kk * jnp.exp(b_end - b)).astype(BF16), jnp.exp(b_end)


def _gla_readout(o, g_raw, o_gain):
    ms = jnp.mean(o * o, axis=-1, keepdims=True)
    y = o * lax.rsqrt(ms + EPS) * o_gain
    return (y * _silu(g_raw.astype(F32))).astype(BF16)


def _gla_kernel(qc, ffc, fbc, vc, gc, ql, ffl, fbl, vl, gl, lb_ref, og_ref, yc_ref, yl_ref,
                o_ref, qb_ref, ke_ref, dec_ref, st_ref, tb_ref, tq_ref, ts_ref):
    c = GLA_CHUNK
    tc = qc.shape[0]
    n_ctx = tc // c
    n_lat = ql.shape[0] // c
    lb = (lb_ref[0], lb_ref[1])
    row = lax.broadcasted_iota(jnp.int32, (c, c), 0)
    col = lax.broadcasted_iota(jnp.int32, (c, c), 1)
    within = (col <= row, col >= row)

    def rows(start):
        return pl.ds(start if isinstance(start, int) else pl.multiple_of(start, c), c)

    def local(refs, src0, dst0, chunk0, n, exact):
        q, ff, fb, v = refs
        src = [rows(src0 + j * c) for j in range(n)]
        dst = [rows(dst0 + j * c) for j in range(n)]
        qs = [_silu(q[src[j], :].astype(F32)) * (HEAD ** -0.5) for j in range(n)]
        gates = [[_gla_gates(f[src[j], :], lb[d], within[d]) for d, f in enumerate((ff, fb))] for j in range(n)]
        scores = []
        for j in range(n):
            per_dir = []
            for d in range(2):
                kk, b = gates[j][d]
                if exact:
                    sc, qb, ke, dec = _gla_pairs_exact(qs[j], kk, b, d == 0, within[d], (tb_ref, tq_ref, ts_ref))
                else:
                    sc, qb, ke, dec = _gla_pairs(qs[j], kk, b, d == 0, within[d])
                qb_ref[d, dst[j], :] = qb
                ke_ref[d, dst[j], :] = ke
                dec_ref[d, chunk0 + j] = dec
                per_dir.append(sc)
            scores.append(per_dir)
        for j in range(n):
            vj = v[src[j], :]
            o_ref[dst[j], :] = _dot(scores[j][0], vj) + _dot(scores[j][1], vj)

    def phase1(exact):
        run = 1 if exact else math.gcd(n_lat, GLA_RUN)
        run_ctx = math.gcd(n_ctx, run)
        for i in range(0, n_ctx, run_ctx):
            local((qc, ffc, fbc, vc), i * c, i * c, i, run_ctx, exact)

        def body(i, carry):
            src0 = pl.multiple_of(i * (run * c), c)
            local((ql, ffl, fbl, vl), src0, tc + src0, n_ctx + i * run, run, exact)
            return carry

        lax.fori_loop(0, n_lat // run, body, 0)

    safe = jnp.min(jnp.minimum(lb[0], lb[1])) >= GLA_SAFE_LB

    @pl.when(safe)
    def _():
        phase1(False)

    @pl.when(jnp.logical_not(safe))
    def _():
        phase1(True)

    st_ref[...] = jnp.zeros_like(st_ref)

    def steps(v, first, n, n_seg, row0, chunk0):
        order = [[first + s for s in range(n)], [n_seg - 1 - first - s for s in range(n)]]
        src = [[rows(j * c) for j in order[d]] for d in range(2)]
        dst = [[rows(row0 + j * c) for j in order[d]] for d in range(2)]
        kv = [[_dot_tn(v[src[d][s], :], ke_ref[d, dst[d][s], :]) for s in range(n)] for d in range(2)]
        for d in range(2):
            st = st_ref[d]
            for s in range(n):
                o_ref[dst[d][s], :] += _dot_nt(qb_ref[d, dst[d][s], :], st.astype(BF16))
                st = st * dec_ref[d, chunk0 + order[d][s]] + kv[d][s]
            st_ref[d] = st

    steps(vc, 0, n_ctx, n_ctx, 0, 0)
    run = math.gcd(n_lat, GLA_RUN)

    def steps_body(i, carry):
        steps(vl, i * run, run, n_lat, tc, n_ctx)
        return carry

    lax.fori_loop(0, n_lat // run, steps_body, 0)

    o_gain = og_ref[...]
    for i in range(n_ctx):
        r = slice(i * c, (i + 1) * c)
        yc_ref[r, :] = _gla_readout(o_ref[r, :], gc[r, :], o_gain)

    def out_body(i, carry):
        src = pl.ds(pl.multiple_of(i * c, c), c)
        dst = pl.ds(pl.multiple_of(tc + i * c, c), c)
        yl_ref[src, :] = _gla_readout(o_ref[dst, :], gl[src, :], o_gain)
        return carry

    lax.fori_loop(0, n_lat, out_body, 0, unroll=math.gcd(n_lat, 4))


def _gla_call(p_ctx, p_lat, lower_bound, o_gain):
    b, tc, d5 = p_ctx.shape
    t = p_lat.shape[1]
    d = d5 // 5
    heads = d // HEAD
    assert tc % GLA_CHUNK == 0 and t % GLA_CHUNK == 0

    def col(tt, part):
        return pl.BlockSpec((None, tt, HEAD), lambda bi, h: (bi, 0, part * heads + h))

    out_spec = lambda tt: pl.BlockSpec((None, tt, HEAD), lambda bi, h: (bi, 0, h))
    return pl.pallas_call(
        _gla_kernel,
        grid=(b, heads),
        in_specs=[col(tc, p) for p in range(5)] + [col(t, p) for p in range(5)] + [
            pl.BlockSpec((2, None, 1, HEAD), lambda bi, h: (0, h, 0, 0)),
            pl.BlockSpec((1, HEAD), lambda bi, h: (0, 0)),
        ],
        out_specs=[out_spec(tc), out_spec(t)],
        out_shape=[jax.ShapeDtypeStruct((b, tc, d), BF16), jax.ShapeDtypeStruct((b, t, d), BF16)],
        scratch_shapes=[
            pltpu.VMEM((tc + t, HEAD), F32),
            pltpu.VMEM((2, tc + t, HEAD), BF16),
            pltpu.VMEM((2, tc + t, HEAD), BF16),
            pltpu.VMEM((2, (tc + t) // GLA_CHUNK, 1, HEAD), F32),
            pltpu.VMEM((2, HEAD, HEAD), F32),
            pltpu.VMEM((GLA_CHUNK, HEAD), F32),
            pltpu.VMEM((GLA_CHUNK, HEAD), F32),
            pltpu.VMEM((GLA_CHUNK, GLA_CHUNK), F32),
        ],
        compiler_params=_params(("arbitrary", "arbitrary")),
        name="hgrn_scan",
    )(*([p_ctx] * 5), *([p_lat] * 5), lower_bound.reshape(2, heads, 1, HEAD), o_gain.reshape(1, HEAD))


def _flash_kernel(q_ref, kc_ref, vc_ref, kl_ref, vl_ref, o_ref, *, tk, unroll):
    tq = q_ref.shape[0]
    group = q_ref.shape[1] // HEAD
    qs = [q_ref[:, g * HEAD:(g + 1) * HEAD] for g in range(group)]

    def attend(carry, k, v):
        v_ext = jnp.concatenate([v, jnp.ones_like(v)], axis=1)
        out = []
        for g in range(group):
            m, acc = carry[g]
            s = _dot_nt(qs[g], k)
            m_new = jnp.maximum(m, jnp.max(s, axis=-1, keepdims=True))
            p = jnp.exp2(s - m_new).astype(BF16)
            out.append((m_new, jnp.exp2(m - m_new) * acc + _dot(p, v_ext)))
        return tuple(out)

    def body(j, carry):
        r = pl.ds(pl.multiple_of(j * tk, tk), tk)
        return attend(carry, kl_ref[r, :], vl_ref[r, :])

    init = tuple((jnp.full((tq, 1), -jnp.inf, F32), jnp.zeros((tq, 2 * HEAD), F32)) for _ in range(group))
    carry = attend(init, kc_ref[...], vc_ref[...])
    final = lax.fori_loop(0, kl_ref.shape[0] // tk, body, carry, unroll=unroll)
    for g in range(group):
        acc = final[g][1]
        o_ref[:, g * HEAD:(g + 1) * HEAD] = (acc[:, :HEAD] / acc[:, HEAD:]).astype(o_ref.dtype)


def _flash_call(qkv, kv_ctx, *, heads):
    b, t, n = qkv.shape
    tc = kv_ctx.shape[1]
    kvh = (n // HEAD - heads) // 2
    group = heads // kvh
    tq = _pick(t, 512, BF16_SUBLANES)
    tk = _pick(t, 1024, 2 * HEAD)
    body = functools.partial(_flash_kernel, tk=tk, unroll=math.gcd(t // tk, 8))
    return pl.pallas_call(
        body,
        grid=(b, kvh, t // tq),
        in_specs=[
            pl.BlockSpec((None, tq, group * HEAD), lambda bi, h, i: (bi, i, h)),
            pl.BlockSpec((None, tc, HEAD), lambda bi, h, i: (bi, 0, h)),
            pl.BlockSpec((None, tc, HEAD), lambda bi, h, i: (bi, 0, kvh + h)),
            pl.BlockSpec((None, t, HEAD), lambda bi, h, i: (bi, 0, heads + h)),
            pl.BlockSpec((None, t, HEAD), lambda bi, h, i: (bi, 0, heads + kvh + h)),
        ],
        out_specs=pl.BlockSpec((None, tq, group * HEAD), lambda bi, h, i: (bi, i, h)),
        out_shape=jax.ShapeDtypeStruct((b, t, heads * HEAD), BF16),
        compiler_params=_params(("arbitrary", "arbitrary", "arbitrary")),
        name="gqa_flash",
    )(qkv, kv_ctx, kv_ctx, qkv, qkv)


def _rope_tables(t):
    pos = jnp.arange(t, dtype=jnp.int32)
    rows = (pos // GRID_W).astype(F32)
    cols = (pos % GRID_W).astype(F32)
    axis_dim = HEAD // 2
    inv_freq = ROPE_THETA ** (-jnp.arange(0, axis_dim, 2, dtype=F32) / axis_dim)
    ang = jnp.concatenate([rows[:, None] * inv_freq, cols[:, None] * inv_freq], axis=-1)
    cos, sin = jnp.cos(ang), jnp.sin(ang)
    return jnp.concatenate([cos, cos], axis=-1), jnp.concatenate([-sin, sin], axis=-1)


def _split_mods(mods, b):
    d = mods.shape[1] // N_MOD
    lat = [mods[:b, i * d:(i + 1) * d].reshape(b, 1, d) for i in range(N_MOD)]
    ctx = [mods[b:b + 1, i * d:(i + 1) * d].reshape(1, 1, d) for i in range(N_MOD)]
    return lat, ctx


def kernel(x, c, ctx, c_ctx, ada_w, ada_b, norm_mix_pre, norm_mix_post, norm_ffn_pre, norm_ffn_post, hgrn_w_in, hgrn_lb_logits, hgrn_o_norm, hgrn_w_out, attn_w_qkv, attn_q_norm, attn_k_norm, attn_w_out, ffn_w_in, ffn_conv_w, ffn_conv_b, ffn_w_out):
    b, t, d = x.shape
    tc = ctx.shape[1]
    heads = d // HEAD
    assert ada_w.shape[0] == 2 and b + 1 <= ADA_ROWS

    cond = jnp.zeros((ADA_ROWS, d), F32).at[:b].set(c).at[b].set(c_ctx)
    mods = _ada_call(cond, ada_w, ada_b)
    x_lat = x.reshape(b * t, d)
    x_ctx = ctx.reshape(b * tc, d)

    (sh_ml, sc_ml, gt_ml, sh_fl, sc_fl, gt_fl), (sh_mc, sc_mc, gt_mc, sh_fc, sc_fc, gt_fc) = _split_mods(mods[0], b)
    lower_bound = jnp.cumsum(jax.nn.softmax(hgrn_lb_logits.astype(F32), axis=1), axis=1)[:, 0]
    hgrn_w_in, hgrn_w_out, attn_w_out = hgrn_w_in.astype(BF16), hgrn_w_out.astype(BF16), attn_w_out.astype(BF16)
    ffn_w_in, ffn_w_out = ffn_w_in.astype(BF16), ffn_w_out.astype(BF16)
    p_lat = _nmm_call(x_lat, norm_mix_pre[0], sh_ml, sc_ml, hgrn_w_in, name="hgrn_in_lat")
    p_ctx = _nmm_call(x_ctx, norm_mix_pre[0], sh_mc, sc_mc, hgrn_w_in, name="hgrn_in_ctx")
    y_ctx, y_lat = _gla_call(p_ctx.reshape(b, tc, 5 * d), p_lat.reshape(b, t, 5 * d), lower_bound, hgrn_o_norm[0])
    x_lat = _proj_res_call(y_lat.reshape(b * t, d), hgrn_w_out, x_lat, norm_mix_post[0], gt_ml, name="hgrn_out_lat")
    x_ctx = _proj_res_call(y_ctx.reshape(b * tc, d), hgrn_w_out, x_ctx, norm_mix_post[0], gt_mc, name="hgrn_out_ctx")
    a = _ffn_in_call(x_lat, norm_ffn_pre[0], sh_fl, sc_fl, ffn_w_in, ffn_conv_w[0], ffn_conv_b[0],
                     layer=0, seq_len=t, name="ffn0_in_lat")
    x_lat = _proj_res_call(a, ffn_w_out, x_lat, norm_ffn_post[0], gt_fl, layer=0, name="ffn0_out_lat")
    a = _ffn_in_call(x_ctx, norm_ffn_pre[0], sh_fc, sc_fc, ffn_w_in, ffn_conv_w[0], ffn_conv_b[0],
                     layer=0, seq_len=tc, name="ffn0_in_ctx")
    x_ctx = _proj_res_call(a, ffn_w_out, x_ctx, norm_ffn_post[0], gt_fc, layer=0, name="ffn0_out_ctx")

    (sh_ml, sc_ml, gt_ml, sh_fl, sc_fl, gt_fl), (sh_mc, sc_mc, _, _, _, _) = _split_mods(mods[1], b)
    n_qkv = attn_w_qkv.shape[2]
    kvh = (n_qkv // HEAD - heads) // 2
    perm = jnp.concatenate([jnp.arange(0, HEAD, 2), jnp.arange(1, HEAD, 2)])
    head_perm = (jnp.arange(heads + kvh)[:, None] * HEAD + perm[None, :]).reshape(-1)
    col_perm = jnp.concatenate([head_perm, jnp.arange((heads + kvh) * HEAD, n_qkv)])
    w_qkv = attn_w_qkv[:, :, col_perm].astype(BF16)
    head_gain = jnp.concatenate([
        jnp.tile(attn_q_norm[0][perm] * (LOG2_E * HEAD ** -0.5), heads),
        jnp.tile(attn_k_norm[0][perm], kvh),
        jnp.ones((kvh * HEAD,), F32),
    ]).reshape(1, n_qkv)
    cos, sin = _rope_tables(t)
    qk_cols = (heads + kvh) * HEAD
    qkv = _nmm_call(x_lat, norm_mix_pre[1], sh_ml, sc_ml, w_qkv,
                    rope=(head_gain, cos, sin, qk_cols), name="attn_qkv_lat")
    q_cols = heads * HEAD
    kv_ctx = _nmm_call(x_ctx, norm_mix_pre[1], sh_mc, sc_mc, w_qkv, col0=q_cols,
                       rope=(head_gain[:, q_cols:], jnp.ones((tc, HEAD), F32), jnp.zeros((tc, HEAD), F32),
                             kvh * HEAD), name="attn_kv_ctx")
    qkv = qkv.reshape(b, t, n_qkv)
    kv_ctx = kv_ctx.reshape(b, tc, 2 * kvh * HEAD)
    o = _flash_call(qkv, kv_ctx, heads=heads)
    x_lat = _proj_res_call(o.reshape(b * t, d), attn_w_out, x_lat, norm_mix_post[1], gt_ml, name="attn_out_lat")
    a = _ffn_in_call(x_lat, norm_ffn_pre[1], sh_fl, sc_fl, ffn_w_in, ffn_conv_w[1], ffn_conv_b[1],
                     layer=1, seq_len=t, name="ffn1_in_lat")
    x_lat = _proj_res_call(a, ffn_w_out, x_lat, norm_ffn_post[1], gt_fl, layer=1, name="ffn1_out_lat")
    return x_lat.reshape(b, t, d)
```

```python
import functools
import math

import jax
import jax.numpy as jnp
from jax import lax
from jax.experimental import pallas as pl
from jax.experimental.pallas import tpu as pltpu

F32 = jnp.float32
BF16 = jnp.bfloat16

EPS = 1e-6
GRID_W = 64
ROPE_THETA = 10000.0
LOG2_E = math.log2(math.e)
N_MOD = 6
HEAD = 128
ROW_GROUP = 16
GLA_CHUNK = 128
GLA_RUN = 8
GLA_SAFE_LB = math.exp(-80.0 / (GLA_CHUNK // 2))
BF16_SUBLANES = 16
ADA_ROWS = 8
V7X_VMEM_BYTES = 64 * 1024 * 1024
VMEM_LIMIT = V7X_VMEM_BYTES - 6 * 1024 * 1024


def _pick(n, target, align):
    if n <= target:
        return n
    for t in range(target - target % align, 0, -align):
        if n % t == 0:
            return t
    return n


def _params(sem):
    return pltpu.CompilerParams(dimension_semantics=sem, vmem_limit_bytes=VMEM_LIMIT)


def _sigmoid(x):
    return 0.5 * jnp.tanh(0.5 * x) + 0.5


def _silu(x):
    return x * _sigmoid(x)


def _dot(a, b):
    return jnp.dot(a, b, preferred_element_type=F32)


def _dot_nt(a, b):
    return lax.dot_general(a, b, (((1,), (1,)), ((), ())), preferred_element_type=F32)


def _dot_tn(a, b):
    return lax.dot_general(a, b, (((0,), (0,)), ((), ())), preferred_element_type=F32)


def _ada_kernel(c_ref, w_ref, b_ref, o_ref):
    sc = _silu(c_ref[...]).astype(BF16)
    o_ref[...] = _dot(sc, w_ref[...].astype(BF16)) + b_ref[...]


def _ada_call(cond, ada_w, ada_b):
    depth, d, n = ada_w.shape
    tn = _pick(n, 1024, HEAD)
    return pl.pallas_call(
        _ada_kernel,
        grid=(depth, n // tn),
        in_specs=[
            pl.BlockSpec((ADA_ROWS, d), lambda l, j: (0, 0)),
            pl.BlockSpec((None, d, tn), lambda l, j: (l, 0, j)),
            pl.BlockSpec((None, 1, tn), lambda l, j: (l, 0, j)),
        ],
        out_specs=pl.BlockSpec((None, ADA_ROWS, tn), lambda l, j: (l, 0, j)),
        out_shape=jax.ShapeDtypeStruct((depth, ADA_ROWS, n), F32),
        compiler_params=_params(("arbitrary", "arbitrary")),
        name="ada_ln",
    )(cond, ada_w, ada_b.reshape(depth, 1, n))


def _norm_mod(x, gain, shift, scale):
    ms = jnp.mean(x * x, axis=-1, keepdims=True)
    return (x * lax.rsqrt(ms + EPS) * gain) * (1.0 + scale) + shift


def _norm_mod_rows(x_ref, n_rows, h_ref, h_row0, gain, shift, scale):
    mult = gain * (1.0 + scale)

    def body(r, carry):
        start = pl.multiple_of(r * ROW_GROUP, ROW_GROUP)
        x = x_ref[pl.ds(start, ROW_GROUP), :]
        ms = jnp.mean(x * x, axis=-1, keepdims=True)
        dst = pl.ds(pl.multiple_of(h_row0 + start, ROW_GROUP), ROW_GROUP)
        h_ref[dst, :] = (x * lax.rsqrt(ms + EPS) * mult + shift).astype(BF16)
        return carry

    groups = n_rows // ROW_GROUP
    lax.fori_loop(0, groups, body, 0, unroll=math.gcd(groups, 4))


def _split_rows(tm):
    half = tm // 2
    return half if half % ROW_GROUP == 0 else tm


def _nmm_kernel(x_ref, gain_ref, shift_ref, scale_ref, w_ref, o_ref, h_ref):
    j = pl.program_id(1)
    tm = x_ref.shape[0]
    lead = _split_rows(tm)

    @pl.when(j == 0)
    def _():
        gain, shift, scale = gain_ref[...], shift_ref[...], scale_ref[...]
        _norm_mod_rows(x_ref, lead, h_ref, 0, gain, shift, scale)
        if lead < tm:
            h_ref[lead:, :] = _norm_mod(x_ref[lead:, :], gain, shift, scale).astype(BF16)
        o_ref[:lead, :] = _dot(h_ref[:lead, :], w_ref[...]).astype(o_ref.dtype)
        if lead < tm:
            o_ref[lead:, :] = _dot(h_ref[lead:, :], w_ref[...]).astype(o_ref.dtype)

    @pl.when(j > 0)
    def _():
        o_ref[...] = _dot(h_ref[...], w_ref[...]).astype(o_ref.dtype)


def _nmm_rope_kernel(x_ref, gain_ref, shift_ref, scale_ref, w_ref, hg_ref, cos_ref, sin_ref,
                     o_ref, h_ref, *, n_normed_blocks):
    j = pl.program_id(1)

    @pl.when(j == 0)
    def _():
        _norm_mod_rows(x_ref, x_ref.shape[0], h_ref, 0, gain_ref[...], shift_ref[...], scale_ref[...])

    acc = _dot(h_ref[...], w_ref[...])

    @pl.when(j < n_normed_blocks)
    def _():
        cos = cos_ref[...]
        sin = sin_ref[...]
        for hh in range(acc.shape[1] // HEAD):
            cols = slice(hh * HEAD, (hh + 1) * HEAD)
            a = acc[:, cols]
            ms = jnp.mean(a * a, axis=-1, keepdims=True)
            a = a * lax.rsqrt(ms + EPS) * hg_ref[:, cols]
            a = a * cos + pltpu.roll(a, HEAD // 2, axis=1) * sin
            o_ref[:, cols] = a.astype(o_ref.dtype)

    @pl.when(j >= n_normed_blocks)
    def _():
        o_ref[...] = acc.astype(o_ref.dtype)


def _nmm_call(x, gain, shift, scale, w, *, layer=0, col0=0, rope=None, name):
    m, d = x.shape
    n = w.shape[2] - col0
    bm = shift.shape[0]
    rows_per_mod = m // bm
    row_period = rows_per_mod if rope is None else math.gcd(rows_per_mod, rope[1].shape[0])
    tm = _pick(row_period, 1024, BF16_SUBLANES)
    tn = _pick(n, 1024, HEAD) if rope is None else _pick(math.gcd(n, rope[3]), 512, HEAD)
    mod_blocks = rows_per_mod // tm
    assert col0 % tn == 0
    col_block0 = col0 // tn
    in_specs = [
        pl.BlockSpec((tm, d), lambda i, j: (i, 0)),
        pl.BlockSpec((1, d), lambda i, j: (0, 0)),
        pl.BlockSpec((None, 1, d), lambda i, j: (i // mod_blocks, 0, 0)),
        pl.BlockSpec((None, 1, d), lambda i, j: (i // mod_blocks, 0, 0)),
        pl.BlockSpec((None, d, tn), lambda i, j: (layer, 0, col_block0 + j)),
    ]
    args = [x, gain.reshape(1, d), shift, scale, w]
    if rope is None:
        body = _nmm_kernel
    else:
        head_gain, cos, sin, n_normed_cols = rope
        assert n_normed_cols % tn == 0 and cos.shape[0] % tm == 0
        table_blocks = cos.shape[0] // tm
        in_specs += [
            pl.BlockSpec((1, tn), lambda i, j: (0, j)),
            pl.BlockSpec((tm, HEAD), lambda i, j: (i % table_blocks, 0)),
            pl.BlockSpec((tm, HEAD), lambda i, j: (i % table_blocks, 0)),
        ]
        args += [head_gain, cos, sin]
        body = functools.partial(_nmm_rope_kernel, n_normed_blocks=n_normed_cols // tn)
    return pl.pallas_call(
        body,
        grid=(m // tm, n // tn),
        in_specs=in_specs,
        out_specs=pl.BlockSpec((tm, tn), lambda i, j: (i, j)),
        out_shape=jax.ShapeDtypeStruct((m, n), BF16),
        scratch_shapes=[pltpu.VMEM((tm, d), BF16)],
        compiler_params=_params(("arbitrary", "arbitrary")),
        name=name,
    )(*args)


def _residual(x, y, gain, gate):
    ms = jnp.mean(y * y, axis=-1, keepdims=True)
    return x + gate * (y * lax.rsqrt(ms + EPS) * gain)


def _proj_res_kernel(a_ref, w_ref, x_ref, gain_ref, gate_ref, o_ref):
    y = _dot(a_ref[...], w_ref[...])
    o_ref[...] = _residual(x_ref[...], y, gain_ref[...], gate_ref[...])


def _proj_res_ktiled_kernel(a_ref, w_ref, x_ref, gain_ref, gate_ref, o_ref):
    k = pl.program_id(1)

    @pl.when(k == 0)
    def _():
        o_ref[...] = _dot(a_ref[...], w_ref[...])

    @pl.when(k > 0)
    def _():
        o_ref[...] += _dot(a_ref[...], w_ref[...])

    @pl.when(k == pl.num_programs(1) - 1)
    def _():
        o_ref[...] = _residual(x_ref[...], o_ref[...], gain_ref[...], gate_ref[...])


def _proj_res_call(a, w, x, gain, gate, *, layer=0, name):
    m, k = a.shape
    d = w.shape[2]
    bm = gate.shape[0]
    rows_per_mod = m // bm
    tk = k if k <= d else _pick(k, 512, 2 * HEAD)
    kb = k // tk
    tm = _pick(rows_per_mod, 512 if kb == 1 else 1024, BF16_SUBLANES)
    mod_blocks = rows_per_mod // tm
    return pl.pallas_call(
        _proj_res_kernel if kb == 1 else _proj_res_ktiled_kernel,
        grid=(m // tm, kb),
        in_specs=[
            pl.BlockSpec((tm, tk), lambda i, kk: (i, kk)),
            pl.BlockSpec((None, tk, d), lambda i, kk: (layer, kk, 0)),
            pl.BlockSpec((tm, d), lambda i, kk: (i, 0)),
            pl.BlockSpec((1, d), lambda i, kk: (0, 0)),
            pl.BlockSpec((None, 1, d), lambda i, kk: (i // mod_blocks, 0, 0)),
        ],
        out_specs=pl.BlockSpec((tm, d), lambda i, kk: (i, 0)),
        out_shape=jax.ShapeDtypeStruct((m, d), F32),
        compiler_params=_params(("arbitrary", "arbitrary")),
        name=name,
    )(a, w, x, gain.reshape(1, d), gate)


def _ffn_in_kernel(x_ref, xprev_ref, xnext_ref, gain_ref, shift_ref, scale_ref, wg_ref, wu_ref, cw_ref, cb_ref,
                   o_ref, h_ref, *, blocks_per_seq):
    i = pl.program_id(0)
    j = pl.program_id(1)
    tm = x_ref.shape[0]
    halo = BF16_SUBLANES
    lead = _split_rows(tm)

    def gated(gate, up):
        rows = gate.shape[0]
        g_prev = pltpu.roll(gate, 1, axis=0)[halo:halo + tm, :]
        g_next = pltpu.roll(gate, rows - 1, axis=0)[halo:halo + tm, :]
        conv = (g_prev * cw_ref[0:1, :] + gate[halo:halo + tm, :] * cw_ref[1:2, :] + g_next * cw_ref[2:3, :]
                + cb_ref[...])
        o_ref[...] = (_silu(conv) * up).astype(o_ref.dtype)

    @pl.when(j == 0)
    def _():
        gain, shift, scale = gain_ref[...], shift_ref[...], scale_ref[...]
        seq_block = i % blocks_per_seq
        h_prev = jnp.where(seq_block > 0, _norm_mod(xprev_ref[...], gain, shift, scale), 0.0)
        h_next = jnp.where(seq_block < blocks_per_seq - 1, _norm_mod(xnext_ref[...], gain, shift, scale), 0.0)
        zeros = jnp.zeros((halo - h_prev.shape[0], h_prev.shape[1]), F32)
        h_ref[0:halo, :] = jnp.concatenate([zeros, h_prev], axis=0).astype(BF16)
        h_ref[halo + tm:, :] = jnp.concatenate([h_next, zeros], axis=0).astype(BF16)
        _norm_mod_rows(x_ref, lead, h_ref, halo, gain, shift, scale)
        if lead == tm:
            gated(_dot(h_ref[...], wg_ref[...]), _dot(h_ref[halo:halo + tm, :], wu_ref[...]))
        else:
            cut = halo + lead
            h_ref[cut:halo + tm, :] = _norm_mod(x_ref[lead:, :], gain, shift, scale).astype(BF16)
            gate = jnp.concatenate([_dot(h_ref[:cut, :], wg_ref[...]), _dot(h_ref[cut:, :], wg_ref[...])], axis=0)
            up = jnp.concatenate([_dot(h_ref[halo:cut, :], wu_ref[...]),
                                  _dot(h_ref[cut:halo + tm, :], wu_ref[...])], axis=0)
            gated(gate, up)

    @pl.when(j > 0)
    def _():
        gated(_dot(h_ref[...], wg_ref[...]), _dot(h_ref[halo:halo + tm, :], wu_ref[...]))


def _ffn_in_call(x, gain, shift, scale, w, conv_w, conv_b, *, layer, seq_len, name):
    m, d = x.shape
    f = w.shape[2] // 2
    bm = shift.shape[0]
    rows_per_mod = m // bm
    tm = _pick(seq_len, 1024, BF16_SUBLANES)
    tn = _pick(f, 512, HEAD)
    assert rows_per_mod % tm == 0
    mod_blocks = rows_per_mod // tm
    nb = f // tn
    xh = 8
    tiles_per_tm = tm // xh
    last_tile = m // xh - 1
    body = functools.partial(_ffn_in_kernel, blocks_per_seq=seq_len // tm)
    return pl.pallas_call(
        body,
        grid=(m // tm, nb),
        in_specs=[
            pl.BlockSpec((tm, d), lambda i, j: (i, 0)),
            pl.BlockSpec((xh, d), lambda i, j: (jnp.maximum(i * tiles_per_tm - 1, 0), 0)),
            pl.BlockSpec((xh, d), lambda i, j: (jnp.minimum((i + 1) * tiles_per_tm, last_tile), 0)),
            pl.BlockSpec((1, d), lambda i, j: (0, 0)),
            pl.BlockSpec((None, 1, d), lambda i, j: (i // mod_blocks, 0, 0)),
            pl.BlockSpec((None, 1, d), lambda i, j: (i // mod_blocks, 0, 0)),
            pl.BlockSpec((None, d, tn), lambda i, j: (layer, 0, j)),
            pl.BlockSpec((None, d, tn), lambda i, j: (layer, 0, nb + j)),
            pl.BlockSpec((3, tn), lambda i, j: (0, j)),
            pl.BlockSpec((1, tn), lambda i, j: (0, j)),
        ],
        out_specs=pl.BlockSpec((tm, tn), lambda i, j: (i, j)),
        out_shape=jax.ShapeDtypeStruct((m, f), BF16),
        scratch_shapes=[pltpu.VMEM((tm + 2 * BF16_SUBLANES, d), BF16)],
        compiler_params=_params(("arbitrary", "arbitrary")),
        name=name,
    )(x, x, x, gain.reshape(1, d), shift, scale, w, w, conv_w, conv_b.reshape(1, f))


def _gla_gates(f_raw, lb, within):
    half = 0.5 * (1.0 - lb)
    ht = half * jnp.tanh(0.5 * f_raw.astype(F32))
    lf = jnp.log((1.0 - half) + ht)
    ones = jnp.where(within, 1.0, 0.0).astype(BF16)
    lf_hi = lf.astype(BF16)
    lf_lo = (lf - lf_hi.astype(F32)).astype(BF16)
    return half - ht, _dot(ones, lf_hi) + _dot(ones, lf_lo)


def _gla_pairs(qs, kk, b, forward, within):
    c = qs.shape[0]
    a_row = c // 2 - 1 if forward else c // 2
    e_row = c - 1 if forward else 0
    anchor = b[a_row:a_row + 1, :]
    b_end = b[e_row:e_row + 1, :]
    qa = qs * jnp.exp(b - anchor)
    ka = kk * jnp.exp(anchor - b)
    scores = jnp.where(within, _dot_nt(qa.astype(BF16), ka.astype(BF16)), 0.0).astype(BF16)
    return (scores, (qa * jnp.exp(anchor)).astype(BF16), (ka * jnp.exp(b_end - anchor)).astype(BF16),
            jnp.exp(b_end))


def _gla_pairs_exact(qs, kk, b, forward, within, tmp):
    tb_ref, tq_ref, ts_ref = tmp
    c = qs.shape[0]
    b_end = b[c - 1:c, :] if forward else b[0:1, :]
    tb_ref[...] = b
    tq_ref[...] = qs
    ones = jnp.ones((8, qs.shape[1]), BF16)

    def row(t, carry):
        rel = jnp.minimum(tb_ref[pl.ds(t, 1), :] - b, 0.0)
        e = (tq_ref[pl.ds(t, 1), :] * kk) * jnp.exp(rel)
        ts_ref[pl.ds(t, 1), :] = _dot_nt(ones, e.astype(BF16))[0:1, :]
        return carry

    lax.fori_loop(0, c, row, 0)
    scores = jnp.where(within, ts_ref[...], 0.0).astype(BF16)
    return scores, (qs * jnp.exp(b)).astype(BF16), (kk * jnp.exp(b_end - b)).astype(BF16), jnp.exp(b_end)


def _gla_readout(o, g_raw, o_gain):
    ms = jnp.mean(o * o, axis=-1, keepdims=True)
    y = o * lax.rsqrt(ms + EPS) * o_gain
    return (y * _silu(g_raw.astype(F32))).astype(BF16)


def _gla_kernel(qc, ffc, fbc, vc, gc, ql, ffl, fbl, vl, gl, lb_ref, og_ref, yc_ref, yl_ref,
                o_ref, qb_ref, ke_ref, dec_ref, st_ref, tb_ref, tq_ref, ts_ref):
    c = GLA_CHUNK
    tc = qc.shape[0]
    n_ctx = tc // c
    n_lat = ql.shape[0] // c
    lb = (lb_ref[0], lb_ref[1])
    row = lax.broadcasted_iota(jnp.int32, (c, c), 0)
    col = lax.broadcasted_iota(jnp.int32, (c, c), 1)
    within = (col <= row, col >= row)

    def rows(start):
        return pl.ds(start if isinstance(start, int) else pl.multiple_of(start, c), c)

    def local(refs, src0, dst0, chunk0, n, exact):
        q, ff, fb, v = refs
        src = [rows(src0 + j * c) for j in range(n)]
        dst = [rows(dst0 + j * c) for j in range(n)]
        qs = [_silu(q[src[j], :].astype(F32)) * (HEAD ** -0.5) for j in range(n)]
        gates = [[_gla_gates(f[src[j], :], lb[d], within[d]) for d, f in enumerate((ff, fb))] for j in range(n)]
        scores = []
        for j in range(n):
            per_dir = []
            for d in range(2):
                kk, b = gates[j][d]
                if exact:
                    sc, qb, ke, dec = _gla_pairs_exact(qs[j], kk, b, d == 0, within[d], (tb_ref, tq_ref, ts_ref))
                else:
                    sc, qb, ke, dec = _gla_pairs(qs[j], kk, b, d == 0, within[d])
                qb_ref[d, dst[j], :] = qb
                ke_ref[d, dst[j], :] = ke
                dec_ref[d, chunk0 + j] = dec
                per_dir.append(sc)
            scores.append(per_dir)
        for j in range(n):
            vj = v[src[j], :]
            o_ref[dst[j], :] = _dot(scores[j][0], vj) + _dot(scores[j][1], vj)

    def phase1(exact):
        run = 1 if exact else math.gcd(n_lat, GLA_RUN)
        run_ctx = math.gcd(n_ctx, run)
        for i in range(0, n_ctx, run_ctx):
            local((qc, ffc, fbc, vc), i * c, i * c, i, run_ctx, exact)

        def body(i, carry):
            src0 = pl.multiple_of(i * (run * c), c)
            local((ql, ffl, fbl, vl), src0, tc + src0, n_ctx + i * run, run, exact)
            return carry

        lax.fori_loop(0, n_lat // run, body, 0)

    safe = jnp.min(jnp.minimum(lb[0], lb[1])) >= GLA_SAFE_LB

    @pl.when(safe)
    def _():
        phase1(False)

    @pl.when(jnp.logical_not(safe))
    def _():
        phase1(True)

    st_ref[...] = jnp.zeros_like(st_ref)

    def steps(v, first, n, n_seg, row0, chunk0):
        order = [[first + s for s in range(n)], [n_seg - 1 - first - s for s in range(n)]]
        src = [[rows(j * c) for j in order[d]] for d in range(2)]
        dst = [[rows(row0 + j * c) for j in order[d]] for d in range(2)]
        kv = [[_dot_tn(v[src[d][s], :], ke_ref[d, dst[d][s], :]) for s in range(n)] for d in range(2)]
        for d in range(2):
            st = st_ref[d]
            for s in range(n):
                o_ref[dst[d][s], :] += _dot_nt(qb_ref[d, dst[d][s], :], st.astype(BF16))
                st = st * dec_ref[d, chunk0 + order[d][s]] + kv[d][s]
            st_ref[d] = st

    steps(vc, 0, n_ctx, n_ctx, 0, 0)
    run = math.gcd(n_lat, GLA_RUN)

    def steps_body(i, carry):
        steps(vl, i * run, run, n_lat, tc, n_ctx)
        return carry

    lax.fori_loop(0, n_lat // run, steps_body, 0)

    o_gain = og_ref[...]
    for i in range(n_ctx):
        r = slice(i * c, (i + 1) * c)
        yc_ref[r, :] = _gla_readout(o_ref[r, :], gc[r, :], o_gain)

    def out_body(i, carry):
        src = pl.ds(pl.multiple_of(i * c, c), c)
        dst = pl.ds(pl.multiple_of(tc + i * c, c), c)
        yl_ref[src, :] = _gla_readout(o_ref[dst, :], gl[src, :], o_gain)
        return carry

    lax.fori_loop(0, n_lat, out_body, 0, unroll=math.gcd(n_lat, 4))


def _gla_call(p_ctx, p_lat, lower_bound, o_gain):
    b, tc, d5 = p_ctx.shape
    t = p_lat.shape[1]
    d = d5 // 5
    heads = d // HEAD
    assert tc % GLA_CHUNK == 0 and t % GLA_CHUNK == 0

    def col(tt, part):
        return pl.BlockSpec((None, tt, HEAD), lambda bi, h: (bi, 0, part * heads + h))

    out_spec = lambda tt: pl.BlockSpec((None, tt, HEAD), lambda bi, h: (bi, 0, h))
    return pl.pallas_call(
        _gla_kernel,
        grid=(b, heads),
        in_specs=[col(tc, p) for p in range(5)] + [col(t, p) for p in range(5)] + [
            pl.BlockSpec((2, None, 1, HEAD), lambda bi, h: (0, h, 0, 0)),
            pl.BlockSpec((1, HEAD), lambda bi, h: (0, 0)),
        ],
        out_specs=[out_spec(tc), out_spec(t)],
        out_shape=[jax.ShapeDtypeStruct((b, tc, d), BF16), jax.ShapeDtypeStruct((b, t, d), BF16)],
        scratch_shapes=[
            pltpu.VMEM((tc + t, HEAD), F32),
            pltpu.VMEM((2, tc + t, HEAD), BF16),
            pltpu.VMEM((2, tc + t, HEAD), BF16),
            pltpu.VMEM((2, (tc + t) // GLA_CHUNK, 1, HEAD), F32),
            pltpu.VMEM((2, HEAD, HEAD), F32),
            pltpu.VMEM((GLA_CHUNK, HEAD), F32),
            pltpu.VMEM((GLA_CHUNK, HEAD), F32),
            pltpu.VMEM((GLA_CHUNK, GLA_CHUNK), F32),
        ],
        compiler_params=_params(("arbitrary", "arbitrary")),
        name="hgrn_scan",
    )(*([p_ctx] * 5), *([p_lat] * 5), lower_bound.reshape(2, heads, 1, HEAD), o_gain.reshape(1, HEAD))


def _flash_kernel(q_ref, kc_ref, vc_ref, kl_ref, vl_ref, o_ref, *, tk, unroll):
    tq = q_ref.shape[0]
    group = q_ref.shape[1] // HEAD
    qs = [q_ref[:, g * HEAD:(g + 1) * HEAD] for g in range(group)]

    def attend(carry, k, v):
        v_ext = jnp.concatenate([v, jnp.ones_like(v)], axis=1)
        out = []
        for g in range(group):
            m, acc = carry[g]
            s = _dot_nt(qs[g], k)
            m_new = jnp.maximum(m, jnp.max(s, axis=-1, keepdims=True))
            p = jnp.exp2(s - m_new).astype(BF16)
            out.append((m_new, jnp.exp2(m - m_new) * acc + _dot(p, v_ext)))
        return tuple(out)

    def body(j, carry):
        r = pl.ds(pl.multiple_of(j * tk, tk), tk)
        return attend(carry, kl_ref[r, :], vl_ref[r, :])

    init = tuple((jnp.full((tq, 1), -jnp.inf, F32), jnp.zeros((tq, 2 * HEAD), F32)) for _ in range(group))
    carry = attend(init, kc_ref[...], vc_ref[...])
    final = lax.fori_loop(0, kl_ref.shape[0] // tk, body, carry, unroll=unroll)
    for g in range(group):
        acc = final[g][1]
        o_ref[:, g * HEAD:(g + 1) * HEAD] = (acc[:, :HEAD] / acc[:, HEAD:]).astype(o_ref.dtype)


def _flash_call(qkv, kv_ctx, *, heads):
    b, t, n = qkv.shape
    tc = kv_ctx.shape[1]
    kvh = (n // HEAD - heads) // 2
    group = heads // kvh
    tq = _pick(t, 512, BF16_SUBLANES)
    tk = _pick(t, 1024, 2 * HEAD)
    body = functools.partial(_flash_kernel, tk=tk, unroll=math.gcd(t // tk, 8))
    return pl.pallas_call(
        body,
        grid=(b, kvh, t // tq),
        in_specs=[
            pl.BlockSpec((None, tq, group * HEAD), lambda bi, h, i: (bi, i, h)),
            pl.BlockSpec((None, tc, HEAD), lambda bi, h, i: (bi, 0, h)),
            pl.BlockSpec((None, tc, HEAD), lambda bi, h, i: (bi, 0, kvh + h)),
            pl.BlockSpec((None, t, HEAD), lambda bi, h, i: (bi, 0, heads + h)),
            pl.BlockSpec((None, t, HEAD), lambda bi, h, i: (bi, 0, heads + kvh + h)),
        ],
        out_specs=pl.BlockSpec((None, tq, group * HEAD), lambda bi, h, i: (bi, i, h)),
        out_shape=jax.ShapeDtypeStruct((b, t, heads * HEAD), BF16),
        compiler_params=_params(("arbitrary", "arbitrary", "arbitrary")),
        name="gqa_flash",
    )(qkv, kv_ctx, kv_ctx, qkv, qkv)


def _rope_tables(t):
    pos = jnp.arange(t, dtype=jnp.int32)
    rows = (pos // GRID_W).astype(F32)
    cols = (pos % GRID_W).astype(F32)
    axis_dim = HEAD // 2
    inv_freq = ROPE_THETA ** (-jnp.arange(0, axis_dim, 2, dtype=F32) / axis_dim)
    ang = jnp.concatenate([rows[:, None] * inv_freq, cols[:, None] * inv_freq], axis=-1)
    cos, sin = jnp.cos(ang), jnp.sin(ang)
    return jnp.concatenate([cos, cos], axis=-1), jnp.concatenate([-sin, sin], axis=-1)


def _split_mods(mods, b):
    d = mods.shape[1] // N_MOD
    lat = [mods[:b, i * d:(i + 1) * d].reshape(b, 1, d) for i in range(N_MOD)]
    ctx = [mods[b:b + 1, i * d:(i + 1) * d].reshape(1, 1, d) for i in range(N_MOD)]
    return lat, ctx


def kernel(x, c, ctx, c_ctx, ada_w, ada_b, norm_mix_pre, norm_mix_post, norm_ffn_pre, norm_ffn_post, hgrn_w_in, hgrn_lb_logits, hgrn_o_norm, hgrn_w_out, attn_w_qkv, attn_q_norm, attn_k_norm, attn_w_out, ffn_w_in, ffn_conv_w, ffn_conv_b, ffn_w_out):
    b, t, d = x.shape
    tc = ctx.shape[1]
    heads = d // HEAD
    assert ada_w.shape[0] == 2 and b + 1 <= ADA_ROWS

    cond = jnp.zeros((ADA_ROWS, d), F32).at[:b].set(c).at[b].set(c_ctx)
    mods = _ada_call(cond, ada_w, ada_b)
    x_lat = x.reshape(b * t, d)
    x_ctx = ctx.reshape(b * tc, d)

    (sh_ml, sc_ml, gt_ml, sh_fl, sc_fl, gt_fl), (sh_mc, sc_mc, gt_mc, sh_fc, sc_fc, gt_fc) = _split_mods(mods[0], b)
    lower_bound = jnp.cumsum(jax.nn.softmax(hgrn_lb_logits.astype(F32), axis=1), axis=1)[:, 0]
    hgrn_w_in, hgrn_w_out, attn_w_out = hgrn_w_in.astype(BF16), hgrn_w_out.astype(BF16), attn_w_out.astype(BF16)
    ffn_w_in, ffn_w_out = ffn_w_in.astype(BF16), ffn_w_out.astype(BF16)
    p_lat = _nmm_call(x_lat, norm_mix_pre[0], sh_ml, sc_ml, hgrn_w_in, name="hgrn_in_lat")
    p_ctx = _nmm_call(x_ctx, norm_mix_pre[0], sh_mc, sc_mc, hgrn_w_in, name="hgrn_in_ctx")
    y_ctx, y_lat = _gla_call(p_ctx.reshape(b, tc, 5 * d), p_lat.reshape(b, t, 5 * d), lower_bound, hgrn_o_norm[0])
    x_lat = _proj_res_call(y_lat.reshape(b * t, d), hgrn_w_out, x_lat, norm_mix_post[0], gt_ml, name="hgrn_out_lat")
    x_ctx = _proj_res_call(y_ctx.reshape(b * tc, d), hgrn_w_out, x_ctx, norm_mix_post[0], gt_mc, name="hgrn_out_ctx")
    a = _ffn_in_call(x_lat, norm_ffn_pre[0], sh_fl, sc_fl, ffn_w_in, ffn_conv_w[0], ffn_conv_b[0],
                     layer=0, seq_len=t, name="ffn0_in_lat")
    x_lat = _proj_res_call(a, ffn_w_out, x_lat, norm_ffn_post[0], gt_fl, layer=0, name="ffn0_out_lat")
    a = _ffn_in_call(x_ctx, norm_ffn_pre[0], sh_fc, sc_fc, ffn_w_in, ffn_conv_w[0], ffn_conv_b[0],
                     layer=0, seq_len=tc, name="ffn0_in_ctx")
    x_ctx = _proj_res_call(a, ffn_w_out, x_ctx, norm_ffn_post[0], gt_fc, layer=0, name="ffn0_out_ctx")

    (sh_ml, sc_ml, gt_ml, sh_fl, sc_fl, gt_fl), (sh_mc, sc_mc, _, _, _, _) = _split_mods(mods[1], b)
    n_qkv = attn_w_qkv.shape[2]
    kvh = (n_qkv // HEAD - heads) // 2
    perm = jnp.concatenate([jnp.arange(0, HEAD, 2), jnp.arange(1, HEAD, 2)])
    head_perm = (jnp.arange(heads + kvh)[:, None] * HEAD + perm[None, :]).reshape(-1)
    col_perm = jnp.concatenate([head_perm, jnp.arange((heads + kvh) * HEAD, n_qkv)])
    w_qkv = attn_w_qkv[:, :, col_perm].astype(BF16)
    head_gain = jnp.concatenate([
        jnp.tile(attn_q_norm[0][perm] * (LOG2_E * HEAD ** -0.5), heads),
        jnp.tile(attn_k_norm[0][perm], kvh),
        jnp.ones((kvh * HEAD,), F32),
    ]).reshape(1, n_qkv)
    cos, sin = _rope_tables(t)
    qk_cols = (heads + kvh) * HEAD
    qkv = _nmm_call(x_lat, norm_mix_pre[1], sh_ml, sc_ml, w_qkv,
                    rope=(head_gain, cos, sin, qk_cols), name="attn_qkv_lat")
    q_cols = heads * HEAD
    kv_ctx = _nmm_call(x_ctx, norm_mix_pre[1], sh_mc, sc_mc, w_qkv, col0=q_cols,
                       rope=(head_gain[:, q_cols:], jnp.ones((tc, HEAD), F32), jnp.zeros((tc, HEAD), F32),
                             kvh * HEAD), name="attn_kv_ctx")
    qkv = qkv.reshape(b, t, n_qkv)
    kv_ctx = kv_ctx.reshape(b, tc, 2 * kvh * HEAD)
    o = _flash_call(qkv, kv_ctx, heads=heads)
    x_lat = _proj_res_call(o.reshape(b * t, d), attn_w_out, x_lat, norm_mix_post[1], gt_ml, name="attn_out_lat")
    a = _ffn_in_call(x_lat, norm_ffn_pre[1], sh_fl, sc_fl, ffn_w_in, ffn_conv_w[1], ffn_conv_b[1],
                     layer=1, seq_len=t, name="ffn1_in_lat")
    x_lat = _proj_res_call(a, ffn_w_out, x_lat, norm_ffn_post[1], gt_fl, layer=1, name="ffn1_out_lat")
    return x_lat.reshape(b, t, d)
```

```python
import functools
import math

import jax
import jax.numpy as jnp
from jax import lax
from jax.experimental import pallas as pl
from jax.experimental.pallas import tpu as pltpu

F32 = jnp.float32
BF16 = jnp.bfloat16

EPS = 1e-6
GRID_W = 64
ROPE_THETA = 10000.0
LOG2_E = math.log2(math.e)
N_MOD = 6
HEAD = 128
ROW_GROUP = 16
GLA_CHUNK = 128
GLA_RUN = 8
GLA_SAFE_LB = math.exp(-80.0 / (GLA_CHUNK // 2))
BF16_SUBLANES = 16
ADA_ROWS = 8
V7X_VMEM_BYTES = 64 * 1024 * 1024
VMEM_LIMIT = V7X_VMEM_BYTES - 6 * 1024 * 1024


def _pick(n, target, align):
    if n <= target:
        return n
    for t in range(target - target % align, 0, -align):
        if n % t == 0:
            return t
    return n


def _params(sem):
    return pltpu.CompilerParams(dimension_semantics=sem, vmem_limit_bytes=VMEM_LIMIT)


def _sigmoid(x):
    return 0.5 * jnp.tanh(0.5 * x) + 0.5


def _silu(x):
    return x * _sigmoid(x)


def _dot(a, b):
    return jnp.dot(a, b, preferred_element_type=F32)


def _dot_nt(a, b):
    return lax.dot_general(a, b, (((1,), (1,)), ((), ())), preferred_element_type=F32)


def _dot_tn(a, b):
    return lax.dot_general(a, b, (((0,), (0,)), ((), ())), preferred_element_type=F32)


def _ada_kernel(c_ref, w_ref, b_ref, o_ref):
    sc = _silu(c_ref[...]).astype(BF16)
    o_ref[...] = _dot(sc, w_ref[...].astype(BF16)) + b_ref[...]


def _ada_call(cond, ada_w, ada_b):
    depth, d, n = ada_w.shape
    tn = _pick(n, 1024, HEAD)
    return pl.pallas_call(
        _ada_kernel,
        grid=(depth, n // tn),
        in_specs=[
            pl.BlockSpec((ADA_ROWS, d), lambda l, j: (0, 0)),
            pl.BlockSpec((None, d, tn), lambda l, j: (l, 0, j)),
            pl.BlockSpec((None, 1, tn), lambda l, j: (l, 0, j)),
        ],
        out_specs=pl.BlockSpec((None, ADA_ROWS, tn), lambda l, j: (l, 0, j)),
        out_shape=jax.ShapeDtypeStruct((depth, ADA_ROWS, n), F32),
        compiler_params=_params(("arbitrary", "arbitrary")),
        name="ada_ln",
    )(cond, ada_w, ada_b.reshape(depth, 1, n))


def _norm_mod(x, gain, shift, scale):
    ms = jnp.mean(x * x, axis=-1, keepdims=True)
    return (x * lax.rsqrt(ms + EPS) * gain) * (1.0 + scale) + shift


def _norm_mod_rows(x_ref, n_rows, h_ref, h_row0, gain, shift, scale):
    mult = gain * (1.0 + scale)

    def body(r, carry):
        start = pl.multiple_of(r * ROW_GROUP, ROW_GROUP)
        x = x_ref[pl.ds(start, ROW_GROUP), :]
        ms = jnp.mean(x * x, axis=-1, keepdims=True)
        dst = pl.ds(pl.multiple_of(h_row0 + start, ROW_GROUP), ROW_GROUP)
        h_ref[dst, :] = (x * lax.rsqrt(ms + EPS) * mult + shift).astype(BF16)
        return carry

    groups = n_rows // ROW_GROUP
    lax.fori_loop(0, groups, body, 0, unroll=math.gcd(groups, 4))


def _split_rows(tm):
    half = tm // 2
    return half if half % ROW_GROUP == 0 else tm


def _nmm_kernel(x_ref, gain_ref, shift_ref, scale_ref, w_ref, o_ref, h_ref):
    j = pl.program_id(1)
    tm = x_ref.shape[0]
    lead = _split_rows(tm)

    @pl.when(j == 0)
    def _():
        gain, shift, scale = gain_ref[...], shift_ref[...], scale_ref[...]
        _norm_mod_rows(x_ref, lead, h_ref, 0, gain, shift, scale)
        if lead < tm:
            h_ref[lead:, :] = _norm_mod(x_ref[lead:, :], gain, shift, scale).astype(BF16)
        o_ref[:lead, :] = _dot(h_ref[:lead, :], w_ref[...]).astype(o_ref.dtype)
        if lead < tm:
            o_ref[lead:, :] = _dot(h_ref[lead:, :], w_ref[...]).astype(o_ref.dtype)

    @pl.when(j > 0)
    def _():
        o_ref[...] = _dot(h_ref[...], w_ref[...]).astype(o_ref.dtype)


def _nmm_call(x, gain, shift, scale, w, *, layer=0, col0=0, name):
    m, d = x.shape
    n = w.shape[2] - col0
    bm = shift.shape[0]
    rows_per_mod = m // bm
    tm = _pick(rows_per_mod, 1024, BF16_SUBLANES)
    tn = _pick(math.gcd(n, col0) if col0 else n, 1024, HEAD)
    mod_blocks = rows_per_mod // tm
    col_block0 = col0 // tn
    return pl.pallas_call(
        _nmm_kernel,
        grid=(m // tm, n // tn),
        in_specs=[
            pl.BlockSpec((tm, d), lambda i, j: (i, 0)),
            pl.BlockSpec((1, d), lambda i, j: (0, 0)),
            pl.BlockSpec((None, 1, d), lambda i, j: (i // mod_blocks, 0, 0)),
            pl.BlockSpec((None, 1, d), lambda i, j: (i // mod_blocks, 0, 0)),
            pl.BlockSpec((None, d, tn), lambda i, j: (layer, 0, col_block0 + j)),
        ],
        out_specs=pl.BlockSpec((tm, tn), lambda i, j: (i, j)),
        out_shape=jax.ShapeDtypeStruct((m, n), BF16),
        scratch_shapes=[pltpu.VMEM((tm, d), BF16)],
        compiler_params=_params(("arbitrary", "arbitrary")),
        name=name,
    )(x, gain.reshape(1, d), shift, scale, w)


def _qk_rope_kernel(x_ref, hg_ref, cos_ref, sin_ref, o_ref):
    cos = cos_ref[...]
    sin = sin_ref[...]
    mean_lanes = jnp.full((HEAD, HEAD), 1.0 / HEAD, BF16)
    for hh in range(x_ref.shape[1] // HEAD):
        cols = slice(hh * HEAD, (hh + 1) * HEAD)
        a = x_ref[:, cols].astype(F32)
        ms = _dot((a * a).astype(BF16), mean_lanes)
        a = a * lax.rsqrt(ms + EPS) * hg_ref[:, cols]
        a = a * cos + pltpu.roll(a, HEAD // 2, axis=1) * sin
        o_ref[:, cols] = a.astype(o_ref.dtype)


def _qk_rope_call(p, head_gain, cos, sin, *, name):
    m, n = p.shape
    nn = head_gain.shape[1]
    tm = _pick(math.gcd(m, cos.shape[0]), 2048, BF16_SUBLANES)
    tn = _pick(math.gcd(n, nn), 1024, HEAD)
    table_blocks = cos.shape[0] // tm
    return pl.pallas_call(
        _qk_rope_kernel,
        grid=(m // tm, nn // tn),
        in_specs=[
            pl.BlockSpec((tm, tn), lambda i, j: (i, j)),
            pl.BlockSpec((1, tn), lambda i, j: (0, j)),
            pl.BlockSpec((tm, HEAD), lambda i, j: (i % table_blocks, 0)),
            pl.BlockSpec((tm, HEAD), lambda i, j: (i % table_blocks, 0)),
        ],
        out_specs=pl.BlockSpec((tm, tn), lambda i, j: (i, j)),
        out_shape=jax.ShapeDtypeStruct((m, n), p.dtype),
        input_output_aliases={0: 0},
        compiler_params=_params(("arbitrary", "arbitrary")),
        name=name,
    )(p, head_gain, cos, sin)


def _residual(x, y, gain, gate):
    ms = jnp.mean(y * y, axis=-1, keepdims=True)
    return x + gate * (y * lax.rsqrt(ms + EPS) * gain)


def _proj_res_kernel(a_ref, w_ref, x_ref, gain_ref, gate_ref, o_ref):
    y = _dot(a_ref[...], w_ref[...])
    o_ref[...] = _residual(x_ref[...], y, gain_ref[...], gate_ref[...])


def _proj_res_ktiled_kernel(a_ref, w_ref, x_ref, gain_ref, gate_ref, o_ref):
    k = pl.program_id(1)

    @pl.when(k == 0)
    def _():
        o_ref[...] = _dot(a_ref[...], w_ref[...])

    @pl.when(k > 0)
    def _():
        o_ref[...] += _dot(a_ref[...], w_ref[...])

    @pl.when(k == pl.num_programs(1) - 1)
    def _():
        o_ref[...] = _residual(x_ref[...], o_ref[...], gain_ref[...], gate_ref[...])


def _proj_res_call(a, w, x, gain, gate, *, layer=0, name):
    m, k = a.shape
    d = w.shape[2]
    bm = gate.shape[0]
    rows_per_mod = m // bm
    tk = k if k <= d else _pick(k, 512, 2 * HEAD)
    kb = k // tk
    tm = _pick(rows_per_mod, 512 if kb == 1 else 1024, BF16_SUBLANES)
    mod_blocks = rows_per_mod // tm
    return pl.pallas_call(
        _proj_res_kernel if kb == 1 else _proj_res_ktiled_kernel,
        grid=(m // tm, kb),
        in_specs=[
            pl.BlockSpec((tm, tk), lambda i, kk: (i, kk)),
            pl.BlockSpec((None, tk, d), lambda i, kk: (layer, kk, 0)),
            pl.BlockSpec((tm, d), lambda i, kk: (i, 0)),
            pl.BlockSpec((1, d), lambda i, kk: (0, 0)),
            pl.BlockSpec((None, 1, d), lambda i, kk: (i // mod_blocks, 0, 0)),
        ],
        out_specs=pl.BlockSpec((tm, d), lambda i, kk: (i, 0)),
        out_shape=jax.ShapeDtypeStruct((m, d), F32),
        compiler_params=_params(("arbitrary", "arbitrary")),
        name=name,
    )(a, w, x, gain.reshape(1, d), gate)


def _ffn_in_kernel(x_ref, xprev_ref, xnext_ref, gain_ref, shift_ref, scale_ref, wg_ref, wu_ref, cw_ref, cb_ref,
                   o_ref, h_ref, *, blocks_per_seq):
    i = pl.program_id(0)
    j = pl.program_id(1)
    tm = x_ref.shape[0]
    halo = BF16_SUBLANES
    lead = _split_rows(tm)

    def gated(gate, up):
        rows = gate.shape[0]
        g_prev = pltpu.roll(gate, 1, axis=0)[halo:halo + tm, :]
        g_next = pltpu.roll(gate, rows - 1, axis=0)[halo:halo + tm, :]
        conv = (g_prev * cw_ref[0:1, :] + gate[halo:halo + tm, :] * cw_ref[1:2, :] + g_next * cw_ref[2:3, :]
                + cb_ref[...])
        o_ref[...] = (_silu(conv) * up).astype(o_ref.dtype)

    @pl.when(j == 0)
    def _():
        gain, shift, scale = gain_ref[...], shift_ref[...], scale_ref[...]
        seq_block = i % blocks_per_seq
        h_prev = jnp.where(seq_block > 0, _norm_mod(xprev_ref[...], gain, shift, scale), 0.0)
        h_next = jnp.where(seq_block < blocks_per_seq - 1, _norm_mod(xnext_ref[...], gain, shift, scale), 0.0)
        zeros = jnp.zeros((halo - h_prev.shape[0], h_prev.shape[1]), F32)
        h_ref[0:halo, :] = jnp.concatenate([zeros, h_prev], axis=0).astype(BF16)
        h_ref[halo + tm:, :] = jnp.concatenate([h_next, zeros], axis=0).astype(BF16)
        _norm_mod_rows(x_ref, lead, h_ref, halo, gain, shift, scale)
        if lead == tm:
            gated(_dot(h_ref[...], wg_ref[...]), _dot(h_ref[halo:halo + tm, :], wu_ref[...]))
        else:
            cut = halo + lead
            h_ref[cut:halo + tm, :] = _norm_mod(x_ref[lead:, :], gain, shift, scale).astype(BF16)
            gate = jnp.concatenate([_dot(h_ref[:cut, :], wg_ref[...]), _dot(h_ref[cut:, :], wg_ref[...])], axis=0)
            up = jnp.concatenate([_dot(h_ref[halo:cut, :], wu_ref[...]),
                                  _dot(h_ref[cut:halo + tm, :], wu_ref[...])], axis=0)
            gated(gate, up)

    @pl.when(j > 0)
    def _():
        gated(_dot(h_ref[...], wg_ref[...]), _dot(h_ref[halo:halo + tm, :], wu_ref[...]))


def _ffn_in_call(x, gain, shift, scale, w, conv_w, conv_b, *, layer, seq_len, name):
    m, d = x.shape
    f = w.shape[2] // 2
    bm = shift.shape[0]
    rows_per_mod = m // bm
    tm = _pick(seq_len, 1024, BF16_SUBLANES)
    tn = _pick(f, 512, HEAD)
    assert rows_per_mod % tm == 0
    mod_blocks = rows_per_mod // tm
    nb = f // tn
    xh = 8
    tiles_per_tm = tm // xh
    last_tile = m // xh - 1
    body = functools.partial(_ffn_in_kernel, blocks_per_seq=seq_len // tm)
    return pl.pallas_call(
        body,
        grid=(m // tm, nb),
        in_specs=[
            pl.BlockSpec((tm, d), lambda i, j: (i, 0)),
            pl.BlockSpec((xh, d), lambda i, j: (jnp.maximum(i * tiles_per_tm - 1, 0), 0)),
            pl.BlockSpec((xh, d), lambda i, j: (jnp.minimum((i + 1) * tiles_per_tm, last_tile), 0)),
            pl.BlockSpec((1, d), lambda i, j: (0, 0)),
            pl.BlockSpec((None, 1, d), lambda i, j: (i // mod_blocks, 0, 0)),
            pl.BlockSpec((None, 1, d), lambda i, j: (i // mod_blocks, 0, 0)),
            pl.BlockSpec((None, d, tn), lambda i, j: (layer, 0, j)),
            pl.BlockSpec((None, d, tn), lambda i, j: (layer, 0, nb + j)),
            pl.BlockSpec((3, tn), lambda i, j: (0, j)),
            pl.BlockSpec((1, tn), lambda i, j: (0, j)),
        ],
        out_specs=pl.BlockSpec((tm, tn), lambda i, j: (i, j)),
        out_shape=jax.ShapeDtypeStruct((m, f), BF16),
        scratch_shapes=[pltpu.VMEM((tm + 2 * BF16_SUBLANES, d), BF16)],
        compiler_params=_params(("arbitrary", "arbitrary")),
        name=name,
    )(x, x, x, gain.reshape(1, d), shift, scale, w, w, conv_w, conv_b.reshape(1, f))


def _gla_gates(f_raw, lb, within):
    half = 0.5 * (1.0 - lb)
    ht = half * jnp.tanh(0.5 * f_raw.astype(F32))
    lf = jnp.log((1.0 - half) + ht)
    ones = jnp.where(within, 1.0, 0.0).astype(BF16)
    lf_hi = lf.astype(BF16)
    lf_lo = (lf - lf_hi.astype(F32)).astype(BF16)
    return half - ht, _dot(ones, lf_hi) + _dot(ones, lf_lo)


def _gla_pairs(qs, kk, b, forward, within):
    c = qs.shape[0]
    a_row = c // 2 - 1 if forward else c // 2
    e_row = c - 1 if forward else 0
    anchor = b[a_row:a_row + 1, :]
    b_end = b[e_row:e_row + 1, :]
    qa = qs * jnp.exp(b - anchor)
    ka = kk * jnp.exp(anchor - b)
    scores = jnp.where(within, _dot_nt(qa.astype(BF16), ka.astype(BF16)), 0.0).astype(BF16)
    return (scores, (qa * jnp.exp(anchor)).astype(BF16), (ka * jnp.exp(b_end - anchor)).astype(BF16),
            jnp.exp(b_end))


def _gla_pairs_exact(qs, kk, b, forward, within, tmp):
    tb_ref, tq_ref, ts_ref = tmp
    c = qs.shape[0]
    b_end = b[c - 1:c, :] if forward else b[0:1, :]
    tb_ref[...] = b
    tq_ref[...] = qs
    ones = jnp.ones((8, qs.shape[1]), BF16)

    def row(t, carry):
        rel = jnp.minimum(tb_ref[pl.ds(t, 1), :] - b, 0.0)
        e = (tq_ref[pl.ds(t, 1), :] * kk) * jnp.exp(rel)
        ts_ref[pl.ds(t, 1), :] = _dot_nt(ones, e.astype(BF16))[0:1, :]
        return carry

    lax.fori_loop(0, c, row, 0)
    scores = jnp.where(within, ts_ref[...], 0.0).astype(BF16)
    return scores, (qs * jnp.exp(b)).astype(BF16), (kk * jnp.exp(b_end - b)).astype(BF16), jnp.exp(b_end)


def _gla_readout(o, g_raw, o_gain):
    ms = jnp.mean(o * o, axis=-1, keepdims=True)
    y = o * lax.rsqrt(ms + EPS) * o_gain
    return (y * _silu(g_raw.astype(F32))).astype(BF16)


def _gla_kernel(qc, ffc, fbc, vc, gc, ql, ffl, fbl, vl, gl, lb_ref, og_ref, yc_ref, yl_ref,
                o_ref, qb_ref, ke_ref, dec_ref, st_ref, tb_ref, tq_ref, ts_ref):
    c = GLA_CHUNK
    tc = qc.shape[0]
    n_ctx = tc // c
    n_lat = ql.shape[0] // c
    lb = (lb_ref[0], lb_ref[1])
    row = lax.broadcasted_iota(jnp.int32, (c, c), 0)
    col = lax.broadcasted_iota(jnp.int32, (c, c), 1)
    within = (col <= row, col >= row)

    def rows(start):
        return pl.ds(start if isinstance(start, int) else pl.multiple_of(start, c), c)

    def local(refs, src0, dst0, chunk0, n, exact):
        q, ff, fb, v = refs
        src = [rows(src0 + j * c) for j in range(n)]
        dst = [rows(dst0 + j * c) for j in range(n)]
        qs = [_silu(q[src[j], :].astype(F32)) * (HEAD ** -0.5) for j in range(n)]
        gates = [[_gla_gates(f[src[j], :], lb[d], within[d]) for d, f in enumerate((ff, fb))] for j in range(n)]
        scores = []
        for j in range(n):
            per_dir = []
            for d in range(2):
                kk, b = gates[j][d]
                if exact:
                    sc, qb, ke, dec = _gla_pairs_exact(qs[j], kk, b, d == 0, within[d], (tb_ref, tq_ref, ts_ref))
                else:
                    sc, qb, ke, dec = _gla_pairs(qs[j], kk, b, d == 0, within[d])
                qb_ref[d, dst[j], :] = qb
                ke_ref[d, dst[j], :] = ke
                dec_ref[d, chunk0 + j] = dec
                per_dir.append(sc)
            scores.append(per_dir)
        for j in range(n):
            vj = v[src[j], :]
            o_ref[dst[j], :] = _dot(scores[j][0], vj) + _dot(scores[j][1], vj)

    def phase1(exact):
        run = 1 if exact else math.gcd(n_lat, GLA_RUN)
        run_ctx = math.gcd(n_ctx, run)
        for i in range(0, n_ctx, run_ctx):
            local((qc, ffc, fbc, vc), i * c, i * c, i, run_ctx, exact)

        def body(i, carry):
            src0 = pl.multiple_of(i * (run * c), c)
            local((ql, ffl, fbl, vl), src0, tc + src0, n_ctx + i * run, run, exact)
            return carry

        lax.fori_loop(0, n_lat // run, body, 0)

    safe = jnp.min(jnp.minimum(lb[0], lb[1])) >= GLA_SAFE_LB

    @pl.when(safe)
    def _():
        phase1(False)

    @pl.when(jnp.logical_not(safe))
    def _():
        phase1(True)

    st_ref[...] = jnp.zeros_like(st_ref)

    def steps(v, first, n, n_seg, row0, chunk0):
        order = [[first + s for s in range(n)], [n_seg - 1 - first - s for s in range(n)]]
        src = [[rows(j * c) for j in order[d]] for d in range(2)]
        dst = [[rows(row0 + j * c) for j in order[d]] for d in range(2)]
        kv = [[_dot_tn(v[src[d][s], :], ke_ref[d, dst[d][s], :]) for s in range(n)] for d in range(2)]
        for d in range(2):
            st = st_ref[d]
            for s in range(n):
                o_ref[dst[d][s], :] += _dot_nt(qb_ref[d, dst[d][s], :], st.astype(BF16))
                st = st * dec_ref[d, chunk0 + order[d][s]] + kv[d][s]
            st_ref[d] = st

    steps(vc, 0, n_ctx, n_ctx, 0, 0)
    run = math.gcd(n_lat, GLA_RUN)

    def steps_body(i, carry):
        steps(vl, i * run, run, n_lat, tc, n_ctx)
        return carry

    lax.fori_loop(0, n_lat // run, steps_body, 0)

    o_gain = og_ref[...]
    for i in range(n_ctx):
        r = slice(i * c, (i + 1) * c)
        yc_ref[r, :] = _gla_readout(o_ref[r, :], gc[r, :], o_gain)

    def out_body(i, carry):
        src = pl.ds(pl.multiple_of(i * c, c), c)
        dst = pl.ds(pl.multiple_of(tc + i * c, c), c)
        yl_ref[src, :] = _gla_readout(o_ref[dst, :], gl[src, :], o_gain)
        return carry

    lax.fori_loop(0, n_lat, out_body, 0, unroll=math.gcd(n_lat, 4))


def _gla_call(p_ctx, p_lat, lower_bound, o_gain):
    b, tc, d5 = p_ctx.shape
    t = p_lat.shape[1]
    d = d5 // 5
    heads = d // HEAD
    assert tc % GLA_CHUNK == 0 and t % GLA_CHUNK == 0

    def col(tt, part):
        return pl.BlockSpec((None, tt, HEAD), lambda bi, h: (bi, 0, part * heads + h))

    out_spec = lambda tt: pl.BlockSpec((None, tt, HEAD), lambda bi, h: (bi, 0, h))
    return pl.pallas_call(
        _gla_kernel,
        grid=(b, heads),
        in_specs=[col(tc, p) for p in range(5)] + [col(t, p) for p in range(5)] + [
            pl.BlockSpec((2, None, 1, HEAD), lambda bi, h: (0, h, 0, 0)),
            pl.BlockSpec((1, HEAD), lambda bi, h: (0, 0)),
        ],
        out_specs=[out_spec(tc), out_spec(t)],
        out_shape=[jax.ShapeDtypeStruct((b, tc, d), BF16), jax.ShapeDtypeStruct((b, t, d), BF16)],
        scratch_shapes=[
            pltpu.VMEM((tc + t, HEAD), F32),
            pltpu.VMEM((2, tc + t, HEAD), BF16),
            pltpu.VMEM((2, tc + t, HEAD), BF16),
            pltpu.VMEM((2, (tc + t) // GLA_CHUNK, 1, HEAD), F32),
            pltpu.VMEM((2, HEAD, HEAD), F32),
            pltpu.VMEM((GLA_CHUNK, HEAD), F32),
            pltpu.VMEM((GLA_CHUNK, HEAD), F32),
            pltpu.VMEM((GLA_CHUNK, GLA_CHUNK), F32),
        ],
        compiler_params=_params(("arbitrary", "arbitrary")),
        name="hgrn_scan",
    )(*([p_ctx] * 5), *([p_lat] * 5), lower_bound.reshape(2, heads, 1, HEAD), o_gain.reshape(1, HEAD))


def _flash_kernel(q_ref, kc_ref, vc_ref, kl_ref, vl_ref, o_ref, *, tk, unroll):
    tq = q_ref.shape[0]
    group = q_ref.shape[1] // HEAD
    qs = [q_ref[:, g * HEAD:(g + 1) * HEAD] for g in range(group)]

    def attend(carry, k, v):
        v_ext = jnp.concatenate([v, jnp.ones_like(v)], axis=1)
        out = []
        for g in range(group):
            m, acc = carry[g]
            s = _dot_nt(qs[g], k)
            m_new = jnp.maximum(m, jnp.max(s, axis=-1, keepdims=True))
            p = jnp.exp2(s - m_new).astype(BF16)
            out.append((m_new, jnp.exp2(m - m_new) * acc + _dot(p, v_ext)))
        return tuple(out)

    def body(j, carry):
        r = pl.ds(pl.multiple_of(j * tk, tk), tk)
        return attend(carry, kl_ref[r, :], vl_ref[r, :])

    init = tuple((jnp.full((tq, 1), -jnp.inf, F32), jnp.zeros((tq, 2 * HEAD), F32)) for _ in range(group))
    carry = attend(init, kc_ref[...], vc_ref[...])
    final = lax.fori_loop(0, kl_ref.shape[0] // tk, body, carry, unroll=unroll)
    for g in range(group):
        acc = final[g][1]
        o_ref[:, g * HEAD:(g + 1) * HEAD] = (acc[:, :HEAD] / acc[:, HEAD:]).astype(o_ref.dtype)


def _flash_call(qkv, kv_ctx, *, heads):
    b, t, n = qkv.shape
    tc = kv_ctx.shape[1]
    kvh = (n // HEAD - heads) // 2
    group = heads // kvh
    tq = _pick(t, 512, BF16_SUBLANES)
    tk = _pick(t, 1024, 2 * HEAD)
    body = functools.partial(_flash_kernel, tk=tk, unroll=math.gcd(t // tk, 8))
    return pl.pallas_call(
        body,
        grid=(b, kvh, t // tq),
        in_specs=[
            pl.BlockSpec((None, tq, group * HEAD), lambda bi, h, i: (bi, i, h)),
            pl.BlockSpec((None, tc, HEAD), lambda bi, h, i: (bi, 0, h)),
            pl.BlockSpec((None, tc, HEAD), lambda bi, h, i: (bi, 0, kvh + h)),
            pl.BlockSpec((None, t, HEAD), lambda bi, h, i: (bi, 0, heads + h)),
            pl.BlockSpec((None, t, HEAD), lambda bi, h, i: (bi, 0, heads + kvh + h)),
        ],
        out_specs=pl.BlockSpec((None, tq, group * HEAD), lambda bi, h, i: (bi, i, h)),
        out_shape=jax.ShapeDtypeStruct((b, t, heads * HEAD), BF16),
        compiler_params=_params(("arbitrary", "arbitrary", "arbitrary")),
        name="gqa_flash",
    )(qkv, kv_ctx, kv_ctx, qkv, qkv)


def _rope_tables(t):
    pos = jnp.arange(t, dtype=jnp.int32)
    rows = (pos // GRID_W).astype(F32)
    cols = (pos % GRID_W).astype(F32)
    axis_dim = HEAD // 2
    inv_freq = ROPE_THETA ** (-jnp.arange(0, axis_dim, 2, dtype=F32) / axis_dim)
    ang = jnp.concatenate([rows[:, None] * inv_freq, cols[:, None] * inv_freq], axis=-1)
    cos, sin = jnp.cos(ang), jnp.sin(ang)
    return jnp.concatenate([cos, cos], axis=-1), jnp.concatenate([-sin, sin], axis=-1)


def _split_mods(mods, b):
    d = mods.shape[1] // N_MOD
    lat = [mods[:b, i * d:(i + 1) * d].reshape(b, 1, d) for i in range(N_MOD)]
    ctx = [mods[b:b + 1, i * d:(i + 1) * d].reshape(1, 1, d) for i in range(N_MOD)]
    return lat, ctx


def kernel(x, c, ctx, c_ctx, ada_w, ada_b, norm_mix_pre, norm_mix_post, norm_ffn_pre, norm_ffn_post, hgrn_w_in, hgrn_lb_logits, hgrn_o_norm, hgrn_w_out, attn_w_qkv, attn_q_norm, attn_k_norm, attn_w_out, ffn_w_in, ffn_conv_w, ffn_conv_b, ffn_w_out):
    b, t, d = x.shape
    tc = ctx.shape[1]
    heads = d // HEAD
    assert ada_w.shape[0] == 2 and b + 1 <= ADA_ROWS

    cond = jnp.zeros((ADA_ROWS, d), F32).at[:b].set(c).at[b].set(c_ctx)
    mods = _ada_call(cond, ada_w, ada_b)
    x_lat = x.reshape(b * t, d)
    x_ctx = ctx.reshape(b * tc, d)

    (sh_ml, sc_ml, gt_ml, sh_fl, sc_fl, gt_fl), (sh_mc, sc_mc, gt_mc, sh_fc, sc_fc, gt_fc) = _split_mods(mods[0], b)
    lower_bound = jnp.cumsum(jax.nn.softmax(hgrn_lb_logits.astype(F32), axis=1), axis=1)[:, 0]
    hgrn_w_in, hgrn_w_out, attn_w_out = hgrn_w_in.astype(BF16), hgrn_w_out.astype(BF16), attn_w_out.astype(BF16)
    ffn_w_in, ffn_w_out = ffn_w_in.astype(BF16), ffn_w_out.astype(BF16)
    p_lat = _nmm_call(x_lat, norm_mix_pre[0], sh_ml, sc_ml, hgrn_w_in, name="hgrn_in_lat")
    p_ctx = _nmm_call(x_ctx, norm_mix_pre[0], sh_mc, sc_mc, hgrn_w_in, name="hgrn_in_ctx")
    y_ctx, y_lat = _gla_call(p_ctx.reshape(b, tc, 5 * d), p_lat.reshape(b, t, 5 * d), lower_bound, hgrn_o_norm[0])
    x_lat = _proj_res_call(y_lat.reshape(b * t, d), hgrn_w_out, x_lat, norm_mix_post[0], gt_ml, name="hgrn_out_lat")
    x_ctx = _proj_res_call(y_ctx.reshape(b * tc, d), hgrn_w_out, x_ctx, norm_mix_post[0], gt_mc, name="hgrn_out_ctx")
    a = _ffn_in_call(x_lat, norm_ffn_pre[0], sh_fl, sc_fl, ffn_w_in, ffn_conv_w[0], ffn_conv_b[0],
                     layer=0, seq_len=t, name="ffn0_in_lat")
    x_lat = _proj_res_call(a, ffn_w_out, x_lat, norm_ffn_post[0], gt_fl, layer=0, name="ffn0_out_lat")
    a = _ffn_in_call(x_ctx, norm_ffn_pre[0], sh_fc, sc_fc, ffn_w_in, ffn_conv_w[0], ffn_conv_b[0],
                     layer=0, seq_len=tc, name="ffn0_in_ctx")
    x_ctx = _proj_res_call(a, ffn_w_out, x_ctx, norm_ffn_post[0], gt_fc, layer=0, name="ffn0_out_ctx")

    (sh_ml, sc_ml, gt_ml, sh_fl, sc_fl, gt_fl), (sh_mc, sc_mc, _, _, _, _) = _split_mods(mods[1], b)
    n_qkv = attn_w_qkv.shape[2]
    kvh = (n_qkv // HEAD - heads) // 2
    perm = jnp.concatenate([jnp.arange(0, HEAD, 2), jnp.arange(1, HEAD, 2)])
    head_perm = (jnp.arange(heads + kvh)[:, None] * HEAD + perm[None, :]).reshape(-1)
    col_perm = jnp.concatenate([head_perm, jnp.arange((heads + kvh) * HEAD, n_qkv)])
    w_qkv = attn_w_qkv[:, :, col_perm].astype(BF16)
    head_gain = jnp.concatenate([
        jnp.tile(attn_q_norm[0][perm] * (LOG2_E * HEAD ** -0.5), heads),
        jnp.tile(attn_k_norm[0][perm], kvh),
        jnp.ones((kvh * HEAD,), F32),
    ]).reshape(1, n_qkv)
    cos, sin = _rope_tables(t)
    qk_cols = (heads + kvh) * HEAD
    q_cols = heads * HEAD
    qkv = _nmm_call(x_lat, norm_mix_pre[1], sh_ml, sc_ml, w_qkv, name="attn_qkv_lat")
    qkv = _qk_rope_call(qkv, head_gain[:, :qk_cols], cos, sin, name="attn_qk_rope_lat")
    kv_ctx = _nmm_call(x_ctx, norm_mix_pre[1], sh_mc, sc_mc, w_qkv, col0=q_cols, name="attn_kv_ctx")
    kv_ctx = _qk_rope_call(kv_ctx, head_gain[:, q_cols:qk_cols], jnp.ones((tc, HEAD), F32),
                           jnp.zeros((tc, HEAD), F32), name="attn_k_norm_ctx")
    qkv = qkv.reshape(b, t, n_qkv)
    kv_ctx = kv_ctx.reshape(b, tc, 2 * kvh * HEAD)
    o = _flash_call(qkv, kv_ctx, heads=heads)
    x_lat = _proj_res_call(o.reshape(b * t, d), attn_w_out, x_lat, norm_mix_post[1], gt_ml, name="attn_out_lat")
    a = _ffn_in_call(x_lat, norm_ffn_pre[1], sh_fl, sc_fl, ffn_w_in, ffn_conv_w[1], ffn_conv_b[1],
                     layer=1, seq_len=t, name="ffn1_in_lat")
    x_lat = _proj_res_call(a, ffn_w_out, x_lat, norm_ffn_post[1], gt_fl, layer=1, name="ffn1_out_lat")
    return x_lat.reshape(b, t, d)
```

```python
import functools
import math

import jax
import jax.numpy as jnp
from jax import lax
from jax.experimental import pallas as pl
from jax.experimental.pallas import tpu as pltpu

F32 = jnp.float32
BF16 = jnp.bfloat16

EPS = 1e-6
GRID_W = 64
ROPE_THETA = 10000.0
LOG2_E = math.log2(math.e)
N_MOD = 6
HEAD = 128
ROW_GROUP = 16
GLA_CHUNK = 128
GLA_RUN = 8
GLA_SAFE_LB = math.exp(-80.0 / (GLA_CHUNK // 2))
BF16_SUBLANES = 16
ADA_ROWS = 8
V7X_VMEM_BYTES = 64 * 1024 * 1024
VMEM_LIMIT = V7X_VMEM_BYTES - 6 * 1024 * 1024


def _pick(n, target, align):
    if n <= target:
        return n
    for t in range(target - target % align, 0, -align):
        if n % t == 0:
            return t
    return n


def _params(sem):
    return pltpu.CompilerParams(dimension_semantics=sem, vmem_limit_bytes=VMEM_LIMIT)


def _sigmoid(x):
    return 0.5 * jnp.tanh(0.5 * x) + 0.5


def _silu(x):
    return x * _sigmoid(x)


def _dot(a, b):
    return jnp.dot(a, b, preferred_element_type=F32)


def _dot_nt(a, b):
    return lax.dot_general(a, b, (((1,), (1,)), ((), ())), preferred_element_type=F32)


def _dot_tn(a, b):
    return lax.dot_general(a, b, (((0,), (0,)), ((), ())), preferred_element_type=F32)


def _ada_kernel(c_ref, w_ref, b_ref, o_ref):
    sc = _silu(c_ref[...]).astype(BF16)
    o_ref[...] = _dot(sc, w_ref[...].astype(BF16)) + b_ref[...]


def _ada_call(cond, ada_w, ada_b):
    depth, d, n = ada_w.shape
    tn = _pick(n, 1024, HEAD)
    return pl.pallas_call(
        _ada_kernel,
        grid=(depth, n // tn),
        in_specs=[
            pl.BlockSpec((ADA_ROWS, d), lambda l, j: (0, 0)),
            pl.BlockSpec((None, d, tn), lambda l, j: (l, 0, j)),
            pl.BlockSpec((None, 1, tn), lambda l, j: (l, 0, j)),
        ],
        out_specs=pl.BlockSpec((None, ADA_ROWS, tn), lambda l, j: (l, 0, j)),
        out_shape=jax.ShapeDtypeStruct((depth, ADA_ROWS, n), F32),
        compiler_params=_params(("arbitrary", "arbitrary")),
        name="ada_ln",
    )(cond, ada_w, ada_b.reshape(depth, 1, n))


def _norm_mod(x, gain, shift, scale):
    ms = jnp.mean(x * x, axis=-1, keepdims=True)
    return (x * lax.rsqrt(ms + EPS) * gain) * (1.0 + scale) + shift


def _norm_mod_rows(x_ref, n_rows, h_ref, h_row0, gain, shift, scale):
    mult = gain * (1.0 + scale)

    def body(r, carry):
        start = pl.multiple_of(r * ROW_GROUP, ROW_GROUP)
        x = x_ref[pl.ds(start, ROW_GROUP), :]
        ms = jnp.mean(x * x, axis=-1, keepdims=True)
        dst = pl.ds(pl.multiple_of(h_row0 + start, ROW_GROUP), ROW_GROUP)
        h_ref[dst, :] = (x * lax.rsqrt(ms + EPS) * mult + shift).astype(BF16)
        return carry

    groups = n_rows // ROW_GROUP
    lax.fori_loop(0, groups, body, 0, unroll=math.gcd(groups, 4))


def _split_rows(tm):
    half = tm // 2
    return half if half % ROW_GROUP == 0 else tm


def _nmm_kernel(x_ref, gain_ref, shift_ref, scale_ref, w_ref, o_ref, h_ref):
    j = pl.program_id(1)
    tm = x_ref.shape[0]
    lead = _split_rows(tm)

    @pl.when(j == 0)
    def _():
        gain, shift, scale = gain_ref[...], shift_ref[...], scale_ref[...]
        _norm_mod_rows(x_ref, lead, h_ref, 0, gain, shift, scale)
        if lead < tm:
            h_ref[lead:, :] = _norm_mod(x_ref[lead:, :], gain, shift, scale).astype(BF16)
        o_ref[:lead, :] = _dot(h_ref[:lead, :], w_ref[...]).astype(o_ref.dtype)
        if lead < tm:
            o_ref[lead:, :] = _dot(h_ref[lead:, :], w_ref[...]).astype(o_ref.dtype)

    @pl.when(j > 0)
    def _():
        o_ref[...] = _dot(h_ref[...], w_ref[...]).astype(o_ref.dtype)


def _nmm_call(x, gain, shift, scale, w, *, layer=0, col0=0, name):
    m, d = x.shape
    n = w.shape[2] - col0
    bm = shift.shape[0]
    rows_per_mod = m // bm
    tm = _pick(rows_per_mod, 1024, BF16_SUBLANES)
    tn = _pick(math.gcd(n, col0) if col0 else n, 1024, HEAD)
    mod_blocks = rows_per_mod // tm
    col_block0 = col0 // tn
    return pl.pallas_call(
        _nmm_kernel,
        grid=(m // tm, n // tn),
        in_specs=[
            pl.BlockSpec((tm, d), lambda i, j: (i, 0)),
            pl.BlockSpec((1, d), lambda i, j: (0, 0)),
            pl.BlockSpec((None, 1, d), lambda i, j: (i // mod_blocks, 0, 0)),
            pl.BlockSpec((None, 1, d), lambda i, j: (i // mod_blocks, 0, 0)),
            pl.BlockSpec((None, d, tn), lambda i, j: (layer, 0, col_block0 + j)),
        ],
        out_specs=pl.BlockSpec((tm, tn), lambda i, j: (i, j)),
        out_shape=jax.ShapeDtypeStruct((m, n), BF16),
        scratch_shapes=[pltpu.VMEM((tm, d), BF16)],
        compiler_params=_params(("arbitrary", "arbitrary")),
        name=name,
    )(x, gain.reshape(1, d), shift, scale, w)


def _qk_rope_kernel(x_ref, hg_ref, cos_ref, sin_ref, o_ref):
    cos = cos_ref[...]
    sin = sin_ref[...]
    mean_lanes = jnp.full((HEAD, HEAD), 1.0 / HEAD, BF16)
    for hh in range(x_ref.shape[1] // HEAD):
        cols = slice(hh * HEAD, (hh + 1) * HEAD)
        a = x_ref[:, cols].astype(F32)
        ms = _dot((a * a).astype(BF16), mean_lanes)
        a = a * lax.rsqrt(ms + EPS) * hg_ref[:, cols]
        a = a * cos + pltpu.roll(a, HEAD // 2, axis=1) * sin
        o_ref[:, cols] = a.astype(o_ref.dtype)


def _qk_rope_call(p, head_gain, cos, sin, *, name):
    m, n = p.shape
    nn = head_gain.shape[1]
    tm = _pick(math.gcd(m, cos.shape[0]), 2048, BF16_SUBLANES)
    tn = _pick(math.gcd(n, nn), 1024, HEAD)
    table_blocks = cos.shape[0] // tm
    return pl.pallas_call(
        _qk_rope_kernel,
        grid=(m // tm, nn // tn),
        in_specs=[
            pl.BlockSpec((tm, tn), lambda i, j: (i, j)),
            pl.BlockSpec((1, tn), lambda i, j: (0, j)),
            pl.BlockSpec((tm, HEAD), lambda i, j: (i % table_blocks, 0)),
            pl.BlockSpec((tm, HEAD), lambda i, j: (i % table_blocks, 0)),
        ],
        out_specs=pl.BlockSpec((tm, tn), lambda i, j: (i, j)),
        out_shape=jax.ShapeDtypeStruct((m, n), p.dtype),
        input_output_aliases={0: 0},
        compiler_params=_params(("arbitrary", "arbitrary")),
        name=name,
    )(p, head_gain, cos, sin)


def _residual(x, y, gain, gate):
    ms = jnp.mean(y * y, axis=-1, keepdims=True)
    return x + gate * (y * lax.rsqrt(ms + EPS) * gain)


def _proj_res_kernel(a_ref, w_ref, x_ref, gain_ref, gate_ref, o_ref):
    y = _dot(a_ref[...], w_ref[...])
    o_ref[...] = _residual(x_ref[...], y, gain_ref[...], gate_ref[...])


def _proj_res_ktiled_kernel(a_ref, w_ref, x_ref, gain_ref, gate_ref, o_ref):
    k = pl.program_id(1)
    last = pl.num_programs(1) - 1
    tm = o_ref.shape[0]
    lead = _split_rows(tm)

    @pl.when(k == 0)
    def _():
        o_ref[...] = _dot(a_ref[...], w_ref[...])

    @pl.when(jnp.logical_and(k > 0, k < last))
    def _():
        o_ref[...] += _dot(a_ref[...], w_ref[...])

    @pl.when(k == last)
    def _():
        gain, gate = gain_ref[...], gate_ref[...]
        parts = [slice(0, lead), slice(lead, tm)] if lead < tm else [slice(0, tm)]
        ys = [o_ref[r, :] + _dot(a_ref[r, :], w_ref[...]) for r in parts]
        for r, y in zip(parts, ys):
            o_ref[r, :] = _residual(x_ref[r, :], y, gain, gate)


def _proj_res_call(a, w, x, gain, gate, *, layer=0, name):
    m, k = a.shape
    d = w.shape[2]
    bm = gate.shape[0]
    rows_per_mod = m // bm
    tk = k if k <= d else _pick(k, 512, 2 * HEAD)
    kb = k // tk
    tm = _pick(rows_per_mod, 512 if kb == 1 else 1024, BF16_SUBLANES)
    mod_blocks = rows_per_mod // tm
    return pl.pallas_call(
        _proj_res_kernel if kb == 1 else _proj_res_ktiled_kernel,
        grid=(m // tm, kb),
        in_specs=[
            pl.BlockSpec((tm, tk), lambda i, kk: (i, kk)),
            pl.BlockSpec((None, tk, d), lambda i, kk: (layer, kk, 0)),
            pl.BlockSpec((tm, d), lambda i, kk: (i, 0)),
            pl.BlockSpec((1, d), lambda i, kk: (0, 0)),
            pl.BlockSpec((None, 1, d), lambda i, kk: (i // mod_blocks, 0, 0)),
        ],
        out_specs=pl.BlockSpec((tm, d), lambda i, kk: (i, 0)),
        out_shape=jax.ShapeDtypeStruct((m, d), F32),
        compiler_params=_params(("arbitrary", "arbitrary")),
        name=name,
    )(a, w, x, gain.reshape(1, d), gate)


def _ffn_in_kernel(x_ref, xprev_ref, xnext_ref, gain_ref, shift_ref, scale_ref, wg_ref, wu_ref, cw_ref, cb_ref,
                   o_ref, h_ref, *, blocks_per_seq):
    i = pl.program_id(0)
    j = pl.program_id(1)
    tm = x_ref.shape[0]
    halo = BF16_SUBLANES
    lead = _split_rows(tm)

    def gated(gate, up):
        rows = gate.shape[0]
        g_prev = pltpu.roll(gate, 1, axis=0)[halo:halo + tm, :]
        g_next = pltpu.roll(gate, rows - 1, axis=0)[halo:halo + tm, :]
        conv = (g_prev * cw_ref[0:1, :] + gate[halo:halo + tm, :] * cw_ref[1:2, :] + g_next * cw_ref[2:3, :]
                + cb_ref[...])
        o_ref[...] = (_silu(conv) * up).astype(o_ref.dtype)

    @pl.when(j == 0)
    def _():
        gain, shift, scale = gain_ref[...], shift_ref[...], scale_ref[...]
        seq_block = i % blocks_per_seq
        h_prev = jnp.where(seq_block > 0, _norm_mod(xprev_ref[...], gain, shift, scale), 0.0)
        h_next = jnp.where(seq_block < blocks_per_seq - 1, _norm_mod(xnext_ref[...], gain, shift, scale), 0.0)
        zeros = jnp.zeros((halo - h_prev.shape[0], h_prev.shape[1]), F32)
        h_ref[0:halo, :] = jnp.concatenate([zeros, h_prev], axis=0).astype(BF16)
        h_ref[halo + tm:, :] = jnp.concatenate([h_next, zeros], axis=0).astype(BF16)
        _norm_mod_rows(x_ref, lead, h_ref, halo, gain, shift, scale)
        if lead == tm:
            gated(_dot(h_ref[...], wg_ref[...]), _dot(h_ref[halo:halo + tm, :], wu_ref[...]))
        else:
            cut = halo + lead
            h_ref[cut:halo + tm, :] = _norm_mod(x_ref[lead:, :], gain, shift, scale).astype(BF16)
            gate = jnp.concatenate([_dot(h_ref[:cut, :], wg_ref[...]), _dot(h_ref[cut:, :], wg_ref[...])], axis=0)
            up = jnp.concatenate([_dot(h_ref[halo:cut, :], wu_ref[...]),
                                  _dot(h_ref[cut:halo + tm, :], wu_ref[...])], axis=0)
            gated(gate, up)

    @pl.when(j > 0)
    def _():
        gated(_dot(h_ref[...], wg_ref[...]), _dot(h_ref[halo:halo + tm, :], wu_ref[...]))


def _ffn_in_call(x, gain, shift, scale, w, conv_w, conv_b, *, layer, seq_len, name):
    m, d = x.shape
    f = w.shape[2] // 2
    bm = shift.shape[0]
    rows_per_mod = m // bm
    tm = _pick(seq_len, 1024, BF16_SUBLANES)
    tn = _pick(f, 512, HEAD)
    assert rows_per_mod % tm == 0
    mod_blocks = rows_per_mod // tm
    nb = f // tn
    xh = 8
    tiles_per_tm = tm // xh
    last_tile = m // xh - 1
    body = functools.partial(_ffn_in_kernel, blocks_per_seq=seq_len // tm)
    return pl.pallas_call(
        body,
        grid=(m // tm, nb),
        in_specs=[
            pl.BlockSpec((tm, d), lambda i, j: (i, 0)),
            pl.BlockSpec((xh, d), lambda i, j: (jnp.maximum(i * tiles_per_tm - 1, 0), 0)),
            pl.BlockSpec((xh, d), lambda i, j: (jnp.minimum((i + 1) * tiles_per_tm, last_tile), 0)),
            pl.BlockSpec((1, d), lambda i, j: (0, 0)),
            pl.BlockSpec((None, 1, d), lambda i, j: (i // mod_blocks, 0, 0)),
            pl.BlockSpec((None, 1, d), lambda i, j: (i // mod_blocks, 0, 0)),
            pl.BlockSpec((None, d, tn), lambda i, j: (layer, 0, j)),
            pl.BlockSpec((None, d, tn), lambda i, j: (layer, 0, nb + j)),
            pl.BlockSpec((3, tn), lambda i, j: (0, j)),
            pl.BlockSpec((1, tn), lambda i, j: (0, j)),
        ],
        out_specs=pl.BlockSpec((tm, tn), lambda i, j: (i, j)),
        out_shape=jax.ShapeDtypeStruct((m, f), BF16),
        scratch_shapes=[pltpu.VMEM((tm + 2 * BF16_SUBLANES, d), BF16)],
        compiler_params=_params(("arbitrary", "arbitrary")),
        name=name,
    )(x, x, x, gain.reshape(1, d), shift, scale, w, w, conv_w, conv_b.reshape(1, f))


def _gla_gates(f_raw, lb, within):
    half = 0.5 * (1.0 - lb)
    ht = half * jnp.tanh(0.5 * f_raw.astype(F32))
    lf = jnp.log((1.0 - half) + ht)
    ones = jnp.where(within, 1.0, 0.0).astype(BF16)
    lf_hi = lf.astype(BF16)
    lf_lo = (lf - lf_hi.astype(F32)).astype(BF16)
    return half - ht, _dot(ones, lf_hi) + _dot(ones, lf_lo)


def _gla_pairs(qs, kk, b, forward, within):
    c = qs.shape[0]
    a_row = c // 2 - 1 if forward else c // 2
    e_row = c - 1 if forward else 0
    anchor = b[a_row:a_row + 1, :]
    b_end = b[e_row:e_row + 1, :]
    qa = qs * jnp.exp(b - anchor)
    ka = kk * jnp.exp(anchor - b)
    scores = jnp.where(within, _dot_nt(qa.astype(BF16), ka.astype(BF16)), 0.0).astype(BF16)
    return (scores, (qa * jnp.exp(anchor)).astype(BF16), (ka * jnp.exp(b_end - anchor)).astype(BF16),
            jnp.exp(b_end))


def _gla_pairs_exact(qs, kk, b, forward, within, tmp):
    tb_ref, tq_ref, ts_ref = tmp
    c = qs.shape[0]
    b_end = b[c - 1:c, :] if forward else b[0:1, :]
    tb_ref[...] = b
    tq_ref[...] = qs
    ones = jnp.ones((8, qs.shape[1]), BF16)

    def row(t, carry):
        rel = jnp.minimum(tb_ref[pl.ds(t, 1), :] - b, 0.0)
        e = (tq_ref[pl.ds(t, 1), :] * kk) * jnp.exp(rel)
        ts_ref[pl.ds(t, 1), :] = _dot_nt(ones, e.astype(BF16))[0:1, :]
        return carry

    lax.fori_loop(0, c, row, 0)
    scores = jnp.where(within, ts_ref[...], 0.0).astype(BF16)
    return scores, (qs * jnp.exp(b)).astype(BF16), (kk * jnp.exp(b_end - b)).astype(BF16), jnp.exp(b_end)


def _gla_readout(o, g_raw, o_gain):
    ms = jnp.mean(o * o, axis=-1, keepdims=True)
    y = o * lax.rsqrt(ms + EPS) * o_gain
    return (y * _silu(g_raw.astype(F32))).astype(BF16)


def _gla_kernel(qc, ffc, fbc, vc, gc, ql, ffl, fbl, vl, gl, lb_ref, og_ref, yc_ref, yl_ref,
                o_ref, qb_ref, ke_ref, dec_ref, st_ref, tb_ref, tq_ref, ts_ref):
    c = GLA_CHUNK
    tc = qc.shape[0]
    n_ctx = tc // c
    n_lat = ql.shape[0] // c
    lb = (lb_ref[0], lb_ref[1])
    row = lax.broadcasted_iota(jnp.int32, (c, c), 0)
    col = lax.broadcasted_iota(jnp.int32, (c, c), 1)
    within = (col <= row, col >= row)

    def rows(start):
        return pl.ds(start if isinstance(start, int) else pl.multiple_of(start, c), c)

    def local(refs, src0, dst0, chunk0, n, exact):
        q, ff, fb, v = refs
        src = [rows(src0 + j * c) for j in range(n)]
        dst = [rows(dst0 + j * c) for j in range(n)]
        qs = [_silu(q[src[j], :].astype(F32)) * (HEAD ** -0.5) for j in range(n)]
        gates = [[_gla_gates(f[src[j], :], lb[d], within[d]) for d, f in enumerate((ff, fb))] for j in range(n)]
        scores = []
        for j in range(n):
            per_dir = []
            for d in range(2):
                kk, b = gates[j][d]
                if exact:
                    sc, qb, ke, dec = _gla_pairs_exact(qs[j], kk, b, d == 0, within[d], (tb_ref, tq_ref, ts_ref))
                else:
                    sc, qb, ke, dec = _gla_pairs(qs[j], kk, b, d == 0, within[d])
                qb_ref[d, dst[j], :] = qb
                ke_ref[d, dst[j], :] = ke
                dec_ref[d, chunk0 + j] = dec
                per_dir.append(sc)
            scores.append(per_dir)
        for j in range(n):
            vj = v[src[j], :]
            o_ref[dst[j], :] = _dot(scores[j][0], vj) + _dot(scores[j][1], vj)

    def phase1(exact):
        run = 1 if exact else math.gcd(n_lat, GLA_RUN)
        run_ctx = math.gcd(n_ctx, run)
        for i in range(0, n_ctx, run_ctx):
            local((qc, ffc, fbc, vc), i * c, i * c, i, run_ctx, exact)

        def body(i, carry):
            src0 = pl.multiple_of(i * (run * c), c)
            local((ql, ffl, fbl, vl), src0, tc + src0, n_ctx + i * run, run, exact)
            return carry

        lax.fori_loop(0, n_lat // run, body, 0)

    safe = jnp.min(jnp.minimum(lb[0], lb[1])) >= GLA_SAFE_LB

    @pl.when(safe)
    def _():
        phase1(False)

    @pl.when(jnp.logical_not(safe))
    def _():
        phase1(True)

    st_ref[...] = jnp.zeros_like(st_ref)

    def steps(v, first, n, n_seg, row0, chunk0):
        order = [[first + s for s in range(n)], [n_seg - 1 - first - s for s in range(n)]]
        src = [[rows(j * c) for j in order[d]] for d in range(2)]
        dst = [[rows(row0 + j * c) for j in order[d]] for d in range(2)]
        kv = [[_dot_tn(v[src[d][s], :], ke_ref[d, dst[d][s], :]) for s in range(n)] for d in range(2)]
        for d in range(2):
            st = st_ref[d]
            for s in range(n):
                o_ref[dst[d][s], :] += _dot_nt(qb_ref[d, dst[d][s], :], st.astype(BF16))
                st = st * dec_ref[d, chunk0 + order[d][s]] + kv[d][s]
            st_ref[d] = st

    steps(vc, 0, n_ctx, n_ctx, 0, 0)
    run = math.gcd(n_lat, GLA_RUN)

    def steps_body(i, carry):
        steps(vl, i * run, run, n_lat, tc, n_ctx)
        return carry

    lax.fori_loop(0, n_lat // run, steps_body, 0)

    o_gain = og_ref[...]
    for i in range(n_ctx):
        r = slice(i * c, (i + 1) * c)
        yc_ref[r, :] = _gla_readout(o_ref[r, :], gc[r, :], o_gain)

    def out_body(i, carry):
        src = pl.ds(pl.multiple_of(i * c, c), c)
        dst = pl.ds(pl.multiple_of(tc + i * c, c), c)
        yl_ref[src, :] = _gla_readout(o_ref[dst, :], gl[src, :], o_gain)
        return carry

    lax.fori_loop(0, n_lat, out_body, 0, unroll=math.gcd(n_lat, 4))


def _gla_call(p_ctx, p_lat, lower_bound, o_gain):
    b, tc, d5 = p_ctx.shape
    t = p_lat.shape[1]
    d = d5 // 5
    heads = d // HEAD
    assert tc % GLA_CHUNK == 0 and t % GLA_CHUNK == 0

    def col(tt, part):
        return pl.BlockSpec((None, tt, HEAD), lambda bi, h: (bi, 0, part * heads + h))

    out_spec = lambda tt: pl.BlockSpec((None, tt, HEAD), lambda bi, h: (bi, 0, h))
    return pl.pallas_call(
        _gla_kernel,
        grid=(b, heads),
        in_specs=[col(tc, p) for p in range(5)] + [col(t, p) for p in range(5)] + [
            pl.BlockSpec((2, None, 1, HEAD), lambda bi, h: (0, h, 0, 0)),
            pl.BlockSpec((1, HEAD), lambda bi, h: (0, 0)),
        ],
        out_specs=[out_spec(tc), out_spec(t)],
        out_shape=[jax.ShapeDtypeStruct((b, tc, d), BF16), jax.ShapeDtypeStruct((b, t, d), BF16)],
        scratch_shapes=[
            pltpu.VMEM((tc + t, HEAD), F32),
            pltpu.VMEM((2, tc + t, HEAD), BF16),
            pltpu.VMEM((2, tc + t, HEAD), BF16),
            pltpu.VMEM((2, (tc + t) // GLA_CHUNK, 1, HEAD), F32),
            pltpu.VMEM((2, HEAD, HEAD), F32),
            pltpu.VMEM((GLA_CHUNK, HEAD), F32),
            pltpu.VMEM((GLA_CHUNK, HEAD), F32),
            pltpu.VMEM((GLA_CHUNK, GLA_CHUNK), F32),
        ],
        compiler_params=_params(("arbitrary", "arbitrary")),
        name="hgrn_scan",
    )(*([p_ctx] * 5), *([p_lat] * 5), lower_bound.reshape(2, heads, 1, HEAD), o_gain.reshape(1, HEAD))


def _flash_kernel(q_ref, kc_ref, vc_ref, kl_ref, vl_ref, o_ref, *, tk, unroll):
    tq = q_ref.shape[0]
    group = q_ref.shape[1] // HEAD
    qs = [q_ref[:, g * HEAD:(g + 1) * HEAD] for g in range(group)]

    def attend(carry, k, v):
        v_ext = jnp.concatenate([v, jnp.ones_like(v)], axis=1)
        out = []
        for g in range(group):
            m, acc = carry[g]
            s = _dot_nt(qs[g], k)
            m_new = jnp.maximum(m, jnp.max(s, axis=-1, keepdims=True))
            p = jnp.exp2(s - m_new).astype(BF16)
            out.append((m_new, jnp.exp2(m - m_new) * acc + _dot(p, v_ext)))
        return tuple(out)

    def body(j, carry):
        r = pl.ds(pl.multiple_of(j * tk, tk), tk)
        return attend(carry, kl_ref[r, :], vl_ref[r, :])

    init = tuple((jnp.full((tq, 1), -jnp.inf, F32), jnp.zeros((tq, 2 * HEAD), F32)) for _ in range(group))
    carry = attend(init, kc_ref[...], vc_ref[...])
    final = lax.fori_loop(0, kl_ref.shape[0] // tk, body, carry, unroll=unroll)
    for g in range(group):
        acc = final[g][1]
        o_ref[:, g * HEAD:(g + 1) * HEAD] = (acc[:, :HEAD] / acc[:, HEAD:]).astype(o_ref.dtype)


def _flash_call(qkv, kv_ctx, *, heads):
    b, t, n = qkv.shape
    tc = kv_ctx.shape[1]
    kvh = (n // HEAD - heads) // 2
    group = heads // kvh
    tq = _pick(t, 512, BF16_SUBLANES)
    tk = _pick(t, 1024, 2 * HEAD)
    body = functools.partial(_flash_kernel, tk=tk, unroll=math.gcd(t // tk, 8))
    return pl.pallas_call(
        body,
        grid=(b, kvh, t // tq),
        in_specs=[
            pl.BlockSpec((None, tq, group * HEAD), lambda bi, h, i: (bi, i, h)),
            pl.BlockSpec((None, tc, HEAD), lambda bi, h, i: (bi, 0, h)),
            pl.BlockSpec((None, tc, HEAD), lambda bi, h, i: (bi, 0, kvh + h)),
            pl.BlockSpec((None, t, HEAD), lambda bi, h, i: (bi, 0, heads + h)),
            pl.BlockSpec((None, t, HEAD), lambda bi, h, i: (bi, 0, heads + kvh + h)),
        ],
        out_specs=pl.BlockSpec((None, tq, group * HEAD), lambda bi, h, i: (bi, i, h)),
        out_shape=jax.ShapeDtypeStruct((b, t, heads * HEAD), BF16),
        compiler_params=_params(("arbitrary", "arbitrary", "arbitrary")),
        name="gqa_flash",
    )(qkv, kv_ctx, kv_ctx, qkv, qkv)


def _rope_tables(t):
    pos = jnp.arange(t, dtype=jnp.int32)
    rows = (pos // GRID_W).astype(F32)
    cols = (pos % GRID_W).astype(F32)
    axis_dim = HEAD // 2
    inv_freq = ROPE_THETA ** (-jnp.arange(0, axis_dim, 2, dtype=F32) / axis_dim)
    ang = jnp.concatenate([rows[:, None] * inv_freq, cols[:, None] * inv_freq], axis=-1)
    cos, sin = jnp.cos(ang), jnp.sin(ang)
    return jnp.concatenate([cos, cos], axis=-1), jnp.concatenate([-sin, sin], axis=-1)


def _split_mods(mods, b):
    d = mods.shape[1] // N_MOD
    lat = [mods[:b, i * d:(i + 1) * d].reshape(b, 1, d) for i in range(N_MOD)]
    ctx = [mods[b:b + 1, i * d:(i + 1) * d].reshape(1, 1, d) for i in range(N_MOD)]
    return lat, ctx


def kernel(x, c, ctx, c_ctx, ada_w, ada_b, norm_mix_pre, norm_mix_post, norm_ffn_pre, norm_ffn_post, hgrn_w_in, hgrn_lb_logits, hgrn_o_norm, hgrn_w_out, attn_w_qkv, attn_q_norm, attn_k_norm, attn_w_out, ffn_w_in, ffn_conv_w, ffn_conv_b, ffn_w_out):
    b, t, d = x.shape
    tc = ctx.shape[1]
    heads = d // HEAD
    assert ada_w.shape[0] == 2 and b + 1 <= ADA_ROWS

    cond = jnp.zeros((ADA_ROWS, d), F32).at[:b].set(c).at[b].set(c_ctx)
    mods = _ada_call(cond, ada_w, ada_b)
    x_lat = x.reshape(b * t, d)
    x_ctx = ctx.reshape(b * tc, d)

    (sh_ml, sc_ml, gt_ml, sh_fl, sc_fl, gt_fl), (sh_mc, sc_mc, gt_mc, sh_fc, sc_fc, gt_fc) = _split_mods(mods[0], b)
    lower_bound = jnp.cumsum(jax.nn.softmax(hgrn_lb_logits.astype(F32), axis=1), axis=1)[:, 0]
    hgrn_w_in, hgrn_w_out, attn_w_out = hgrn_w_in.astype(BF16), hgrn_w_out.astype(BF16), attn_w_out.astype(BF16)
    ffn_w_in, ffn_w_out = ffn_w_in.astype(BF16), ffn_w_out.astype(BF16)
    p_lat = _nmm_call(x_lat, norm_mix_pre[0], sh_ml, sc_ml, hgrn_w_in, name="hgrn_in_lat")
    p_ctx = _nmm_call(x_ctx, norm_mix_pre[0], sh_mc, sc_mc, hgrn_w_in, name="hgrn_in_ctx")
    y_ctx, y_lat = _gla_call(p_ctx.reshape(b, tc, 5 * d), p_lat.reshape(b, t, 5 * d), lower_bound, hgrn_o_norm[0])
    x_lat = _proj_res_call(y_lat.reshape(b * t, d), hgrn_w_out, x_lat, norm_mix_post[0], gt_ml, name="hgrn_out_lat")
    x_ctx = _proj_res_call(y_ctx.reshape(b * tc, d), hgrn_w_out, x_ctx, norm_mix_post[0], gt_mc, name="hgrn_out_ctx")
    a = _ffn_in_call(x_lat, norm_ffn_pre[0], sh_fl, sc_fl, ffn_w_in, ffn_conv_w[0], ffn_conv_b[0],
                     layer=0, seq_len=t, name="ffn0_in_lat")
    x_lat = _proj_res_call(a, ffn_w_out, x_lat, norm_ffn_post[0], gt_fl, layer=0, name="ffn0_out_lat")
    a = _ffn_in_call(x_ctx, norm_ffn_pre[0], sh_fc, sc_fc, ffn_w_in, ffn_conv_w[0], ffn_conv_b[0],
                     layer=0, seq_len=tc, name="ffn0_in_ctx")
    x_ctx = _proj_res_call(a, ffn_w_out, x_ctx, norm_ffn_post[0], gt_fc, layer=0, name="ffn0_out_ctx")

    (sh_ml, sc_ml, gt_ml, sh_fl, sc_fl, gt_fl), (sh_mc, sc_mc, _, _, _, _) = _split_mods(mods[1], b)
    n_qkv = attn_w_qkv.shape[2]
    kvh = (n_qkv // HEAD - heads) // 2
    perm = jnp.concatenate([jnp.arange(0, HEAD, 2), jnp.arange(1, HEAD, 2)])
    head_perm = (jnp.arange(heads + kvh)[:, None] * HEAD + perm[None, :]).reshape(-1)
    col_perm = jnp.concatenate([head_perm, jnp.arange((heads + kvh) * HEAD, n_qkv)])
    w_qkv = attn_w_qkv[:, :, col_perm].astype(BF16)
    head_gain = jnp.concatenate([
        jnp.tile(attn_q_norm[0][perm] * (LOG2_E * HEAD ** -0.5), heads),
        jnp.tile(attn_k_norm[0][perm], kvh),
        jnp.ones((kvh * HEAD,), F32),
    ]).reshape(1, n_qkv)
    cos, sin = _rope_tables(t)
    qk_cols = (heads + kvh) * HEAD
    q_cols = heads * HEAD
    qkv = _nmm_call(x_lat, norm_mix_pre[1], sh_ml, sc_ml, w_qkv, name="attn_qkv_lat")
    qkv = _qk_rope_call(qkv, head_gain[:, :qk_cols], cos, sin, name="attn_qk_rope_lat")
    kv_ctx = _nmm_call(x_ctx, norm_mix_pre[1], sh_mc, sc_mc, w_qkv, col0=q_cols, name="attn_kv_ctx")
    kv_ctx = _qk_rope_call(kv_ctx, head_gain[:, q_cols:qk_cols], jnp.ones((tc, HEAD), F32),
                           jnp.zeros((tc, HEAD), F32), name="attn_k_norm_ctx")
    qkv = qkv.reshape(b, t, n_qkv)
    kv_ctx = kv_ctx.reshape(b, tc, 2 * kvh * HEAD)
    o = _flash_call(qkv, kv_ctx, heads=heads)
    x_lat = _proj_res_call(o.reshape(b * t, d), attn_w_out, x_lat, norm_mix_post[1], gt_ml, name="attn_out_lat")
    a = _ffn_in_call(x_lat, norm_ffn_pre[1], sh_fl, sc_fl, ffn_w_in, ffn_conv_w[1], ffn_conv_b[1],
                     layer=1, seq_len=t, name="ffn1_in_lat")
    x_lat = _proj_res_call(a, ffn_w_out, x_lat, norm_ffn_post[1], gt_fl, layer=1, name="ffn1_out_lat")
    return x_lat.reshape(b, t, d)
```

```python
import functools
import math

import jax
import jax.numpy as jnp
from jax import lax
from jax.experimental import pallas as pl
from jax.experimental.pallas import tpu as pltpu

F32 = jnp.float32
BF16 = jnp.bfloat16

EPS = 1e-6
GRID_W = 64
ROPE_THETA = 10000.0
LOG2_E = math.log2(math.e)
N_MOD = 6
HEAD = 128
ROW_GROUP = 16
GLA_CHUNK = 128
GLA_RUN = 8
GLA_SAFE_LB = math.exp(-80.0 / (GLA_CHUNK // 2))
BF16_SUBLANES = 16
ADA_ROWS = 8
V7X_VMEM_BYTES = 64 * 1024 * 1024
VMEM_LIMIT = V7X_VMEM_BYTES - 6 * 1024 * 1024


def _pick(n, target, align):
    if n <= target:
        return n
    for t in range(target - target % align, 0, -align):
        if n % t == 0:
            return t
    return n


def _params(sem):
    return pltpu.CompilerParams(dimension_semantics=sem, vmem_limit_bytes=VMEM_LIMIT)


def _sigmoid(x):
    return 0.5 * jnp.tanh(0.5 * x) + 0.5


def _silu(x):
    return x * _sigmoid(x)


def _dot(a, b):
    return jnp.dot(a, b, preferred_element_type=F32)


def _dot_nt(a, b):
    return lax.dot_general(a, b, (((1,), (1,)), ((), ())), preferred_element_type=F32)


def _dot_tn(a, b):
    return lax.dot_general(a, b, (((0,), (0,)), ((), ())), preferred_element_type=F32)


def _ada_kernel(c_ref, w_ref, b_ref, o_ref):
    sc = _silu(c_ref[...]).astype(BF16)
    o_ref[...] = _dot(sc, w_ref[...].astype(BF16)) + b_ref[...]


def _ada_call(cond, ada_w, ada_b):
    depth, d, n = ada_w.shape
    tn = _pick(n, 1024, HEAD)
    return pl.pallas_call(
        _ada_kernel,
        grid=(depth, n // tn),
        in_specs=[
            pl.BlockSpec((ADA_ROWS, d), lambda l, j: (0, 0)),
            pl.BlockSpec((None, d, tn), lambda l, j: (l, 0, j)),
            pl.BlockSpec((None, 1, tn), lambda l, j: (l, 0, j)),
        ],
        out_specs=pl.BlockSpec((None, ADA_ROWS, tn), lambda l, j: (l, 0, j)),
        out_shape=jax.ShapeDtypeStruct((depth, ADA_ROWS, n), F32),
        compiler_params=_params(("arbitrary", "arbitrary")),
        name="ada_ln",
    )(cond, ada_w, ada_b.reshape(depth, 1, n))


def _norm_mod(x, gain, shift, scale):
    ms = jnp.mean(x * x, axis=-1, keepdims=True)
    return (x * lax.rsqrt(ms + EPS) * gain) * (1.0 + scale) + shift


def _norm_mod_rows(x_ref, n_rows, h_ref, h_row0, gain, shift, scale):
    mult = gain * (1.0 + scale)

    def body(r, carry):
        start = pl.multiple_of(r * ROW_GROUP, ROW_GROUP)
        x = x_ref[pl.ds(start, ROW_GROUP), :]
        ms = jnp.mean(x * x, axis=-1, keepdims=True)
        dst = pl.ds(pl.multiple_of(h_row0 + start, ROW_GROUP), ROW_GROUP)
        h_ref[dst, :] = (x * lax.rsqrt(ms + EPS) * mult + shift).astype(BF16)
        return carry

    groups = n_rows // ROW_GROUP
    lax.fori_loop(0, groups, body, 0, unroll=math.gcd(groups, 4))


def _split_rows(tm):
    half = tm // 2
    return half if half % ROW_GROUP == 0 else tm


def _nmm_kernel(x_ref, gain_ref, shift_ref, scale_ref, w_ref, o_ref, h_ref):
    j = pl.program_id(1)
    tm = x_ref.shape[0]
    lead = _split_rows(tm)

    @pl.when(j == 0)
    def _():
        gain, shift, scale = gain_ref[...], shift_ref[...], scale_ref[...]
        _norm_mod_rows(x_ref, lead, h_ref, 0, gain, shift, scale)
        if lead < tm:
            h_ref[lead:, :] = _norm_mod(x_ref[lead:, :], gain, shift, scale).astype(BF16)
        o_ref[:lead, :] = _dot(h_ref[:lead, :], w_ref[...]).astype(o_ref.dtype)
        if lead < tm:
            o_ref[lead:, :] = _dot(h_ref[lead:, :], w_ref[...]).astype(o_ref.dtype)

    @pl.when(j > 0)
    def _():
        o_ref[...] = _dot(h_ref[...], w_ref[...]).astype(o_ref.dtype)


def _nmm_call(x, gain, shift, scale, w, *, layer=0, col0=0, name):
    m, d = x.shape
    n = w.shape[2] - col0
    bm = shift.shape[0]
    rows_per_mod = m // bm
    tm = _pick(rows_per_mod, 1024, BF16_SUBLANES)
    tn = _pick(math.gcd(n, col0) if col0 else n, 1024, HEAD)
    mod_blocks = rows_per_mod // tm
    col_block0 = col0 // tn
    return pl.pallas_call(
        _nmm_kernel,
        grid=(m // tm, n // tn),
        in_specs=[
            pl.BlockSpec((tm, d), lambda i, j: (i, 0)),
            pl.BlockSpec((1, d), lambda i, j: (0, 0)),
            pl.BlockSpec((None, 1, d), lambda i, j: (i // mod_blocks, 0, 0)),
            pl.BlockSpec((None, 1, d), lambda i, j: (i // mod_blocks, 0, 0)),
            pl.BlockSpec((None, d, tn), lambda i, j: (layer, 0, col_block0 + j)),
        ],
        out_specs=pl.BlockSpec((tm, tn), lambda i, j: (i, j)),
        out_shape=jax.ShapeDtypeStruct((m, n), BF16),
        scratch_shapes=[pltpu.VMEM((tm, d), BF16)],
        compiler_params=_params(("arbitrary", "arbitrary")),
        name=name,
    )(x, gain.reshape(1, d), shift, scale, w)


def _qk_rope_kernel(x_ref, hg_ref, cos_ref, sin_ref, o_ref):
    cos = cos_ref[...]
    sin = sin_ref[...]
    mean_lanes = jnp.full((HEAD, HEAD), 1.0 / HEAD, BF16)
    for hh in range(x_ref.shape[1] // HEAD):
        cols = slice(hh * HEAD, (hh + 1) * HEAD)
        a = x_ref[:, cols].astype(F32)
        ms = _dot((a * a).astype(BF16), mean_lanes)
        a = a * lax.rsqrt(ms + EPS) * hg_ref[:, cols]
        a = a * cos + pltpu.roll(a, HEAD // 2, axis=1) * sin
        o_ref[:, cols] = a.astype(o_ref.dtype)


def _qk_rope_call(p, head_gain, cos, sin, *, name):
    m, n = p.shape
    nn = head_gain.shape[1]
    tm = _pick(math.gcd(m, cos.shape[0]), 2048, BF16_SUBLANES)
    tn = _pick(math.gcd(n, nn), 1024, HEAD)
    table_blocks = cos.shape[0] // tm
    return pl.pallas_call(
        _qk_rope_kernel,
        grid=(m // tm, nn // tn),
        in_specs=[
            pl.BlockSpec((tm, tn), lambda i, j: (i, j)),
            pl.BlockSpec((1, tn), lambda i, j: (0, j)),
            pl.BlockSpec((tm, HEAD), lambda i, j: (i % table_blocks, 0)),
            pl.BlockSpec((tm, HEAD), lambda i, j: (i % table_blocks, 0)),
        ],
        out_specs=pl.BlockSpec((tm, tn), lambda i, j: (i, j)),
        out_shape=jax.ShapeDtypeStruct((m, n), p.dtype),
        input_output_aliases={0: 0},
        compiler_params=_params(("arbitrary", "arbitrary")),
        name=name,
    )(p, head_gain, cos, sin)


def _residual(x, y, gain, gate):
    ms = jnp.mean(y * y, axis=-1, keepdims=True)
    return x + gate * (y * lax.rsqrt(ms + EPS) * gain)


def _proj_res_kernel(a_ref, w_ref, x_ref, gain_ref, gate_ref, o_ref):
    y = _dot(a_ref[...], w_ref[...])
    o_ref[...] = _residual(x_ref[...], y, gain_ref[...], gate_ref[...])


def _proj_res_ktiled_kernel(a_ref, w_ref, x_ref, gain_ref, gate_ref, o_ref):
    k = pl.program_id(1)
    last = pl.num_programs(1) - 1
    tm = o_ref.shape[0]
    lead = _split_rows(tm)

    @pl.when(k == 0)
    def _():
        o_ref[...] = _dot(a_ref[...], w_ref[...])

    @pl.when(jnp.logical_and(k > 0, k < last))
    def _():
        o_ref[...] += _dot(a_ref[...], w_ref[...])

    @pl.when(k == last)
    def _():
        gain, gate = gain_ref[...], gate_ref[...]
        parts = [slice(0, lead), slice(lead, tm)] if lead < tm else [slice(0, tm)]
        ys = [o_ref[r, :] + _dot(a_ref[r, :], w_ref[...]) for r in parts]
        for r, y in zip(parts, ys):
            o_ref[r, :] = _residual(x_ref[r, :], y, gain, gate)


def _proj_res_call(a, w, x, gain, gate, *, layer=0, name):
    m, k = a.shape
    d = w.shape[2]
    bm = gate.shape[0]
    rows_per_mod = m // bm
    tk = k if k <= d else _pick(k, 512, 2 * HEAD)
    kb = k // tk
    tm = _pick(rows_per_mod, 512 if kb == 1 else 1024, BF16_SUBLANES)
    mod_blocks = rows_per_mod // tm
    return pl.pallas_call(
        _proj_res_kernel if kb == 1 else _proj_res_ktiled_kernel,
        grid=(m // tm, kb),
        in_specs=[
            pl.BlockSpec((tm, tk), lambda i, kk: (i, kk)),
            pl.BlockSpec((None, tk, d), lambda i, kk: (layer, kk, 0)),
            pl.BlockSpec((tm, d), lambda i, kk: (i, 0)),
            pl.BlockSpec((1, d), lambda i, kk: (0, 0)),
            pl.BlockSpec((None, 1, d), lambda i, kk: (i // mod_blocks, 0, 0)),
        ],
        out_specs=pl.BlockSpec((tm, d), lambda i, kk: (i, 0)),
        out_shape=jax.ShapeDtypeStruct((m, d), F32),
        compiler_params=_params(("arbitrary", "arbitrary")),
        name=name,
    )(a, w, x, gain.reshape(1, d), gate)


def _ffn_in_kernel(x_ref, xprev_ref, xnext_ref, gain_ref, shift_ref, scale_ref, wg_ref, wu_ref, cw_ref, cb_ref,
                   o_ref, h_ref, *, blocks_per_seq):
    i = pl.program_id(0)
    j = pl.program_id(1)
    tm = x_ref.shape[0]
    halo = BF16_SUBLANES
    lead = _split_rows(tm)

    def gated(gate, up):
        rows = gate.shape[0]
        g_prev = pltpu.roll(gate, 1, axis=0)[halo:halo + tm, :]
        g_next = pltpu.roll(gate, rows - 1, axis=0)[halo:halo + tm, :]
        conv = (g_prev * cw_ref[0:1, :] + gate[halo:halo + tm, :] * cw_ref[1:2, :] + g_next * cw_ref[2:3, :]
                + cb_ref[...])
        o_ref[...] = (_silu(conv) * up).astype(o_ref.dtype)

    @pl.when(j == 0)
    def _():
        gain, shift, scale = gain_ref[...], shift_ref[...], scale_ref[...]
        seq_block = i % blocks_per_seq
        h_prev = jnp.where(seq_block > 0, _norm_mod(xprev_ref[...], gain, shift, scale), 0.0)
        h_next = jnp.where(seq_block < blocks_per_seq - 1, _norm_mod(xnext_ref[...], gain, shift, scale), 0.0)
        zeros = jnp.zeros((halo - h_prev.shape[0], h_prev.shape[1]), F32)
        h_ref[0:halo, :] = jnp.concatenate([zeros, h_prev], axis=0).astype(BF16)
        h_ref[halo + tm:, :] = jnp.concatenate([h_next, zeros], axis=0).astype(BF16)
        _norm_mod_rows(x_ref, lead, h_ref, halo, gain, shift, scale)
        if lead == tm:
            gated(_dot(h_ref[...], wg_ref[...]), _dot(h_ref[halo:halo + tm, :], wu_ref[...]))
        else:
            cut = halo + lead
            h_ref[cut:halo + tm, :] = _norm_mod(x_ref[lead:, :], gain, shift, scale).astype(BF16)
            gate = jnp.concatenate([_dot(h_ref[:cut, :], wg_ref[...]), _dot(h_ref[cut:, :], wg_ref[...])], axis=0)
            up = jnp.concatenate([_dot(h_ref[halo:cut, :], wu_ref[...]),
                                  _dot(h_ref[cut:halo + tm, :], wu_ref[...])], axis=0)
            gated(gate, up)

    @pl.when(j > 0)
    def _():
        gated(_dot(h_ref[...], wg_ref[...]), _dot(h_ref[halo:halo + tm, :], wu_ref[...]))


def _ffn_in_call(x, gain, shift, scale, w, conv_w, conv_b, *, layer, seq_len, name):
    m, d = x.shape
    f = w.shape[2] // 2
    bm = shift.shape[0]
    rows_per_mod = m // bm
    tm = _pick(seq_len, 1024, BF16_SUBLANES)
    tn = _pick(f, 512, HEAD)
    assert rows_per_mod % tm == 0
    mod_blocks = rows_per_mod // tm
    nb = f // tn
    xh = 8
    tiles_per_tm = tm // xh
    last_tile = m // xh - 1
    body = functools.partial(_ffn_in_kernel, blocks_per_seq=seq_len // tm)
    return pl.pallas_call(
        body,
        grid=(m // tm, nb),
        in_specs=[
            pl.BlockSpec((tm, d), lambda i, j: (i, 0)),
            pl.BlockSpec((xh, d), lambda i, j: (jnp.maximum(i * tiles_per_tm - 1, 0), 0)),
            pl.BlockSpec((xh, d), lambda i, j: (jnp.minimum((i + 1) * tiles_per_tm, last_tile), 0)),
            pl.BlockSpec((1, d), lambda i, j: (0, 0)),
            pl.BlockSpec((None, 1, d), lambda i, j: (i // mod_blocks, 0, 0)),
            pl.BlockSpec((None, 1, d), lambda i, j: (i // mod_blocks, 0, 0)),
            pl.BlockSpec((None, d, tn), lambda i, j: (layer, 0, j)),
            pl.BlockSpec((None, d, tn), lambda i, j: (layer, 0, nb + j)),
            pl.BlockSpec((3, tn), lambda i, j: (0, j)),
            pl.BlockSpec((1, tn), lambda i, j: (0, j)),
        ],
        out_specs=pl.BlockSpec((tm, tn), lambda i, j: (i, j)),
        out_shape=jax.ShapeDtypeStruct((m, f), BF16),
        scratch_shapes=[pltpu.VMEM((tm + 2 * BF16_SUBLANES, d), BF16)],
        compiler_params=_params(("arbitrary", "arbitrary")),
        name=name,
    )(x, x, x, gain.reshape(1, d), shift, scale, w, w, conv_w, conv_b.reshape(1, f))


def _gla_gates(f_raw, lb, within):
    half = 0.5 * (1.0 - lb)
    ht = half * jnp.tanh(0.5 * f_raw.astype(F32))
    lf = jnp.log((1.0 - half) + ht)
    ones = jnp.where(within, 1.0, 0.0).astype(BF16)
    lf_hi = lf.astype(BF16)
    lf_lo = (lf - lf_hi.astype(F32)).astype(BF16)
    return half - ht, _dot(ones, lf_hi) + _dot(ones, lf_lo)


def _gla_pairs(qs, kk, b, forward, within):
    c = qs.shape[0]
    a_row = c // 2 - 1 if forward else c // 2
    e_row = c - 1 if forward else 0
    anchor = b[a_row:a_row + 1, :]
    b_end = b[e_row:e_row + 1, :]
    qa = qs * jnp.exp(b - anchor)
    ka = kk * jnp.exp(anchor - b)
    scores = jnp.where(within, _dot_nt(qa.astype(BF16), ka.astype(BF16)), 0.0).astype(BF16)
    return (scores, (qa * jnp.exp(anchor)).astype(BF16), (ka * jnp.exp(b_end - anchor)).astype(BF16),
            jnp.exp(b_end))


def _gla_pairs_exact(qs, kk, b, forward, within, tmp):
    tb_ref, tq_ref, ts_ref = tmp
    c = qs.shape[0]
    b_end = b[c - 1:c, :] if forward else b[0:1, :]
    tb_ref[...] = b
    tq_ref[...] = qs
    ones = jnp.ones((8, qs.shape[1]), BF16)

    def row(t, carry):
        rel = jnp.minimum(tb_ref[pl.ds(t, 1), :] - b, 0.0)
        e = (tq_ref[pl.ds(t, 1), :] * kk) * jnp.exp(rel)
        ts_ref[pl.ds(t, 1), :] = _dot_nt(ones, e.astype(BF16))[0:1, :]
        return carry

    lax.fori_loop(0, c, row, 0)
    scores = jnp.where(within, ts_ref[...], 0.0).astype(BF16)
    return scores, (qs * jnp.exp(b)).astype(BF16), (kk * jnp.exp(b_end - b)).astype(BF16), jnp.exp(b_end)


def _gla_readout(o, g_raw, o_gain):
    ms = jnp.mean(o * o, axis=-1, keepdims=True)
    y = o * lax.rsqrt(ms + EPS) * o_gain
    return (y * _silu(g_raw.astype(F32))).astype(BF16)


def _gla_kernel(qc, ffc, fbc, vc, gc, ql, ffl, fbl, vl, gl, lb_ref, og_ref, yc_ref, yl_ref,
                o_ref, qb_ref, ke_ref, dec_ref, st_ref, tb_ref, tq_ref, ts_ref):
    c = GLA_CHUNK
    tc = qc.shape[0]
    n_ctx = tc // c
    n_lat = ql.shape[0] // c
    lb = (lb_ref[0], lb_ref[1])
    row = lax.broadcasted_iota(jnp.int32, (c, c), 0)
    col = lax.broadcasted_iota(jnp.int32, (c, c), 1)
    within = (col <= row, col >= row)

    def rows(start):
        return pl.ds(start if isinstance(start, int) else pl.multiple_of(start, c), c)

    def local(refs, src0, dst0, chunk0, n, exact):
        q, ff, fb, v = refs
        src = [rows(src0 + j * c) for j in range(n)]
        dst = [rows(dst0 + j * c) for j in range(n)]
        qs = [_silu(q[src[j], :].astype(F32)) * (HEAD ** -0.5) for j in range(n)]
        gates = [[_gla_gates(f[src[j], :], lb[d], within[d]) for d, f in enumerate((ff, fb))] for j in range(n)]
        scores = []
        for j in range(n):
            per_dir = []
            for d in range(2):
                kk, b = gates[j][d]
                if exact:
                    sc, qb, ke, dec = _gla_pairs_exact(qs[j], kk, b, d == 0, within[d], (tb_ref, tq_ref, ts_ref))
                else:
                    sc, qb, ke, dec = _gla_pairs(qs[j], kk, b, d == 0, within[d])
                qb_ref[d, dst[j], :] = qb
                ke_ref[d, dst[j], :] = ke
                dec_ref[d, chunk0 + j] = dec
                per_dir.append(sc)
            scores.append(per_dir)
        for j in range(n):
            vj = v[src[j], :]
            o_ref[dst[j], :] = _dot(scores[j][0], vj) + _dot(scores[j][1], vj)

    def phase1(exact):
        run = 1 if exact else math.gcd(n_lat, GLA_RUN)
        run_ctx = math.gcd(n_ctx, run)
        for i in range(0, n_ctx, run_ctx):
            local((qc, ffc, fbc, vc), i * c, i * c, i, run_ctx, exact)

        def body(i, carry):
            src0 = pl.multiple_of(i * (run * c), c)
            local((ql, ffl, fbl, vl), src0, tc + src0, n_ctx + i * run, run, exact)
            return carry

        lax.fori_loop(0, n_lat // run, body, 0)

    safe = jnp.min(jnp.minimum(lb[0], lb[1])) >= GLA_SAFE_LB

    @pl.when(safe)
    def _():
        phase1(False)

    @pl.when(jnp.logical_not(safe))
    def _():
        phase1(True)

    st_ref[...] = jnp.zeros_like(st_ref)

    o_gain = og_ref[...]

    def steps(v, first, n, n_seg, row0, chunk0, finish=None):
        order = [[first + s for s in range(n)], [n_seg - 1 - first - s for s in range(n)]]
        src = [[rows(j * c) for j in order[d]] for d in range(2)]
        dst = [[rows(row0 + j * c) for j in order[d]] for d in range(2)]
        kv = [[_dot_tn(v[src[d][s], :], ke_ref[d, dst[d][s], :]) for s in range(n)] for d in range(2)]
        for d in range(2):
            st = st_ref[d]
            for s in range(n):
                o = o_ref[dst[d][s], :] + _dot_nt(qb_ref[d, dst[d][s], :], st.astype(BF16))
                if finish is None:
                    o_ref[dst[d][s], :] = o
                else:
                    g, y = finish
                    y[src[d][s], :] = _gla_readout(o, g[src[d][s], :], o_gain)
                st = st * dec_ref[d, chunk0 + order[d][s]] + kv[d][s]
            st_ref[d] = st

    steps(vc, 0, n_ctx, n_ctx, 0, 0)
    for i in range(n_ctx):
        r = slice(i * c, (i + 1) * c)
        yc_ref[r, :] = _gla_readout(o_ref[r, :], gc[r, :], o_gain)

    run = math.gcd(n_lat, GLA_RUN)
    n_runs = n_lat // run
    meet = (n_runs + 1) // 2

    def steps_body(i, carry):
        steps(vl, i * run, run, n_lat, tc, n_ctx)
        return carry

    def last_steps_body(i, carry):
        steps(vl, i * run, run, n_lat, tc, n_ctx, finish=(gl, yl_ref))
        return carry

    lax.fori_loop(0, meet, steps_body, 0)
    if n_runs % 2:
        for j in range((n_runs // 2) * run, (n_runs // 2 + 1) * run):
            yl_ref[rows(j * c), :] = _gla_readout(o_ref[rows(tc + j * c), :], gl[rows(j * c), :], o_gain)
    lax.fori_loop(meet, n_runs, last_steps_body, 0)


def _gla_call(p_ctx, p_lat, lower_bound, o_gain):
    b, tc, d5 = p_ctx.shape
    t = p_lat.shape[1]
    d = d5 // 5
    heads = d // HEAD
    assert tc % GLA_CHUNK == 0 and t % GLA_CHUNK == 0

    def col(tt, part):
        return pl.BlockSpec((None, tt, HEAD), lambda bi, h: (bi, 0, part * heads + h))

    out_spec = lambda tt: pl.BlockSpec((None, tt, HEAD), lambda bi, h: (bi, 0, h))
    return pl.pallas_call(
        _gla_kernel,
        grid=(b, heads),
        in_specs=[col(tc, p) for p in range(5)] + [col(t, p) for p in range(5)] + [
            pl.BlockSpec((2, None, 1, HEAD), lambda bi, h: (0, h, 0, 0)),
            pl.BlockSpec((1, HEAD), lambda bi, h: (0, 0)),
        ],
        out_specs=[out_spec(tc), out_spec(t)],
        out_shape=[jax.ShapeDtypeStruct((b, tc, d), BF16), jax.ShapeDtypeStruct((b, t, d), BF16)],
        scratch_shapes=[
            pltpu.VMEM((tc + t, HEAD), F32),
            pltpu.VMEM((2, tc + t, HEAD), BF16),
            pltpu.VMEM((2, tc + t, HEAD), BF16),
            pltpu.VMEM((2, (tc + t) // GLA_CHUNK, 1, HEAD), F32),
            pltpu.VMEM((2, HEAD, HEAD), F32),
            pltpu.VMEM((GLA_CHUNK, HEAD), F32),
            pltpu.VMEM((GLA_CHUNK, HEAD), F32),
            pltpu.VMEM((GLA_CHUNK, GLA_CHUNK), F32),
        ],
        compiler_params=_params(("arbitrary", "arbitrary")),
        name="hgrn_scan",
    )(*([p_ctx] * 5), *([p_lat] * 5), lower_bound.reshape(2, heads, 1, HEAD), o_gain.reshape(1, HEAD))


def _flash_kernel(q_ref, kc_ref, vc_ref, kl_ref, vl_ref, o_ref, *, tk, unroll):
    tq = q_ref.shape[0]
    group = q_ref.shape[1] // HEAD
    qs = [q_ref[:, g * HEAD:(g + 1) * HEAD] for g in range(group)]

    def attend(carry, k, v):
        v_ext = jnp.concatenate([v, jnp.ones_like(v)], axis=1)
        out = []
        for g in range(group):
            m, acc = carry[g]
            s = _dot_nt(qs[g], k)
            m_new = jnp.maximum(m, jnp.max(s, axis=-1, keepdims=True))
            p = jnp.exp2(s - m_new).astype(BF16)
            out.append((m_new, jnp.exp2(m - m_new) * acc + _dot(p, v_ext)))
        return tuple(out)

    def body(j, carry):
        r = pl.ds(pl.multiple_of(j * tk, tk), tk)
        return attend(carry, kl_ref[r, :], vl_ref[r, :])

    init = tuple((jnp.full((tq, 1), -jnp.inf, F32), jnp.zeros((tq, 2 * HEAD), F32)) for _ in range(group))
    carry = attend(init, kc_ref[...], vc_ref[...])
    final = lax.fori_loop(0, kl_ref.shape[0] // tk, body, carry, unroll=unroll)
    for g in range(group):
        acc = final[g][1]
        o_ref[:, g * HEAD:(g + 1) * HEAD] = (acc[:, :HEAD] / acc[:, HEAD:]).astype(o_ref.dtype)


def _flash_call(qkv, kv_ctx, *, heads):
    b, t, n = qkv.shape
    tc = kv_ctx.shape[1]
    kvh = (n // HEAD - heads) // 2
    group = heads // kvh
    tq = _pick(t, 512, BF16_SUBLANES)
    tk = _pick(t, 1024, 2 * HEAD)
    body = functools.partial(_flash_kernel, tk=tk, unroll=math.gcd(t // tk, 8))
    return pl.pallas_call(
        body,
        grid=(b, kvh, t // tq),
        in_specs=[
            pl.BlockSpec((None, tq, group * HEAD), lambda bi, h, i: (bi, i, h)),
            pl.BlockSpec((None, tc, HEAD), lambda bi, h, i: (bi, 0, h)),
            pl.BlockSpec((None, tc, HEAD), lambda bi, h, i: (bi, 0, kvh + h)),
            pl.BlockSpec((None, t, HEAD), lambda bi, h, i: (bi, 0, heads + h)),
            pl.BlockSpec((None, t, HEAD), lambda bi, h, i: (bi, 0, heads + kvh + h)),
        ],
        out_specs=pl.BlockSpec((None, tq, group * HEAD), lambda bi, h, i: (bi, i, h)),
        out_shape=jax.ShapeDtypeStruct((b, t, heads * HEAD), BF16),
        compiler_params=_params(("arbitrary", "arbitrary", "arbitrary")),
        name="gqa_flash",
    )(qkv, kv_ctx, kv_ctx, qkv, qkv)


def _rope_tables(t):
    pos = jnp.arange(t, dtype=jnp.int32)
    rows = (pos // GRID_W).astype(F32)
    cols = (pos % GRID_W).astype(F32)
    axis_dim = HEAD // 2
    inv_freq = ROPE_THETA ** (-jnp.arange(0, axis_dim, 2, dtype=F32) / axis_dim)
    ang = jnp.concatenate([rows[:, None] * inv_freq, cols[:, None] * inv_freq], axis=-1)
    cos, sin = jnp.cos(ang), jnp.sin(ang)
    return jnp.concatenate([cos, cos], axis=-1), jnp.concatenate([-sin, sin], axis=-1)


def _split_mods(mods, b):
    d = mods.shape[1] // N_MOD
    lat = [mods[:b, i * d:(i + 1) * d].reshape(b, 1, d) for i in range(N_MOD)]
    ctx = [mods[b:b + 1, i * d:(i + 1) * d].reshape(1, 1, d) for i in range(N_MOD)]
    return lat, ctx


def kernel(x, c, ctx, c_ctx, ada_w, ada_b, norm_mix_pre, norm_mix_post, norm_ffn_pre, norm_ffn_post, hgrn_w_in, hgrn_lb_logits, hgrn_o_norm, hgrn_w_out, attn_w_qkv, attn_q_norm, attn_k_norm, attn_w_out, ffn_w_in, ffn_conv_w, ffn_conv_b, ffn_w_out):
    b, t, d = x.shape
    tc = ctx.shape[1]
    heads = d // HEAD
    assert ada_w.shape[0] == 2 and b + 1 <= ADA_ROWS

    cond = jnp.zeros((ADA_ROWS, d), F32).at[:b].set(c).at[b].set(c_ctx)
    mods = _ada_call(cond, ada_w, ada_b)
    x_lat = x.reshape(b * t, d)
    x_ctx = ctx.reshape(b * tc, d)

    (sh_ml, sc_ml, gt_ml, sh_fl, sc_fl, gt_fl), (sh_mc, sc_mc, gt_mc, sh_fc, sc_fc, gt_fc) = _split_mods(mods[0], b)
    lower_bound = jnp.cumsum(jax.nn.softmax(hgrn_lb_logits.astype(F32), axis=1), axis=1)[:, 0]
    hgrn_w_in, hgrn_w_out, attn_w_out = hgrn_w_in.astype(BF16), hgrn_w_out.astype(BF16), attn_w_out.astype(BF16)
    ffn_w_in, ffn_w_out = ffn_w_in.astype(BF16), ffn_w_out.astype(BF16)
    p_lat = _nmm_call(x_lat, norm_mix_pre[0], sh_ml, sc_ml, hgrn_w_in, name="hgrn_in_lat")
    p_ctx = _nmm_call(x_ctx, norm_mix_pre[0], sh_mc, sc_mc, hgrn_w_in, name="hgrn_in_ctx")
    y_ctx, y_lat = _gla_call(p_ctx.reshape(b, tc, 5 * d), p_lat.reshape(b, t, 5 * d), lower_bound, hgrn_o_norm[0])
    x_lat = _proj_res_call(y_lat.reshape(b * t, d), hgrn_w_out, x_lat, norm_mix_post[0], gt_ml, name="hgrn_out_lat")
    x_ctx = _proj_res_call(y_ctx.reshape(b * tc, d), hgrn_w_out, x_ctx, norm_mix_post[0], gt_mc, name="hgrn_out_ctx")
    a = _ffn_in_call(x_lat, norm_ffn_pre[0], sh_fl, sc_fl, ffn_w_in, ffn_conv_w[0], ffn_conv_b[0],
                     layer=0, seq_len=t, name="ffn0_in_lat")
    x_lat = _proj_res_call(a, ffn_w_out, x_lat, norm_ffn_post[0], gt_fl, layer=0, name="ffn0_out_lat")
    a = _ffn_in_call(x_ctx, norm_ffn_pre[0], sh_fc, sc_fc, ffn_w_in, ffn_conv_w[0], ffn_conv_b[0],
                     layer=0, seq_len=tc, name="ffn0_in_ctx")
    x_ctx = _proj_res_call(a, ffn_w_out, x_ctx, norm_ffn_post[0], gt_fc, layer=0, name="ffn0_out_ctx")

    (sh_ml, sc_ml, gt_ml, sh_fl, sc_fl, gt_fl), (sh_mc, sc_mc, _, _, _, _) = _split_mods(mods[1], b)
    n_qkv = attn_w_qkv.shape[2]
    kvh = (n_qkv // HEAD - heads) // 2
    perm = jnp.concatenate([jnp.arange(0, HEAD, 2), jnp.arange(1, HEAD, 2)])
    head_perm = (jnp.arange(heads + kvh)[:, None] * HEAD + perm[None, :]).reshape(-1)
    col_perm = jnp.concatenate([head_perm, jnp.arange((heads + kvh) * HEAD, n_qkv)])
    w_qkv = attn_w_qkv[:, :, col_perm].astype(BF16)
    head_gain = jnp.concatenate([
        jnp.tile(attn_q_norm[0][perm] * (LOG2_E * HEAD ** -0.5), heads),
        jnp.tile(attn_k_norm[0][perm], kvh),
        jnp.ones((kvh * HEAD,), F32),
    ]).reshape(1, n_qkv)
    cos, sin = _rope_tables(t)
    qk_cols = (heads + kvh) * HEAD
    q_cols = heads * HEAD
    qkv = _nmm_call(x_lat, norm_mix_pre[1], sh_ml, sc_ml, w_qkv, name="attn_qkv_lat")
    qkv = _qk_rope_call(qkv, head_gain[:, :qk_cols], cos, sin, name="attn_qk_rope_lat")
    kv_ctx = _nmm_call(x_ctx, norm_mix_pre[1], sh_mc, sc_mc, w_qkv, col0=q_cols, name="attn_kv_ctx")
    kv_ctx = _qk_rope_call(kv_ctx, head_gain[:, q_cols:qk_cols], jnp.ones((tc, HEAD), F32),
                           jnp.zeros((tc, HEAD), F32), name="attn_k_norm_ctx")
    qkv = qkv.reshape(b, t, n_qkv)
    kv_ctx = kv_ctx.reshape(b, tc, 2 * kvh * HEAD)
    o = _flash_call(qkv, kv_ctx, heads=heads)
    x_lat = _proj_res_call(o.reshape(b * t, d), attn_w_out, x_lat, norm_mix_post[1], gt_ml, name="attn_out_lat")
    a = _ffn_in_call(x_lat, norm_ffn_pre[1], sh_fl, sc_fl, ffn_w_in, ffn_conv_w[1], ffn_conv_b[1],
                     layer=1, seq_len=t, name="ffn1_in_lat")
    x_lat = _proj_res_call(a, ffn_w_out, x_lat, norm_ffn_post[1], gt_fl, layer=1, name="ffn1_out_lat")
    return x_lat.reshape(b, t, d)
```

```python
import functools
import math

import jax
import jax.numpy as jnp
from jax import lax
from jax.experimental import pallas as pl
from jax.experimental.pallas import tpu as pltpu

F32 = jnp.float32
BF16 = jnp.bfloat16

EPS = 1e-6
GRID_W = 64
ROPE_THETA = 10000.0
LOG2_E = math.log2(math.e)
N_MOD = 6
HEAD = 128
ROW_GROUP = 16
GLA_CHUNK = 128
GLA_RUN = 16
GLA_SAFE_LB = math.exp(-80.0 / (GLA_CHUNK // 2))
BF16_SUBLANES = 16
ADA_ROWS = 8
V7X_VMEM_BYTES = 64 * 1024 * 1024
VMEM_LIMIT = V7X_VMEM_BYTES - 6 * 1024 * 1024


def _pick(n, target, align):
    if n <= target:
        return n
    for t in range(target - target % align, 0, -align):
        if n % t == 0:
            return t
    return n


def _params(sem):
    return pltpu.CompilerParams(dimension_semantics=sem, vmem_limit_bytes=VMEM_LIMIT)


def _sigmoid(x):
    return 0.5 * jnp.tanh(0.5 * x) + 0.5


def _silu(x):
    return x * _sigmoid(x)


def _dot(a, b):
    return jnp.dot(a, b, preferred_element_type=F32)


def _dot_nt(a, b):
    return lax.dot_general(a, b, (((1,), (1,)), ((), ())), preferred_element_type=F32)


def _dot_tn(a, b):
    return lax.dot_general(a, b, (((0,), (0,)), ((), ())), preferred_element_type=F32)


def _ada_kernel(c_ref, w_ref, b_ref, o_ref):
    sc = _silu(c_ref[...]).astype(BF16)
    o_ref[...] = _dot(sc, w_ref[...].astype(BF16)) + b_ref[...]


def _ada_call(cond, ada_w, ada_b):
    depth, d, n = ada_w.shape
    tn = _pick(n, 1024, HEAD)
    return pl.pallas_call(
        _ada_kernel,
        grid=(depth, n // tn),
        in_specs=[
            pl.BlockSpec((ADA_ROWS, d), lambda l, j: (0, 0)),
            pl.BlockSpec((None, d, tn), lambda l, j: (l, 0, j)),
            pl.BlockSpec((None, 1, tn), lambda l, j: (l, 0, j)),
        ],
        out_specs=pl.BlockSpec((None, ADA_ROWS, tn), lambda l, j: (l, 0, j)),
        out_shape=jax.ShapeDtypeStruct((depth, ADA_ROWS, n), F32),
        compiler_params=_params(("arbitrary", "arbitrary")),
        name="ada_ln",
    )(cond, ada_w, ada_b.reshape(depth, 1, n))


def _norm_mod(x, gain, shift, scale):
    ms = jnp.mean(x * x, axis=-1, keepdims=True)
    return (x * lax.rsqrt(ms + EPS) * gain) * (1.0 + scale) + shift


def _norm_mod_rows(x_ref, n_rows, h_ref, h_row0, gain, shift, scale):
    mult = gain * (1.0 + scale)

    def body(r, carry):
        start = pl.multiple_of(r * ROW_GROUP, ROW_GROUP)
        x = x_ref[pl.ds(start, ROW_GROUP), :]
        ms = jnp.mean(x * x, axis=-1, keepdims=True)
        dst = pl.ds(pl.multiple_of(h_row0 + start, ROW_GROUP), ROW_GROUP)
        h_ref[dst, :] = (x * lax.rsqrt(ms + EPS) * mult + shift).astype(BF16)
        return carry

    groups = n_rows // ROW_GROUP
    lax.fori_loop(0, groups, body, 0, unroll=math.gcd(groups, 4))


def _split_rows(tm):
    half = tm // 2
    return half if half % ROW_GROUP == 0 else tm


def _nmm_kernel(x_ref, gain_ref, shift_ref, scale_ref, w_ref, o_ref, h_ref):
    j = pl.program_id(1)
    tm = x_ref.shape[0]
    lead = _split_rows(tm)

    @pl.when(j == 0)
    def _():
        gain, shift, scale = gain_ref[...], shift_ref[...], scale_ref[...]
        _norm_mod_rows(x_ref, lead, h_ref, 0, gain, shift, scale)
        if lead < tm:
            h_ref[lead:, :] = _norm_mod(x_ref[lead:, :], gain, shift, scale).astype(BF16)
        o_ref[:lead, :] = _dot(h_ref[:lead, :], w_ref[...]).astype(o_ref.dtype)
        if lead < tm:
            o_ref[lead:, :] = _dot(h_ref[lead:, :], w_ref[...]).astype(o_ref.dtype)

    @pl.when(j > 0)
    def _():
        o_ref[...] = _dot(h_ref[...], w_ref[...]).astype(o_ref.dtype)


def _nmm_call(x, gain, shift, scale, w, *, layer=0, col0=0, name):
    m, d = x.shape
    n = w.shape[2] - col0
    bm = shift.shape[0]
    rows_per_mod = m // bm
    tm = _pick(rows_per_mod, 1024, BF16_SUBLANES)
    tn = _pick(math.gcd(n, col0) if col0 else n, 1024, HEAD)
    mod_blocks = rows_per_mod // tm
    col_block0 = col0 // tn
    return pl.pallas_call(
        _nmm_kernel,
        grid=(m // tm, n // tn),
        in_specs=[
            pl.BlockSpec((tm, d), lambda i, j: (i, 0)),
            pl.BlockSpec((1, d), lambda i, j: (0, 0)),
            pl.BlockSpec((None, 1, d), lambda i, j: (i // mod_blocks, 0, 0)),
            pl.BlockSpec((None, 1, d), lambda i, j: (i // mod_blocks, 0, 0)),
            pl.BlockSpec((None, d, tn), lambda i, j: (layer, 0, col_block0 + j)),
        ],
        out_specs=pl.BlockSpec((tm, tn), lambda i, j: (i, j)),
        out_shape=jax.ShapeDtypeStruct((m, n), BF16),
        scratch_shapes=[pltpu.VMEM((tm, d), BF16)],
        compiler_params=_params(("arbitrary", "arbitrary")),
        name=name,
    )(x, gain.reshape(1, d), shift, scale, w)


def _qk_rope_kernel(x_ref, hg_ref, cos_ref, sin_ref, o_ref):
    cos = cos_ref[...]
    sin = sin_ref[...]
    mean_lanes = jnp.full((HEAD, HEAD), 1.0 / HEAD, BF16)
    for hh in range(x_ref.shape[1] // HEAD):
        cols = slice(hh * HEAD, (hh + 1) * HEAD)
        a = x_ref[:, cols].astype(F32)
        ms = _dot((a * a).astype(BF16), mean_lanes)
        a = a * lax.rsqrt(ms + EPS) * hg_ref[:, cols]
        a = a * cos + pltpu.roll(a, HEAD // 2, axis=1) * sin
        o_ref[:, cols] = a.astype(o_ref.dtype)


def _qk_rope_call(p, head_gain, cos, sin, *, name):
    m, n = p.shape
    nn = head_gain.shape[1]
    tm = _pick(math.gcd(m, cos.shape[0]), 2048, BF16_SUBLANES)
    tn = _pick(math.gcd(n, nn), 1024, HEAD)
    table_blocks = cos.shape[0] // tm
    return pl.pallas_call(
        _qk_rope_kernel,
        grid=(m // tm, nn // tn),
        in_specs=[
            pl.BlockSpec((tm, tn), lambda i, j: (i, j)),
            pl.BlockSpec((1, tn), lambda i, j: (0, j)),
            pl.BlockSpec((tm, HEAD), lambda i, j: (i % table_blocks, 0)),
            pl.BlockSpec((tm, HEAD), lambda i, j: (i % table_blocks, 0)),
        ],
        out_specs=pl.BlockSpec((tm, tn), lambda i, j: (i, j)),
        out_shape=jax.ShapeDtypeStruct((m, n), p.dtype),
        input_output_aliases={0: 0},
        compiler_params=_params(("arbitrary", "arbitrary")),
        name=name,
    )(p, head_gain, cos, sin)


def _residual(x, y, gain, gate):
    ms = jnp.mean(y * y, axis=-1, keepdims=True)
    return x + gate * (y * lax.rsqrt(ms + EPS) * gain)


def _proj_res_kernel(a_ref, w_ref, x_ref, gain_ref, gate_ref, o_ref):
    y = _dot(a_ref[...], w_ref[...])
    o_ref[...] = _residual(x_ref[...], y, gain_ref[...], gate_ref[...])


def _proj_res_ktiled_kernel(a_ref, w_ref, x_ref, gain_ref, gate_ref, o_ref):
    k = pl.program_id(1)
    last = pl.num_programs(1) - 1
    tm = o_ref.shape[0]
    lead = _split_rows(tm)

    @pl.when(k == 0)
    def _():
        o_ref[...] = _dot(a_ref[...], w_ref[...])

    @pl.when(jnp.logical_and(k > 0, k < last))
    def _():
        o_ref[...] += _dot(a_ref[...], w_ref[...])

    @pl.when(k == last)
    def _():
        gain, gate = gain_ref[...], gate_ref[...]
        parts = [slice(0, lead), slice(lead, tm)] if lead < tm else [slice(0, tm)]
        ys = [o_ref[r, :] + _dot(a_ref[r, :], w_ref[...]) for r in parts]
        for r, y in zip(parts, ys):
            o_ref[r, :] = _residual(x_ref[r, :], y, gain, gate)


def _proj_res_call(a, w, x, gain, gate, *, layer=0, name):
    m, k = a.shape
    d = w.shape[2]
    bm = gate.shape[0]
    rows_per_mod = m // bm
    tk = k if k <= d else _pick(k, 512, 2 * HEAD)
    kb = k // tk
    tm = _pick(rows_per_mod, 512 if kb == 1 else 1024, BF16_SUBLANES)
    mod_blocks = rows_per_mod // tm
    return pl.pallas_call(
        _proj_res_kernel if kb == 1 else _proj_res_ktiled_kernel,
        grid=(m // tm, kb),
        in_specs=[
            pl.BlockSpec((tm, tk), lambda i, kk: (i, kk)),
            pl.BlockSpec((None, tk, d), lambda i, kk: (layer, kk, 0)),
            pl.BlockSpec((tm, d), lambda i, kk: (i, 0)),
            pl.BlockSpec((1, d), lambda i, kk: (0, 0)),
            pl.BlockSpec((None, 1, d), lambda i, kk: (i // mod_blocks, 0, 0)),
        ],
        out_specs=pl.BlockSpec((tm, d), lambda i, kk: (i, 0)),
        out_shape=jax.ShapeDtypeStruct((m, d), F32),
        compiler_params=_params(("arbitrary", "arbitrary")),
        name=name,
    )(a, w, x, gain.reshape(1, d), gate)


def _ffn_in_kernel(x_ref, xprev_ref, xnext_ref, gain_ref, shift_ref, scale_ref, wg_ref, wu_ref, cw_ref, cb_ref,
                   o_ref, h_ref, *, blocks_per_seq):
    i = pl.program_id(0)
    j = pl.program_id(1)
    tm = x_ref.shape[0]
    halo = BF16_SUBLANES
    lead = _split_rows(tm)

    def gated(gate, up):
        rows = gate.shape[0]
        g_prev = pltpu.roll(gate, 1, axis=0)[halo:halo + tm, :]
        g_next = pltpu.roll(gate, rows - 1, axis=0)[halo:halo + tm, :]
        conv = (g_prev * cw_ref[0:1, :] + gate[halo:halo + tm, :] * cw_ref[1:2, :] + g_next * cw_ref[2:3, :]
                + cb_ref[...])
        o_ref[...] = (_silu(conv) * up).astype(o_ref.dtype)

    @pl.when(j == 0)
    def _():
        gain, shift, scale = gain_ref[...], shift_ref[...], scale_ref[...]
        seq_block = i % blocks_per_seq
        h_prev = jnp.where(seq_block > 0, _norm_mod(xprev_ref[...], gain, shift, scale), 0.0)
        h_next = jnp.where(seq_block < blocks_per_seq - 1, _norm_mod(xnext_ref[...], gain, shift, scale), 0.0)
        zeros = jnp.zeros((halo - h_prev.shape[0], h_prev.shape[1]), F32)
        h_ref[0:halo, :] = jnp.concatenate([zeros, h_prev], axis=0).astype(BF16)
        h_ref[halo + tm:, :] = jnp.concatenate([h_next, zeros], axis=0).astype(BF16)
        _norm_mod_rows(x_ref, lead, h_ref, halo, gain, shift, scale)
        if lead == tm:
            gated(_dot(h_ref[...], wg_ref[...]), _dot(h_ref[halo:halo + tm, :], wu_ref[...]))
        else:
            cut = halo + lead
            h_ref[cut:halo + tm, :] = _norm_mod(x_ref[lead:, :], gain, shift, scale).astype(BF16)
            gate = jnp.concatenate([_dot(h_ref[:cut, :], wg_ref[...]), _dot(h_ref[cut:, :], wg_ref[...])], axis=0)
            up = jnp.concatenate([_dot(h_ref[halo:cut, :], wu_ref[...]),
                                  _dot(h_ref[cut:halo + tm, :], wu_ref[...])], axis=0)
            gated(gate, up)

    @pl.when(j > 0)
    def _():
        gated(_dot(h_ref[...], wg_ref[...]), _dot(h_ref[halo:halo + tm, :], wu_ref[...]))


def _ffn_in_call(x, gain, shift, scale, w, conv_w, conv_b, *, layer, seq_len, name):
    m, d = x.shape
    f = w.shape[2] // 2
    bm = shift.shape[0]
    rows_per_mod = m // bm
    tm = _pick(seq_len, 1024, BF16_SUBLANES)
    tn = _pick(f, 512, HEAD)
    assert rows_per_mod % tm == 0
    mod_blocks = rows_per_mod // tm
    nb = f // tn
    xh = 8
    tiles_per_tm = tm // xh
    last_tile = m // xh - 1
    body = functools.partial(_ffn_in_kernel, blocks_per_seq=seq_len // tm)
    return pl.pallas_call(
        body,
        grid=(m // tm, nb),
        in_specs=[
            pl.BlockSpec((tm, d), lambda i, j: (i, 0)),
            pl.BlockSpec((xh, d), lambda i, j: (jnp.maximum(i * tiles_per_tm - 1, 0), 0)),
            pl.BlockSpec((xh, d), lambda i, j: (jnp.minimum((i + 1) * tiles_per_tm, last_tile), 0)),
            pl.BlockSpec((1, d), lambda i, j: (0, 0)),
            pl.BlockSpec((None, 1, d), lambda i, j: (i // mod_blocks, 0, 0)),
            pl.BlockSpec((None, 1, d), lambda i, j: (i // mod_blocks, 0, 0)),
            pl.BlockSpec((None, d, tn), lambda i, j: (layer, 0, j)),
            pl.BlockSpec((None, d, tn), lambda i, j: (layer, 0, nb + j)),
            pl.BlockSpec((3, tn), lambda i, j: (0, j)),
            pl.BlockSpec((1, tn), lambda i, j: (0, j)),
        ],
        out_specs=pl.BlockSpec((tm, tn), lambda i, j: (i, j)),
        out_shape=jax.ShapeDtypeStruct((m, f), BF16),
        scratch_shapes=[pltpu.VMEM((tm + 2 * BF16_SUBLANES, d), BF16)],
        compiler_params=_params(("arbitrary", "arbitrary")),
        name=name,
    )(x, x, x, gain.reshape(1, d), shift, scale, w, w, conv_w, conv_b.reshape(1, f))


def _gla_gates(f_raw, lb, within):
    half = 0.5 * (1.0 - lb)
    ht = half * jnp.tanh(0.5 * f_raw.astype(F32))
    lf = jnp.log((1.0 - half) + ht)
    ones = jnp.where(within, 1.0, 0.0).astype(BF16)
    lf_hi = lf.astype(BF16)
    lf_lo = (lf - lf_hi.astype(F32)).astype(BF16)
    return half - ht, _dot(ones, lf_hi) + _dot(ones, lf_lo)


def _gla_pairs(qs, kk, b, forward, within):
    c = qs.shape[0]
    a_row = c // 2 - 1 if forward else c // 2
    e_row = c - 1 if forward else 0
    anchor = b[a_row:a_row + 1, :]
    b_end = b[e_row:e_row + 1, :]
    qa = qs * jnp.exp(b - anchor)
    ka = kk * jnp.exp(anchor - b)
    scores = jnp.where(within, _dot_nt(qa.astype(BF16), ka.astype(BF16)), 0.0).astype(BF16)
    return (scores, (qa * jnp.exp(anchor)).astype(BF16), (ka * jnp.exp(b_end - anchor)).astype(BF16),
            jnp.exp(b_end))


def _gla_pairs_exact(qs, kk, b, forward, within, tmp):
    tb_ref, tq_ref, ts_ref = tmp
    c = qs.shape[0]
    b_end = b[c - 1:c, :] if forward else b[0:1, :]
    tb_ref[...] = b
    tq_ref[...] = qs
    ones = jnp.ones((8, qs.shape[1]), BF16)

    def row(t, carry):
        rel = jnp.minimum(tb_ref[pl.ds(t, 1), :] - b, 0.0)
        e = (tq_ref[pl.ds(t, 1), :] * kk) * jnp.exp(rel)
        ts_ref[pl.ds(t, 1), :] = _dot_nt(ones, e.astype(BF16))[0:1, :]
        return carry

    lax.fori_loop(0, c, row, 0)
    scores = jnp.where(within, ts_ref[...], 0.0).astype(BF16)
    return scores, (qs * jnp.exp(b)).astype(BF16), (kk * jnp.exp(b_end - b)).astype(BF16), jnp.exp(b_end)


def _gla_readout(o, g_raw, o_gain):
    ms = jnp.mean(o * o, axis=-1, keepdims=True)
    y = o * lax.rsqrt(ms + EPS) * o_gain
    return (y * _silu(g_raw.astype(F32))).astype(BF16)


def _gla_kernel(qc, ffc, fbc, vc, gc, ql, ffl, fbl, vl, gl, lb_ref, og_ref, yc_ref, yl_ref,
                o_ref, qb_ref, ke_ref, dec_ref, st_ref, tb_ref, tq_ref, ts_ref):
    c = GLA_CHUNK
    tc = qc.shape[0]
    n_ctx = tc // c
    n_lat = ql.shape[0] // c
    lb = (lb_ref[0], lb_ref[1])
    row = lax.broadcasted_iota(jnp.int32, (c, c), 0)
    col = lax.broadcasted_iota(jnp.int32, (c, c), 1)
    within = (col <= row, col >= row)

    def rows(start):
        return pl.ds(start if isinstance(start, int) else pl.multiple_of(start, c), c)

    def local(refs, src0, dst0, chunk0, n, exact):
        q, ff, fb, v = refs
        src = [rows(src0 + j * c) for j in range(n)]
        dst = [rows(dst0 + j * c) for j in range(n)]
        qs = [_silu(q[src[j], :].astype(F32)) * (HEAD ** -0.5) for j in range(n)]
        gates = [[_gla_gates(f[src[j], :], lb[d], within[d]) for d, f in enumerate((ff, fb))] for j in range(n)]
        scores = []
        for j in range(n):
            per_dir = []
            for d in range(2):
                kk, b = gates[j][d]
                if exact:
                    sc, qb, ke, dec = _gla_pairs_exact(qs[j], kk, b, d == 0, within[d], (tb_ref, tq_ref, ts_ref))
                else:
                    sc, qb, ke, dec = _gla_pairs(qs[j], kk, b, d == 0, within[d])
                qb_ref[d, dst[j], :] = qb
                ke_ref[d, dst[j], :] = ke
                dec_ref[d, chunk0 + j] = dec
                per_dir.append(sc)
            scores.append(per_dir)
        for j in range(n):
            vj = v[src[j], :]
            o_ref[dst[j], :] = _dot(scores[j][0], vj) + _dot(scores[j][1], vj)

    def phase1(exact):
        run = 1 if exact else math.gcd(n_lat, GLA_RUN)
        run_ctx = math.gcd(n_ctx, run)
        for i in range(0, n_ctx, run_ctx):
            local((qc, ffc, fbc, vc), i * c, i * c, i, run_ctx, exact)

        def body(i, carry):
            src0 = pl.multiple_of(i * (run * c), c)
            local((ql, ffl, fbl, vl), src0, tc + src0, n_ctx + i * run, run, exact)
            return carry

        lax.fori_loop(0, n_lat // run, body, 0)

    safe = jnp.min(jnp.minimum(lb[0], lb[1])) >= GLA_SAFE_LB

    @pl.when(safe)
    def _():
        phase1(False)

    @pl.when(jnp.logical_not(safe))
    def _():
        phase1(True)

    st_ref[...] = jnp.zeros_like(st_ref)

    o_gain = og_ref[...]

    def steps(v, first, n, n_seg, row0, chunk0, finish=None):
        order = [[first + s for s in range(n)], [n_seg - 1 - first - s for s in range(n)]]
        src = [[rows(j * c) for j in order[d]] for d in range(2)]
        dst = [[rows(row0 + j * c) for j in order[d]] for d in range(2)]
        kv = [[_dot_tn(v[src[d][s], :], ke_ref[d, dst[d][s], :]) for s in range(n)] for d in range(2)]
        for d in range(2):
            st = st_ref[d]
            for s in range(n):
                o = o_ref[dst[d][s], :] + _dot_nt(qb_ref[d, dst[d][s], :], st.astype(BF16))
                if finish is None:
                    o_ref[dst[d][s], :] = o
                else:
                    g, y = finish
                    y[src[d][s], :] = _gla_readout(o, g[src[d][s], :], o_gain)
                st = st * dec_ref[d, chunk0 + order[d][s]] + kv[d][s]
            st_ref[d] = st

    steps(vc, 0, n_ctx, n_ctx, 0, 0)
    for i in range(n_ctx):
        r = slice(i * c, (i + 1) * c)
        yc_ref[r, :] = _gla_readout(o_ref[r, :], gc[r, :], o_gain)

    run = math.gcd(n_lat, GLA_RUN)
    n_runs = n_lat // run
    meet = (n_runs + 1) // 2

    def steps_body(i, carry):
        steps(vl, i * run, run, n_lat, tc, n_ctx)
        return carry

    def last_steps_body(i, carry):
        steps(vl, i * run, run, n_lat, tc, n_ctx, finish=(gl, yl_ref))
        return carry

    lax.fori_loop(0, meet, steps_body, 0)
    if n_runs % 2:
        for j in range((n_runs // 2) * run, (n_runs // 2 + 1) * run):
            yl_ref[rows(j * c), :] = _gla_readout(o_ref[rows(tc + j * c), :], gl[rows(j * c), :], o_gain)
    lax.fori_loop(meet, n_runs, last_steps_body, 0)


def _gla_call(p_ctx, p_lat, lower_bound, o_gain):
    b, tc, d5 = p_ctx.shape
    t = p_lat.shape[1]
    d = d5 // 5
    heads = d // HEAD
    assert tc % GLA_CHUNK == 0 and t % GLA_CHUNK == 0

    def col(tt, part):
        return pl.BlockSpec((None, tt, HEAD), lambda bi, h: (bi, 0, part * heads + h))

    out_spec = lambda tt: pl.BlockSpec((None, tt, HEAD), lambda bi, h: (bi, 0, h))
    return pl.pallas_call(
        _gla_kernel,
        grid=(b, heads),
        in_specs=[col(tc, p) for p in range(5)] + [col(t, p) for p in range(5)] + [
            pl.BlockSpec((2, None, 1, HEAD), lambda bi, h: (0, h, 0, 0)),
            pl.BlockSpec((1, HEAD), lambda bi, h: (0, 0)),
        ],
        out_specs=[out_spec(tc), out_spec(t)],
        out_shape=[jax.ShapeDtypeStruct((b, tc, d), BF16), jax.ShapeDtypeStruct((b, t, d), BF16)],
        scratch_shapes=[
            pltpu.VMEM((tc + t, HEAD), F32),
            pltpu.VMEM((2, tc + t, HEAD), BF16),
            pltpu.VMEM((2, tc + t, HEAD), BF16),
            pltpu.VMEM((2, (tc + t) // GLA_CHUNK, 1, HEAD), F32),
            pltpu.VMEM((2, HEAD, HEAD), F32),
            pltpu.VMEM((GLA_CHUNK, HEAD), F32),
            pltpu.VMEM((GLA_CHUNK, HEAD), F32),
            pltpu.VMEM((GLA_CHUNK, GLA_CHUNK), F32),
        ],
        compiler_params=_params(("arbitrary", "arbitrary")),
        name="hgrn_scan",
    )(*([p_ctx] * 5), *([p_lat] * 5), lower_bound.reshape(2, heads, 1, HEAD), o_gain.reshape(1, HEAD))


def _flash_kernel(q_ref, kc_ref, vc_ref, kl_ref, vl_ref, o_ref, *, tk, unroll):
    tq = q_ref.shape[0]
    group = q_ref.shape[1] // HEAD
    qs = [q_ref[:, g * HEAD:(g + 1) * HEAD] for g in range(group)]

    def attend(carry, k, v):
        v_ext = jnp.concatenate([v, jnp.ones_like(v)], axis=1)
        out = []
        for g in range(group):
            m, acc = carry[g]
            s = _dot_nt(qs[g], k)
            m_new = jnp.maximum(m, jnp.max(s, axis=-1, keepdims=True))
            p = jnp.exp2(s - m_new).astype(BF16)
            out.append((m_new, jnp.exp2(m - m_new) * acc + _dot(p, v_ext)))
        return tuple(out)

    def body(j, carry):
        r = pl.ds(pl.multiple_of(j * tk, tk), tk)
        return attend(carry, kl_ref[r, :], vl_ref[r, :])

    init = tuple((jnp.full((tq, 1), -jnp.inf, F32), jnp.zeros((tq, 2 * HEAD), F32)) for _ in range(group))
    carry = attend(init, kc_ref[...], vc_ref[...])
    final = lax.fori_loop(0, kl_ref.shape[0] // tk, body, carry, unroll=unroll)
    for g in range(group):
        acc = final[g][1]
        o_ref[:, g * HEAD:(g + 1) * HEAD] = (acc[:, :HEAD] / acc[:, HEAD:]).astype(o_ref.dtype)


def _flash_call(qkv, kv_ctx, *, heads):
    b, t, n = qkv.shape
    tc = kv_ctx.shape[1]
    kvh = (n // HEAD - heads) // 2
    group = heads // kvh
    tq = _pick(t, 512, BF16_SUBLANES)
    tk = _pick(t, 1024, 2 * HEAD)
    body = functools.partial(_flash_kernel, tk=tk, unroll=math.gcd(t // tk, 8))
    return pl.pallas_call(
        body,
        grid=(b, kvh, t // tq),
        in_specs=[
            pl.BlockSpec((None, tq, group * HEAD), lambda bi, h, i: (bi, i, h)),
            pl.BlockSpec((None, tc, HEAD), lambda bi, h, i: (bi, 0, h)),
            pl.BlockSpec((None, tc, HEAD), lambda bi, h, i: (bi, 0, kvh + h)),
            pl.BlockSpec((None, t, HEAD), lambda bi, h, i: (bi, 0, heads + h)),
            pl.BlockSpec((None, t, HEAD), lambda bi, h, i: (bi, 0, heads + kvh + h)),
        ],
        out_specs=pl.BlockSpec((None, tq, group * HEAD), lambda bi, h, i: (bi, i, h)),
        out_shape=jax.ShapeDtypeStruct((b, t, heads * HEAD), BF16),
        compiler_params=_params(("arbitrary", "arbitrary", "arbitrary")),
        name="gqa_flash",
    )(qkv, kv_ctx, kv_ctx, qkv, qkv)


def _rope_tables(t):
    pos = jnp.arange(t, dtype=jnp.int32)
    rows = (pos // GRID_W).astype(F32)
    cols = (pos % GRID_W).astype(F32)
    axis_dim = HEAD // 2
    inv_freq = ROPE_THETA ** (-jnp.arange(0, axis_dim, 2, dtype=F32) / axis_dim)
    ang = jnp.concatenate([rows[:, None] * inv_freq, cols[:, None] * inv_freq], axis=-1)
    cos, sin = jnp.cos(ang), jnp.sin(ang)
    return jnp.concatenate([cos, cos], axis=-1), jnp.concatenate([-sin, sin], axis=-1)


def _split_mods(mods, b):
    d = mods.shape[1] // N_MOD
    lat = [mods[:b, i * d:(i + 1) * d].reshape(b, 1, d) for i in range(N_MOD)]
    ctx = [mods[b:b + 1, i * d:(i + 1) * d].reshape(1, 1, d) for i in range(N_MOD)]
    return lat, ctx


def kernel(x, c, ctx, c_ctx, ada_w, ada_b, norm_mix_pre, norm_mix_post, norm_ffn_pre, norm_ffn_post, hgrn_w_in, hgrn_lb_logits, hgrn_o_norm, hgrn_w_out, attn_w_qkv, attn_q_norm, attn_k_norm, attn_w_out, ffn_w_in, ffn_conv_w, ffn_conv_b, ffn_w_out):
    b, t, d = x.shape
    tc = ctx.shape[1]
    heads = d // HEAD
    assert ada_w.shape[0] == 2 and b + 1 <= ADA_ROWS

    cond = jnp.zeros((ADA_ROWS, d), F32).at[:b].set(c).at[b].set(c_ctx)
    mods = _ada_call(cond, ada_w, ada_b)
    x_lat = x.reshape(b * t, d)
    x_ctx = ctx.reshape(b * tc, d)

    (sh_ml, sc_ml, gt_ml, sh_fl, sc_fl, gt_fl), (sh_mc, sc_mc, gt_mc, sh_fc, sc_fc, gt_fc) = _split_mods(mods[0], b)
    lower_bound = jnp.cumsum(jax.nn.softmax(hgrn_lb_logits.astype(F32), axis=1), axis=1)[:, 0]
    hgrn_w_in, hgrn_w_out, attn_w_out = hgrn_w_in.astype(BF16), hgrn_w_out.astype(BF16), attn_w_out.astype(BF16)
    ffn_w_in, ffn_w_out = ffn_w_in.astype(BF16), ffn_w_out.astype(BF16)
    p_lat = _nmm_call(x_lat, norm_mix_pre[0], sh_ml, sc_ml, hgrn_w_in, name="hgrn_in_lat")
    p_ctx = _nmm_call(x_ctx, norm_mix_pre[0], sh_mc, sc_mc, hgrn_w_in, name="hgrn_in_ctx")
    y_ctx, y_lat = _gla_call(p_ctx.reshape(b, tc, 5 * d), p_lat.reshape(b, t, 5 * d), lower_bound, hgrn_o_norm[0])
    x_lat = _proj_res_call(y_lat.reshape(b * t, d), hgrn_w_out, x_lat, norm_mix_post[0], gt_ml, name="hgrn_out_lat")
    x_ctx = _proj_res_call(y_ctx.reshape(b * tc, d), hgrn_w_out, x_ctx, norm_mix_post[0], gt_mc, name="hgrn_out_ctx")
    a = _ffn_in_call(x_lat, norm_ffn_pre[0], sh_fl, sc_fl, ffn_w_in, ffn_conv_w[0], ffn_conv_b[0],
                     layer=0, seq_len=t, name="ffn0_in_lat")
    x_lat = _proj_res_call(a, ffn_w_out, x_lat, norm_ffn_post[0], gt_fl, layer=0, name="ffn0_out_lat")
    a = _ffn_in_call(x_ctx, norm_ffn_pre[0], sh_fc, sc_fc, ffn_w_in, ffn_conv_w[0], ffn_conv_b[0],
                     layer=0, seq_len=tc, name="ffn0_in_ctx")
    x_ctx = _proj_res_call(a, ffn_w_out, x_ctx, norm_ffn_post[0], gt_fc, layer=0, name="ffn0_out_ctx")

    (sh_ml, sc_ml, gt_ml, sh_fl, sc_fl, gt_fl), (sh_mc, sc_mc, _, _, _, _) = _split_mods(mods[1], b)
    n_qkv = attn_w_qkv.shape[2]
    kvh = (n_qkv // HEAD - heads) // 2
    perm = jnp.concatenate([jnp.arange(0, HEAD, 2), jnp.arange(1, HEAD, 2)])
    head_perm = (jnp.arange(heads + kvh)[:, None] * HEAD + perm[None, :]).reshape(-1)
    col_perm = jnp.concatenate([head_perm, jnp.arange((heads + kvh) * HEAD, n_qkv)])
    w_qkv = attn_w_qkv[:, :, col_perm].astype(BF16)
    head_gain = jnp.concatenate([
        jnp.tile(attn_q_norm[0][perm] * (LOG2_E * HEAD ** -0.5), heads),
        jnp.tile(attn_k_norm[0][perm], kvh),
        jnp.ones((kvh * HEAD,), F32),
    ]).reshape(1, n_qkv)
    cos, sin = _rope_tables(t)
    qk_cols = (heads + kvh) * HEAD
    q_cols = heads * HEAD
    qkv = _nmm_call(x_lat, norm_mix_pre[1], sh_ml, sc_ml, w_qkv, name="attn_qkv_lat")
    qkv = _qk_rope_call(qkv, head_gain[:, :qk_cols], cos, sin, name="attn_qk_rope_lat")
    kv_ctx = _nmm_call(x_ctx, norm_mix_pre[1], sh_mc, sc_mc, w_qkv, col0=q_cols, name="attn_kv_ctx")
    kv_ctx = _qk_rope_call(kv_ctx, head_gain[:, q_cols:qk_cols], jnp.ones((tc, HEAD), F32),
                           jnp.zeros((tc, HEAD), F32), name="attn_k_norm_ctx")
    qkv = qkv.reshape(b, t, n_qkv)
    kv_ctx = kv_ctx.reshape(b, tc, 2 * kvh * HEAD)
    o = _flash_call(qkv, kv_ctx, heads=heads)
    x_lat = _proj_res_call(o.reshape(b * t, d), attn_w_out, x_lat, norm_mix_post[1], gt_ml, name="attn_out_lat")
    a = _ffn_in_call(x_lat, norm_ffn_pre[1], sh_fl, sc_fl, ffn_w_in, ffn_conv_w[1], ffn_conv_b[1],
                     layer=1, seq_len=t, name="ffn1_in_lat")
    x_lat = _proj_res_call(a, ffn_w_out, x_lat, norm_ffn_post[1], gt_fl, layer=1, name="ffn1_out_lat")
    return x_lat.reshape(b, t, d)
```

```python
import functools
import math

import jax
import jax.numpy as jnp
from jax import lax
from jax.experimental import pallas as pl
from jax.experimental.pallas import tpu as pltpu

F32 = jnp.float32
BF16 = jnp.bfloat16

EPS = 1e-6
GRID_W = 64
ROPE_THETA = 10000.0
LOG2_E = math.log2(math.e)
N_MOD = 6
HEAD = 128
ROW_GROUP = 16
GLA_CHUNK = 128
GLA_RUN = 16
GLA_SAFE_LB = math.exp(-80.0 / (GLA_CHUNK // 2))
BF16_SUBLANES = 16
ADA_ROWS = 8
V7X_VMEM_BYTES = 64 * 1024 * 1024
VMEM_LIMIT = V7X_VMEM_BYTES - 6 * 1024 * 1024


def _pick(n, target, align):
    if n <= target:
        return n
    for t in range(target - target % align, 0, -align):
        if n % t == 0:
            return t
    return n


def _params(sem):
    return pltpu.CompilerParams(dimension_semantics=sem, vmem_limit_bytes=VMEM_LIMIT)


def _sigmoid(x):
    return 0.5 * jnp.tanh(0.5 * x) + 0.5


def _silu(x):
    return x * _sigmoid(x)


def _dot(a, b):
    return jnp.dot(a, b, preferred_element_type=F32)


def _dot_nt(a, b):
    return lax.dot_general(a, b, (((1,), (1,)), ((), ())), preferred_element_type=F32)


def _dot_tn(a, b):
    return lax.dot_general(a, b, (((0,), (0,)), ((), ())), preferred_element_type=F32)


def _ada_kernel(c_ref, w_ref, b_ref, o_ref):
    sc = _silu(c_ref[...]).astype(BF16)
    o_ref[...] = _dot(sc, w_ref[...].astype(BF16)) + b_ref[...]


def _ada_call(cond, ada_w, ada_b):
    depth, d, n = ada_w.shape
    tn = _pick(n, 1024, HEAD)
    return pl.pallas_call(
        _ada_kernel,
        grid=(depth, n // tn),
        in_specs=[
            pl.BlockSpec((ADA_ROWS, d), lambda l, j: (0, 0)),
            pl.BlockSpec((None, d, tn), lambda l, j: (l, 0, j)),
            pl.BlockSpec((None, 1, tn), lambda l, j: (l, 0, j)),
        ],
        out_specs=pl.BlockSpec((None, ADA_ROWS, tn), lambda l, j: (l, 0, j)),
        out_shape=jax.ShapeDtypeStruct((depth, ADA_ROWS, n), F32),
        compiler_params=_params(("arbitrary", "arbitrary")),
        name="ada_ln",
    )(cond, ada_w, ada_b.reshape(depth, 1, n))


def _norm_mod(x, gain, shift, scale):
    ms = jnp.mean(x * x, axis=-1, keepdims=True)
    return (x * lax.rsqrt(ms + EPS) * gain) * (1.0 + scale) + shift


def _norm_mod_rows(x_ref, n_rows, h_ref, h_row0, gain, shift, scale):
    mult = gain * (1.0 + scale)

    def body(r, carry):
        start = pl.multiple_of(r * ROW_GROUP, ROW_GROUP)
        x = x_ref[pl.ds(start, ROW_GROUP), :]
        ms = jnp.mean(x * x, axis=-1, keepdims=True)
        dst = pl.ds(pl.multiple_of(h_row0 + start, ROW_GROUP), ROW_GROUP)
        h_ref[dst, :] = (x * lax.rsqrt(ms + EPS) * mult + shift).astype(BF16)
        return carry

    groups = n_rows // ROW_GROUP
    lax.fori_loop(0, groups, body, 0, unroll=math.gcd(groups, 4))


def _split_rows(tm):
    half = tm // 2
    return half if half % ROW_GROUP == 0 else tm


def _nmm_kernel(x_ref, gain_ref, shift_ref, scale_ref, w_ref, o_ref, h_ref):
    j = pl.program_id(1)
    tm = x_ref.shape[0]
    lead = _split_rows(tm)

    @pl.when(j == 0)
    def _():
        gain, shift, scale = gain_ref[...], shift_ref[...], scale_ref[...]
        _norm_mod_rows(x_ref, lead, h_ref, 0, gain, shift, scale)
        if lead < tm:
            h_ref[lead:, :] = _norm_mod(x_ref[lead:, :], gain, shift, scale).astype(BF16)
        o_ref[:lead, :] = _dot(h_ref[:lead, :], w_ref[...]).astype(o_ref.dtype)
        if lead < tm:
            o_ref[lead:, :] = _dot(h_ref[lead:, :], w_ref[...]).astype(o_ref.dtype)

    @pl.when(j > 0)
    def _():
        o_ref[...] = _dot(h_ref[...], w_ref[...]).astype(o_ref.dtype)


def _nmm_call(x, gain, shift, scale, w, *, layer=0, col0=0, name):
    m, d = x.shape
    n = w.shape[2] - col0
    bm = shift.shape[0]
    rows_per_mod = m // bm
    tm = _pick(rows_per_mod, 1024, BF16_SUBLANES)
    tn = _pick(math.gcd(n, col0) if col0 else n, 1024, HEAD)
    mod_blocks = rows_per_mod // tm
    col_block0 = col0 // tn
    return pl.pallas_call(
        _nmm_kernel,
        grid=(m // tm, n // tn),
        in_specs=[
            pl.BlockSpec((tm, d), lambda i, j: (i, 0)),
            pl.BlockSpec((1, d), lambda i, j: (0, 0)),
            pl.BlockSpec((None, 1, d), lambda i, j: (i // mod_blocks, 0, 0)),
            pl.BlockSpec((None, 1, d), lambda i, j: (i // mod_blocks, 0, 0)),
            pl.BlockSpec((None, d, tn), lambda i, j: (layer, 0, col_block0 + j)),
        ],
        out_specs=pl.BlockSpec((tm, tn), lambda i, j: (i, j)),
        out_shape=jax.ShapeDtypeStruct((m, n), BF16),
        scratch_shapes=[pltpu.VMEM((tm, d), BF16)],
        compiler_params=_params(("arbitrary", "arbitrary")),
        name=name,
    )(x, gain.reshape(1, d), shift, scale, w)


def _qk_rope_kernel(x_ref, hg_ref, cos_ref, sin_ref, o_ref):
    cos = cos_ref[...]
    sin = sin_ref[...]
    mean_lanes = jnp.full((HEAD, HEAD), 1.0 / HEAD, BF16)
    for hh in range(x_ref.shape[1] // HEAD):
        cols = slice(hh * HEAD, (hh + 1) * HEAD)
        a = x_ref[:, cols].astype(F32)
        ms = _dot((a * a).astype(BF16), mean_lanes)
        a = a * lax.rsqrt(ms + EPS) * hg_ref[:, cols]
        a = a * cos + pltpu.roll(a, HEAD // 2, axis=1) * sin
        o_ref[:, cols] = a.astype(o_ref.dtype)


def _qk_rope_call(p, head_gain, cos, sin, *, name):
    m, n = p.shape
    nn = head_gain.shape[1]
    tm = _pick(math.gcd(m, cos.shape[0]), 2048, BF16_SUBLANES)
    tn = _pick(math.gcd(n, nn), 1024, HEAD)
    table_blocks = cos.shape[0] // tm
    return pl.pallas_call(
        _qk_rope_kernel,
        grid=(m // tm, nn // tn),
        in_specs=[
            pl.BlockSpec((tm, tn), lambda i, j: (i, j)),
            pl.BlockSpec((1, tn), lambda i, j: (0, j)),
            pl.BlockSpec((tm, HEAD), lambda i, j: (i % table_blocks, 0)),
            pl.BlockSpec((tm, HEAD), lambda i, j: (i % table_blocks, 0)),
        ],
        out_specs=pl.BlockSpec((tm, tn), lambda i, j: (i, j)),
        out_shape=jax.ShapeDtypeStruct((m, n), p.dtype),
        input_output_aliases={0: 0},
        compiler_params=_params(("arbitrary", "arbitrary")),
        name=name,
    )(p, head_gain, cos, sin)


def _residual(x, y, gain, gate):
    ms = jnp.mean(y * y, axis=-1, keepdims=True)
    return x + gate * (y * lax.rsqrt(ms + EPS) * gain)


def _proj_res_kernel(a_ref, w_ref, x_ref, gain_ref, gate_ref, o_ref):
    y = _dot(a_ref[...], w_ref[...])
    o_ref[...] = _residual(x_ref[...], y, gain_ref[...], gate_ref[...])


def _proj_res_ktiled_kernel(a_ref, w_ref, x_ref, gain_ref, gate_ref, o_ref):
    k = pl.program_id(1)
    last = pl.num_programs(1) - 1
    tm = o_ref.shape[0]
    lead = _split_rows(tm)

    @pl.when(k == 0)
    def _():
        o_ref[...] = _dot(a_ref[...], w_ref[...])

    @pl.when(jnp.logical_and(k > 0, k < last))
    def _():
        o_ref[...] += _dot(a_ref[...], w_ref[...])

    @pl.when(k == last)
    def _():
        gain, gate = gain_ref[...], gate_ref[...]
        parts = [slice(0, lead), slice(lead, tm)] if lead < tm else [slice(0, tm)]
        ys = [o_ref[r, :] + _dot(a_ref[r, :], w_ref[...]) for r in parts]
        for r, y in zip(parts, ys):
            o_ref[r, :] = _residual(x_ref[r, :], y, gain, gate)


def _proj_res_call(a, w, x, gain, gate, *, layer=0, name):
    m, k = a.shape
    d = w.shape[2]
    bm = gate.shape[0]
    rows_per_mod = m // bm
    tk = k if k <= d else _pick(k, 512, 2 * HEAD)
    kb = k // tk
    tm = _pick(rows_per_mod, 512 if kb == 1 else 1024, BF16_SUBLANES)
    mod_blocks = rows_per_mod // tm
    return pl.pallas_call(
        _proj_res_kernel if kb == 1 else _proj_res_ktiled_kernel,
        grid=(m // tm, kb),
        in_specs=[
            pl.BlockSpec((tm, tk), lambda i, kk: (i, kk)),
            pl.BlockSpec((None, tk, d), lambda i, kk: (layer, kk, 0)),
            pl.BlockSpec((tm, d), lambda i, kk: (i, 0)),
            pl.BlockSpec((1, d), lambda i, kk: (0, 0)),
            pl.BlockSpec((None, 1, d), lambda i, kk: (i // mod_blocks, 0, 0)),
        ],
        out_specs=pl.BlockSpec((tm, d), lambda i, kk: (i, 0)),
        out_shape=jax.ShapeDtypeStruct((m, d), F32),
        compiler_params=_params(("arbitrary", "arbitrary")),
        name=name,
    )(a, w, x, gain.reshape(1, d), gate)


def _ffn_in_kernel(x_ref, xprev_ref, xnext_ref, gain_ref, shift_ref, scale_ref, wg_ref, wu_ref, cw_ref, cb_ref,
                   o_ref, h_ref, *, blocks_per_seq):
    i = pl.program_id(0)
    j = pl.program_id(1)
    tm = x_ref.shape[0]
    halo = BF16_SUBLANES
    lead = _split_rows(tm)

    def gated(gate, up):
        rows = gate.shape[0]
        g_prev = pltpu.roll(gate, 1, axis=0)[halo:halo + tm, :]
        g_next = pltpu.roll(gate, rows - 1, axis=0)[halo:halo + tm, :]
        conv = (g_prev * cw_ref[0:1, :] + gate[halo:halo + tm, :] * cw_ref[1:2, :] + g_next * cw_ref[2:3, :]
                + cb_ref[...])
        o_ref[...] = (_silu(conv) * up).astype(o_ref.dtype)

    @pl.when(j == 0)
    def _():
        gain, shift, scale = gain_ref[...], shift_ref[...], scale_ref[...]
        seq_block = i % blocks_per_seq
        h_prev = jnp.where(seq_block > 0, _norm_mod(xprev_ref[...], gain, shift, scale), 0.0)
        h_next = jnp.where(seq_block < blocks_per_seq - 1, _norm_mod(xnext_ref[...], gain, shift, scale), 0.0)
        zeros = jnp.zeros((halo - h_prev.shape[0], h_prev.shape[1]), F32)
        h_ref[0:halo, :] = jnp.concatenate([zeros, h_prev], axis=0).astype(BF16)
        h_ref[halo + tm:, :] = jnp.concatenate([h_next, zeros], axis=0).astype(BF16)
        _norm_mod_rows(x_ref, lead, h_ref, halo, gain, shift, scale)
        if lead == tm:
            gated(_dot(h_ref[...], wg_ref[...]), _dot(h_ref[halo:halo + tm, :], wu_ref[...]))
        else:
            cut = halo + lead
            h_ref[cut:halo + tm, :] = _norm_mod(x_ref[lead:, :], gain, shift, scale).astype(BF16)
            gate = jnp.concatenate([_dot(h_ref[:cut, :], wg_ref[...]), _dot(h_ref[cut:, :], wg_ref[...])], axis=0)
            up = jnp.concatenate([_dot(h_ref[halo:cut, :], wu_ref[...]),
                                  _dot(h_ref[cut:halo + tm, :], wu_ref[...])], axis=0)
            gated(gate, up)

    @pl.when(j > 0)
    def _():
        gated(_dot(h_ref[...], wg_ref[...]), _dot(h_ref[halo:halo + tm, :], wu_ref[...]))


def _ffn_in_call(x, gain, shift, scale, w, conv_w, conv_b, *, layer, seq_len, name):
    m, d = x.shape
    f = w.shape[2] // 2
    bm = shift.shape[0]
    rows_per_mod = m // bm
    tm = _pick(seq_len, 1024, BF16_SUBLANES)
    tn = _pick(f, 512, HEAD)
    assert rows_per_mod % tm == 0
    mod_blocks = rows_per_mod // tm
    nb = f // tn
    xh = 8
    tiles_per_tm = tm // xh
    last_tile = m // xh - 1
    body = functools.partial(_ffn_in_kernel, blocks_per_seq=seq_len // tm)
    return pl.pallas_call(
        body,
        grid=(m // tm, nb),
        in_specs=[
            pl.BlockSpec((tm, d), lambda i, j: (i, 0)),
            pl.BlockSpec((xh, d), lambda i, j: (jnp.maximum(i * tiles_per_tm - 1, 0), 0)),
            pl.BlockSpec((xh, d), lambda i, j: (jnp.minimum((i + 1) * tiles_per_tm, last_tile), 0)),
            pl.BlockSpec((1, d), lambda i, j: (0, 0)),
            pl.BlockSpec((None, 1, d), lambda i, j: (i // mod_blocks, 0, 0)),
            pl.BlockSpec((None, 1, d), lambda i, j: (i // mod_blocks, 0, 0)),
            pl.BlockSpec((None, d, tn), lambda i, j: (layer, 0, j)),
            pl.BlockSpec((None, d, tn), lambda i, j: (layer, 0, nb + j)),
            pl.BlockSpec((3, tn), lambda i, j: (0, j)),
            pl.BlockSpec((1, tn), lambda i, j: (0, j)),
        ],
        out_specs=pl.BlockSpec((tm, tn), lambda i, j: (i, j)),
        out_shape=jax.ShapeDtypeStruct((m, f), BF16),
        scratch_shapes=[pltpu.VMEM((tm + 2 * BF16_SUBLANES, d), BF16)],
        compiler_params=_params(("arbitrary", "arbitrary")),
        name=name,
    )(x, x, x, gain.reshape(1, d), shift, scale, w, w, conv_w, conv_b.reshape(1, f))


def _gla_gates(f_raw, lb, within):
    half = 0.5 * (1.0 - lb)
    ht = half * jnp.tanh(0.5 * f_raw.astype(F32))
    lf = jnp.log((1.0 - half) + ht)
    ones = jnp.where(within, 1.0, 0.0).astype(BF16)
    lf_hi = lf.astype(BF16)
    lf_lo = (lf - lf_hi.astype(F32)).astype(BF16)
    return half - ht, _dot(ones, lf_hi) + _dot(ones, lf_lo)


def _gla_pairs(qs, kk, b, forward, within):
    c = qs.shape[0]
    a_row = c // 2 - 1 if forward else c // 2
    e_row = c - 1 if forward else 0
    anchor = b[a_row:a_row + 1, :]
    b_end = b[e_row:e_row + 1, :]
    qa = qs * jnp.exp(b - anchor)
    ka = kk * jnp.exp(anchor - b)
    scores = jnp.where(within, _dot_nt(qa.astype(BF16), ka.astype(BF16)), 0.0).astype(BF16)
    return (scores, (qa * jnp.exp(anchor)).astype(BF16), (ka * jnp.exp(b_end - anchor)).astype(BF16),
            jnp.exp(b_end))


def _gla_pairs_exact(qs, kk, b, forward, within, tmp):
    tb_ref, tq_ref, ts_ref = tmp
    c = qs.shape[0]
    b_end = b[c - 1:c, :] if forward else b[0:1, :]
    tb_ref[...] = b
    tq_ref[...] = qs
    ones = jnp.ones((8, qs.shape[1]), BF16)

    def row(t, carry):
        rel = jnp.minimum(tb_ref[pl.ds(t, 1), :] - b, 0.0)
        e = (tq_ref[pl.ds(t, 1), :] * kk) * jnp.exp(rel)
        ts_ref[pl.ds(t, 1), :] = _dot_nt(ones, e.astype(BF16))[0:1, :]
        return carry

    lax.fori_loop(0, c, row, 0)
    scores = jnp.where(within, ts_ref[...], 0.0).astype(BF16)
    return scores, (qs * jnp.exp(b)).astype(BF16), (kk * jnp.exp(b_end - b)).astype(BF16), jnp.exp(b_end)


def _gla_readout(o, g_raw, o_gain):
    ms = jnp.mean(o * o, axis=-1, keepdims=True)
    y = o * lax.rsqrt(ms + EPS) * o_gain
    return (y * _silu(g_raw.astype(F32))).astype(BF16)


def _gla_kernel(qc, ffc, fbc, vc, gc, ql, ffl, fbl, vl, gl, lb_ref, og_ref, yc_ref, yl_ref,
                o_ref, qb_ref, ke_ref, dec_ref, st_ref, tb_ref, tq_ref, ts_ref):
    c = GLA_CHUNK
    tc = qc.shape[0]
    n_ctx = tc // c
    n_lat = ql.shape[0] // c
    lb = (lb_ref[0], lb_ref[1])
    row = lax.broadcasted_iota(jnp.int32, (c, c), 0)
    col = lax.broadcasted_iota(jnp.int32, (c, c), 1)
    within = (col <= row, col >= row)

    def rows(start):
        return pl.ds(start if isinstance(start, int) else pl.multiple_of(start, c), c)

    def local(refs, src0, dst0, chunk0, n, exact):
        q, ff, fb, v = refs
        src = [rows(src0 + j * c) for j in range(n)]
        dst = [rows(dst0 + j * c) for j in range(n)]
        qs = [_silu(q[src[j], :].astype(F32)) * (HEAD ** -0.5) for j in range(n)]
        gates = [[_gla_gates(f[src[j], :], lb[d], within[d]) for d, f in enumerate((ff, fb))] for j in range(n)]
        scores = []
        for j in range(n):
            per_dir = []
            for d in range(2):
                kk, b = gates[j][d]
                if exact:
                    sc, qb, ke, dec = _gla_pairs_exact(qs[j], kk, b, d == 0, within[d], (tb_ref, tq_ref, ts_ref))
                else:
                    sc, qb, ke, dec = _gla_pairs(qs[j], kk, b, d == 0, within[d])
                qb_ref[d, dst[j], :] = qb
                ke_ref[d, dst[j], :] = ke
                dec_ref[d, chunk0 + j] = dec
                per_dir.append(sc)
            scores.append(per_dir)
        for j in range(n):
            vj = v[src[j], :]
            o_ref[dst[j], :] = _dot(scores[j][0], vj) + _dot(scores[j][1], vj)

    def phase1(exact):
        run = 1 if exact else math.gcd(n_lat, GLA_RUN)
        run_ctx = math.gcd(n_ctx, run)
        for i in range(0, n_ctx, run_ctx):
            local((qc, ffc, fbc, vc), i * c, i * c, i, run_ctx, exact)

        def body(i, carry):
            src0 = pl.multiple_of(i * (run * c), c)
            local((ql, ffl, fbl, vl), src0, tc + src0, n_ctx + i * run, run, exact)
            return carry

        lax.fori_loop(0, n_lat // run, body, 0)

    safe = jnp.min(jnp.minimum(lb[0], lb[1])) >= GLA_SAFE_LB

    @pl.when(safe)
    def _():
        phase1(False)

    @pl.when(jnp.logical_not(safe))
    def _():
        phase1(True)

    st_ref[...] = jnp.zeros_like(st_ref)

    o_gain = og_ref[...]

    def steps(v, first, n, n_seg, row0, chunk0, finish=None):
        order = [[first + s for s in range(n)], [n_seg - 1 - first - s for s in range(n)]]
        src = [[rows(j * c) for j in order[d]] for d in range(2)]
        dst = [[rows(row0 + j * c) for j in order[d]] for d in range(2)]
        kv = [[_dot_tn(v[src[d][s], :], ke_ref[d, dst[d][s], :]) for s in range(n)] for d in range(2)]
        for d in range(2):
            st = st_ref[d]
            for s in range(n):
                o = o_ref[dst[d][s], :] + _dot_nt(qb_ref[d, dst[d][s], :], st.astype(BF16))
                if finish is None:
                    o_ref[dst[d][s], :] = o
                else:
                    g, y = finish
                    y[src[d][s], :] = _gla_readout(o, g[src[d][s], :], o_gain)
                st = st * dec_ref[d, chunk0 + order[d][s]] + kv[d][s]
            st_ref[d] = st

    steps(vc, 0, n_ctx, n_ctx, 0, 0)
    for i in range(n_ctx):
        r = slice(i * c, (i + 1) * c)
        yc_ref[r, :] = _gla_readout(o_ref[r, :], gc[r, :], o_gain)

    run = math.gcd(n_lat, GLA_RUN)
    n_runs = n_lat // run
    meet = (n_runs + 1) // 2

    def steps_body(i, carry):
        steps(vl, i * run, run, n_lat, tc, n_ctx)
        return carry

    def last_steps_body(i, carry):
        steps(vl, i * run, run, n_lat, tc, n_ctx, finish=(gl, yl_ref))
        return carry

    lax.fori_loop(0, meet, steps_body, 0)
    if n_runs % 2:
        for j in range((n_runs // 2) * run, (n_runs // 2 + 1) * run):
            yl_ref[rows(j * c), :] = _gla_readout(o_ref[rows(tc + j * c), :], gl[rows(j * c), :], o_gain)
    lax.fori_loop(meet, n_runs, last_steps_body, 0)


def _gla_call(p_ctx, p_lat, lower_bound, o_gain):
    b, tc, d5 = p_ctx.shape
    t = p_lat.shape[1]
    d = d5 // 5
    heads = d // HEAD
    assert tc % GLA_CHUNK == 0 and t % GLA_CHUNK == 0

    def col(tt, part):
        return pl.BlockSpec((None, tt, HEAD), lambda bi, h: (bi, 0, part * heads + h))

    out_spec = lambda tt: pl.BlockSpec((None, tt, HEAD), lambda bi, h: (bi, 0, h))
    return pl.pallas_call(
        _gla_kernel,
        grid=(b, heads),
        in_specs=[col(tc, p) for p in range(5)] + [col(t, p) for p in range(5)] + [
            pl.BlockSpec((2, None, 1, HEAD), lambda bi, h: (0, h, 0, 0)),
            pl.BlockSpec((1, HEAD), lambda bi, h: (0, 0)),
        ],
        out_specs=[out_spec(tc), out_spec(t)],
        out_shape=[jax.ShapeDtypeStruct((b, tc, d), BF16), jax.ShapeDtypeStruct((b, t, d), BF16)],
        scratch_shapes=[
            pltpu.VMEM((tc + t, HEAD), F32),
            pltpu.VMEM((2, tc + t, HEAD), BF16),
            pltpu.VMEM((2, tc + t, HEAD), BF16),
            pltpu.VMEM((2, (tc + t) // GLA_CHUNK, 1, HEAD), F32),
            pltpu.VMEM((2, HEAD, HEAD), F32),
            pltpu.VMEM((GLA_CHUNK, HEAD), F32),
            pltpu.VMEM((GLA_CHUNK, HEAD), F32),
            pltpu.VMEM((GLA_CHUNK, GLA_CHUNK), F32),
        ],
        compiler_params=_params(("arbitrary", "arbitrary")),
        name="hgrn_scan",
    )(*([p_ctx] * 5), *([p_lat] * 5), lower_bound.reshape(2, heads, 1, HEAD), o_gain.reshape(1, HEAD))


def _flash_kernel(q_ref, kc_ref, vc_ref, kl_ref, vl_ref, o_ref, *, tk, unroll):
    tq = q_ref.shape[0]
    group = q_ref.shape[1] // HEAD
    qs = [q_ref[:, g * HEAD:(g + 1) * HEAD] for g in range(group)]

    def attend(carry, k, v):
        v_ext = jnp.concatenate([v, jnp.ones_like(v)], axis=1)
        out = []
        for g in range(group):
            m, acc = carry[g]
            s = _dot_nt(qs[g], k)
            m_new = jnp.maximum(m, jnp.max(s, axis=-1, keepdims=True))
            p = jnp.exp2(s - m_new).astype(BF16)
            out.append((m_new, jnp.exp2(m - m_new) * acc + _dot(p, v_ext)))
        return tuple(out)

    def body(j, carry):
        r = pl.ds(pl.multiple_of(j * tk, tk), tk)
        return attend(carry, kl_ref[r, :], vl_ref[r, :])

    init = tuple((jnp.full((tq, 1), -jnp.inf, F32), jnp.zeros((tq, 2 * HEAD), F32)) for _ in range(group))
    carry = lax.fori_loop(0, kl_ref.shape[0] // tk, body, init, unroll=unroll)
    final = attend(carry, kc_ref[...], vc_ref[...])
    for g in range(group):
        acc = final[g][1]
        o_ref[:, g * HEAD:(g + 1) * HEAD] = (acc[:, :HEAD] / acc[:, HEAD:]).astype(o_ref.dtype)


def _flash_call(qkv, kv_ctx, *, heads):
    b, t, n = qkv.shape
    tc = kv_ctx.shape[1]
    kvh = (n // HEAD - heads) // 2
    group = heads // kvh
    tq = _pick(t, 512, BF16_SUBLANES)
    tk = _pick(t, 1024, 2 * HEAD)
    body = functools.partial(_flash_kernel, tk=tk, unroll=math.gcd(t // tk, 8))
    return pl.pallas_call(
        body,
        grid=(b, kvh, t // tq),
        in_specs=[
            pl.BlockSpec((None, tq, group * HEAD), lambda bi, h, i: (bi, i, h)),
            pl.BlockSpec((None, tc, HEAD), lambda bi, h, i: (bi, 0, h)),
            pl.BlockSpec((None, tc, HEAD), lambda bi, h, i: (bi, 0, kvh + h)),
            pl.BlockSpec((None, t, HEAD), lambda bi, h, i: (bi, 0, heads + h)),
            pl.BlockSpec((None, t, HEAD), lambda bi, h, i: (bi, 0, heads + kvh + h)),
        ],
        out_specs=pl.BlockSpec((None, tq, group * HEAD), lambda bi, h, i: (bi, i, h)),
        out_shape=jax.ShapeDtypeStruct((b, t, heads * HEAD), BF16),
        compiler_params=_params(("arbitrary", "arbitrary", "arbitrary")),
        name="gqa_flash",
    )(qkv, kv_ctx, kv_ctx, qkv, qkv)


def _rope_tables(t):
    pos = jnp.arange(t, dtype=jnp.int32)
    rows = (pos // GRID_W).astype(F32)
    cols = (pos % GRID_W).astype(F32)
    axis_dim = HEAD // 2
    inv_freq = ROPE_THETA ** (-jnp.arange(0, axis_dim, 2, dtype=F32) / axis_dim)
    ang = jnp.concatenate([rows[:, None] * inv_freq, cols[:, None] * inv_freq], axis=-1)
    cos, sin = jnp.cos(ang), jnp.sin(ang)
    return jnp.concatenate([cos, cos], axis=-1), jnp.concatenate([-sin, sin], axis=-1)


def _split_mods(mods, b):
    d = mods.shape[1] // N_MOD
    lat = [mods[:b, i * d:(i + 1) * d].reshape(b, 1, d) for i in range(N_MOD)]
    ctx = [mods[b:b + 1, i * d:(i + 1) * d].reshape(1, 1, d) for i in range(N_MOD)]
    return lat, ctx


def kernel(x, c, ctx, c_ctx, ada_w, ada_b, norm_mix_pre, norm_mix_post, norm_ffn_pre, norm_ffn_post, hgrn_w_in, hgrn_lb_logits, hgrn_o_norm, hgrn_w_out, attn_w_qkv, attn_q_norm, attn_k_norm, attn_w_out, ffn_w_in, ffn_conv_w, ffn_conv_b, ffn_w_out):
    b, t, d = x.shape
    tc = ctx.shape[1]
    heads = d // HEAD
    assert ada_w.shape[0] == 2 and b + 1 <= ADA_ROWS

    cond = jnp.zeros((ADA_ROWS, d), F32).at[:b].set(c).at[b].set(c_ctx)
    mods = _ada_call(cond, ada_w, ada_b)
    x_lat = x.reshape(b * t, d)
    x_ctx = ctx.reshape(b * tc, d)

    (sh_ml, sc_ml, gt_ml, sh_fl, sc_fl, gt_fl), (sh_mc, sc_mc, gt_mc, sh_fc, sc_fc, gt_fc) = _split_mods(mods[0], b)
    lower_bound = jnp.cumsum(jax.nn.softmax(hgrn_lb_logits.astype(F32), axis=1), axis=1)[:, 0]
    hgrn_w_in, hgrn_w_out, attn_w_out = hgrn_w_in.astype(BF16), hgrn_w_out.astype(BF16), attn_w_out.astype(BF16)
    ffn_w_in, ffn_w_out = ffn_w_in.astype(BF16), ffn_w_out.astype(BF16)
    p_lat = _nmm_call(x_lat, norm_mix_pre[0], sh_ml, sc_ml, hgrn_w_in, name="hgrn_in_lat")
    p_ctx = _nmm_call(x_ctx, norm_mix_pre[0], sh_mc, sc_mc, hgrn_w_in, name="hgrn_in_ctx")
    y_ctx, y_lat = _gla_call(p_ctx.reshape(b, tc, 5 * d), p_lat.reshape(b, t, 5 * d), lower_bound, hgrn_o_norm[0])
    x_lat = _proj_res_call(y_lat.reshape(b * t, d), hgrn_w_out, x_lat, norm_mix_post[0], gt_ml, name="hgrn_out_lat")
    x_ctx = _proj_res_call(y_ctx.reshape(b * tc, d), hgrn_w_out, x_ctx, norm_mix_post[0], gt_mc, name="hgrn_out_ctx")
    a = _ffn_in_call(x_lat, norm_ffn_pre[0], sh_fl, sc_fl, ffn_w_in, ffn_conv_w[0], ffn_conv_b[0],
                     layer=0, seq_len=t, name="ffn0_in_lat")
    x_lat = _proj_res_call(a, ffn_w_out, x_lat, norm_ffn_post[0], gt_fl, layer=0, name="ffn0_out_lat")
    a = _ffn_in_call(x_ctx, norm_ffn_pre[0], sh_fc, sc_fc, ffn_w_in, ffn_conv_w[0], ffn_conv_b[0],
                     layer=0, seq_len=tc, name="ffn0_in_ctx")
    x_ctx = _proj_res_call(a, ffn_w_out, x_ctx, norm_ffn_post[0], gt_fc, layer=0, name="ffn0_out_ctx")

    (sh_ml, sc_ml, gt_ml, sh_fl, sc_fl, gt_fl), (sh_mc, sc_mc, _, _, _, _) = _split_mods(mods[1], b)
    n_qkv = attn_w_qkv.shape[2]
    kvh = (n_qkv // HEAD - heads) // 2
    perm = jnp.concatenate([jnp.arange(0, HEAD, 2), jnp.arange(1, HEAD, 2)])
    head_perm = (jnp.arange(heads + kvh)[:, None] * HEAD + perm[None, :]).reshape(-1)
    col_perm = jnp.concatenate([head_perm, jnp.arange((heads + kvh) * HEAD, n_qkv)])
    w_qkv = attn_w_qkv[:, :, col_perm].astype(BF16)
    head_gain = jnp.concatenate([
        jnp.tile(attn_q_norm[0][perm] * (LOG2_E * HEAD ** -0.5), heads),
        jnp.tile(attn_k_norm[0][perm], kvh),
        jnp.ones((kvh * HEAD,), F32),
    ]).reshape(1, n_qkv)
    cos, sin = _rope_tables(t)
    qk_cols = (heads + kvh) * HEAD
    q_cols = heads * HEAD
    qkv = _nmm_call(x_lat, norm_mix_pre[1], sh_ml, sc_ml, w_qkv, name="attn_qkv_lat")
    qkv = _qk_rope_call(qkv, head_gain[:, :qk_cols], cos, sin, name="attn_qk_rope_lat")
    kv_ctx = _nmm_call(x_ctx, norm_mix_pre[1], sh_mc, sc_mc, w_qkv, col0=q_cols, name="attn_kv_ctx")
    kv_ctx = _qk_rope_call(kv_ctx, head_gain[:, q_cols:qk_cols], jnp.ones((tc, HEAD), F32),
                           jnp.zeros((tc, HEAD), F32), name="attn_k_norm_ctx")
    qkv = qkv.reshape(b, t, n_qkv)
    kv_ctx = kv_ctx.reshape(b, tc, 2 * kvh * HEAD)
    o = _flash_call(qkv, kv_ctx, heads=heads)
    x_lat = _proj_res_call(o.reshape(b * t, d), attn_w_out, x_lat, norm_mix_post[1], gt_ml, name="attn_out_lat")
    a = _ffn_in_call(x_lat, norm_ffn_pre[1], sh_fl, sc_fl, ffn_w_in, ffn_conv_w[1], ffn_conv_b[1],
                     layer=1, seq_len=t, name="ffn1_in_lat")
    x_lat = _proj_res_call(a, ffn_w_out, x_lat, norm_ffn_post[1], gt_fl, layer=1, name="ffn1_out_lat")
    return x_lat.reshape(b, t, d)
```

```python
import functools
import math

import jax
import jax.numpy as jnp
from jax import lax
from jax.experimental import pallas as pl
from jax.experimental.pallas import tpu as pltpu

F32 = jnp.float32
BF16 = jnp.bfloat16

EPS = 1e-6
GRID_W = 64
ROPE_THETA = 10000.0
LOG2_E = math.log2(math.e)
N_MOD = 6
HEAD = 128
ROW_GROUP = 16
GLA_CHUNK = 128
GLA_RUN = 32
GLA_SAFE_LB = math.exp(-80.0 / (GLA_CHUNK // 2))
BF16_SUBLANES = 16
ADA_ROWS = 8
V7X_VMEM_BYTES = 64 * 1024 * 1024
VMEM_LIMIT = V7X_VMEM_BYTES - 6 * 1024 * 1024


def _pick(n, target, align):
    if n <= target:
        return n
    for t in range(target - target % align, 0, -align):
        if n % t == 0:
            return t
    return n


def _params(sem):
    return pltpu.CompilerParams(dimension_semantics=sem, vmem_limit_bytes=VMEM_LIMIT)


def _sigmoid(x):
    return 0.5 * jnp.tanh(0.5 * x) + 0.5


def _silu(x):
    return x * _sigmoid(x)


def _dot(a, b):
    return jnp.dot(a, b, preferred_element_type=F32)


def _dot_nt(a, b):
    return lax.dot_general(a, b, (((1,), (1,)), ((), ())), preferred_element_type=F32)


def _dot_tn(a, b):
    return lax.dot_general(a, b, (((0,), (0,)), ((), ())), preferred_element_type=F32)


def _ada_kernel(c_ref, w_ref, b_ref, o_ref):
    sc = _silu(c_ref[...]).astype(BF16)
    o_ref[...] = _dot(sc, w_ref[...].astype(BF16)) + b_ref[...]


def _ada_call(cond, ada_w, ada_b):
    depth, d, n = ada_w.shape
    tn = _pick(n, 1024, HEAD)
    return pl.pallas_call(
        _ada_kernel,
        grid=(depth, n // tn),
        in_specs=[
            pl.BlockSpec((ADA_ROWS, d), lambda l, j: (0, 0)),
            pl.BlockSpec((None, d, tn), lambda l, j: (l, 0, j)),
            pl.BlockSpec((None, 1, tn), lambda l, j: (l, 0, j)),
        ],
        out_specs=pl.BlockSpec((None, ADA_ROWS, tn), lambda l, j: (l, 0, j)),
        out_shape=jax.ShapeDtypeStruct((depth, ADA_ROWS, n), F32),
        compiler_params=_params(("arbitrary", "arbitrary")),
        name="ada_ln",
    )(cond, ada_w, ada_b.reshape(depth, 1, n))


def _norm_mod(x, gain, shift, scale):
    ms = jnp.mean(x * x, axis=-1, keepdims=True)
    return (x * lax.rsqrt(ms + EPS) * gain) * (1.0 + scale) + shift


def _norm_mod_rows(x_ref, n_rows, h_ref, h_row0, gain, shift, scale):
    mult = gain * (1.0 + scale)

    def body(r, carry):
        start = pl.multiple_of(r * ROW_GROUP, ROW_GROUP)
        x = x_ref[pl.ds(start, ROW_GROUP), :]
        ms = jnp.mean(x * x, axis=-1, keepdims=True)
        dst = pl.ds(pl.multiple_of(h_row0 + start, ROW_GROUP), ROW_GROUP)
        h_ref[dst, :] = (x * lax.rsqrt(ms + EPS) * mult + shift).astype(BF16)
        return carry

    groups = n_rows // ROW_GROUP
    lax.fori_loop(0, groups, body, 0, unroll=math.gcd(groups, 4))


def _split_rows(tm):
    half = tm // 2
    return half if half % ROW_GROUP == 0 else tm


def _nmm_kernel(x_ref, gain_ref, shift_ref, scale_ref, w_ref, o_ref, h_ref):
    j = pl.program_id(1)
    tm = x_ref.shape[0]
    lead = _split_rows(tm)

    @pl.when(j == 0)
    def _():
        gain, shift, scale = gain_ref[...], shift_ref[...], scale_ref[...]
        _norm_mod_rows(x_ref, lead, h_ref, 0, gain, shift, scale)
        if lead < tm:
            h_ref[lead:, :] = _norm_mod(x_ref[lead:, :], gain, shift, scale).astype(BF16)
        o_ref[:lead, :] = _dot(h_ref[:lead, :], w_ref[...]).astype(o_ref.dtype)
        if lead < tm:
            o_ref[lead:, :] = _dot(h_ref[lead:, :], w_ref[...]).astype(o_ref.dtype)

    @pl.when(j > 0)
    def _():
        o_ref[...] = _dot(h_ref[...], w_ref[...]).astype(o_ref.dtype)


def _nmm_call(x, gain, shift, scale, w, *, layer=0, col0=0, name):
    m, d = x.shape
    n = w.shape[2] - col0
    bm = shift.shape[0]
    rows_per_mod = m // bm
    tm = _pick(rows_per_mod, 1024, BF16_SUBLANES)
    tn = _pick(math.gcd(n, col0) if col0 else n, 1024, HEAD)
    mod_blocks = rows_per_mod // tm
    col_block0 = col0 // tn
    return pl.pallas_call(
        _nmm_kernel,
        grid=(m // tm, n // tn),
        in_specs=[
            pl.BlockSpec((tm, d), lambda i, j: (i, 0)),
            pl.BlockSpec((1, d), lambda i, j: (0, 0)),
            pl.BlockSpec((None, 1, d), lambda i, j: (i // mod_blocks, 0, 0)),
            pl.BlockSpec((None, 1, d), lambda i, j: (i // mod_blocks, 0, 0)),
            pl.BlockSpec((None, d, tn), lambda i, j: (layer, 0, col_block0 + j)),
        ],
        out_specs=pl.BlockSpec((tm, tn), lambda i, j: (i, j)),
        out_shape=jax.ShapeDtypeStruct((m, n), BF16),
        scratch_shapes=[pltpu.VMEM((tm, d), BF16)],
        compiler_params=_params(("arbitrary", "arbitrary")),
        name=name,
    )(x, gain.reshape(1, d), shift, scale, w)


def _qk_rope_kernel(x_ref, hg_ref, cos_ref, sin_ref, o_ref):
    cos = cos_ref[...]
    sin = sin_ref[...]
    mean_lanes = jnp.full((HEAD, HEAD), 1.0 / HEAD, BF16)
    for hh in range(x_ref.shape[1] // HEAD):
        cols = slice(hh * HEAD, (hh + 1) * HEAD)
        a = x_ref[:, cols].astype(F32)
        ms = _dot((a * a).astype(BF16), mean_lanes)
        a = a * lax.rsqrt(ms + EPS) * hg_ref[:, cols]
        a = a * cos + pltpu.roll(a, HEAD // 2, axis=1) * sin
        o_ref[:, cols] = a.astype(o_ref.dtype)


def _qk_rope_call(p, head_gain, cos, sin, *, name):
    m, n = p.shape
    nn = head_gain.shape[1]
    tm = _pick(math.gcd(m, cos.shape[0]), 2048, BF16_SUBLANES)
    tn = _pick(math.gcd(n, nn), 1024, HEAD)
    table_blocks = cos.shape[0] // tm
    return pl.pallas_call(
        _qk_rope_kernel,
        grid=(m // tm, nn // tn),
        in_specs=[
            pl.BlockSpec((tm, tn), lambda i, j: (i, j)),
            pl.BlockSpec((1, tn), lambda i, j: (0, j)),
            pl.BlockSpec((tm, HEAD), lambda i, j: (i % table_blocks, 0)),
            pl.BlockSpec((tm, HEAD), lambda i, j: (i % table_blocks, 0)),
        ],
        out_specs=pl.BlockSpec((tm, tn), lambda i, j: (i, j)),
        out_shape=jax.ShapeDtypeStruct((m, n), p.dtype),
        input_output_aliases={0: 0},
        compiler_params=_params(("arbitrary", "arbitrary")),
        name=name,
    )(p, head_gain, cos, sin)


def _residual(x, y, gain, gate):
    ms = jnp.mean(y * y, axis=-1, keepdims=True)
    return x + gate * (y * lax.rsqrt(ms + EPS) * gain)


def _proj_res_kernel(a_ref, w_ref, x_ref, gain_ref, gate_ref, o_ref):
    y = _dot(a_ref[...], w_ref[...])
    o_ref[...] = _residual(x_ref[...], y, gain_ref[...], gate_ref[...])


def _proj_res_ktiled_kernel(a_ref, w_ref, x_ref, gain_ref, gate_ref, o_ref):
    k = pl.program_id(1)
    last = pl.num_programs(1) - 1
    tm = o_ref.shape[0]
    lead = _split_rows(tm)

    @pl.when(k == 0)
    def _():
        o_ref[...] = _dot(a_ref[...], w_ref[...])

    @pl.when(jnp.logical_and(k > 0, k < last))
    def _():
        o_ref[...] += _dot(a_ref[...], w_ref[...])

    @pl.when(k == last)
    def _():
        gain, gate = gain_ref[...], gate_ref[...]
        parts = [slice(0, lead), slice(lead, tm)] if lead < tm else [slice(0, tm)]
        ys = [o_ref[r, :] + _dot(a_ref[r, :], w_ref[...]) for r in parts]
        for r, y in zip(parts, ys):
            o_ref[r, :] = _residual(x_ref[r, :], y, gain, gate)


def _proj_res_call(a, w, x, gain, gate, *, layer=0, name):
    m, k = a.shape
    d = w.shape[2]
    bm = gate.shape[0]
    rows_per_mod = m // bm
    tk = k if k <= d else _pick(k, 512, 2 * HEAD)
    kb = k // tk
    tm = _pick(rows_per_mod, 512 if kb == 1 else 1024, BF16_SUBLANES)
    mod_blocks = rows_per_mod // tm
    return pl.pallas_call(
        _proj_res_kernel if kb == 1 else _proj_res_ktiled_kernel,
        grid=(m // tm, kb),
        in_specs=[
            pl.BlockSpec((tm, tk), lambda i, kk: (i, kk)),
            pl.BlockSpec((None, tk, d), lambda i, kk: (layer, kk, 0)),
            pl.BlockSpec((tm, d), lambda i, kk: (i, 0)),
            pl.BlockSpec((1, d), lambda i, kk: (0, 0)),
            pl.BlockSpec((None, 1, d), lambda i, kk: (i // mod_blocks, 0, 0)),
        ],
        out_specs=pl.BlockSpec((tm, d), lambda i, kk: (i, 0)),
        out_shape=jax.ShapeDtypeStruct((m, d), F32),
        compiler_params=_params(("arbitrary", "arbitrary")),
        name=name,
    )(a, w, x, gain.reshape(1, d), gate)


def _ffn_in_kernel(x_ref, xprev_ref, xnext_ref, gain_ref, shift_ref, scale_ref, wg_ref, wu_ref, cw_ref, cb_ref,
                   o_ref, h_ref, *, blocks_per_seq):
    i = pl.program_id(0)
    j = pl.program_id(1)
    tm = x_ref.shape[0]
    halo = BF16_SUBLANES
    lead = _split_rows(tm)

    def gated(gate, up):
        rows = gate.shape[0]
        g_prev = pltpu.roll(gate, 1, axis=0)[halo:halo + tm, :]
        g_next = pltpu.roll(gate, rows - 1, axis=0)[halo:halo + tm, :]
        conv = (g_prev * cw_ref[0:1, :] + gate[halo:halo + tm, :] * cw_ref[1:2, :] + g_next * cw_ref[2:3, :]
                + cb_ref[...])
        o_ref[...] = (_silu(conv) * up).astype(o_ref.dtype)

    @pl.when(j == 0)
    def _():
        gain, shift, scale = gain_ref[...], shift_ref[...], scale_ref[...]
        seq_block = i % blocks_per_seq
        h_prev = jnp.where(seq_block > 0, _norm_mod(xprev_ref[...], gain, shift, scale), 0.0)
        h_next = jnp.where(seq_block < blocks_per_seq - 1, _norm_mod(xnext_ref[...], gain, shift, scale), 0.0)
        zeros = jnp.zeros((halo - h_prev.shape[0], h_prev.shape[1]), F32)
        h_ref[0:halo, :] = jnp.concatenate([zeros, h_prev], axis=0).astype(BF16)
        h_ref[halo + tm:, :] = jnp.concatenate([h_next, zeros], axis=0).astype(BF16)
        _norm_mod_rows(x_ref, lead, h_ref, halo, gain, shift, scale)
        if lead == tm:
            gated(_dot(h_ref[...], wg_ref[...]), _dot(h_ref[halo:halo + tm, :], wu_ref[...]))
        else:
            cut = halo + lead
            h_ref[cut:halo + tm, :] = _norm_mod(x_ref[lead:, :], gain, shift, scale).astype(BF16)
            gate = jnp.concatenate([_dot(h_ref[:cut, :], wg_ref[...]), _dot(h_ref[cut:, :], wg_ref[...])], axis=0)
            up = jnp.concatenate([_dot(h_ref[halo:cut, :], wu_ref[...]),
                                  _dot(h_ref[cut:halo + tm, :], wu_ref[...])], axis=0)
            gated(gate, up)

    @pl.when(j > 0)
    def _():
        gated(_dot(h_ref[...], wg_ref[...]), _dot(h_ref[halo:halo + tm, :], wu_ref[...]))


def _ffn_in_call(x, gain, shift, scale, w, conv_w, conv_b, *, layer, seq_len, name):
    m, d = x.shape
    f = w.shape[2] // 2
    bm = shift.shape[0]
    rows_per_mod = m // bm
    tm = _pick(seq_len, 1024, BF16_SUBLANES)
    tn = _pick(f, 512, HEAD)
    assert rows_per_mod % tm == 0
    mod_blocks = rows_per_mod // tm
    nb = f // tn
    xh = 8
    tiles_per_tm = tm // xh
    last_tile = m // xh - 1
    body = functools.partial(_ffn_in_kernel, blocks_per_seq=seq_len // tm)
    return pl.pallas_call(
        body,
        grid=(m // tm, nb),
        in_specs=[
            pl.BlockSpec((tm, d), lambda i, j: (i, 0)),
            pl.BlockSpec((xh, d), lambda i, j: (jnp.maximum(i * tiles_per_tm - 1, 0), 0)),
            pl.BlockSpec((xh, d), lambda i, j: (jnp.minimum((i + 1) * tiles_per_tm, last_tile), 0)),
            pl.BlockSpec((1, d), lambda i, j: (0, 0)),
            pl.BlockSpec((None, 1, d), lambda i, j: (i // mod_blocks, 0, 0)),
            pl.BlockSpec((None, 1, d), lambda i, j: (i // mod_blocks, 0, 0)),
            pl.BlockSpec((None, d, tn), lambda i, j: (layer, 0, j)),
            pl.BlockSpec((None, d, tn), lambda i, j: (layer, 0, nb + j)),
            pl.BlockSpec((3, tn), lambda i, j: (0, j)),
            pl.BlockSpec((1, tn), lambda i, j: (0, j)),
        ],
        out_specs=pl.BlockSpec((tm, tn), lambda i, j: (i, j)),
        out_shape=jax.ShapeDtypeStruct((m, f), BF16),
        scratch_shapes=[pltpu.VMEM((tm + 2 * BF16_SUBLANES, d), BF16)],
        compiler_params=_params(("arbitrary", "arbitrary")),
        name=name,
    )(x, x, x, gain.reshape(1, d), shift, scale, w, w, conv_w, conv_b.reshape(1, f))


def _gla_gates(f_raw, lb, within):
    half = 0.5 * (1.0 - lb)
    ht = half * jnp.tanh(0.5 * f_raw.astype(F32))
    lf = jnp.log((1.0 - half) + ht)
    ones = jnp.where(within, 1.0, 0.0).astype(BF16)
    lf_hi = lf.astype(BF16)
    lf_lo = (lf - lf_hi.astype(F32)).astype(BF16)
    return half - ht, _dot(ones, lf_hi) + _dot(ones, lf_lo)


def _gla_pairs(qs, kk, b, forward, within):
    c = qs.shape[0]
    a_row = c // 2 - 1 if forward else c // 2
    e_row = c - 1 if forward else 0
    anchor = b[a_row:a_row + 1, :]
    b_end = b[e_row:e_row + 1, :]
    qa = qs * jnp.exp(b - anchor)
    ka = kk * jnp.exp(anchor - b)
    scores = jnp.where(within, _dot_nt(qa.astype(BF16), ka.astype(BF16)), 0.0).astype(BF16)
    return (scores, (qa * jnp.exp(anchor)).astype(BF16), (ka * jnp.exp(b_end - anchor)).astype(BF16),
            jnp.exp(b_end))


def _gla_pairs_exact(qs, kk, b, forward, within, tmp):
    tb_ref, tq_ref, ts_ref = tmp
    c = qs.shape[0]
    b_end = b[c - 1:c, :] if forward else b[0:1, :]
    tb_ref[...] = b
    tq_ref[...] = qs
    ones = jnp.ones((8, qs.shape[1]), BF16)

    def row(t, carry):
        rel = jnp.minimum(tb_ref[pl.ds(t, 1), :] - b, 0.0)
        e = (tq_ref[pl.ds(t, 1), :] * kk) * jnp.exp(rel)
        ts_ref[pl.ds(t, 1), :] = _dot_nt(ones, e.astype(BF16))[0:1, :]
        return carry

    lax.fori_loop(0, c, row, 0)
    scores = jnp.where(within, ts_ref[...], 0.0).astype(BF16)
    return scores, (qs * jnp.exp(b)).astype(BF16), (kk * jnp.exp(b_end - b)).astype(BF16), jnp.exp(b_end)


def _gla_readout(o, g_raw, o_gain):
    ms = jnp.mean(o * o, axis=-1, keepdims=True)
    y = o * lax.rsqrt(ms + EPS) * o_gain
    return (y * _silu(g_raw.astype(F32))).astype(BF16)


def _gla_kernel(qc, ffc, fbc, vc, gc, ql, ffl, fbl, vl, gl, lb_ref, og_ref, yc_ref, yl_ref,
                o_ref, qb_ref, ke_ref, dec_ref, st_ref, tb_ref, tq_ref, ts_ref):
    c = GLA_CHUNK
    tc = qc.shape[0]
    n_ctx = tc // c
    n_lat = ql.shape[0] // c
    lb = (lb_ref[0], lb_ref[1])
    row = lax.broadcasted_iota(jnp.int32, (c, c), 0)
    col = lax.broadcasted_iota(jnp.int32, (c, c), 1)
    within = (col <= row, col >= row)

    def rows(start):
        return pl.ds(start if isinstance(start, int) else pl.multiple_of(start, c), c)

    def local(refs, src0, dst0, chunk0, n, exact):
        q, ff, fb, v = refs
        src = [rows(src0 + j * c) for j in range(n)]
        dst = [rows(dst0 + j * c) for j in range(n)]
        qs = [_silu(q[src[j], :].astype(F32)) * (HEAD ** -0.5) for j in range(n)]
        gates = [[_gla_gates(f[src[j], :], lb[d], within[d]) for d, f in enumerate((ff, fb))] for j in range(n)]
        scores = []
        for j in range(n):
            per_dir = []
            for d in range(2):
                kk, b = gates[j][d]
                if exact:
                    sc, qb, ke, dec = _gla_pairs_exact(qs[j], kk, b, d == 0, within[d], (tb_ref, tq_ref, ts_ref))
                else:
                    sc, qb, ke, dec = _gla_pairs(qs[j], kk, b, d == 0, within[d])
                qb_ref[d, dst[j], :] = qb
                ke_ref[d, dst[j], :] = ke
                dec_ref[d, chunk0 + j] = dec
                per_dir.append(sc)
            scores.append(per_dir)
        for j in range(n):
            vj = v[src[j], :]
            o_ref[dst[j], :] = _dot(scores[j][0], vj) + _dot(scores[j][1], vj)

    def phase1(exact):
        run = 1 if exact else math.gcd(n_lat, GLA_RUN)
        run_ctx = math.gcd(n_ctx, run)
        for i in range(0, n_ctx, run_ctx):
            local((qc, ffc, fbc, vc), i * c, i * c, i, run_ctx, exact)

        def body(i, carry):
            src0 = pl.multiple_of(i * (run * c), c)
            local((ql, ffl, fbl, vl), src0, tc + src0, n_ctx + i * run, run, exact)
            return carry

        lax.fori_loop(0, n_lat // run, body, 0)

    safe = jnp.min(jnp.minimum(lb[0], lb[1])) >= GLA_SAFE_LB

    @pl.when(safe)
    def _():
        phase1(False)

    @pl.when(jnp.logical_not(safe))
    def _():
        phase1(True)

    st_ref[...] = jnp.zeros_like(st_ref)

    o_gain = og_ref[...]

    def steps(v, first, n, n_seg, row0, chunk0, finish=None):
        order = [[first + s for s in range(n)], [n_seg - 1 - first - s for s in range(n)]]
        src = [[rows(j * c) for j in order[d]] for d in range(2)]
        dst = [[rows(row0 + j * c) for j in order[d]] for d in range(2)]
        kv = [[_dot_tn(v[src[d][s], :], ke_ref[d, dst[d][s], :]) for s in range(n)] for d in range(2)]
        for d in range(2):
            st = st_ref[d]
            for s in range(n):
                o = o_ref[dst[d][s], :] + _dot_nt(qb_ref[d, dst[d][s], :], st.astype(BF16))
                if finish is None:
                    o_ref[dst[d][s], :] = o
                else:
                    g, y = finish
                    y[src[d][s], :] = _gla_readout(o, g[src[d][s], :], o_gain)
                st = st * dec_ref[d, chunk0 + order[d][s]] + kv[d][s]
            st_ref[d] = st

    steps(vc, 0, n_ctx, n_ctx, 0, 0)
    for i in range(n_ctx):
        r = slice(i * c, (i + 1) * c)
        yc_ref[r, :] = _gla_readout(o_ref[r, :], gc[r, :], o_gain)

    run = math.gcd(n_lat, GLA_RUN)
    n_runs = n_lat // run
    meet = (n_runs + 1) // 2

    def steps_body(i, carry):
        steps(vl, i * run, run, n_lat, tc, n_ctx)
        return carry

    def last_steps_body(i, carry):
        steps(vl, i * run, run, n_lat, tc, n_ctx, finish=(gl, yl_ref))
        return carry

    lax.fori_loop(0, meet, steps_body, 0)
    if n_runs % 2:
        for j in range((n_runs // 2) * run, (n_runs // 2 + 1) * run):
            yl_ref[rows(j * c), :] = _gla_readout(o_ref[rows(tc + j * c), :], gl[rows(j * c), :], o_gain)
    lax.fori_loop(meet, n_runs, last_steps_body, 0)


def _gla_call(p_ctx, p_lat, lower_bound, o_gain):
    b, tc, d5 = p_ctx.shape
    t = p_lat.shape[1]
    d = d5 // 5
    heads = d // HEAD
    assert tc % GLA_CHUNK == 0 and t % GLA_CHUNK == 0

    def col(tt, part):
        return pl.BlockSpec((None, tt, HEAD), lambda bi, h: (bi, 0, part * heads + h))

    out_spec = lambda tt: pl.BlockSpec((None, tt, HEAD), lambda bi, h: (bi, 0, h))
    return pl.pallas_call(
        _gla_kernel,
        grid=(b, heads),
        in_specs=[col(tc, p) for p in range(5)] + [col(t, p) for p in range(5)] + [
            pl.BlockSpec((2, None, 1, HEAD), lambda bi, h: (0, h, 0, 0)),
            pl.BlockSpec((1, HEAD), lambda bi, h: (0, 0)),
        ],
        out_specs=[out_spec(tc), out_spec(t)],
        out_shape=[jax.ShapeDtypeStruct((b, tc, d), BF16), jax.ShapeDtypeStruct((b, t, d), BF16)],
        scratch_shapes=[
            pltpu.VMEM((tc + t, HEAD), F32),
            pltpu.VMEM((2, tc + t, HEAD), BF16),
            pltpu.VMEM((2, tc + t, HEAD), BF16),
            pltpu.VMEM((2, (tc + t) // GLA_CHUNK, 1, HEAD), F32),
            pltpu.VMEM((2, HEAD, HEAD), F32),
            pltpu.VMEM((GLA_CHUNK, HEAD), F32),
            pltpu.VMEM((GLA_CHUNK, HEAD), F32),
            pltpu.VMEM((GLA_CHUNK, GLA_CHUNK), F32),
        ],
        compiler_params=_params(("arbitrary", "arbitrary")),
        name="hgrn_scan",
    )(*([p_ctx] * 5), *([p_lat] * 5), lower_bound.reshape(2, heads, 1, HEAD), o_gain.reshape(1, HEAD))


def _flash_kernel(q_ref, kc_ref, vc_ref, kl_ref, vl_ref, o_ref, *, tk):
    tq = q_ref.shape[0]
    group = q_ref.shape[1] // HEAD
    qs = [q_ref[:, g * HEAD:(g + 1) * HEAD] for g in range(group)]

    def attend(carry, k, v):
        v_ext = jnp.concatenate([v, jnp.ones_like(v)], axis=1)
        out = []
        for g in range(group):
            m, acc = carry[g]
            s = _dot_nt(qs[g], k)
            m_new = jnp.maximum(m, jnp.max(s, axis=-1, keepdims=True))
            p = jnp.exp2(s - m_new).astype(BF16)
            out.append((m_new, jnp.exp2(m - m_new) * acc + _dot(p, v_ext)))
        return tuple(out)

    carry = tuple((jnp.full((tq, 1), -jnp.inf, F32), jnp.zeros((tq, 2 * HEAD), F32)) for _ in range(group))
    carry = attend(carry, kc_ref[...], vc_ref[...])
    t = kl_ref.shape[0]
    short = 2 * HEAD
    sizes = [tk] * (t // tk - 1) + ([tk - short, short] if tk > short else [tk])
    start = 0
    for size in sizes:
        carry = attend(carry, kl_ref[start:start + size, :], vl_ref[start:start + size, :])
        start += size
    for g in range(group):
        acc = carry[g][1]
        o_ref[:, g * HEAD:(g + 1) * HEAD] = (acc[:, :HEAD] / acc[:, HEAD:]).astype(o_ref.dtype)


def _flash_call(qkv, kv_ctx, *, heads):
    b, t, n = qkv.shape
    tc = kv_ctx.shape[1]
    kvh = (n // HEAD - heads) // 2
    group = heads // kvh
    tq = _pick(t, 512, BF16_SUBLANES)
    tk = _pick(t, 1024, 2 * HEAD)
    assert t // tk <= 16
    body = functools.partial(_flash_kernel, tk=tk)
    return pl.pallas_call(
        body,
        grid=(b, kvh, t // tq),
        in_specs=[
            pl.BlockSpec((None, tq, group * HEAD), lambda bi, h, i: (bi, i, h)),
            pl.BlockSpec((None, tc, HEAD), lambda bi, h, i: (bi, 0, h)),
            pl.BlockSpec((None, tc, HEAD), lambda bi, h, i: (bi, 0, kvh + h)),
            pl.BlockSpec((None, t, HEAD), lambda bi, h, i: (bi, 0, heads + h)),
            pl.BlockSpec((None, t, HEAD), lambda bi, h, i: (bi, 0, heads + kvh + h)),
        ],
        out_specs=pl.BlockSpec((None, tq, group * HEAD), lambda bi, h, i: (bi, i, h)),
        out_shape=jax.ShapeDtypeStruct((b, t, heads * HEAD), BF16),
        compiler_params=_params(("arbitrary", "arbitrary", "arbitrary")),
        name="gqa_flash",
    )(qkv, kv_ctx, kv_ctx, qkv, qkv)


def _rope_tables(t):
    pos = jnp.arange(t, dtype=jnp.int32)
    rows = (pos // GRID_W).astype(F32)
    cols = (pos % GRID_W).astype(F32)
    axis_dim = HEAD // 2
    inv_freq = ROPE_THETA ** (-jnp.arange(0, axis_dim, 2, dtype=F32) / axis_dim)
    ang = jnp.concatenate([rows[:, None] * inv_freq, cols[:, None] * inv_freq], axis=-1)
    cos, sin = jnp.cos(ang), jnp.sin(ang)
    return jnp.concatenate([cos, cos], axis=-1), jnp.concatenate([-sin, sin], axis=-1)


def _split_mods(mods, b):
    d = mods.shape[1] // N_MOD
    lat = [mods[:b, i * d:(i + 1) * d].reshape(b, 1, d) for i in range(N_MOD)]
    ctx = [mods[b:b + 1, i * d:(i + 1) * d].reshape(1, 1, d) for i in range(N_MOD)]
    return lat, ctx


def kernel(x, c, ctx, c_ctx, ada_w, ada_b, norm_mix_pre, norm_mix_post, norm_ffn_pre, norm_ffn_post, hgrn_w_in, hgrn_lb_logits, hgrn_o_norm, hgrn_w_out, attn_w_qkv, attn_q_norm, attn_k_norm, attn_w_out, ffn_w_in, ffn_conv_w, ffn_conv_b, ffn_w_out):
    b, t, d = x.shape
    tc = ctx.shape[1]
    heads = d // HEAD
    assert ada_w.shape[0] == 2 and b + 1 <= ADA_ROWS

    cond = jnp.zeros((ADA_ROWS, d), F32).at[:b].set(c).at[b].set(c_ctx)
    mods = _ada_call(cond, ada_w, ada_b)
    x_lat = x.reshape(b * t, d)
    x_ctx = ctx.reshape(b * tc, d)

    (sh_ml, sc_ml, gt_ml, sh_fl, sc_fl, gt_fl), (sh_mc, sc_mc, gt_mc, sh_fc, sc_fc, gt_fc) = _split_mods(mods[0], b)
    lower_bound = jnp.cumsum(jax.nn.softmax(hgrn_lb_logits.astype(F32), axis=1), axis=1)[:, 0]
    hgrn_w_in, hgrn_w_out, attn_w_out = hgrn_w_in.astype(BF16), hgrn_w_out.astype(BF16), attn_w_out.astype(BF16)
    ffn_w_in, ffn_w_out = ffn_w_in.astype(BF16), ffn_w_out.astype(BF16)
    p_lat = _nmm_call(x_lat, norm_mix_pre[0], sh_ml, sc_ml, hgrn_w_in, name="hgrn_in_lat")
    p_ctx = _nmm_call(x_ctx, norm_mix_pre[0], sh_mc, sc_mc, hgrn_w_in, name="hgrn_in_ctx")
    y_ctx, y_lat = _gla_call(p_ctx.reshape(b, tc, 5 * d), p_lat.reshape(b, t, 5 * d), lower_bound, hgrn_o_norm[0])
    x_lat = _proj_res_call(y_lat.reshape(b * t, d), hgrn_w_out, x_lat, norm_mix_post[0], gt_ml, name="hgrn_out_lat")
    x_ctx = _proj_res_call(y_ctx.reshape(b * tc, d), hgrn_w_out, x_ctx, norm_mix_post[0], gt_mc, name="hgrn_out_ctx")
    a = _ffn_in_call(x_lat, norm_ffn_pre[0], sh_fl, sc_fl, ffn_w_in, ffn_conv_w[0], ffn_conv_b[0],
                     layer=0, seq_len=t, name="ffn0_in_lat")
    x_lat = _proj_res_call(a, ffn_w_out, x_lat, norm_ffn_post[0], gt_fl, layer=0, name="ffn0_out_lat")
    a = _ffn_in_call(x_ctx, norm_ffn_pre[0], sh_fc, sc_fc, ffn_w_in, ffn_conv_w[0], ffn_conv_b[0],
                     layer=0, seq_len=tc, name="ffn0_in_ctx")
    x_ctx = _proj_res_call(a, ffn_w_out, x_ctx, norm_ffn_post[0], gt_fc, layer=0, name="ffn0_out_ctx")

    (sh_ml, sc_ml, gt_ml, sh_fl, sc_fl, gt_fl), (sh_mc, sc_mc, _, _, _, _) = _split_mods(mods[1], b)
    n_qkv = attn_w_qkv.shape[2]
    kvh = (n_qkv // HEAD - heads) // 2
    perm = jnp.concatenate([jnp.arange(0, HEAD, 2), jnp.arange(1, HEAD, 2)])
    head_perm = (jnp.arange(heads + kvh)[:, None] * HEAD + perm[None, :]).reshape(-1)
    col_perm = jnp.concatenate([head_perm, jnp.arange((heads + kvh) * HEAD, n_qkv)])
    w_qkv = attn_w_qkv[:, :, col_perm].astype(BF16)
    head_gain = jnp.concatenate([
        jnp.tile(attn_q_norm[0][perm] * (LOG2_E * HEAD ** -0.5), heads),
        jnp.tile(attn_k_norm[0][perm], kvh),
        jnp.ones((kvh * HEAD,), F32),
    ]).reshape(1, n_qkv)
    cos, sin = _rope_tables(t)
    qk_cols = (heads + kvh) * HEAD
    q_cols = heads * HEAD
    qkv = _nmm_call(x_lat, norm_mix_pre[1], sh_ml, sc_ml, w_qkv, name="attn_qkv_lat")
    qkv = _qk_rope_call(qkv, head_gain[:, :qk_cols], cos, sin, name="attn_qk_rope_lat")
    kv_ctx = _nmm_call(x_ctx, norm_mix_pre[1], sh_mc, sc_mc, w_qkv, col0=q_cols, name="attn_kv_ctx")
    kv_ctx = _qk_rope_call(kv_ctx, head_gain[:, q_cols:qk_cols], jnp.ones((tc, HEAD), F32),
                           jnp.zeros((tc, HEAD), F32), name="attn_k_norm_ctx")
    qkv = qkv.reshape(b, t, n_qkv)
    kv_ctx = kv_ctx.reshape(b, tc, 2 * kvh * HEAD)
    o = _flash_call(qkv, kv_ctx, heads=heads)
    x_lat = _proj_res_call(o.reshape(b * t, d), attn_w_out, x_lat, norm_mix_post[1], gt_ml, name="attn_out_lat")
    a = _ffn_in_call(x_lat, norm_ffn_pre[1], sh_fl, sc_fl, ffn_w_in, ffn_conv_w[1], ffn_conv_b[1],
                     layer=1, seq_len=t, name="ffn1_in_lat")
    x_lat = _proj_res_call(a, ffn_w_out, x_lat, norm_ffn_post[1], gt_fl, layer=1, name="ffn1_out_lat")
    return x_lat.reshape(b, t, d)
```

```python
import functools
import math

import jax
import jax.numpy as jnp
from jax import lax
from jax.experimental import pallas as pl
from jax.experimental.pallas import tpu as pltpu

F32 = jnp.float32
BF16 = jnp.bfloat16

EPS = 1e-6
GRID_W = 64
ROPE_THETA = 10000.0
LOG2_E = math.log2(math.e)
N_MOD = 6
HEAD = 128
ROW_GROUP = 16
GLA_CHUNK = 128
GLA_RUN = 32
GLA_SAFE_LB = math.exp(-80.0 / (GLA_CHUNK // 2))
BF16_SUBLANES = 16
ADA_ROWS = 8
V7X_VMEM_BYTES = 64 * 1024 * 1024
VMEM_LIMIT = V7X_VMEM_BYTES - 6 * 1024 * 1024


def _pick(n, target, align):
    if n <= target:
        return n
    for t in range(target - target % align, 0, -align):
        if n % t == 0:
            return t
    return n


def _params(sem):
    return pltpu.CompilerParams(dimension_semantics=sem, vmem_limit_bytes=VMEM_LIMIT)


def _sigmoid(x):
    return 0.5 * jnp.tanh(0.5 * x) + 0.5


def _silu(x):
    return x * _sigmoid(x)


def _dot(a, b):
    return jnp.dot(a, b, preferred_element_type=F32)


def _dot_nt(a, b):
    return lax.dot_general(a, b, (((1,), (1,)), ((), ())), preferred_element_type=F32)


def _dot_tn(a, b):
    return lax.dot_general(a, b, (((0,), (0,)), ((), ())), preferred_element_type=F32)


def _ada_kernel(c_ref, w_ref, b_ref, o_ref):
    sc = _silu(c_ref[...]).astype(BF16)
    o_ref[...] = _dot(sc, w_ref[...].astype(BF16)) + b_ref[...]


def _ada_call(cond, ada_w, ada_b):
    depth, d, n = ada_w.shape
    tn = _pick(n, 1024, HEAD)
    return pl.pallas_call(
        _ada_kernel,
        grid=(depth, n // tn),
        in_specs=[
            pl.BlockSpec((ADA_ROWS, d), lambda l, j: (0, 0)),
            pl.BlockSpec((None, d, tn), lambda l, j: (l, 0, j)),
            pl.BlockSpec((None, 1, tn), lambda l, j: (l, 0, j)),
        ],
        out_specs=pl.BlockSpec((None, ADA_ROWS, tn), lambda l, j: (l, 0, j)),
        out_shape=jax.ShapeDtypeStruct((depth, ADA_ROWS, n), F32),
        compiler_params=_params(("arbitrary", "arbitrary")),
        name="ada_ln",
    )(cond, ada_w, ada_b.reshape(depth, 1, n))


def _norm_mod(x, gain, shift, scale):
    ms = jnp.mean(x * x, axis=-1, keepdims=True)
    return (x * lax.rsqrt(ms + EPS) * gain) * (1.0 + scale) + shift


def _norm_mod_rows(x_ref, n_rows, h_ref, h_row0, gain, shift, scale):
    mult = gain * (1.0 + scale)

    def body(r, carry):
        start = pl.multiple_of(r * ROW_GROUP, ROW_GROUP)
        x = x_ref[pl.ds(start, ROW_GROUP), :]
        ms = jnp.mean(x * x, axis=-1, keepdims=True)
        dst = pl.ds(pl.multiple_of(h_row0 + start, ROW_GROUP), ROW_GROUP)
        h_ref[dst, :] = (x * lax.rsqrt(ms + EPS) * mult + shift).astype(BF16)
        return carry

    groups = n_rows // ROW_GROUP
    lax.fori_loop(0, groups, body, 0, unroll=math.gcd(groups, 4))


def _split_rows(tm):
    half = tm // 2
    return half if half % ROW_GROUP == 0 else tm


def _nmm_kernel(x_ref, gain_ref, shift_ref, scale_ref, w_ref, o_ref, h_ref):
    j = pl.program_id(1)
    tm = x_ref.shape[0]
    lead = _split_rows(tm)

    @pl.when(j == 0)
    def _():
        gain, shift, scale = gain_ref[...], shift_ref[...], scale_ref[...]
        _norm_mod_rows(x_ref, lead, h_ref, 0, gain, shift, scale)
        if lead < tm:
            h_ref[lead:, :] = _norm_mod(x_ref[lead:, :], gain, shift, scale).astype(BF16)
        o_ref[:lead, :] = _dot(h_ref[:lead, :], w_ref[...]).astype(o_ref.dtype)
        if lead < tm:
            o_ref[lead:, :] = _dot(h_ref[lead:, :], w_ref[...]).astype(o_ref.dtype)

    @pl.when(j > 0)
    def _():
        o_ref[...] = _dot(h_ref[...], w_ref[...]).astype(o_ref.dtype)


def _nmm_call(x, gain, shift, scale, w, *, layer=0, col0=0, name):
    m, d = x.shape
    n = w.shape[2] - col0
    bm = shift.shape[0]
    rows_per_mod = m // bm
    tm = _pick(rows_per_mod, 1024, BF16_SUBLANES)
    tn = _pick(math.gcd(n, col0) if col0 else n, 1024, HEAD)
    mod_blocks = rows_per_mod // tm
    col_block0 = col0 // tn
    return pl.pallas_call(
        _nmm_kernel,
        grid=(m // tm, n // tn),
        in_specs=[
            pl.BlockSpec((tm, d), lambda i, j: (i, 0)),
            pl.BlockSpec((1, d), lambda i, j: (0, 0)),
            pl.BlockSpec((None, 1, d), lambda i, j: (i // mod_blocks, 0, 0)),
            pl.BlockSpec((None, 1, d), lambda i, j: (i // mod_blocks, 0, 0)),
            pl.BlockSpec((None, d, tn), lambda i, j: (layer, 0, col_block0 + j)),
        ],
        out_specs=pl.BlockSpec((tm, tn), lambda i, j: (i, j)),
        out_shape=jax.ShapeDtypeStruct((m, n), BF16),
        scratch_shapes=[pltpu.VMEM((tm, d), BF16)],
        compiler_params=_params(("arbitrary", "arbitrary")),
        name=name,
    )(x, gain.reshape(1, d), shift, scale, w)


def _qk_rope_kernel(x_ref, hg_ref, cos_ref, sin_ref, o_ref):
    cos = cos_ref[...]
    sin = sin_ref[...]
    mean_lanes = jnp.full((HEAD, HEAD), 1.0 / HEAD, BF16)
    for hh in range(x_ref.shape[1] // HEAD):
        cols = slice(hh * HEAD, (hh + 1) * HEAD)
        a = x_ref[:, cols].astype(F32)
        ms = _dot((a * a).astype(BF16), mean_lanes)
        a = a * lax.rsqrt(ms + EPS) * hg_ref[:, cols]
        a = a * cos + pltpu.roll(a, HEAD // 2, axis=1) * sin
        o_ref[:, cols] = a.astype(o_ref.dtype)


def _qk_rope_call(p, head_gain, cos, sin, *, name):
    m, n = p.shape
    nn = head_gain.shape[1]
    tm = _pick(math.gcd(m, cos.shape[0]), 2048, BF16_SUBLANES)
    tn = _pick(math.gcd(n, nn), 1024, HEAD)
    table_blocks = cos.shape[0] // tm
    return pl.pallas_call(
        _qk_rope_kernel,
        grid=(m // tm, nn // tn),
        in_specs=[
            pl.BlockSpec((tm, tn), lambda i, j: (i, j)),
            pl.BlockSpec((1, tn), lambda i, j: (0, j)),
            pl.BlockSpec((tm, HEAD), lambda i, j: (i % table_blocks, 0)),
            pl.BlockSpec((tm, HEAD), lambda i, j: (i % table_blocks, 0)),
        ],
        out_specs=pl.BlockSpec((tm, tn), lambda i, j: (i, j)),
        out_shape=jax.ShapeDtypeStruct((m, n), p.dtype),
        input_output_aliases={0: 0},
        compiler_params=_params(("arbitrary", "arbitrary")),
        name=name,
    )(p, head_gain, cos, sin)


def _residual(x, y, gain, gate):
    ms = jnp.mean(y * y, axis=-1, keepdims=True)
    return x + gate * (y * lax.rsqrt(ms + EPS) * gain)


def _proj_res_kernel(a_ref, w_ref, x_ref, gain_ref, gate_ref, o_ref):
    tm = o_ref.shape[0]
    lead = _split_rows(tm)
    gain, gate = gain_ref[...], gate_ref[...]
    parts = [slice(0, lead), slice(lead, tm)] if lead < tm else [slice(0, tm)]
    ys = [_dot(a_ref[r, :], w_ref[...]) for r in parts]
    for r, y in zip(parts, ys):
        o_ref[r, :] = _residual(x_ref[r, :], y, gain, gate)


def _proj_res_ktiled_kernel(a_ref, w_ref, x_ref, gain_ref, gate_ref, o_ref):
    k = pl.program_id(1)
    last = pl.num_programs(1) - 1
    tm = o_ref.shape[0]
    lead = _split_rows(tm)

    @pl.when(k == 0)
    def _():
        o_ref[...] = _dot(a_ref[...], w_ref[...])

    @pl.when(jnp.logical_and(k > 0, k < last))
    def _():
        o_ref[...] += _dot(a_ref[...], w_ref[...])

    @pl.when(k == last)
    def _():
        gain, gate = gain_ref[...], gate_ref[...]
        parts = [slice(0, lead), slice(lead, tm)] if lead < tm else [slice(0, tm)]
        ys = [o_ref[r, :] + _dot(a_ref[r, :], w_ref[...]) for r in parts]
        for r, y in zip(parts, ys):
            o_ref[r, :] = _residual(x_ref[r, :], y, gain, gate)


def _proj_res_call(a, w, x, gain, gate, *, layer=0, name):
    m, k = a.shape
    d = w.shape[2]
    bm = gate.shape[0]
    rows_per_mod = m // bm
    tk = k if k <= d else _pick(k, 512, 2 * HEAD)
    kb = k // tk
    tm = _pick(rows_per_mod, 512 if kb == 1 else 1024, BF16_SUBLANES)
    mod_blocks = rows_per_mod // tm
    return pl.pallas_call(
        _proj_res_kernel if kb == 1 else _proj_res_ktiled_kernel,
        grid=(m // tm, kb),
        in_specs=[
            pl.BlockSpec((tm, tk), lambda i, kk: (i, kk)),
            pl.BlockSpec((None, tk, d), lambda i, kk: (layer, kk, 0)),
            pl.BlockSpec((tm, d), lambda i, kk: (i, 0)),
            pl.BlockSpec((1, d), lambda i, kk: (0, 0)),
            pl.BlockSpec((None, 1, d), lambda i, kk: (i // mod_blocks, 0, 0)),
        ],
        out_specs=pl.BlockSpec((tm, d), lambda i, kk: (i, 0)),
        out_shape=jax.ShapeDtypeStruct((m, d), F32),
        compiler_params=_params(("arbitrary", "arbitrary")),
        name=name,
    )(a, w, x, gain.reshape(1, d), gate)


def _ffn_in_kernel(x_ref, xprev_ref, xnext_ref, gain_ref, shift_ref, scale_ref, wg_ref, wu_ref, cw_ref, cb_ref,
                   o_ref, h_ref, *, blocks_per_seq):
    i = pl.program_id(0)
    j = pl.program_id(1)
    tm = x_ref.shape[0]
    halo = BF16_SUBLANES
    lead = _split_rows(tm)

    def gated(gate, up):
        rows = gate.shape[0]
        g_prev = pltpu.roll(gate, 1, axis=0)[halo:halo + tm, :]
        g_next = pltpu.roll(gate, rows - 1, axis=0)[halo:halo + tm, :]
        conv = (g_prev * cw_ref[0:1, :] + gate[halo:halo + tm, :] * cw_ref[1:2, :] + g_next * cw_ref[2:3, :]
                + cb_ref[...])
        o_ref[...] = (_silu(conv) * up).astype(o_ref.dtype)

    @pl.when(j == 0)
    def _():
        gain, shift, scale = gain_ref[...], shift_ref[...], scale_ref[...]
        seq_block = i % blocks_per_seq
        h_prev = jnp.where(seq_block > 0, _norm_mod(xprev_ref[...], gain, shift, scale), 0.0)
        h_next = jnp.where(seq_block < blocks_per_seq - 1, _norm_mod(xnext_ref[...], gain, shift, scale), 0.0)
        zeros = jnp.zeros((halo - h_prev.shape[0], h_prev.shape[1]), F32)
        h_ref[0:halo, :] = jnp.concatenate([zeros, h_prev], axis=0).astype(BF16)
        h_ref[halo + tm:, :] = jnp.concatenate([h_next, zeros], axis=0).astype(BF16)
        _norm_mod_rows(x_ref, lead, h_ref, halo, gain, shift, scale)
        if lead == tm:
            gated(_dot(h_ref[...], wg_ref[...]), _dot(h_ref[halo:halo + tm, :], wu_ref[...]))
        else:
            cut = halo + lead
            h_ref[cut:halo + tm, :] = _norm_mod(x_ref[lead:, :], gain, shift, scale).astype(BF16)
            gate = jnp.concatenate([_dot(h_ref[:cut, :], wg_ref[...]), _dot(h_ref[cut:, :], wg_ref[...])], axis=0)
            up = jnp.concatenate([_dot(h_ref[halo:cut, :], wu_ref[...]),
                                  _dot(h_ref[cut:halo + tm, :], wu_ref[...])], axis=0)
            gated(gate, up)

    @pl.when(j > 0)
    def _():
        gated(_dot(h_ref[...], wg_ref[...]), _dot(h_ref[halo:halo + tm, :], wu_ref[...]))


def _ffn_in_call(x, gain, shift, scale, w, conv_w, conv_b, *, layer, seq_len, name):
    m, d = x.shape
    f = w.shape[2] // 2
    bm = shift.shape[0]
    rows_per_mod = m // bm
    tm = _pick(seq_len, 1024, BF16_SUBLANES)
    tn = _pick(f, 512, HEAD)
    assert rows_per_mod % tm == 0
    mod_blocks = rows_per_mod // tm
    nb = f // tn
    xh = 8
    tiles_per_tm = tm // xh
    last_tile = m // xh - 1
    body = functools.partial(_ffn_in_kernel, blocks_per_seq=seq_len // tm)
    return pl.pallas_call(
        body,
        grid=(m // tm, nb),
        in_specs=[
            pl.BlockSpec((tm, d), lambda i, j: (i, 0)),
            pl.BlockSpec((xh, d), lambda i, j: (jnp.maximum(i * tiles_per_tm - 1, 0), 0)),
            pl.BlockSpec((xh, d), lambda i, j: (jnp.minimum((i + 1) * tiles_per_tm, last_tile), 0)),
            pl.BlockSpec((1, d), lambda i, j: (0, 0)),
            pl.BlockSpec((None, 1, d), lambda i, j: (i // mod_blocks, 0, 0)),
            pl.BlockSpec((None, 1, d), lambda i, j: (i // mod_blocks, 0, 0)),
            pl.BlockSpec((None, d, tn), lambda i, j: (layer, 0, j)),
            pl.BlockSpec((None, d, tn), lambda i, j: (layer, 0, nb + j)),
            pl.BlockSpec((3, tn), lambda i, j: (0, j)),
            pl.BlockSpec((1, tn), lambda i, j: (0, j)),
        ],
        out_specs=pl.BlockSpec((tm, tn), lambda i, j: (i, j)),
        out_shape=jax.ShapeDtypeStruct((m, f), BF16),
        scratch_shapes=[pltpu.VMEM((tm + 2 * BF16_SUBLANES, d), BF16)],
        compiler_params=_params(("arbitrary", "arbitrary")),
        name=name,
    )(x, x, x, gain.reshape(1, d), shift, scale, w, w, conv_w, conv_b.reshape(1, f))


def _gla_gates(f_raw, lb, within):
    half = 0.5 * (1.0 - lb)
    ht = half * jnp.tanh(0.5 * f_raw.astype(F32))
    lf = jnp.log((1.0 - half) + ht)
    ones = jnp.where(within, 1.0, 0.0).astype(BF16)
    lf_hi = lf.astype(BF16)
    lf_lo = (lf - lf_hi.astype(F32)).astype(BF16)
    return half - ht, _dot(ones, lf_hi) + _dot(ones, lf_lo)


def _gla_pairs(qs, kk, b, forward, within):
    c = qs.shape[0]
    a_row = c // 2 - 1 if forward else c // 2
    e_row = c - 1 if forward else 0
    anchor = b[a_row:a_row + 1, :]
    b_end = b[e_row:e_row + 1, :]
    qa = qs * jnp.exp(b - anchor)
    ka = kk * jnp.exp(anchor - b)
    scores = jnp.where(within, _dot_nt(qa.astype(BF16), ka.astype(BF16)), 0.0).astype(BF16)
    return (scores, (qa * jnp.exp(anchor)).astype(BF16), (ka * jnp.exp(b_end - anchor)).astype(BF16),
            jnp.exp(b_end))


def _gla_pairs_exact(qs, kk, b, forward, within, tmp):
    tb_ref, tq_ref, ts_ref = tmp
    c = qs.shape[0]
    b_end = b[c - 1:c, :] if forward else b[0:1, :]
    tb_ref[...] = b
    tq_ref[...] = qs
    ones = jnp.ones((8, qs.shape[1]), BF16)

    def row(t, carry):
        rel = jnp.minimum(tb_ref[pl.ds(t, 1), :] - b, 0.0)
        e = (tq_ref[pl.ds(t, 1), :] * kk) * jnp.exp(rel)
        ts_ref[pl.ds(t, 1), :] = _dot_nt(ones, e.astype(BF16))[0:1, :]
        return carry

    lax.fori_loop(0, c, row, 0)
    scores = jnp.where(within, ts_ref[...], 0.0).astype(BF16)
    return scores, (qs * jnp.exp(b)).astype(BF16), (kk * jnp.exp(b_end - b)).astype(BF16), jnp.exp(b_end)


def _gla_readout(o, g_raw, o_gain):
    ms = jnp.mean(o * o, axis=-1, keepdims=True)
    y = o * lax.rsqrt(ms + EPS) * o_gain
    return (y * _silu(g_raw.astype(F32))).astype(BF16)


def _gla_kernel(qc, ffc, fbc, vc, gc, ql, ffl, fbl, vl, gl, lb_ref, og_ref, yc_ref, yl_ref,
                o_ref, qb_ref, ke_ref, dec_ref, st_ref, tb_ref, tq_ref, ts_ref):
    c = GLA_CHUNK
    tc = qc.shape[0]
    n_ctx = tc // c
    n_lat = ql.shape[0] // c
    lb = (lb_ref[0], lb_ref[1])
    row = lax.broadcasted_iota(jnp.int32, (c, c), 0)
    col = lax.broadcasted_iota(jnp.int32, (c, c), 1)
    within = (col <= row, col >= row)

    def rows(start):
        return pl.ds(start if isinstance(start, int) else pl.multiple_of(start, c), c)

    def local(refs, src0, dst0, chunk0, n, exact):
        q, ff, fb, v = refs
        src = [rows(src0 + j * c) for j in range(n)]
        dst = [rows(dst0 + j * c) for j in range(n)]
        qs = [_silu(q[src[j], :].astype(F32)) * (HEAD ** -0.5) for j in range(n)]
        gates = [[_gla_gates(f[src[j], :], lb[d], within[d]) for d, f in enumerate((ff, fb))] for j in range(n)]
        scores = []
        for j in range(n):
            per_dir = []
            for d in range(2):
                kk, b = gates[j][d]
                if exact:
                    sc, qb, ke, dec = _gla_pairs_exact(qs[j], kk, b, d == 0, within[d], (tb_ref, tq_ref, ts_ref))
                else:
                    sc, qb, ke, dec = _gla_pairs(qs[j], kk, b, d == 0, within[d])
                qb_ref[d, dst[j], :] = qb
                ke_ref[d, dst[j], :] = ke
                dec_ref[d, chunk0 + j] = dec
                per_dir.append(sc)
            scores.append(per_dir)
        for j in range(n):
            vj = v[src[j], :]
            o_ref[dst[j], :] = _dot(scores[j][0], vj) + _dot(scores[j][1], vj)

    def phase1(exact):
        run = 1 if exact else math.gcd(n_lat, GLA_RUN)
        run_ctx = math.gcd(n_ctx, run)
        for i in range(0, n_ctx, run_ctx):
            local((qc, ffc, fbc, vc), i * c, i * c, i, run_ctx, exact)

        def body(i, carry):
            src0 = pl.multiple_of(i * (run * c), c)
            local((ql, ffl, fbl, vl), src0, tc + src0, n_ctx + i * run, run, exact)
            return carry

        lax.fori_loop(0, n_lat // run, body, 0)

    safe = jnp.min(jnp.minimum(lb[0], lb[1])) >= GLA_SAFE_LB

    @pl.when(safe)
    def _():
        phase1(False)

    @pl.when(jnp.logical_not(safe))
    def _():
        phase1(True)

    st_ref[...] = jnp.zeros_like(st_ref)

    o_gain = og_ref[...]

    def steps(v, first, n, n_seg, row0, chunk0, finish=None):
        order = [[first + s for s in range(n)], [n_seg - 1 - first - s for s in range(n)]]
        src = [[rows(j * c) for j in order[d]] for d in range(2)]
        dst = [[rows(row0 + j * c) for j in order[d]] for d in range(2)]
        kv = [[_dot_tn(v[src[d][s], :], ke_ref[d, dst[d][s], :]) for s in range(n)] for d in range(2)]
        for d in range(2):
            st = st_ref[d]
            for s in range(n):
                o = o_ref[dst[d][s], :] + _dot_nt(qb_ref[d, dst[d][s], :], st.astype(BF16))
                if finish is None:
                    o_ref[dst[d][s], :] = o
                else:
                    g, y = finish
                    y[src[d][s], :] = _gla_readout(o, g[src[d][s], :], o_gain)
                st = st * dec_ref[d, chunk0 + order[d][s]] + kv[d][s]
            st_ref[d] = st

    steps(vc, 0, n_ctx, n_ctx, 0, 0)
    for i in range(n_ctx):
        r = slice(i * c, (i + 1) * c)
        yc_ref[r, :] = _gla_readout(o_ref[r, :], gc[r, :], o_gain)

    run = math.gcd(n_lat, GLA_RUN)
    n_runs = n_lat // run
    meet = (n_runs + 1) // 2

    def steps_body(i, carry):
        steps(vl, i * run, run, n_lat, tc, n_ctx)
        return carry

    def last_steps_body(i, carry):
        steps(vl, i * run, run, n_lat, tc, n_ctx, finish=(gl, yl_ref))
        return carry

    lax.fori_loop(0, meet, steps_body, 0)
    if n_runs % 2:
        for j in range((n_runs // 2) * run, (n_runs // 2 + 1) * run):
            yl_ref[rows(j * c), :] = _gla_readout(o_ref[rows(tc + j * c), :], gl[rows(j * c), :], o_gain)
    lax.fori_loop(meet, n_runs, last_steps_body, 0)


def _gla_call(p_ctx, p_lat, lower_bound, o_gain):
    b, tc, d5 = p_ctx.shape
    t = p_lat.shape[1]
    d = d5 // 5
    heads = d // HEAD
    assert tc % GLA_CHUNK == 0 and t % GLA_CHUNK == 0

    def col(tt, part):
        return pl.BlockSpec((None, tt, HEAD), lambda bi, h: (bi, 0, part * heads + h))

    out_spec = lambda tt: pl.BlockSpec((None, tt, HEAD), lambda bi, h: (bi, 0, h))
    return pl.pallas_call(
        _gla_kernel,
        grid=(b, heads),
        in_specs=[col(tc, p) for p in range(5)] + [col(t, p) for p in range(5)] + [
            pl.BlockSpec((2, None, 1, HEAD), lambda bi, h: (0, h, 0, 0)),
            pl.BlockSpec((1, HEAD), lambda bi, h: (0, 0)),
        ],
        out_specs=[out_spec(tc), out_spec(t)],
        out_shape=[jax.ShapeDtypeStruct((b, tc, d), BF16), jax.ShapeDtypeStruct((b, t, d), BF16)],
        scratch_shapes=[
            pltpu.VMEM((tc + t, HEAD), F32),
            pltpu.VMEM((2, tc + t, HEAD), BF16),
            pltpu.VMEM((2, tc + t, HEAD), BF16),
            pltpu.VMEM((2, (tc + t) // GLA_CHUNK, 1, HEAD), F32),
            pltpu.VMEM((2, HEAD, HEAD), F32),
            pltpu.VMEM((GLA_CHUNK, HEAD), F32),
            pltpu.VMEM((GLA_CHUNK, HEAD), F32),
            pltpu.VMEM((GLA_CHUNK, GLA_CHUNK), F32),
        ],
        compiler_params=_params(("arbitrary", "arbitrary")),
        name="hgrn_scan",
    )(*([p_ctx] * 5), *([p_lat] * 5), lower_bound.reshape(2, heads, 1, HEAD), o_gain.reshape(1, HEAD))


def _flash_kernel(q_ref, kc_ref, vc_ref, kl_ref, vl_ref, o_ref, *, tk):
    tq = q_ref.shape[0]
    group = q_ref.shape[1] // HEAD
    qs = [q_ref[:, g * HEAD:(g + 1) * HEAD] for g in range(group)]

    def attend(carry, k, v):
        v_ext = jnp.concatenate([v, jnp.ones_like(v)], axis=1)
        out = []
        for g in range(group):
            m, acc = carry[g]
            s = _dot_nt(qs[g], k)
            m_new = jnp.maximum(m, jnp.max(s, axis=-1, keepdims=True))
            p = jnp.exp2(s - m_new).astype(BF16)
            out.append((m_new, jnp.exp2(m - m_new) * acc + _dot(p, v_ext)))
        return tuple(out)

    carry = tuple((jnp.full((tq, 1), -jnp.inf, F32), jnp.zeros((tq, 2 * HEAD), F32)) for _ in range(group))
    for start in range(0, kl_ref.shape[0], tk):
        carry = attend(carry, kl_ref[start:start + tk, :], vl_ref[start:start + tk, :])
    carry = attend(carry, kc_ref[...], vc_ref[...])
    for g in range(group):
        acc = carry[g][1]
        o_ref[:, g * HEAD:(g + 1) * HEAD] = (acc[:, :HEAD] / acc[:, HEAD:]).astype(o_ref.dtype)


def _flash_call(qkv, kv_ctx, *, heads):
    b, t, n = qkv.shape
    tc = kv_ctx.shape[1]
    kvh = (n // HEAD - heads) // 2
    group = heads // kvh
    tq = _pick(t, 512, BF16_SUBLANES)
    tk = _pick(t, 1024, 2 * HEAD)
    assert t // tk <= 16
    body = functools.partial(_flash_kernel, tk=tk)
    return pl.pallas_call(
        body,
        grid=(b, kvh, t // tq),
        in_specs=[
            pl.BlockSpec((None, tq, group * HEAD), lambda bi, h, i: (bi, i, h)),
            pl.BlockSpec((None, tc, HEAD), lambda bi, h, i: (bi, 0, h)),
            pl.BlockSpec((None, tc, HEAD), lambda bi, h, i: (bi, 0, kvh + h)),
            pl.BlockSpec((None, t, HEAD), lambda bi, h, i: (bi, 0, heads + h)),
            pl.BlockSpec((None, t, HEAD), lambda bi, h, i: (bi, 0, heads + kvh + h)),
        ],
        out_specs=pl.BlockSpec((None, tq, group * HEAD), lambda bi, h, i: (bi, i, h)),
        out_shape=jax.ShapeDtypeStruct((b, t, heads * HEAD), BF16),
        compiler_params=_params(("arbitrary", "arbitrary", "arbitrary")),
        name="gqa_flash",
    )(qkv, kv_ctx, kv_ctx, qkv, qkv)


def _rope_tables(t):
    pos = jnp.arange(t, dtype=jnp.int32)
    rows = (pos // GRID_W).astype(F32)
    cols = (pos % GRID_W).astype(F32)
    axis_dim = HEAD // 2
    inv_freq = ROPE_THETA ** (-jnp.arange(0, axis_dim, 2, dtype=F32) / axis_dim)
    ang = jnp.concatenate([rows[:, None] * inv_freq, cols[:, None] * inv_freq], axis=-1)
    cos, sin = jnp.cos(ang), jnp.sin(ang)
    return jnp.concatenate([cos, cos], axis=-1), jnp.concatenate([-sin, sin], axis=-1)


def _split_mods(mods, b):
    d = mods.shape[1] // N_MOD
    lat = [mods[:b, i * d:(i + 1) * d].reshape(b, 1, d) for i in range(N_MOD)]
    ctx = [mods[b:b + 1, i * d:(i + 1) * d].reshape(1, 1, d) for i in range(N_MOD)]
    return lat, ctx


def kernel(x, c, ctx, c_ctx, ada_w, ada_b, norm_mix_pre, norm_mix_post, norm_ffn_pre, norm_ffn_post, hgrn_w_in, hgrn_lb_logits, hgrn_o_norm, hgrn_w_out, attn_w_qkv, attn_q_norm, attn_k_norm, attn_w_out, ffn_w_in, ffn_conv_w, ffn_conv_b, ffn_w_out):
    b, t, d = x.shape
    tc = ctx.shape[1]
    heads = d // HEAD
    assert ada_w.shape[0] == 2 and b + 1 <= ADA_ROWS

    cond = jnp.zeros((ADA_ROWS, d), F32).at[:b].set(c).at[b].set(c_ctx)
    mods = _ada_call(cond, ada_w, ada_b)
    x_lat = x.reshape(b * t, d)
    x_ctx = ctx.reshape(b * tc, d)

    (sh_ml, sc_ml, gt_ml, sh_fl, sc_fl, gt_fl), (sh_mc, sc_mc, gt_mc, sh_fc, sc_fc, gt_fc) = _split_mods(mods[0], b)
    lower_bound = jnp.cumsum(jax.nn.softmax(hgrn_lb_logits.astype(F32), axis=1), axis=1)[:, 0]
    hgrn_w_in, hgrn_w_out, attn_w_out = hgrn_w_in.astype(BF16), hgrn_w_out.astype(BF16), attn_w_out.astype(BF16)
    ffn_w_in, ffn_w_out = ffn_w_in.astype(BF16), ffn_w_out.astype(BF16)
    p_lat = _nmm_call(x_lat, norm_mix_pre[0], sh_ml, sc_ml, hgrn_w_in, name="hgrn_in_lat")
    p_ctx = _nmm_call(x_ctx, norm_mix_pre[0], sh_mc, sc_mc, hgrn_w_in, name="hgrn_in_ctx")
    y_ctx, y_lat = _gla_call(p_ctx.reshape(b, tc, 5 * d), p_lat.reshape(b, t, 5 * d), lower_bound, hgrn_o_norm[0])
    x_lat = _proj_res_call(y_lat.reshape(b * t, d), hgrn_w_out, x_lat, norm_mix_post[0], gt_ml, name="hgrn_out_lat")
    x_ctx = _proj_res_call(y_ctx.reshape(b * tc, d), hgrn_w_out, x_ctx, norm_mix_post[0], gt_mc, name="hgrn_out_ctx")
    a = _ffn_in_call(x_lat, norm_ffn_pre[0], sh_fl, sc_fl, ffn_w_in, ffn_conv_w[0], ffn_conv_b[0],
                     layer=0, seq_len=t, name="ffn0_in_lat")
    x_lat = _proj_res_call(a, ffn_w_out, x_lat, norm_ffn_post[0], gt_fl, layer=0, name="ffn0_out_lat")
    a = _ffn_in_call(x_ctx, norm_ffn_pre[0], sh_fc, sc_fc, ffn_w_in, ffn_conv_w[0], ffn_conv_b[0],
                     layer=0, seq_len=tc, name="ffn0_in_ctx")
    x_ctx = _proj_res_call(a, ffn_w_out, x_ctx, norm_ffn_post[0], gt_fc, layer=0, name="ffn0_out_ctx")

    (sh_ml, sc_ml, gt_ml, sh_fl, sc_fl, gt_fl), (sh_mc, sc_mc, _, _, _, _) = _split_mods(mods[1], b)
    n_qkv = attn_w_qkv.shape[2]
    kvh = (n_qkv // HEAD - heads) // 2
    perm = jnp.concatenate([jnp.arange(0, HEAD, 2), jnp.arange(1, HEAD, 2)])
    head_perm = (jnp.arange(heads + kvh)[:, None] * HEAD + perm[None, :]).reshape(-1)
    col_perm = jnp.concatenate([head_perm, jnp.arange((heads + kvh) * HEAD, n_qkv)])
    w_qkv = attn_w_qkv[:, :, col_perm].astype(BF16)
    head_gain = jnp.concatenate([
        jnp.tile(attn_q_norm[0][perm] * (LOG2_E * HEAD ** -0.5), heads),
        jnp.tile(attn_k_norm[0][perm], kvh),
        jnp.ones((kvh * HEAD,), F32),
    ]).reshape(1, n_qkv)
    cos, sin = _rope_tables(t)
    qk_cols = (heads + kvh) * HEAD
    q_cols = heads * HEAD
    qkv = _nmm_call(x_lat, norm_mix_pre[1], sh_ml, sc_ml, w_qkv, name="attn_qkv_lat")
    qkv = _qk_rope_call(qkv, head_gain[:, :qk_cols], cos, sin, name="attn_qk_rope_lat")
    kv_ctx = _nmm_call(x_ctx, norm_mix_pre[1], sh_mc, sc_mc, w_qkv, col0=q_cols, name="attn_kv_ctx")
    kv_ctx = _qk_rope_call(kv_ctx, head_gain[:, q_cols:qk_cols], jnp.ones((tc, HEAD), F32),
                           jnp.zeros((tc, HEAD), F32), name="attn_k_norm_ctx")
    qkv = qkv.reshape(b, t, n_qkv)
    kv_ctx = kv_ctx.reshape(b, tc, 2 * kvh * HEAD)
    o = _flash_call(qkv, kv_ctx, heads=heads)
    x_lat = _proj_res_call(o.reshape(b * t, d), attn_w_out, x_lat, norm_mix_post[1], gt_ml, name="attn_out_lat")
    a = _ffn_in_call(x_lat, norm_ffn_pre[1], sh_fl, sc_fl, ffn_w_in, ffn_conv_w[1], ffn_conv_b[1],
                     layer=1, seq_len=t, name="ffn1_in_lat")
    x_lat = _proj_res_call(a, ffn_w_out, x_lat, norm_ffn_post[1], gt_fl, layer=1, name="ffn1_out_lat")
    return x_lat.reshape(b, t, d)
```

```python
import functools
import math

import jax
import jax.numpy as jnp
from jax import lax
from jax.experimental import pallas as pl
from jax.experimental.pallas import tpu as pltpu

F32 = jnp.float32
BF16 = jnp.bfloat16

EPS = 1e-6
GRID_W = 64
ROPE_THETA = 10000.0
LOG2_E = math.log2(math.e)
N_MOD = 6
HEAD = 128
ROW_GROUP = 16
GLA_CHUNK = 128
GLA_RUN = 32
GLA_SAFE_LB = math.exp(-80.0 / (GLA_CHUNK // 2))
BF16_SUBLANES = 16
ADA_ROWS = 8
V7X_VMEM_BYTES = 64 * 1024 * 1024
VMEM_LIMIT = V7X_VMEM_BYTES - 6 * 1024 * 1024


def _pick(n, target, align):
    if n <= target:
        return n
    for t in range(target - target % align, 0, -align):
        if n % t == 0:
            return t
    return n


def _params(sem):
    return pltpu.CompilerParams(dimension_semantics=sem, vmem_limit_bytes=VMEM_LIMIT)


def _sigmoid(x):
    return 0.5 * jnp.tanh(0.5 * x) + 0.5


def _silu(x):
    return x * _sigmoid(x)


def _dot(a, b):
    return jnp.dot(a, b, preferred_element_type=F32)


def _dot_nt(a, b):
    return lax.dot_general(a, b, (((1,), (1,)), ((), ())), preferred_element_type=F32)


def _dot_tn(a, b):
    return lax.dot_general(a, b, (((0,), (0,)), ((), ())), preferred_element_type=F32)


def _ada_kernel(c_ref, w_ref, b_ref, o_ref):
    sc = _silu(c_ref[...]).astype(BF16)
    o_ref[...] = _dot(sc, w_ref[...].astype(BF16)) + b_ref[...]


def _ada_call(cond, ada_w, ada_b):
    depth, d, n = ada_w.shape
    tn = _pick(n, 1024, HEAD)
    return pl.pallas_call(
        _ada_kernel,
        grid=(depth, n // tn),
        in_specs=[
            pl.BlockSpec((ADA_ROWS, d), lambda l, j: (0, 0)),
            pl.BlockSpec((None, d, tn), lambda l, j: (l, 0, j)),
            pl.BlockSpec((None, 1, tn), lambda l, j: (l, 0, j)),
        ],
        out_specs=pl.BlockSpec((None, ADA_ROWS, tn), lambda l, j: (l, 0, j)),
        out_shape=jax.ShapeDtypeStruct((depth, ADA_ROWS, n), F32),
        compiler_params=_params(("arbitrary", "arbitrary")),
        name="ada_ln",
    )(cond, ada_w, ada_b.reshape(depth, 1, n))


def _norm_mod(x, gain, shift, scale):
    ms = jnp.mean(x * x, axis=-1, keepdims=True)
    return (x * lax.rsqrt(ms + EPS) * gain) * (1.0 + scale) + shift


def _norm_mod_rows(x_ref, n_rows, h_ref, h_row0, gain, shift, scale):
    mult = gain * (1.0 + scale)

    def body(r, carry):
        start = pl.multiple_of(r * ROW_GROUP, ROW_GROUP)
        x = x_ref[pl.ds(start, ROW_GROUP), :]
        ms = jnp.mean(x * x, axis=-1, keepdims=True)
        dst = pl.ds(pl.multiple_of(h_row0 + start, ROW_GROUP), ROW_GROUP)
        h_ref[dst, :] = (x * lax.rsqrt(ms + EPS) * mult + shift).astype(BF16)
        return carry

    groups = n_rows // ROW_GROUP
    lax.fori_loop(0, groups, body, 0, unroll=math.gcd(groups, 4))


def _split_rows(tm):
    half = tm // 2
    return half if half % ROW_GROUP == 0 else tm


def _nmm_kernel(x_ref, gain_ref, shift_ref, scale_ref, w_ref, o_ref, h_ref):
    j = pl.program_id(1)
    tm = x_ref.shape[0]
    lead = _split_rows(tm)

    @pl.when(j == 0)
    def _():
        gain, shift, scale = gain_ref[...], shift_ref[...], scale_ref[...]
        _norm_mod_rows(x_ref, lead, h_ref, 0, gain, shift, scale)
        if lead < tm:
            h_ref[lead:, :] = _norm_mod(x_ref[lead:, :], gain, shift, scale).astype(BF16)
        o_ref[:lead, :] = _dot(h_ref[:lead, :], w_ref[...]).astype(o_ref.dtype)
        if lead < tm:
            o_ref[lead:, :] = _dot(h_ref[lead:, :], w_ref[...]).astype(o_ref.dtype)

    @pl.when(j > 0)
    def _():
        o_ref[...] = _dot(h_ref[...], w_ref[...]).astype(o_ref.dtype)


def _nmm_call(x, gain, shift, scale, w, *, layer=0, col0=0, name):
    m, d = x.shape
    n = w.shape[2] - col0
    bm = shift.shape[0]
    rows_per_mod = m // bm
    tm = _pick(rows_per_mod, 1024, BF16_SUBLANES)
    tn = _pick(math.gcd(n, col0) if col0 else n, 2048, HEAD)
    mod_blocks = rows_per_mod // tm
    col_block0 = col0 // tn
    return pl.pallas_call(
        _nmm_kernel,
        grid=(m // tm, n // tn),
        in_specs=[
            pl.BlockSpec((tm, d), lambda i, j: (i, 0)),
            pl.BlockSpec((1, d), lambda i, j: (0, 0)),
            pl.BlockSpec((None, 1, d), lambda i, j: (i // mod_blocks, 0, 0)),
            pl.BlockSpec((None, 1, d), lambda i, j: (i // mod_blocks, 0, 0)),
            pl.BlockSpec((None, d, tn), lambda i, j: (layer, 0, col_block0 + j)),
        ],
        out_specs=pl.BlockSpec((tm, tn), lambda i, j: (i, j)),
        out_shape=jax.ShapeDtypeStruct((m, n), BF16),
        scratch_shapes=[pltpu.VMEM((tm, d), BF16)],
        compiler_params=_params(("arbitrary", "arbitrary")),
        name=name,
    )(x, gain.reshape(1, d), shift, scale, w)


def _qk_rope_kernel(x_ref, hg_ref, cos_ref, sin_ref, o_ref):
    cos = cos_ref[...]
    sin = sin_ref[...]
    mean_lanes = jnp.full((HEAD, HEAD), 1.0 / HEAD, BF16)
    for hh in range(x_ref.shape[1] // HEAD):
        cols = slice(hh * HEAD, (hh + 1) * HEAD)
        a = x_ref[:, cols].astype(F32)
        ms = _dot((a * a).astype(BF16), mean_lanes)
        a = a * lax.rsqrt(ms + EPS) * hg_ref[:, cols]
        a = a * cos + pltpu.roll(a, HEAD // 2, axis=1) * sin
        o_ref[:, cols] = a.astype(o_ref.dtype)


def _qk_rope_call(p, head_gain, cos, sin, *, name):
    m, n = p.shape
    nn = head_gain.shape[1]
    tm = _pick(math.gcd(m, cos.shape[0]), 2048, BF16_SUBLANES)
    tn = _pick(math.gcd(n, nn), 1024, HEAD)
    table_blocks = cos.shape[0] // tm
    return pl.pallas_call(
        _qk_rope_kernel,
        grid=(m // tm, nn // tn),
        in_specs=[
            pl.BlockSpec((tm, tn), lambda i, j: (i, j)),
            pl.BlockSpec((1, tn), lambda i, j: (0, j)),
            pl.BlockSpec((tm, HEAD), lambda i, j: (i % table_blocks, 0)),
            pl.BlockSpec((tm, HEAD), lambda i, j: (i % table_blocks, 0)),
        ],
        out_specs=pl.BlockSpec((tm, tn), lambda i, j: (i, j)),
        out_shape=jax.ShapeDtypeStruct((m, n), p.dtype),
        input_output_aliases={0: 0},
        compiler_params=_params(("arbitrary", "arbitrary")),
        name=name,
    )(p, head_gain, cos, sin)


def _residual(x, y, gain, gate):
    ms = jnp.mean(y * y, axis=-1, keepdims=True)
    return x + gate * (y * lax.rsqrt(ms + EPS) * gain)


def _proj_res_kernel(a_ref, w_ref, x_ref, gain_ref, gate_ref, o_ref):
    y = _dot(a_ref[...], w_ref[...])
    o_ref[...] = _residual(x_ref[...], y, gain_ref[...], gate_ref[...])


def _proj_res_ktiled_kernel(a_ref, w_ref, x_ref, gain_ref, gate_ref, o_ref):
    k = pl.program_id(1)
    last = pl.num_programs(1) - 1
    tm = o_ref.shape[0]
    lead = _split_rows(tm)

    @pl.when(k == 0)
    def _():
        o_ref[...] = _dot(a_ref[...], w_ref[...])

    @pl.when(jnp.logical_and(k > 0, k < last))
    def _():
        o_ref[...] += _dot(a_ref[...], w_ref[...])

    @pl.when(k == last)
    def _():
        gain, gate = gain_ref[...], gate_ref[...]
        parts = [slice(0, lead), slice(lead, tm)] if lead < tm else [slice(0, tm)]
        ys = [o_ref[r, :] + _dot(a_ref[r, :], w_ref[...]) for r in parts]
        for r, y in zip(parts, ys):
            o_ref[r, :] = _residual(x_ref[r, :], y, gain, gate)


def _proj_res_call(a, w, x, gain, gate, *, layer=0, name):
    m, k = a.shape
    d = w.shape[2]
    bm = gate.shape[0]
    rows_per_mod = m // bm
    tk = k if k <= d else _pick(k, 512, 2 * HEAD)
    kb = k // tk
    tm = _pick(rows_per_mod, 512 if kb == 1 else 1024, BF16_SUBLANES)
    mod_blocks = rows_per_mod // tm
    return pl.pallas_call(
        _proj_res_kernel if kb == 1 else _proj_res_ktiled_kernel,
        grid=(m // tm, kb),
        in_specs=[
            pl.BlockSpec((tm, tk), lambda i, kk: (i, kk)),
            pl.BlockSpec((None, tk, d), lambda i, kk: (layer, kk, 0)),
            pl.BlockSpec((tm, d), lambda i, kk: (i, 0)),
            pl.BlockSpec((1, d), lambda i, kk: (0, 0)),
            pl.BlockSpec((None, 1, d), lambda i, kk: (i // mod_blocks, 0, 0)),
        ],
        out_specs=pl.BlockSpec((tm, d), lambda i, kk: (i, 0)),
        out_shape=jax.ShapeDtypeStruct((m, d), F32),
        compiler_params=_params(("arbitrary", "arbitrary")),
        name=name,
    )(a, w, x, gain.reshape(1, d), gate)


def _ffn_in_kernel(x_ref, xprev_ref, xnext_ref, gain_ref, shift_ref, scale_ref, wg_ref, wu_ref, cw_ref, cb_ref,
                   o_ref, h_ref, *, blocks_per_seq):
    i = pl.program_id(0)
    j = pl.program_id(1)
    tm = x_ref.shape[0]
    halo = BF16_SUBLANES
    lead = _split_rows(tm)

    def gated(gate, up):
        rows = gate.shape[0]
        g_prev = pltpu.roll(gate, 1, axis=0)[halo:halo + tm, :]
        g_next = pltpu.roll(gate, rows - 1, axis=0)[halo:halo + tm, :]
        conv = (g_prev * cw_ref[0:1, :] + gate[halo:halo + tm, :] * cw_ref[1:2, :] + g_next * cw_ref[2:3, :]
                + cb_ref[...])
        o_ref[...] = (_silu(conv) * up).astype(o_ref.dtype)

    @pl.when(j == 0)
    def _():
        gain, shift, scale = gain_ref[...], shift_ref[...], scale_ref[...]
        seq_block = i % blocks_per_seq
        h_prev = jnp.where(seq_block > 0, _norm_mod(xprev_ref[...], gain, shift, scale), 0.0)
        h_next = jnp.where(seq_block < blocks_per_seq - 1, _norm_mod(xnext_ref[...], gain, shift, scale), 0.0)
        zeros = jnp.zeros((halo - h_prev.shape[0], h_prev.shape[1]), F32)
        h_ref[0:halo, :] = jnp.concatenate([zeros, h_prev], axis=0).astype(BF16)
        h_ref[halo + tm:, :] = jnp.concatenate([h_next, zeros], axis=0).astype(BF16)
        _norm_mod_rows(x_ref, lead, h_ref, halo, gain, shift, scale)
        if lead == tm:
            gated(_dot(h_ref[...], wg_ref[...]), _dot(h_ref[halo:halo + tm, :], wu_ref[...]))
        else:
            cut = halo + lead
            h_ref[cut:halo + tm, :] = _norm_mod(x_ref[lead:, :], gain, shift, scale).astype(BF16)
            gate = jnp.concatenate([_dot(h_ref[:cut, :], wg_ref[...]), _dot(h_ref[cut:, :], wg_ref[...])], axis=0)
            up = jnp.concatenate([_dot(h_ref[halo:cut, :], wu_ref[...]),
                                  _dot(h_ref[cut:halo + tm, :], wu_ref[...])], axis=0)
            gated(gate, up)

    @pl.when(j > 0)
    def _():
        gated(_dot(h_ref[...], wg_ref[...]), _dot(h_ref[halo:halo + tm, :], wu_ref[...]))


def _ffn_in_call(x, gain, shift, scale, w, conv_w, conv_b, *, layer, seq_len, name):
    m, d = x.shape
    f = w.shape[2] // 2
    bm = shift.shape[0]
    rows_per_mod = m // bm
    tm = _pick(seq_len, 1024, BF16_SUBLANES)
    tn = _pick(f, 512, HEAD)
    assert rows_per_mod % tm == 0
    mod_blocks = rows_per_mod // tm
    nb = f // tn
    xh = 8
    tiles_per_tm = tm // xh
    last_tile = m // xh - 1
    body = functools.partial(_ffn_in_kernel, blocks_per_seq=seq_len // tm)
    return pl.pallas_call(
        body,
        grid=(m // tm, nb),
        in_specs=[
            pl.BlockSpec((tm, d), lambda i, j: (i, 0)),
            pl.BlockSpec((xh, d), lambda i, j: (jnp.maximum(i * tiles_per_tm - 1, 0), 0)),
            pl.BlockSpec((xh, d), lambda i, j: (jnp.minimum((i + 1) * tiles_per_tm, last_tile), 0)),
            pl.BlockSpec((1, d), lambda i, j: (0, 0)),
            pl.BlockSpec((None, 1, d), lambda i, j: (i // mod_blocks, 0, 0)),
            pl.BlockSpec((None, 1, d), lambda i, j: (i // mod_blocks, 0, 0)),
            pl.BlockSpec((None, d, tn), lambda i, j: (layer, 0, j)),
            pl.BlockSpec((None, d, tn), lambda i, j: (layer, 0, nb + j)),
            pl.BlockSpec((3, tn), lambda i, j: (0, j)),
            pl.BlockSpec((1, tn), lambda i, j: (0, j)),
        ],
        out_specs=pl.BlockSpec((tm, tn), lambda i, j: (i, j)),
        out_shape=jax.ShapeDtypeStruct((m, f), BF16),
        scratch_shapes=[pltpu.VMEM((tm + 2 * BF16_SUBLANES, d), BF16)],
        compiler_params=_params(("arbitrary", "arbitrary")),
        name=name,
    )(x, x, x, gain.reshape(1, d), shift, scale, w, w, conv_w, conv_b.reshape(1, f))


def _gla_gates(f_raw, lb, within):
    half = 0.5 * (1.0 - lb)
    ht = half * jnp.tanh(0.5 * f_raw.astype(F32))
    lf = jnp.log((1.0 - half) + ht)
    ones = jnp.where(within, 1.0, 0.0).astype(BF16)
    lf_hi = lf.astype(BF16)
    lf_lo = (lf - lf_hi.astype(F32)).astype(BF16)
    return half - ht, _dot(ones, lf_hi) + _dot(ones, lf_lo)


def _gla_pairs(qs, kk, b, forward, within):
    c = qs.shape[0]
    a_row = c // 2 - 1 if forward else c // 2
    e_row = c - 1 if forward else 0
    anchor = b[a_row:a_row + 1, :]
    b_end = b[e_row:e_row + 1, :]
    qa = qs * jnp.exp(b - anchor)
    ka = kk * jnp.exp(anchor - b)
    scores = jnp.where(within, _dot_nt(qa.astype(BF16), ka.astype(BF16)), 0.0).astype(BF16)
    return (scores, (qa * jnp.exp(anchor)).astype(BF16), (ka * jnp.exp(b_end - anchor)).astype(BF16),
            jnp.exp(b_end))


def _gla_pairs_exact(qs, kk, b, forward, within, tmp):
    tb_ref, tq_ref, ts_ref = tmp
    c = qs.shape[0]
    b_end = b[c - 1:c, :] if forward else b[0:1, :]
    tb_ref[...] = b
    tq_ref[...] = qs
    ones = jnp.ones((8, qs.shape[1]), BF16)

    def row(t, carry):
        rel = jnp.minimum(tb_ref[pl.ds(t, 1), :] - b, 0.0)
        e = (tq_ref[pl.ds(t, 1), :] * kk) * jnp.exp(rel)
        ts_ref[pl.ds(t, 1), :] = _dot_nt(ones, e.astype(BF16))[0:1, :]
        return carry

    lax.fori_loop(0, c, row, 0)
    scores = jnp.where(within, ts_ref[...], 0.0).astype(BF16)
    return scores, (qs * jnp.exp(b)).astype(BF16), (kk * jnp.exp(b_end - b)).astype(BF16), jnp.exp(b_end)


def _gla_readout(o, g_raw, o_gain):
    ms = jnp.mean(o * o, axis=-1, keepdims=True)
    y = o * lax.rsqrt(ms + EPS) * o_gain
    return (y * _silu(g_raw.astype(F32))).astype(BF16)


def _gla_kernel(qc, ffc, fbc, vc, gc, ql, ffl, fbl, vl, gl, lb_ref, og_ref, yc_ref, yl_ref,
                o_ref, qb_ref, ke_ref, dec_ref, st_ref, tb_ref, tq_ref, ts_ref):
    c = GLA_CHUNK
    tc = qc.shape[0]
    n_ctx = tc // c
    n_lat = ql.shape[0] // c
    lb = (lb_ref[0], lb_ref[1])
    row = lax.broadcasted_iota(jnp.int32, (c, c), 0)
    col = lax.broadcasted_iota(jnp.int32, (c, c), 1)
    within = (col <= row, col >= row)

    def rows(start):
        return pl.ds(start if isinstance(start, int) else pl.multiple_of(start, c), c)

    def local(refs, src0, dst0, chunk0, n, exact):
        q, ff, fb, v = refs
        src = [rows(src0 + j * c) for j in range(n)]
        dst = [rows(dst0 + j * c) for j in range(n)]
        qs = [_silu(q[src[j], :].astype(F32)) * (HEAD ** -0.5) for j in range(n)]
        gates = [[_gla_gates(f[src[j], :], lb[d], within[d]) for d, f in enumerate((ff, fb))] for j in range(n)]
        scores = []
        for j in range(n):
            per_dir = []
            for d in range(2):
                kk, b = gates[j][d]
                if exact:
                    sc, qb, ke, dec = _gla_pairs_exact(qs[j], kk, b, d == 0, within[d], (tb_ref, tq_ref, ts_ref))
                else:
                    sc, qb, ke, dec = _gla_pairs(qs[j], kk, b, d == 0, within[d])
                qb_ref[d, dst[j], :] = qb
                ke_ref[d, dst[j], :] = ke
                dec_ref[d, chunk0 + j] = dec
                per_dir.append(sc)
            scores.append(per_dir)
        for j in range(n):
            vj = v[src[j], :]
            o_ref[dst[j], :] = _dot(scores[j][0], vj) + _dot(scores[j][1], vj)

    def phase1(exact):
        run = 1 if exact else math.gcd(n_lat, GLA_RUN)
        run_ctx = math.gcd(n_ctx, run)
        for i in range(0, n_ctx, run_ctx):
            local((qc, ffc, fbc, vc), i * c, i * c, i, run_ctx, exact)

        def body(i, carry):
            src0 = pl.multiple_of(i * (run * c), c)
            local((ql, ffl, fbl, vl), src0, tc + src0, n_ctx + i * run, run, exact)
            return carry

        lax.fori_loop(0, n_lat // run, body, 0)

    safe = jnp.min(jnp.minimum(lb[0], lb[1])) >= GLA_SAFE_LB

    @pl.when(safe)
    def _():
        phase1(False)

    @pl.when(jnp.logical_not(safe))
    def _():
        phase1(True)

    st_ref[...] = jnp.zeros_like(st_ref)

    o_gain = og_ref[...]

    def steps(v, first, n, n_seg, row0, chunk0, finish=None):
        order = [[first + s for s in range(n)], [n_seg - 1 - first - s for s in range(n)]]
        src = [[rows(j * c) for j in order[d]] for d in range(2)]
        dst = [[rows(row0 + j * c) for j in order[d]] for d in range(2)]
        kv = [[_dot_tn(v[src[d][s], :], ke_ref[d, dst[d][s], :]) for s in range(n)] for d in range(2)]
        for d in range(2):
            st = st_ref[d]
            for s in range(n):
                o = o_ref[dst[d][s], :] + _dot_nt(qb_ref[d, dst[d][s], :], st.astype(BF16))
                if finish is None:
                    o_ref[dst[d][s], :] = o
                else:
                    g, y = finish
                    y[src[d][s], :] = _gla_readout(o, g[src[d][s], :], o_gain)
                st = st * dec_ref[d, chunk0 + order[d][s]] + kv[d][s]
            st_ref[d] = st

    steps(vc, 0, n_ctx, n_ctx, 0, 0)
    for i in range(n_ctx):
        r = slice(i * c, (i + 1) * c)
        yc_ref[r, :] = _gla_readout(o_ref[r, :], gc[r, :], o_gain)

    run = math.gcd(n_lat, GLA_RUN)
    n_runs = n_lat // run
    meet = (n_runs + 1) // 2

    def steps_body(i, carry):
        steps(vl, i * run, run, n_lat, tc, n_ctx)
        return carry

    def last_steps_body(i, carry):
        steps(vl, i * run, run, n_lat, tc, n_ctx, finish=(gl, yl_ref))
        return carry

    lax.fori_loop(0, meet, steps_body, 0)
    if n_runs % 2:
        for j in range((n_runs // 2) * run, (n_runs // 2 + 1) * run):
            yl_ref[rows(j * c), :] = _gla_readout(o_ref[rows(tc + j * c), :], gl[rows(j * c), :], o_gain)
    lax.fori_loop(meet, n_runs, last_steps_body, 0)


def _gla_call(p_ctx, p_lat, lower_bound, o_gain):
    b, tc, d5 = p_ctx.shape
    t = p_lat.shape[1]
    d = d5 // 5
    heads = d // HEAD
    assert tc % GLA_CHUNK == 0 and t % GLA_CHUNK == 0

    def col(tt, part):
        return pl.BlockSpec((None, tt, HEAD), lambda bi, h: (bi, 0, part * heads + h))

    out_spec = lambda tt: pl.BlockSpec((None, tt, HEAD), lambda bi, h: (bi, 0, h))
    return pl.pallas_call(
        _gla_kernel,
        grid=(b, heads),
        in_specs=[col(tc, p) for p in range(5)] + [col(t, p) for p in range(5)] + [
            pl.BlockSpec((2, None, 1, HEAD), lambda bi, h: (0, h, 0, 0)),
            pl.BlockSpec((1, HEAD), lambda bi, h: (0, 0)),
        ],
        out_specs=[out_spec(tc), out_spec(t)],
        out_shape=[jax.ShapeDtypeStruct((b, tc, d), BF16), jax.ShapeDtypeStruct((b, t, d), BF16)],
        scratch_shapes=[
            pltpu.VMEM((tc + t, HEAD), F32),
            pltpu.VMEM((2, tc + t, HEAD), BF16),
            pltpu.VMEM((2, tc + t, HEAD), BF16),
            pltpu.VMEM((2, (tc + t) // GLA_CHUNK, 1, HEAD), F32),
            pltpu.VMEM((2, HEAD, HEAD), F32),
            pltpu.VMEM((GLA_CHUNK, HEAD), F32),
            pltpu.VMEM((GLA_CHUNK, HEAD), F32),
            pltpu.VMEM((GLA_CHUNK, GLA_CHUNK), F32),
        ],
        compiler_params=_params(("arbitrary", "arbitrary")),
        name="hgrn_scan",
    )(*([p_ctx] * 5), *([p_lat] * 5), lower_bound.reshape(2, heads, 1, HEAD), o_gain.reshape(1, HEAD))


def _flash_kernel(q_ref, kc_ref, vc_ref, kl_ref, vl_ref, o_ref, *, tk):
    tq = q_ref.shape[0]
    group = q_ref.shape[1] // HEAD
    qs = [q_ref[:, g * HEAD:(g + 1) * HEAD] for g in range(group)]

    def attend(carry, k, v):
        v_ext = jnp.concatenate([v, jnp.ones_like(v)], axis=1)
        out = []
        for g in range(group):
            m, acc = carry[g]
            s = _dot_nt(qs[g], k)
            m_new = jnp.maximum(m, jnp.max(s, axis=-1, keepdims=True))
            p = jnp.exp2(s - m_new).astype(BF16)
            out.append((m_new, jnp.exp2(m - m_new) * acc + _dot(p, v_ext)))
        return tuple(out)

    carry = tuple((jnp.full((tq, 1), -jnp.inf, F32), jnp.zeros((tq, 2 * HEAD), F32)) for _ in range(group))
    for start in range(0, kl_ref.shape[0], tk):
        carry = attend(carry, kl_ref[start:start + tk, :], vl_ref[start:start + tk, :])
    carry = attend(carry, kc_ref[...], vc_ref[...])
    for g in range(group):
        acc = carry[g][1]
        o_ref[:, g * HEAD:(g + 1) * HEAD] = (acc[:, :HEAD] / acc[:, HEAD:]).astype(o_ref.dtype)


def _flash_call(qkv, kv_ctx, *, heads):
    b, t, n = qkv.shape
    tc = kv_ctx.shape[1]
    kvh = (n // HEAD - heads) // 2
    group = heads // kvh
    tq = _pick(t, 512, BF16_SUBLANES)
    tk = _pick(t, 1024, 2 * HEAD)
    assert t // tk <= 16
    body = functools.partial(_flash_kernel, tk=tk)
    return pl.pallas_call(
        body,
        grid=(b, kvh, t // tq),
        in_specs=[
            pl.BlockSpec((None, tq, group * HEAD), lambda bi, h, i: (bi, i, h)),
            pl.BlockSpec((None, tc, HEAD), lambda bi, h, i: (bi, 0, h)),
            pl.BlockSpec((None, tc, HEAD), lambda bi, h, i: (bi, 0, kvh + h)),
            pl.BlockSpec((None, t, HEAD), lambda bi, h, i: (bi, 0, heads + h)),
            pl.BlockSpec((None, t, HEAD), lambda bi, h, i: (bi, 0, heads + kvh + h)),
        ],
        out_specs=pl.BlockSpec((None, tq, group * HEAD), lambda bi, h, i: (bi, i, h)),
        out_shape=jax.ShapeDtypeStruct((b, t, heads * HEAD), BF16),
        compiler_params=_params(("arbitrary", "arbitrary", "arbitrary")),
        name="gqa_flash",
    )(qkv, kv_ctx, kv_ctx, qkv, qkv)


def _rope_tables(t):
    pos = jnp.arange(t, dtype=jnp.int32)
    rows = (pos // GRID_W).astype(F32)
    cols = (pos % GRID_W).astype(F32)
    axis_dim = HEAD // 2
    inv_freq = ROPE_THETA ** (-jnp.arange(0, axis_dim, 2, dtype=F32) / axis_dim)
    ang = jnp.concatenate([rows[:, None] * inv_freq, cols[:, None] * inv_freq], axis=-1)
    cos, sin = jnp.cos(ang), jnp.sin(ang)
    return jnp.concatenate([cos, cos], axis=-1), jnp.concatenate([-sin, sin], axis=-1)


def _split_mods(mods, b):
    d = mods.shape[1] // N_MOD
    lat = [mods[:b, i * d:(i + 1) * d].reshape(b, 1, d) for i in range(N_MOD)]
    ctx = [mods[b:b + 1, i * d:(i + 1) * d].reshape(1, 1, d) for i in range(N_MOD)]
    return lat, ctx


def kernel(x, c, ctx, c_ctx, ada_w, ada_b, norm_mix_pre, norm_mix_post, norm_ffn_pre, norm_ffn_post, hgrn_w_in, hgrn_lb_logits, hgrn_o_norm, hgrn_w_out, attn_w_qkv, attn_q_norm, attn_k_norm, attn_w_out, ffn_w_in, ffn_conv_w, ffn_conv_b, ffn_w_out):
    b, t, d = x.shape
    tc = ctx.shape[1]
    heads = d // HEAD
    assert ada_w.shape[0] == 2 and b + 1 <= ADA_ROWS

    cond = jnp.zeros((ADA_ROWS, d), F32).at[:b].set(c).at[b].set(c_ctx)
    mods = _ada_call(cond, ada_w, ada_b)
    x_lat = x.reshape(b * t, d)
    x_ctx = ctx.reshape(b * tc, d)

    (sh_ml, sc_ml, gt_ml, sh_fl, sc_fl, gt_fl), (sh_mc, sc_mc, gt_mc, sh_fc, sc_fc, gt_fc) = _split_mods(mods[0], b)
    lower_bound = jnp.cumsum(jax.nn.softmax(hgrn_lb_logits.astype(F32), axis=1), axis=1)[:, 0]
    hgrn_w_in, hgrn_w_out, attn_w_out = hgrn_w_in.astype(BF16), hgrn_w_out.astype(BF16), attn_w_out.astype(BF16)
    ffn_w_in, ffn_w_out = ffn_w_in.astype(BF16), ffn_w_out.astype(BF16)
    p_lat = _nmm_call(x_lat, norm_mix_pre[0], sh_ml, sc_ml, hgrn_w_in, name="hgrn_in_lat")
    p_ctx = _nmm_call(x_ctx, norm_mix_pre[0], sh_mc, sc_mc, hgrn_w_in, name="hgrn_in_ctx")
    y_ctx, y_lat = _gla_call(p_ctx.reshape(b, tc, 5 * d), p_lat.reshape(b, t, 5 * d), lower_bound, hgrn_o_norm[0])
    x_lat = _proj_res_call(y_lat.reshape(b * t, d), hgrn_w_out, x_lat, norm_mix_post[0], gt_ml, name="hgrn_out_lat")
    x_ctx = _proj_res_call(y_ctx.reshape(b * tc, d), hgrn_w_out, x_ctx, norm_mix_post[0], gt_mc, name="hgrn_out_ctx")
    a = _ffn_in_call(x_lat, norm_ffn_pre[0], sh_fl, sc_fl, ffn_w_in, ffn_conv_w[0], ffn_conv_b[0],
                     layer=0, seq_len=t, name="ffn0_in_lat")
    x_lat = _proj_res_call(a, ffn_w_out, x_lat, norm_ffn_post[0], gt_fl, layer=0, name="ffn0_out_lat")
    a = _ffn_in_call(x_ctx, norm_ffn_pre[0], sh_fc, sc_fc, ffn_w_in, ffn_conv_w[0], ffn_conv_b[0],
                     layer=0, seq_len=tc, name="ffn0_in_ctx")
    x_ctx = _proj_res_call(a, ffn_w_out, x_ctx, norm_ffn_post[0], gt_fc, layer=0, name="ffn0_out_ctx")

    (sh_ml, sc_ml, gt_ml, sh_fl, sc_fl, gt_fl), (sh_mc, sc_mc, _, _, _, _) = _split_mods(mods[1], b)
    n_qkv = attn_w_qkv.shape[2]
    kvh = (n_qkv // HEAD - heads) // 2
    perm = jnp.concatenate([jnp.arange(0, HEAD, 2), jnp.arange(1, HEAD, 2)])
    head_perm = (jnp.arange(heads + kvh)[:, None] * HEAD + perm[None, :]).reshape(-1)
    col_perm = jnp.concatenate([head_perm, jnp.arange((heads + kvh) * HEAD, n_qkv)])
    w_qkv = attn_w_qkv[:, :, col_perm].astype(BF16)
    head_gain = jnp.concatenate([
        jnp.tile(attn_q_norm[0][perm] * (LOG2_E * HEAD ** -0.5), heads),
        jnp.tile(attn_k_norm[0][perm], kvh),
        jnp.ones((kvh * HEAD,), F32),
    ]).reshape(1, n_qkv)
    cos, sin = _rope_tables(t)
    qk_cols = (heads + kvh) * HEAD
    q_cols = heads * HEAD
    qkv = _nmm_call(x_lat, norm_mix_pre[1], sh_ml, sc_ml, w_qkv, name="attn_qkv_lat")
    qkv = _qk_rope_call(qkv, head_gain[:, :qk_cols], cos, sin, name="attn_qk_rope_lat")
    kv_ctx = _nmm_call(x_ctx, norm_mix_pre[1], sh_mc, sc_mc, w_qkv, col0=q_cols, name="attn_kv_ctx")
    kv_ctx = _qk_rope_call(kv_ctx, head_gain[:, q_cols:qk_cols], jnp.ones((tc, HEAD), F32),
                           jnp.zeros((tc, HEAD), F32), name="attn_k_norm_ctx")
    qkv = qkv.reshape(b, t, n_qkv)
    kv_ctx = kv_ctx.reshape(b, tc, 2 * kvh * HEAD)
    o = _flash_call(qkv, kv_ctx, heads=heads)
    x_lat = _proj_res_call(o.reshape(b * t, d), attn_w_out, x_lat, norm_mix_post[1], gt_ml, name="attn_out_lat")
    a = _ffn_in_call(x_lat, norm_ffn_pre[1], sh_fl, sc_fl, ffn_w_in, ffn_conv_w[1], ffn_conv_b[1],
                     layer=1, seq_len=t, name="ffn1_in_lat")
    x_lat = _proj_res_call(a, ffn_w_out, x_lat, norm_ffn_post[1], gt_fl, layer=1, name="ffn1_out_lat")
    return x_lat.reshape(b, t, d)
```

```python
import functools
import math

import jax
import jax.numpy as jnp
from jax import lax
from jax.experimental import pallas as pl
from jax.experimental.pallas import tpu as pltpu

F32 = jnp.float32
BF16 = jnp.bfloat16

EPS = 1e-6
GRID_W = 64
ROPE_THETA = 10000.0
LOG2_E = math.log2(math.e)
N_MOD = 6
HEAD = 128
ROW_GROUP = 16
GLA_CHUNK = 128
GLA_RUN = 32
GLA_SAFE_LB = math.exp(-80.0 / (GLA_CHUNK // 2))
F32_SUBLANES = 8
BF16_SUBLANES = 16
ADA_ROWS = F32_SUBLANES
V7X_VMEM_BYTES = 64 * 1024 * 1024
VMEM_LIMIT = V7X_VMEM_BYTES - 6 * 1024 * 1024


def _pick(n, target, align):
    if n <= target:
        return n
    for t in range(target - target % align, 0, -align):
        if n % t == 0:
            return t
    return n


def _params(sem):
    return pltpu.CompilerParams(dimension_semantics=sem, vmem_limit_bytes=VMEM_LIMIT)


def _sigmoid(x):
    return 0.5 * jnp.tanh(0.5 * x) + 0.5


def _silu(x):
    return x * _sigmoid(x)


def _dot(a, b):
    return jnp.dot(a, b, preferred_element_type=F32)


def _dot_nt(a, b):
    return lax.dot_general(a, b, (((1,), (1,)), ((), ())), preferred_element_type=F32)


def _dot_tn(a, b):
    return lax.dot_general(a, b, (((0,), (0,)), ((), ())), preferred_element_type=F32)


def _ada_kernel(c_ref, w_ref, b_ref, o_ref):
    sc = _silu(c_ref[...]).astype(BF16)
    o_ref[...] = _dot(sc, w_ref[...].astype(BF16)) + b_ref[...]


def _ada_call(cond, ada_w, ada_b):
    depth, d, n = ada_w.shape
    tn = _pick(n, 1024, HEAD)
    return pl.pallas_call(
        _ada_kernel,
        grid=(depth, n // tn),
        in_specs=[
            pl.BlockSpec((ADA_ROWS, d), lambda l, j: (0, 0)),
            pl.BlockSpec((None, d, tn), lambda l, j: (l, 0, j)),
            pl.BlockSpec((None, 1, tn), lambda l, j: (l, 0, j)),
        ],
        out_specs=pl.BlockSpec((None, ADA_ROWS, tn), lambda l, j: (l, 0, j)),
        out_shape=jax.ShapeDtypeStruct((depth, ADA_ROWS, n), F32),
        compiler_params=_params(("arbitrary", "arbitrary")),
        name="ada_ln",
    )(cond, ada_w, ada_b.reshape(depth, 1, n))


def _norm_mod(x, gain, shift, scale):
    ms = jnp.mean(x * x, axis=-1, keepdims=True)
    return (x * lax.rsqrt(ms + EPS) * gain) * (1.0 + scale) + shift


def _norm_mod_rows(x_ref, n_rows, h_ref, h_row0, gain, shift, scale):
    mult = gain * (1.0 + scale)

    def body(r, carry):
        start = pl.multiple_of(r * ROW_GROUP, ROW_GROUP)
        x = x_ref[pl.ds(start, ROW_GROUP), :]
        ms = jnp.mean(x * x, axis=-1, keepdims=True)
        dst = pl.ds(pl.multiple_of(h_row0 + start, ROW_GROUP), ROW_GROUP)
        h_ref[dst, :] = (x * lax.rsqrt(ms + EPS) * mult + shift).astype(BF16)
        return carry

    groups = n_rows // ROW_GROUP
    lax.fori_loop(0, groups, body, 0, unroll=math.gcd(groups, 4))


def _split_rows(tm):
    half = tm // 2
    return half if half % ROW_GROUP == 0 else tm


def _nmm_kernel(x_ref, gain_ref, shift_ref, scale_ref, w_ref, o_ref, h_ref):
    j = pl.program_id(1)
    tm = x_ref.shape[0]
    lead = _split_rows(tm)

    @pl.when(j == 0)
    def _():
        gain, shift, scale = gain_ref[...], shift_ref[...], scale_ref[...]
        _norm_mod_rows(x_ref, lead, h_ref, 0, gain, shift, scale)
        if lead < tm:
            h_ref[lead:, :] = _norm_mod(x_ref[lead:, :], gain, shift, scale).astype(BF16)
        o_ref[:lead, :] = _dot(h_ref[:lead, :], w_ref[...]).astype(o_ref.dtype)
        if lead < tm:
            o_ref[lead:, :] = _dot(h_ref[lead:, :], w_ref[...]).astype(o_ref.dtype)

    @pl.when(j > 0)
    def _():
        o_ref[...] = _dot(h_ref[...], w_ref[...]).astype(o_ref.dtype)


def _nmm_call(x, gain, shift, scale, w, *, layer=0, col0=0, name):
    m, d = x.shape
    n = w.shape[2] - col0
    bm = shift.shape[0]
    rows_per_mod = m // bm
    tm = _pick(rows_per_mod, 1024, BF16_SUBLANES)
    tn = _pick(math.gcd(n, col0) if col0 else n, 2048, HEAD)
    mod_blocks = rows_per_mod // tm
    col_block0 = col0 // tn
    return pl.pallas_call(
        _nmm_kernel,
        grid=(m // tm, n // tn),
        in_specs=[
            pl.BlockSpec((tm, d), lambda i, j: (i, 0)),
            pl.BlockSpec((1, d), lambda i, j: (0, 0)),
            pl.BlockSpec((None, 1, d), lambda i, j: (i // mod_blocks, 0, 0)),
            pl.BlockSpec((None, 1, d), lambda i, j: (i // mod_blocks, 0, 0)),
            pl.BlockSpec((None, d, tn), lambda i, j: (layer, 0, col_block0 + j)),
        ],
        out_specs=pl.BlockSpec((tm, tn), lambda i, j: (i, j)),
        out_shape=jax.ShapeDtypeStruct((m, n), BF16),
        scratch_shapes=[pltpu.VMEM((tm, d), BF16)],
        compiler_params=_params(("arbitrary", "arbitrary")),
        name=name,
    )(x, gain.reshape(1, d), shift, scale, w)


def _qk_rope_kernel(x_ref, hg_ref, cos_ref, sin_ref, o_ref):
    cos = cos_ref[...]
    sin = sin_ref[...]
    mean_lanes = jnp.full((HEAD, HEAD), 1.0 / HEAD, BF16)
    for hh in range(x_ref.shape[1] // HEAD):
        cols = slice(hh * HEAD, (hh + 1) * HEAD)
        a = x_ref[:, cols].astype(F32)
        ms = _dot((a * a).astype(BF16), mean_lanes)
        a = a * lax.rsqrt(ms + EPS) * hg_ref[:, cols]
        a = a * cos + pltpu.roll(a, HEAD // 2, axis=1) * sin
        o_ref[:, cols] = a.astype(o_ref.dtype)


def _qk_rope_call(p, head_gain, cos, sin, *, name):
    m, n = p.shape
    nn = head_gain.shape[1]
    tm = _pick(math.gcd(m, cos.shape[0]), 2048, BF16_SUBLANES)
    tn = _pick(math.gcd(n, nn), 1024, HEAD)
    table_blocks = cos.shape[0] // tm
    return pl.pallas_call(
        _qk_rope_kernel,
        grid=(m // tm, nn // tn),
        in_specs=[
            pl.BlockSpec((tm, tn), lambda i, j: (i, j)),
            pl.BlockSpec((1, tn), lambda i, j: (0, j)),
            pl.BlockSpec((tm, HEAD), lambda i, j: (i % table_blocks, 0)),
            pl.BlockSpec((tm, HEAD), lambda i, j: (i % table_blocks, 0)),
        ],
        out_specs=pl.BlockSpec((tm, tn), lambda i, j: (i, j)),
        out_shape=jax.ShapeDtypeStruct((m, n), p.dtype),
        input_output_aliases={0: 0},
        compiler_params=_params(("arbitrary", "arbitrary")),
        name=name,
    )(p, head_gain, cos, sin)


def _residual(x, y, gain, gate):
    ms = jnp.mean(y * y, axis=-1, keepdims=True)
    return x + gate * (y * lax.rsqrt(ms + EPS) * gain)


def _proj_res_kernel(a_ref, w_ref, x_ref, gain_ref, gate_ref, o_ref):
    y = _dot(a_ref[...], w_ref[...])
    o_ref[...] = _residual(x_ref[...], y, gain_ref[...], gate_ref[...])


def _proj_res_ktiled_kernel(a_ref, w_ref, x_ref, gain_ref, gate_ref, o_ref):
    k = pl.program_id(1)
    last = pl.num_programs(1) - 1
    tm = o_ref.shape[0]
    lead = _split_rows(tm)

    @pl.when(k == 0)
    def _():
        o_ref[...] = _dot(a_ref[...], w_ref[...])

    @pl.when(jnp.logical_and(k > 0, k < last))
    def _():
        o_ref[...] += _dot(a_ref[...], w_ref[...])

    @pl.when(k == last)
    def _():
        gain, gate = gain_ref[...], gate_ref[...]
        parts = [slice(0, lead), slice(lead, tm)] if lead < tm else [slice(0, tm)]
        ys = [o_ref[r, :] + _dot(a_ref[r, :], w_ref[...]) for r in parts]
        for r, y in zip(parts, ys):
            o_ref[r, :] = _residual(x_ref[r, :], y, gain, gate)


def _proj_res_call(a, w, x, gain, gate, *, layer=0, name):
    m, k = a.shape
    d = w.shape[2]
    bm = gate.shape[0]
    rows_per_mod = m // bm
    if k > d:
        tk = _pick(k, 512, 2 * HEAD)
    elif rows_per_mod >= 1024:
        tk = _pick(k, 1024, 2 * HEAD)
    else:
        tk = k
    kb = k // tk
    tm = _pick(rows_per_mod, 512 if kb == 1 else 1024, BF16_SUBLANES)
    mod_blocks = rows_per_mod // tm
    return pl.pallas_call(
        _proj_res_kernel if kb == 1 else _proj_res_ktiled_kernel,
        grid=(m // tm, kb),
        in_specs=[
            pl.BlockSpec((tm, tk), lambda i, kk: (i, kk)),
            pl.BlockSpec((None, tk, d), lambda i, kk: (layer, kk, 0)),
            pl.BlockSpec((tm, d), lambda i, kk: (i, 0)),
            pl.BlockSpec((1, d), lambda i, kk: (0, 0)),
            pl.BlockSpec((None, 1, d), lambda i, kk: (i // mod_blocks, 0, 0)),
        ],
        out_specs=pl.BlockSpec((tm, d), lambda i, kk: (i, 0)),
        out_shape=jax.ShapeDtypeStruct((m, d), F32),
        compiler_params=_params(("arbitrary", "arbitrary")),
        name=name,
    )(a, w, x, gain.reshape(1, d), gate)


def _ffn_in_kernel(x_ref, xprev_ref, xnext_ref, gain_ref, shift_ref, scale_ref, wg_ref, wu_ref, cw_ref, cb_ref,
                   o_ref, h_ref, *, blocks_per_seq):
    i = pl.program_id(0)
    j = pl.program_id(1)
    tm = x_ref.shape[0]
    halo = BF16_SUBLANES
    lead = _split_rows(tm)

    def gated(gate, up):
        rows = gate.shape[0]
        g_prev = pltpu.roll(gate, 1, axis=0)[halo:halo + tm, :]
        g_next = pltpu.roll(gate, rows - 1, axis=0)[halo:halo + tm, :]
        conv = (g_prev * cw_ref[0:1, :] + gate[halo:halo + tm, :] * cw_ref[1:2, :] + g_next * cw_ref[2:3, :]
                + cb_ref[...])
        o_ref[...] = (_silu(conv) * up).astype(o_ref.dtype)

    @pl.when(j == 0)
    def _():
        gain, shift, scale = gain_ref[...], shift_ref[...], scale_ref[...]
        seq_block = i % blocks_per_seq
        h_prev = jnp.where(seq_block > 0, _norm_mod(xprev_ref[...], gain, shift, scale), 0.0)
        h_next = jnp.where(seq_block < blocks_per_seq - 1, _norm_mod(xnext_ref[...], gain, shift, scale), 0.0)
        zeros = jnp.zeros((halo - h_prev.shape[0], h_prev.shape[1]), F32)
        h_ref[0:halo, :] = jnp.concatenate([zeros, h_prev], axis=0).astype(BF16)
        h_ref[halo + tm:, :] = jnp.concatenate([h_next, zeros], axis=0).astype(BF16)
        _norm_mod_rows(x_ref, lead, h_ref, halo, gain, shift, scale)
        if lead == tm:
            gated(_dot(h_ref[...], wg_ref[...]), _dot(h_ref[halo:halo + tm, :], wu_ref[...]))
        else:
            cut = halo + lead
            h_ref[cut:halo + tm, :] = _norm_mod(x_ref[lead:, :], gain, shift, scale).astype(BF16)
            gate = jnp.concatenate([_dot(h_ref[:cut, :], wg_ref[...]), _dot(h_ref[cut:, :], wg_ref[...])], axis=0)
            up = jnp.concatenate([_dot(h_ref[halo:cut, :], wu_ref[...]),
                                  _dot(h_ref[cut:halo + tm, :], wu_ref[...])], axis=0)
            gated(gate, up)

    @pl.when(j > 0)
    def _():
        gated(_dot(h_ref[...], wg_ref[...]), _dot(h_ref[halo:halo + tm, :], wu_ref[...]))


def _ffn_in_call(x, gain, shift, scale, w, conv_w, conv_b, *, layer, seq_len, name):
    m, d = x.shape
    f = w.shape[2] // 2
    bm = shift.shape[0]
    rows_per_mod = m // bm
    tm = _pick(seq_len, 1024, BF16_SUBLANES)
    tn = _pick(f, 512, HEAD)
    assert rows_per_mod % tm == 0
    mod_blocks = rows_per_mod // tm
    nb = f // tn
    xh = F32_SUBLANES
    tiles_per_tm = tm // xh
    last_tile = m // xh - 1
    body = functools.partial(_ffn_in_kernel, blocks_per_seq=seq_len // tm)
    return pl.pallas_call(
        body,
        grid=(m // tm, nb),
        in_specs=[
            pl.BlockSpec((tm, d), lambda i, j: (i, 0)),
            pl.BlockSpec((xh, d), lambda i, j: (jnp.maximum(i * tiles_per_tm - 1, 0), 0)),
            pl.BlockSpec((xh, d), lambda i, j: (jnp.minimum((i + 1) * tiles_per_tm, last_tile), 0)),
            pl.BlockSpec((1, d), lambda i, j: (0, 0)),
            pl.BlockSpec((None, 1, d), lambda i, j: (i // mod_blocks, 0, 0)),
            pl.BlockSpec((None, 1, d), lambda i, j: (i // mod_blocks, 0, 0)),
            pl.BlockSpec((None, d, tn), lambda i, j: (layer, 0, j)),
            pl.BlockSpec((None, d, tn), lambda i, j: (layer, 0, nb + j)),
            pl.BlockSpec((3, tn), lambda i, j: (0, j)),
            pl.BlockSpec((1, tn), lambda i, j: (0, j)),
        ],
        out_specs=pl.BlockSpec((tm, tn), lambda i, j: (i, j)),
        out_shape=jax.ShapeDtypeStruct((m, f), BF16),
        scratch_shapes=[pltpu.VMEM((tm + 2 * BF16_SUBLANES, d), BF16)],
        compiler_params=_params(("arbitrary", "arbitrary")),
        name=name,
    )(x, x, x, gain.reshape(1, d), shift, scale, w, w, conv_w, conv_b.reshape(1, f))


def _gla_gates(f_raw, lb, within):
    half = 0.5 * (1.0 - lb)
    ht = half * jnp.tanh(0.5 * f_raw.astype(F32))
    lf = jnp.log((1.0 - half) + ht)
    ones = jnp.where(within, 1.0, 0.0).astype(BF16)
    lf_hi = lf.astype(BF16)
    lf_lo = (lf - lf_hi.astype(F32)).astype(BF16)
    return half - ht, _dot(ones, lf_hi) + _dot(ones, lf_lo)


def _gla_pairs(qs, kk, b, forward, within):
    c = qs.shape[0]
    a_row = c // 2 - 1 if forward else c // 2
    e_row = c - 1 if forward else 0
    anchor = b[a_row:a_row + 1, :]
    b_end = b[e_row:e_row + 1, :]
    qa = qs * jnp.exp(b - anchor)
    ka = kk * jnp.exp(anchor - b)
    scores = jnp.where(within, _dot_nt(qa.astype(BF16), ka.astype(BF16)), 0.0).astype(BF16)
    return (scores, (qa * jnp.exp(anchor)).astype(BF16), (ka * jnp.exp(b_end - anchor)).astype(BF16),
            jnp.exp(b_end))


def _gla_pairs_exact(qs, kk, b, forward, within, tmp):
    tb_ref, tq_ref, ts_ref = tmp
    c = qs.shape[0]
    b_end = b[c - 1:c, :] if forward else b[0:1, :]
    tb_ref[...] = b
    tq_ref[...] = qs
    ones = jnp.ones((F32_SUBLANES, qs.shape[1]), BF16)

    def row(t, carry):
        rel = jnp.minimum(tb_ref[pl.ds(t, 1), :] - b, 0.0)
        e = (tq_ref[pl.ds(t, 1), :] * kk) * jnp.exp(rel)
        ts_ref[pl.ds(t, 1), :] = _dot_nt(ones, e.astype(BF16))[0:1, :]
        return carry

    lax.fori_loop(0, c, row, 0)
    scores = jnp.where(within, ts_ref[...], 0.0).astype(BF16)
    return scores, (qs * jnp.exp(b)).astype(BF16), (kk * jnp.exp(b_end - b)).astype(BF16), jnp.exp(b_end)


def _gla_readout(o, g_raw, o_gain):
    ms = jnp.mean(o * o, axis=-1, keepdims=True)
    y = o * lax.rsqrt(ms + EPS) * o_gain
    return (y * _silu(g_raw.astype(F32))).astype(BF16)


def _gla_kernel(qc, ffc, fbc, vc, gc, ql, ffl, fbl, vl, gl, lb_ref, og_ref, yc_ref, yl_ref,
                o_ref, qb_ref, ke_ref, dec_ref, st_ref, tb_ref, tq_ref, ts_ref):
    c = GLA_CHUNK
    tc = qc.shape[0]
    n_ctx = tc // c
    n_lat = ql.shape[0] // c
    lb = (lb_ref[0], lb_ref[1])
    row = lax.broadcasted_iota(jnp.int32, (c, c), 0)
    col = lax.broadcasted_iota(jnp.int32, (c, c), 1)
    within = (col <= row, col >= row)

    def rows(start):
        return pl.ds(start if isinstance(start, int) else pl.multiple_of(start, c), c)

    def local(refs, src0, dst0, chunk0, n, exact):
        q, ff, fb, v = refs
        src = [rows(src0 + j * c) for j in range(n)]
        dst = [rows(dst0 + j * c) for j in range(n)]
        qs = [_silu(q[src[j], :].astype(F32)) * (HEAD ** -0.5) for j in range(n)]
        gates = [[_gla_gates(f[src[j], :], lb[d], within[d]) for d, f in enumerate((ff, fb))] for j in range(n)]
        scores = []
        for j in range(n):
            per_dir = []
            for d in range(2):
                kk, b = gates[j][d]
                if exact:
                    sc, qb, ke, dec = _gla_pairs_exact(qs[j], kk, b, d == 0, within[d], (tb_ref, tq_ref, ts_ref))
                else:
                    sc, qb, ke, dec = _gla_pairs(qs[j], kk, b, d == 0, within[d])
                qb_ref[d, dst[j], :] = qb
                ke_ref[d, dst[j], :] = ke
                dec_ref[d, chunk0 + j] = dec
                per_dir.append(sc)
            scores.append(per_dir)
        for j in range(n):
            vj = v[src[j], :]
            o_ref[dst[j], :] = _dot(scores[j][0], vj) + _dot(scores[j][1], vj)

    def phase1(exact):
        run = 1 if exact else math.gcd(n_lat, GLA_RUN)
        run_ctx = math.gcd(n_ctx, run)
        for i in range(0, n_ctx, run_ctx):
            local((qc, ffc, fbc, vc), i * c, i * c, i, run_ctx, exact)

        def body(i, carry):
            src0 = pl.multiple_of(i * (run * c), c)
            local((ql, ffl, fbl, vl), src0, tc + src0, n_ctx + i * run, run, exact)
            return carry

        lax.fori_loop(0, n_lat // run, body, 0)

    safe = jnp.min(jnp.minimum(lb[0], lb[1])) >= GLA_SAFE_LB

    @pl.when(safe)
    def _():
        phase1(False)

    @pl.when(jnp.logical_not(safe))
    def _():
        phase1(True)

    st_ref[...] = jnp.zeros_like(st_ref)

    o_gain = og_ref[...]

    def steps(v, first, n, n_seg, row0, chunk0, finish=None):
        order = [[first + s for s in range(n)], [n_seg - 1 - first - s for s in range(n)]]
        src = [[rows(j * c) for j in order[d]] for d in range(2)]
        dst = [[rows(row0 + j * c) for j in order[d]] for d in range(2)]
        kv = [[_dot_tn(v[src[d][s], :], ke_ref[d, dst[d][s], :]) for s in range(n)] for d in range(2)]
        for d in range(2):
            st = st_ref[d]
            for s in range(n):
                o = o_ref[dst[d][s], :] + _dot_nt(qb_ref[d, dst[d][s], :], st.astype(BF16))
                if finish is None:
                    o_ref[dst[d][s], :] = o
                else:
                    g, y = finish
                    y[src[d][s], :] = _gla_readout(o, g[src[d][s], :], o_gain)
                st = st * dec_ref[d, chunk0 + order[d][s]] + kv[d][s]
            st_ref[d] = st

    steps(vc, 0, n_ctx, n_ctx, 0, 0)
    for i in range(n_ctx):
        r = slice(i * c, (i + 1) * c)
        yc_ref[r, :] = _gla_readout(o_ref[r, :], gc[r, :], o_gain)

    run = math.gcd(n_lat, GLA_RUN)
    n_runs = n_lat // run
    meet = (n_runs + 1) // 2

    def steps_body(i, carry):
        steps(vl, i * run, run, n_lat, tc, n_ctx)
        return carry

    def last_steps_body(i, carry):
        steps(vl, i * run, run, n_lat, tc, n_ctx, finish=(gl, yl_ref))
        return carry

    lax.fori_loop(0, meet, steps_body, 0)
    if n_runs % 2:
        for j in range((n_runs // 2) * run, (n_runs // 2 + 1) * run):
            yl_ref[rows(j * c), :] = _gla_readout(o_ref[rows(tc + j * c), :], gl[rows(j * c), :], o_gain)
    lax.fori_loop(meet, n_runs, last_steps_body, 0)


def _gla_call(p_ctx, p_lat, lower_bound, o_gain):
    b, tc, d5 = p_ctx.shape
    t = p_lat.shape[1]
    d = d5 // 5
    heads = d // HEAD
    assert tc % GLA_CHUNK == 0 and t % GLA_CHUNK == 0

    def col(tt, part):
        return pl.BlockSpec((None, tt, HEAD), lambda bi, h: (bi, 0, part * heads + h))

    out_spec = lambda tt: pl.BlockSpec((None, tt, HEAD), lambda bi, h: (bi, 0, h))
    return pl.pallas_call(
        _gla_kernel,
        grid=(b, heads),
        in_specs=[col(tc, p) for p in range(5)] + [col(t, p) for p in range(5)] + [
            pl.BlockSpec((2, None, 1, HEAD), lambda bi, h: (0, h, 0, 0)),
            pl.BlockSpec((1, HEAD), lambda bi, h: (0, 0)),
        ],
        out_specs=[out_spec(tc), out_spec(t)],
        out_shape=[jax.ShapeDtypeStruct((b, tc, d), BF16), jax.ShapeDtypeStruct((b, t, d), BF16)],
        scratch_shapes=[
            pltpu.VMEM((tc + t, HEAD), F32),
            pltpu.VMEM((2, tc + t, HEAD), BF16),
            pltpu.VMEM((2, tc + t, HEAD), BF16),
            pltpu.VMEM((2, (tc + t) // GLA_CHUNK, 1, HEAD), F32),
            pltpu.VMEM((2, HEAD, HEAD), F32),
            pltpu.VMEM((GLA_CHUNK, HEAD), F32),
            pltpu.VMEM((GLA_CHUNK, HEAD), F32),
            pltpu.VMEM((GLA_CHUNK, GLA_CHUNK), F32),
        ],
        compiler_params=_params(("arbitrary", "arbitrary")),
        name="hgrn_scan",
    )(*([p_ctx] * 5), *([p_lat] * 5), lower_bound.reshape(2, heads, 1, HEAD), o_gain.reshape(1, HEAD))


def _flash_kernel(q_ref, kc_ref, vc_ref, kl_ref, vl_ref, o_ref, *, tk):
    tq = q_ref.shape[0]
    group = q_ref.shape[1] // HEAD
    qs = [q_ref[:, g * HEAD:(g + 1) * HEAD] for g in range(group)]

    def attend(carry, k, v):
        v_ext = jnp.concatenate([v, jnp.ones_like(v)], axis=1)
        out = []
        for g in range(group):
            m, acc = carry[g]
            s = _dot_nt(qs[g], k)
            m_new = jnp.maximum(m, jnp.max(s, axis=-1, keepdims=True))
            p = jnp.exp2(s - m_new).astype(BF16)
            out.append((m_new, jnp.exp2(m - m_new) * acc + _dot(p, v_ext)))
        return tuple(out)

    carry = tuple((jnp.full((tq, 1), -jnp.inf, F32), jnp.zeros((tq, 2 * HEAD), F32)) for _ in range(group))
    for start in range(0, kl_ref.shape[0], tk):
        carry = attend(carry, kl_ref[start:start + tk, :], vl_ref[start:start + tk, :])
    carry = attend(carry, kc_ref[...], vc_ref[...])
    for g in range(group):
        acc = carry[g][1]
        o_ref[:, g * HEAD:(g + 1) * HEAD] = (acc[:, :HEAD] / acc[:, HEAD:]).astype(o_ref.dtype)


def _flash_call(qkv, kv_ctx, *, heads):
    b, t, n = qkv.shape
    tc = kv_ctx.shape[1]
    kvh = (n // HEAD - heads) // 2
    group = heads // kvh
    tq = _pick(t, 512, BF16_SUBLANES)
    tk = _pick(t, 1024, 2 * HEAD)
    assert t // tk <= 16
    body = functools.partial(_flash_kernel, tk=tk)
    return pl.pallas_call(
        body,
        grid=(b, kvh, t // tq),
        in_specs=[
            pl.BlockSpec((None, tq, group * HEAD), lambda bi, h, i: (bi, i, h)),
            pl.BlockSpec((None, tc, HEAD), lambda bi, h, i: (bi, 0, h)),
            pl.BlockSpec((None, tc, HEAD), lambda bi, h, i: (bi, 0, kvh + h)),
            pl.BlockSpec((None, t, HEAD), lambda bi, h, i: (bi, 0, heads + h)),
            pl.BlockSpec((None, t, HEAD), lambda bi, h, i: (bi, 0, heads + kvh + h)),
        ],
        out_specs=pl.BlockSpec((None, tq, group * HEAD), lambda bi, h, i: (bi, i, h)),
        out_shape=jax.ShapeDtypeStruct((b, t, heads * HEAD), BF16),
        compiler_params=_params(("arbitrary", "arbitrary", "arbitrary")),
        name="gqa_flash",
    )(qkv, kv_ctx, kv_ctx, qkv, qkv)


def _rope_tables(t):
    pos = jnp.arange(t, dtype=jnp.int32)
    rows = (pos // GRID_W).astype(F32)
    cols = (pos % GRID_W).astype(F32)
    axis_dim = HEAD // 2
    inv_freq = ROPE_THETA ** (-jnp.arange(0, axis_dim, 2, dtype=F32) / axis_dim)
    ang = jnp.concatenate([rows[:, None] * inv_freq, cols[:, None] * inv_freq], axis=-1)
    cos, sin = jnp.cos(ang), jnp.sin(ang)
    return jnp.concatenate([cos, cos], axis=-1), jnp.concatenate([-sin, sin], axis=-1)


def _split_mods(mods, b):
    d = mods.shape[1] // N_MOD
    lat = [mods[:b, i * d:(i + 1) * d].reshape(b, 1, d) for i in range(N_MOD)]
    ctx = [mods[b:b + 1, i * d:(i + 1) * d].reshape(1, 1, d) for i in range(N_MOD)]
    return lat, ctx


def kernel(x, c, ctx, c_ctx, ada_w, ada_b, norm_mix_pre, norm_mix_post, norm_ffn_pre, norm_ffn_post, hgrn_w_in, hgrn_lb_logits, hgrn_o_norm, hgrn_w_out, attn_w_qkv, attn_q_norm, attn_k_norm, attn_w_out, ffn_w_in, ffn_conv_w, ffn_conv_b, ffn_w_out):
    b, t, d = x.shape
    tc = ctx.shape[1]
    heads = d // HEAD
    assert ada_w.shape[0] == 2 and b + 1 <= ADA_ROWS

    cond = jnp.zeros((ADA_ROWS, d), F32).at[:b].set(c).at[b].set(c_ctx)
    mods = _ada_call(cond, ada_w, ada_b)
    x_lat = x.reshape(b * t, d)
    x_ctx = ctx.reshape(b * tc, d)

    (sh_ml, sc_ml, gt_ml, sh_fl, sc_fl, gt_fl), (sh_mc, sc_mc, gt_mc, sh_fc, sc_fc, gt_fc) = _split_mods(mods[0], b)
    lower_bound = jnp.cumsum(jax.nn.softmax(hgrn_lb_logits.astype(F32), axis=1), axis=1)[:, 0]
    hgrn_w_in, hgrn_w_out, attn_w_out = hgrn_w_in.astype(BF16), hgrn_w_out.astype(BF16), attn_w_out.astype(BF16)
    ffn_w_in, ffn_w_out = ffn_w_in.astype(BF16), ffn_w_out.astype(BF16)
    p_lat = _nmm_call(x_lat, norm_mix_pre[0], sh_ml, sc_ml, hgrn_w_in, name="hgrn_in_lat")
    p_ctx = _nmm_call(x_ctx, norm_mix_pre[0], sh_mc, sc_mc, hgrn_w_in, name="hgrn_in_ctx")
    y_ctx, y_lat = _gla_call(p_ctx.reshape(b, tc, 5 * d), p_lat.reshape(b, t, 5 * d), lower_bound, hgrn_o_norm[0])
    x_lat = _proj_res_call(y_lat.reshape(b * t, d), hgrn_w_out, x_lat, norm_mix_post[0], gt_ml, name="hgrn_out_lat")
    x_ctx = _proj_res_call(y_ctx.reshape(b * tc, d), hgrn_w_out, x_ctx, norm_mix_post[0], gt_mc, name="hgrn_out_ctx")
    a = _ffn_in_call(x_lat, norm_ffn_pre[0], sh_fl, sc_fl, ffn_w_in, ffn_conv_w[0], ffn_conv_b[0],
                     layer=0, seq_len=t, name="ffn0_in_lat")
    x_lat = _proj_res_call(a, ffn_w_out, x_lat, norm_ffn_post[0], gt_fl, layer=0, name="ffn0_out_lat")
    a = _ffn_in_call(x_ctx, norm_ffn_pre[0], sh_fc, sc_fc, ffn_w_in, ffn_conv_w[0], ffn_conv_b[0],
                     layer=0, seq_len=tc, name="ffn0_in_ctx")
    x_ctx = _proj_res_call(a, ffn_w_out, x_ctx, norm_ffn_post[0], gt_fc, layer=0, name="ffn0_out_ctx")

    (sh_ml, sc_ml, gt_ml, sh_fl, sc_fl, gt_fl), (sh_mc, sc_mc, _, _, _, _) = _split_mods(mods[1], b)
    n_qkv = attn_w_qkv.shape[2]
    kvh = (n_qkv // HEAD - heads) // 2
    perm = jnp.concatenate([jnp.arange(0, HEAD, 2), jnp.arange(1, HEAD, 2)])
    head_perm = (jnp.arange(heads + kvh)[:, None] * HEAD + perm[None, :]).reshape(-1)
    col_perm = jnp.concatenate([head_perm, jnp.arange((heads + kvh) * HEAD, n_qkv)])
    w_qkv = attn_w_qkv[:, :, col_perm].astype(BF16)
    head_gain = jnp.concatenate([
        jnp.tile(attn_q_norm[0][perm] * (LOG2_E * HEAD ** -0.5), heads),
        jnp.tile(attn_k_norm[0][perm], kvh),
        jnp.ones((kvh * HEAD,), F32),
    ]).reshape(1, n_qkv)
    cos, sin = _rope_tables(t)
    qk_cols = (heads + kvh) * HEAD
    q_cols = heads * HEAD
    qkv = _nmm_call(x_lat, norm_mix_pre[1], sh_ml, sc_ml, w_qkv, name="attn_qkv_lat")
    qkv = _qk_rope_call(qkv, head_gain[:, :qk_cols], cos, sin, name="attn_qk_rope_lat")
    kv_ctx = _nmm_call(x_ctx, norm_mix_pre[1], sh_mc, sc_mc, w_qkv, col0=q_cols, name="attn_kv_ctx")
    kv_ctx = _qk_rope_call(kv_ctx, head_gain[:, q_cols:qk_cols], jnp.ones((tc, HEAD), F32),
                           jnp.zeros((tc, HEAD), F32), name="attn_k_norm_ctx")
    qkv = qkv.reshape(b, t, n_qkv)
    kv_ctx = kv_ctx.reshape(b, tc, 2 * kvh * HEAD)
    o = _flash_call(qkv, kv_ctx, heads=heads)
    x_lat = _proj_res_call(o.reshape(b * t, d), attn_w_out, x_lat, norm_mix_post[1], gt_ml, name="attn_out_lat")
    a = _ffn_in_call(x_lat, norm_ffn_pre[1], sh_fl, sc_fl, ffn_w_in, ffn_conv_w[1], ffn_conv_b[1],
                     layer=1, seq_len=t, name="ffn1_in_lat")
    x_lat = _proj_res_call(a, ffn_w_out, x_lat, norm_ffn_post[1], gt_fl, layer=1, name="ffn1_out_lat")
    return x_lat.reshape(b, t, d)
```

```python
import functools
import math

import jax
import jax.numpy as jnp
from jax import lax
from jax.experimental import pallas as pl
from jax.experimental.pallas import tpu as pltpu

F32 = jnp.float32
BF16 = jnp.bfloat16

EPS = 1e-6
GRID_W = 64
ROPE_THETA = 10000.0
LOG2_E = math.log2(math.e)
N_MOD = 6
HEAD = 128
ROW_GROUP = 16
GLA_CHUNK = 128
GLA_RUN = 32
GLA_SAFE_LB = math.exp(-80.0 / (GLA_CHUNK // 2))
F32_SUBLANES = 8
BF16_SUBLANES = 16
ADA_ROWS = F32_SUBLANES
V7X_VMEM_BYTES = 64 * 1024 * 1024
VMEM_LIMIT = V7X_VMEM_BYTES - 6 * 1024 * 1024


def _pick(n, target, align):
    if n <= target:
        return n
    for t in range(target - target % align, 0, -align):
        if n % t == 0:
            return t
    return n


def _params(sem):
    return pltpu.CompilerParams(dimension_semantics=sem, vmem_limit_bytes=VMEM_LIMIT)


def _sigmoid(x):
    return 0.5 * jnp.tanh(0.5 * x) + 0.5


def _silu(x):
    return x * _sigmoid(x)


def _dot(a, b):
    return jnp.dot(a, b, preferred_element_type=F32)


def _dot_nt(a, b):
    return lax.dot_general(a, b, (((1,), (1,)), ((), ())), preferred_element_type=F32)


def _dot_tn(a, b):
    return lax.dot_general(a, b, (((0,), (0,)), ((), ())), preferred_element_type=F32)


def _ada_kernel(c_ref, w_ref, b_ref, o_ref):
    sc = _silu(c_ref[...]).astype(BF16)
    o_ref[...] = _dot(sc, w_ref[...].astype(BF16)) + b_ref[...]


def _ada_call(cond, ada_w, ada_b):
    depth, d, n = ada_w.shape
    tn = _pick(n, 1024, HEAD)
    return pl.pallas_call(
        _ada_kernel,
        grid=(depth, n // tn),
        in_specs=[
            pl.BlockSpec((ADA_ROWS, d), lambda l, j: (0, 0)),
            pl.BlockSpec((None, d, tn), lambda l, j: (l, 0, j)),
            pl.BlockSpec((None, 1, tn), lambda l, j: (l, 0, j)),
        ],
        out_specs=pl.BlockSpec((None, ADA_ROWS, tn), lambda l, j: (l, 0, j)),
        out_shape=jax.ShapeDtypeStruct((depth, ADA_ROWS, n), F32),
        compiler_params=_params(("arbitrary", "arbitrary")),
        name="ada_ln",
    )(cond, ada_w, ada_b.reshape(depth, 1, n))


def _norm_mod(x, gain, shift, scale):
    ms = jnp.mean(x * x, axis=-1, keepdims=True)
    return (x * lax.rsqrt(ms + EPS) * gain) * (1.0 + scale) + shift


def _norm_mod_rows(x_ref, n_rows, h_ref, h_row0, gain, shift, scale):
    mult = gain * (1.0 + scale)

    def body(r, carry):
        start = pl.multiple_of(r * ROW_GROUP, ROW_GROUP)
        x = x_ref[pl.ds(start, ROW_GROUP), :]
        ms = jnp.mean(x * x, axis=-1, keepdims=True)
        dst = pl.ds(pl.multiple_of(h_row0 + start, ROW_GROUP), ROW_GROUP)
        h_ref[dst, :] = (x * lax.rsqrt(ms + EPS) * mult + shift).astype(BF16)
        return carry

    groups = n_rows // ROW_GROUP
    lax.fori_loop(0, groups, body, 0, unroll=math.gcd(groups, 4))


def _split_rows(tm):
    half = tm // 2
    return half if half % ROW_GROUP == 0 else tm


def _nmm_kernel(x_ref, gain_ref, shift_ref, scale_ref, w_ref, o_ref, h_ref):
    j = pl.program_id(1)
    tm = x_ref.shape[0]
    lead = _split_rows(tm)

    @pl.when(j == 0)
    def _():
        gain, shift, scale = gain_ref[...], shift_ref[...], scale_ref[...]
        _norm_mod_rows(x_ref, lead, h_ref, 0, gain, shift, scale)
        if lead < tm:
            h_ref[lead:, :] = _norm_mod(x_ref[lead:, :], gain, shift, scale).astype(BF16)
        o_ref[:lead, :] = _dot(h_ref[:lead, :], w_ref[...]).astype(o_ref.dtype)
        if lead < tm:
            o_ref[lead:, :] = _dot(h_ref[lead:, :], w_ref[...]).astype(o_ref.dtype)

    @pl.when(j > 0)
    def _():
        o_ref[...] = _dot(h_ref[...], w_ref[...]).astype(o_ref.dtype)


def _nmm_call(x, gain, shift, scale, w, *, layer=0, col0=0, name):
    m, d = x.shape
    n = w.shape[2] - col0
    bm = shift.shape[0]
    rows_per_mod = m // bm
    tm = _pick(rows_per_mod, 1024, BF16_SUBLANES)
    tn = _pick(math.gcd(n, col0) if col0 else n, 2048, HEAD)
    mod_blocks = rows_per_mod // tm
    col_block0 = col0 // tn
    return pl.pallas_call(
        _nmm_kernel,
        grid=(m // tm, n // tn),
        in_specs=[
            pl.BlockSpec((tm, d), lambda i, j: (i, 0)),
            pl.BlockSpec((1, d), lambda i, j: (0, 0)),
            pl.BlockSpec((None, 1, d), lambda i, j: (i // mod_blocks, 0, 0)),
            pl.BlockSpec((None, 1, d), lambda i, j: (i // mod_blocks, 0, 0)),
            pl.BlockSpec((None, d, tn), lambda i, j: (layer, 0, col_block0 + j)),
        ],
        out_specs=pl.BlockSpec((tm, tn), lambda i, j: (i, j)),
        out_shape=jax.ShapeDtypeStruct((m, n), BF16),
        scratch_shapes=[pltpu.VMEM((tm, d), BF16)],
        compiler_params=_params(("arbitrary", "arbitrary")),
        name=name,
    )(x, gain.reshape(1, d), shift, scale, w)


def _qk_rope_kernel(x_ref, hg_ref, cos_ref, sin_ref, o_ref):
    cos = cos_ref[...]
    sin = sin_ref[...]
    mean_lanes = jnp.full((HEAD, HEAD), 1.0 / HEAD, BF16)
    for hh in range(x_ref.shape[1] // HEAD):
        cols = slice(hh * HEAD, (hh + 1) * HEAD)
        a = x_ref[:, cols].astype(F32)
        ms = _dot((a * a).astype(BF16), mean_lanes)
        a = a * lax.rsqrt(ms + EPS) * hg_ref[:, cols]
        a = a * cos + pltpu.roll(a, HEAD // 2, axis=1) * sin
        o_ref[:, cols] = a.astype(o_ref.dtype)


def _qk_rope_call(p, head_gain, cos, sin, *, name):
    m, n = p.shape
    nn = head_gain.shape[1]
    tm = _pick(math.gcd(m, cos.shape[0]), 2048, BF16_SUBLANES)
    tn = _pick(math.gcd(n, nn), 1024, HEAD)
    table_blocks = cos.shape[0] // tm
    return pl.pallas_call(
        _qk_rope_kernel,
        grid=(m // tm, nn // tn),
        in_specs=[
            pl.BlockSpec((tm, tn), lambda i, j: (i, j)),
            pl.BlockSpec((1, tn), lambda i, j: (0, j)),
            pl.BlockSpec((tm, HEAD), lambda i, j: (i % table_blocks, 0)),
            pl.BlockSpec((tm, HEAD), lambda i, j: (i % table_blocks, 0)),
        ],
        out_specs=pl.BlockSpec((tm, tn), lambda i, j: (i, j)),
        out_shape=jax.ShapeDtypeStruct((m, n), p.dtype),
        input_output_aliases={0: 0},
        compiler_params=_params(("arbitrary", "arbitrary")),
        name=name,
    )(p, head_gain, cos, sin)


def _residual(x, y, gain, gate):
    ms = jnp.mean(y * y, axis=-1, keepdims=True)
    return x + gate * (y * lax.rsqrt(ms + EPS) * gain)


def _proj_res_kernel(a_ref, w_ref, x_ref, gain_ref, gate_ref, o_ref):
    y = _dot(a_ref[...], w_ref[...])
    o_ref[...] = _residual(x_ref[...], y, gain_ref[...], gate_ref[...])


def _proj_res_ktiled_kernel(a_ref, w_ref, x_ref, gain_ref, gate_ref, o_ref):
    k = pl.program_id(1)
    last = pl.num_programs(1) - 1
    tm = o_ref.shape[0]
    lead = _split_rows(tm)

    @pl.when(k == 0)
    def _():
        o_ref[...] = _dot(a_ref[...], w_ref[...])

    @pl.when(jnp.logical_and(k > 0, k < last))
    def _():
        o_ref[...] += _dot(a_ref[...], w_ref[...])

    @pl.when(k == last)
    def _():
        gain, gate = gain_ref[...], gate_ref[...]
        parts = [slice(0, lead), slice(lead, tm)] if lead < tm else [slice(0, tm)]
        ys = [o_ref[r, :] + _dot(a_ref[r, :], w_ref[...]) for r in parts]
        for r, y in zip(parts, ys):
            o_ref[r, :] = _residual(x_ref[r, :], y, gain, gate)


def _proj_res_call(a, w, x, gain, gate, *, layer=0, name):
    m, k = a.shape
    d = w.shape[2]
    bm = gate.shape[0]
    rows_per_mod = m // bm
    tk = k if k <= d else _pick(k, 512, 2 * HEAD)
    kb = k // tk
    tm = _pick(rows_per_mod, 512 if kb == 1 else 1024, BF16_SUBLANES)
    mod_blocks = rows_per_mod // tm
    return pl.pallas_call(
        _proj_res_kernel if kb == 1 else _proj_res_ktiled_kernel,
        grid=(m // tm, kb),
        in_specs=[
            pl.BlockSpec((tm, tk), lambda i, kk: (i, kk)),
            pl.BlockSpec((None, tk, d), lambda i, kk: (layer, kk, 0)),
            pl.BlockSpec((tm, d), lambda i, kk: (i, 0)),
            pl.BlockSpec((1, d), lambda i, kk: (0, 0)),
            pl.BlockSpec((None, 1, d), lambda i, kk: (i // mod_blocks, 0, 0)),
        ],
        out_specs=pl.BlockSpec((tm, d), lambda i, kk: (i, 0)),
        out_shape=jax.ShapeDtypeStruct((m, d), F32),
        compiler_params=_params(("arbitrary", "arbitrary")),
        name=name,
    )(a, w, x, gain.reshape(1, d), gate)


def _ffn_in_kernel(x_ref, xprev_ref, xnext_ref, gain_ref, shift_ref, scale_ref, wg_ref, wu_ref, cw_ref, cb_ref,
                   o_ref, h_ref, *, blocks_per_seq):
    i = pl.program_id(0)
    j = pl.program_id(1)
    tm = x_ref.shape[0]
    halo = BF16_SUBLANES
    lead = _split_rows(tm)

    def gated(gate, up):
        rows = gate.shape[0]
        g_prev = pltpu.roll(gate, 1, axis=0)[halo:halo + tm, :]
        g_next = pltpu.roll(gate, rows - 1, axis=0)[halo:halo + tm, :]
        conv = (g_prev * cw_ref[0:1, :] + gate[halo:halo + tm, :] * cw_ref[1:2, :] + g_next * cw_ref[2:3, :]
                + cb_ref[...])
        o_ref[...] = (_silu(conv) * up).astype(o_ref.dtype)

    @pl.when(j == 0)
    def _():
        gain, shift, scale = gain_ref[...], shift_ref[...], scale_ref[...]
        seq_block = i % blocks_per_seq
        h_prev = jnp.where(seq_block > 0, _norm_mod(xprev_ref[...], gain, shift, scale), 0.0)
        h_next = jnp.where(seq_block < blocks_per_seq - 1, _norm_mod(xnext_ref[...], gain, shift, scale), 0.0)
        zeros = jnp.zeros((halo - h_prev.shape[0], h_prev.shape[1]), F32)
        h_ref[0:halo, :] = jnp.concatenate([zeros, h_prev], axis=0).astype(BF16)
        h_ref[halo + tm:, :] = jnp.concatenate([h_next, zeros], axis=0).astype(BF16)
        _norm_mod_rows(x_ref, lead, h_ref, halo, gain, shift, scale)
        if lead == tm:
            gated(_dot(h_ref[...], wg_ref[...]), _dot(h_ref[halo:halo + tm, :], wu_ref[...]))
        else:
            cut = halo + lead
            h_ref[cut:halo + tm, :] = _norm_mod(x_ref[lead:, :], gain, shift, scale).astype(BF16)
            gate = jnp.concatenate([_dot(h_ref[:cut, :], wg_ref[...]), _dot(h_ref[cut:, :], wg_ref[...])], axis=0)
            up = jnp.concatenate([_dot(h_ref[halo:cut, :], wu_ref[...]),
                                  _dot(h_ref[cut:halo + tm, :], wu_ref[...])], axis=0)
            gated(gate, up)

    @pl.when(j > 0)
    def _():
        gated(_dot(h_ref[...], wg_ref[...]), _dot(h_ref[halo:halo + tm, :], wu_ref[...]))


def _ffn_in_call(x, gain, shift, scale, w, conv_w, conv_b, *, layer, seq_len, name):
    m, d = x.shape
    f = w.shape[2] // 2
    bm = shift.shape[0]
    rows_per_mod = m // bm
    tm = _pick(seq_len, 1024, BF16_SUBLANES)
    tn = _pick(f, 512, HEAD)
    assert rows_per_mod % tm == 0
    mod_blocks = rows_per_mod // tm
    nb = f // tn
    xh = F32_SUBLANES
    tiles_per_tm = tm // xh
    last_tile = m // xh - 1
    body = functools.partial(_ffn_in_kernel, blocks_per_seq=seq_len // tm)
    return pl.pallas_call(
        body,
        grid=(m // tm, nb),
        in_specs=[
            pl.BlockSpec((tm, d), lambda i, j: (i, 0)),
            pl.BlockSpec((xh, d), lambda i, j: (jnp.maximum(i * tiles_per_tm - 1, 0), 0)),
            pl.BlockSpec((xh, d), lambda i, j: (jnp.minimum((i + 1) * tiles_per_tm, last_tile), 0)),
            pl.BlockSpec((1, d), lambda i, j: (0, 0)),
            pl.BlockSpec((None, 1, d), lambda i, j: (i // mod_blocks, 0, 0)),
            pl.BlockSpec((None, 1, d), lambda i, j: (i // mod_blocks, 0, 0)),
            pl.BlockSpec((None, d, tn), lambda i, j: (layer, 0, j)),
            pl.BlockSpec((None, d, tn), lambda i, j: (layer, 0, nb + j)),
            pl.BlockSpec((3, tn), lambda i, j: (0, j)),
            pl.BlockSpec((1, tn), lambda i, j: (0, j)),
        ],
        out_specs=pl.BlockSpec((tm, tn), lambda i, j: (i, j)),
        out_shape=jax.ShapeDtypeStruct((m, f), BF16),
        scratch_shapes=[pltpu.VMEM((tm + 2 * BF16_SUBLANES, d), BF16)],
        compiler_params=_params(("arbitrary", "arbitrary")),
        name=name,
    )(x, x, x, gain.reshape(1, d), shift, scale, w, w, conv_w, conv_b.reshape(1, f))


def _gla_gates(f_raw, lb, within):
    half = 0.5 * (1.0 - lb)
    ht = half * jnp.tanh(0.5 * f_raw.astype(F32))
    lf = jnp.log((1.0 - half) + ht)
    ones = jnp.where(within, 1.0, 0.0).astype(BF16)
    lf_hi = lf.astype(BF16)
    lf_lo = (lf - lf_hi.astype(F32)).astype(BF16)
    return half - ht, _dot(ones, lf_hi) + _dot(ones, lf_lo)


def _gla_pairs(qs, kk, b, forward, within):
    c = qs.shape[0]
    a_row = c // 2 - 1 if forward else c // 2
    e_row = c - 1 if forward else 0
    anchor = b[a_row:a_row + 1, :]
    b_end = b[e_row:e_row + 1, :]
    qa = qs * jnp.exp(b - anchor)
    ka = kk * jnp.exp(anchor - b)
    scores = jnp.where(within, _dot_nt(qa.astype(BF16), ka.astype(BF16)), 0.0).astype(BF16)
    return (scores, (qa * jnp.exp(anchor)).astype(BF16), (ka * jnp.exp(b_end - anchor)).astype(BF16),
            jnp.exp(b_end))


def _gla_pairs_exact(qs, kk, b, forward, within, tmp):
    tb_ref, tq_ref, ts_ref = tmp
    c = qs.shape[0]
    b_end = b[c - 1:c, :] if forward else b[0:1, :]
    tb_ref[...] = b
    tq_ref[...] = qs
    ones = jnp.ones((F32_SUBLANES, qs.shape[1]), BF16)

    def row(t, carry):
        rel = jnp.minimum(tb_ref[pl.ds(t, 1), :] - b, 0.0)
        e = (tq_ref[pl.ds(t, 1), :] * kk) * jnp.exp(rel)
        ts_ref[pl.ds(t, 1), :] = _dot_nt(ones, e.astype(BF16))[0:1, :]
        return carry

    lax.fori_loop(0, c, row, 0)
    scores = jnp.where(within, ts_ref[...], 0.0).astype(BF16)
    return scores, (qs * jnp.exp(b)).astype(BF16), (kk * jnp.exp(b_end - b)).astype(BF16), jnp.exp(b_end)


def _gla_readout(o, g_raw, o_gain):
    ms = jnp.mean(o * o, axis=-1, keepdims=True)
    y = o * lax.rsqrt(ms + EPS) * o_gain
    return (y * _silu(g_raw.astype(F32))).astype(BF16)


def _gla_kernel(qc, ffc, fbc, vc, gc, ql, ffl, fbl, vl, gl, lb_ref, og_ref, yc_ref, yl_ref,
                o_ref, qb_ref, ke_ref, dec_ref, st_ref, tb_ref, tq_ref, ts_ref):
    c = GLA_CHUNK
    tc = qc.shape[0]
    n_ctx = tc // c
    n_lat = ql.shape[0] // c
    lb = (lb_ref[0], lb_ref[1])
    row = lax.broadcasted_iota(jnp.int32, (c, c), 0)
    col = lax.broadcasted_iota(jnp.int32, (c, c), 1)
    within = (col <= row, col >= row)

    def rows(start):
        return pl.ds(start if isinstance(start, int) else pl.multiple_of(start, c), c)

    def local(refs, src0, dst0, chunk0, n, exact):
        q, ff, fb, v = refs
        src = [rows(src0 + j * c) for j in range(n)]
        dst = [rows(dst0 + j * c) for j in range(n)]
        qs = [_silu(q[src[j], :].astype(F32)) * (HEAD ** -0.5) for j in range(n)]
        gates = [[_gla_gates(f[src[j], :], lb[d], within[d]) for d, f in enumerate((ff, fb))] for j in range(n)]
        scores = []
        for j in range(n):
            per_dir = []
            for d in range(2):
                kk, b = gates[j][d]
                if exact:
                    sc, qb, ke, dec = _gla_pairs_exact(qs[j], kk, b, d == 0, within[d], (tb_ref, tq_ref, ts_ref))
                else:
                    sc, qb, ke, dec = _gla_pairs(qs[j], kk, b, d == 0, within[d])
                qb_ref[d, dst[j], :] = qb
                ke_ref[d, dst[j], :] = ke
                dec_ref[d, chunk0 + j] = dec
                per_dir.append(sc)
            scores.append(per_dir)
        for j in range(n):
            vj = v[src[j], :]
            o_ref[dst[j], :] = _dot(scores[j][0], vj) + _dot(scores[j][1], vj)

    def phase1(exact):
        run = 1 if exact else math.gcd(n_lat, GLA_RUN)
        run_ctx = math.gcd(n_ctx, run)
        for i in range(0, n_ctx, run_ctx):
            local((qc, ffc, fbc, vc), i * c, i * c, i, run_ctx, exact)

        def body(i, carry):
            src0 = pl.multiple_of(i * (run * c), c)
            local((ql, ffl, fbl, vl), src0, tc + src0, n_ctx + i * run, run, exact)
            return carry

        lax.fori_loop(0, n_lat // run, body, 0)

    safe = jnp.min(jnp.minimum(lb[0], lb[1])) >= GLA_SAFE_LB

    @pl.when(safe)
    def _():
        phase1(False)

    @pl.when(jnp.logical_not(safe))
    def _():
        phase1(True)

    st_ref[...] = jnp.zeros_like(st_ref)

    o_gain = og_ref[...]

    def steps(v, first, n, n_seg, row0, chunk0, finish=None):
        order = [[first + s for s in range(n)], [n_seg - 1 - first - s for s in range(n)]]
        src = [[rows(j * c) for j in order[d]] for d in range(2)]
        dst = [[rows(row0 + j * c) for j in order[d]] for d in range(2)]
        kv = [[_dot_tn(v[src[d][s], :], ke_ref[d, dst[d][s], :]) for s in range(n)] for d in range(2)]
        for d in range(2):
            st = st_ref[d]
            for s in range(n):
                o = o_ref[dst[d][s], :] + _dot_nt(qb_ref[d, dst[d][s], :], st.astype(BF16))
                if finish is None:
                    o_ref[dst[d][s], :] = o
                else:
                    g, y = finish
                    y[src[d][s], :] = _gla_readout(o, g[src[d][s], :], o_gain)
                st = st * dec_ref[d, chunk0 + order[d][s]] + kv[d][s]
            st_ref[d] = st

    steps(vc, 0, n_ctx, n_ctx, 0, 0)
    for i in range(n_ctx):
        r = slice(i * c, (i + 1) * c)
        yc_ref[r, :] = _gla_readout(o_ref[r, :], gc[r, :], o_gain)

    run = math.gcd(n_lat, GLA_RUN)
    n_runs = n_lat // run
    meet = (n_runs + 1) // 2

    def steps_body(i, carry):
        steps(vl, i * run, run, n_lat, tc, n_ctx)
        return carry

    def last_steps_body(i, carry):
        steps(vl, i * run, run, n_lat, tc, n_ctx, finish=(gl, yl_ref))
        return carry

    lax.fori_loop(0, meet, steps_body, 0)
    if n_runs % 2:
        for j in range((n_runs // 2) * run, (n_runs // 2 + 1) * run):
            yl_ref[rows(j * c), :] = _gla_readout(o_ref[rows(tc + j * c), :], gl[rows(j * c), :], o_gain)
    lax.fori_loop(meet, n_runs, last_steps_body, 0)


def _gla_call(p_ctx, p_lat, lower_bound, o_gain):
    b, tc, d5 = p_ctx.shape
    t = p_lat.shape[1]
    d = d5 // 5
    heads = d // HEAD
    assert tc % GLA_CHUNK == 0 and t % GLA_CHUNK == 0

    def col(tt, part):
        return pl.BlockSpec((None, tt, HEAD), lambda bi, h: (bi, 0, part * heads + h))

    out_spec = lambda tt: pl.BlockSpec((None, tt, HEAD), lambda bi, h: (bi, 0, h))
    return pl.pallas_call(
        _gla_kernel,
        grid=(b, heads),
        in_specs=[col(tc, p) for p in range(5)] + [col(t, p) for p in range(5)] + [
            pl.BlockSpec((2, None, 1, HEAD), lambda bi, h: (0, h, 0, 0)),
            pl.BlockSpec((1, HEAD), lambda bi, h: (0, 0)),
        ],
        out_specs=[out_spec(tc), out_spec(t)],
        out_shape=[jax.ShapeDtypeStruct((b, tc, d), BF16), jax.ShapeDtypeStruct((b, t, d), BF16)],
        scratch_shapes=[
            pltpu.VMEM((tc + t, HEAD), F32),
            pltpu.VMEM((2, tc + t, HEAD), BF16),
            pltpu.VMEM((2, tc + t, HEAD), BF16),
            pltpu.VMEM((2, (tc + t) // GLA_CHUNK, 1, HEAD), F32),
            pltpu.VMEM((2, HEAD, HEAD), F32),
            pltpu.VMEM((GLA_CHUNK, HEAD), F32),
            pltpu.VMEM((GLA_CHUNK, HEAD), F32),
            pltpu.VMEM((GLA_CHUNK, GLA_CHUNK), F32),
        ],
        compiler_params=_params(("arbitrary", "arbitrary")),
        name="hgrn_scan",
    )(*([p_ctx] * 5), *([p_lat] * 5), lower_bound.reshape(2, heads, 1, HEAD), o_gain.reshape(1, HEAD))


def _flash_kernel(q_ref, kc_ref, vc_ref, kl_ref, vl_ref, o_ref, *, tk):
    tq = q_ref.shape[0]
    group = q_ref.shape[1] // HEAD
    qs = [q_ref[:, g * HEAD:(g + 1) * HEAD] for g in range(group)]

    def attend(carry, k, v):
        v_ext = jnp.concatenate([v, jnp.ones_like(v)], axis=1)
        out = []
        for g in range(group):
            m, acc = carry[g]
            s = _dot_nt(qs[g], k)
            m_new = jnp.maximum(m, jnp.max(s, axis=-1, keepdims=True))
            p = jnp.exp2(s - m_new).astype(BF16)
            out.append((m_new, jnp.exp2(m - m_new) * acc + _dot(p, v_ext)))
        return tuple(out)

    carry = tuple((jnp.full((tq, 1), -jnp.inf, F32), jnp.zeros((tq, 2 * HEAD), F32)) for _ in range(group))
    for start in range(0, kl_ref.shape[0], tk):
        carry = attend(carry, kl_ref[start:start + tk, :], vl_ref[start:start + tk, :])
    carry = attend(carry, kc_ref[...], vc_ref[...])
    for g in range(group):
        acc = carry[g][1]
        o_ref[:, g * HEAD:(g + 1) * HEAD] = (acc[:, :HEAD] / acc[:, HEAD:]).astype(o_ref.dtype)


def _flash_call(qkv, kv_ctx, *, heads):
    b, t, n = qkv.shape
    tc = kv_ctx.shape[1]
    kvh = (n // HEAD - heads) // 2
    group = heads // kvh
    tq = _pick(t, 512, BF16_SUBLANES)
    tk = _pick(t, 1024, 2 * HEAD)
    assert t // tk <= 16
    body = functools.partial(_flash_kernel, tk=tk)
    return pl.pallas_call(
        body,
        grid=(b, kvh, t // tq),
        in_specs=[
            pl.BlockSpec((None, tq, group * HEAD), lambda bi, h, i: (bi, i, h)),
            pl.BlockSpec((None, tc, HEAD), lambda bi, h, i: (bi, 0, h)),
            pl.BlockSpec((None, tc, HEAD), lambda bi, h, i: (bi, 0, kvh + h)),
            pl.BlockSpec((None, t, HEAD), lambda bi, h, i: (bi, 0, heads + h)),
            pl.BlockSpec((None, t, HEAD), lambda bi, h, i: (bi, 0, heads + kvh + h)),
        ],
        out_specs=pl.BlockSpec((None, tq, group * HEAD), lambda bi, h, i: (bi, i, h)),
        out_shape=jax.ShapeDtypeStruct((b, t, heads * HEAD), BF16),
        compiler_params=_params(("arbitrary", "arbitrary", "arbitrary")),
        name="gqa_flash",
    )(qkv, kv_ctx, kv_ctx, qkv, qkv)


def _rope_tables(t):
    pos = jnp.arange(t, dtype=jnp.int32)
    rows = (pos // GRID_W).astype(F32)
    cols = (pos % GRID_W).astype(F32)
    axis_dim = HEAD // 2
    inv_freq = ROPE_THETA ** (-jnp.arange(0, axis_dim, 2, dtype=F32) / axis_dim)
    ang = jnp.concatenate([rows[:, None] * inv_freq, cols[:, None] * inv_freq], axis=-1)
    cos, sin = jnp.cos(ang), jnp.sin(ang)
    return jnp.concatenate([cos, cos], axis=-1), jnp.concatenate([-sin, sin], axis=-1)


def _split_mods(mods, b):
    d = mods.shape[1] // N_MOD
    lat = [mods[:b, i * d:(i + 1) * d].reshape(b, 1, d) for i in range(N_MOD)]
    ctx = [mods[b:b + 1, i * d:(i + 1) * d].reshape(1, 1, d) for i in range(N_MOD)]
    return lat, ctx


def kernel(x, c, ctx, c_ctx, ada_w, ada_b, norm_mix_pre, norm_mix_post, norm_ffn_pre, norm_ffn_post, hgrn_w_in, hgrn_lb_logits, hgrn_o_norm, hgrn_w_out, attn_w_qkv, attn_q_norm, attn_k_norm, attn_w_out, ffn_w_in, ffn_conv_w, ffn_conv_b, ffn_w_out):
    b, t, d = x.shape
    tc = ctx.shape[1]
    heads = d // HEAD
    assert ada_w.shape[0] == 2 and b + 1 <= ADA_ROWS

    cond = jnp.zeros((ADA_ROWS, d), F32).at[:b].set(c).at[b].set(c_ctx)
    mods = _ada_call(cond, ada_w, ada_b)
    x_lat = x.reshape(b * t, d)
    x_ctx = ctx.reshape(b * tc, d)

    (sh_ml, sc_ml, gt_ml, sh_fl, sc_fl, gt_fl), (sh_mc, sc_mc, gt_mc, sh_fc, sc_fc, gt_fc) = _split_mods(mods[0], b)
    lower_bound = jnp.cumsum(jax.nn.softmax(hgrn_lb_logits.astype(F32), axis=1), axis=1)[:, 0]
    hgrn_w_in, hgrn_w_out, attn_w_out = hgrn_w_in.astype(BF16), hgrn_w_out.astype(BF16), attn_w_out.astype(BF16)
    ffn_w_in, ffn_w_out = ffn_w_in.astype(BF16), ffn_w_out.astype(BF16)
    p_lat = _nmm_call(x_lat, norm_mix_pre[0], sh_ml, sc_ml, hgrn_w_in, name="hgrn_in_lat")
    p_ctx = _nmm_call(x_ctx, norm_mix_pre[0], sh_mc, sc_mc, hgrn_w_in, name="hgrn_in_ctx")
    y_ctx, y_lat = _gla_call(p_ctx.reshape(b, tc, 5 * d), p_lat.reshape(b, t, 5 * d), lower_bound, hgrn_o_norm[0])
    x_lat = _proj_res_call(y_lat.reshape(b * t, d), hgrn_w_out, x_lat, norm_mix_post[0], gt_ml, name="hgrn_out_lat")
    x_ctx = _proj_res_call(y_ctx.reshape(b * tc, d), hgrn_w_out, x_ctx, norm_mix_post[0], gt_mc, name="hgrn_out_ctx")
    a = _ffn_in_call(x_lat, norm_ffn_pre[0], sh_fl, sc_fl, ffn_w_in, ffn_conv_w[0], ffn_conv_b[0],
                     layer=0, seq_len=t, name="ffn0_in_lat")
    x_lat = _proj_res_call(a, ffn_w_out, x_lat, norm_ffn_post[0], gt_fl, layer=0, name="ffn0_out_lat")
    a = _ffn_in_call(x_ctx, norm_ffn_pre[0], sh_fc, sc_fc, ffn_w_in, ffn_conv_w[0], ffn_conv_b[0],
                     layer=0, seq_len=tc, name="ffn0_in_ctx")
    x_ctx = _proj_res_call(a, ffn_w_out, x_ctx, norm_ffn_post[0], gt_fc, layer=0, name="ffn0_out_ctx")

    (sh_ml, sc_ml, gt_ml, sh_fl, sc_fl, gt_fl), (sh_mc, sc_mc, _, _, _, _) = _split_mods(mods[1], b)
    n_qkv = attn_w_qkv.shape[2]
    kvh = (n_qkv // HEAD - heads) // 2
    perm = jnp.concatenate([jnp.arange(0, HEAD, 2), jnp.arange(1, HEAD, 2)])
    head_perm = (jnp.arange(heads + kvh)[:, None] * HEAD + perm[None, :]).reshape(-1)
    col_perm = jnp.concatenate([head_perm, jnp.arange((heads + kvh) * HEAD, n_qkv)])
    w_qkv = attn_w_qkv[:, :, col_perm].astype(BF16)
    head_gain = jnp.concatenate([
        jnp.tile(attn_q_norm[0][perm] * (LOG2_E * HEAD ** -0.5), heads),
        jnp.tile(attn_k_norm[0][perm], kvh),
        jnp.ones((kvh * HEAD,), F32),
    ]).reshape(1, n_qkv)
    cos, sin = _rope_tables(t)
    qk_cols = (heads + kvh) * HEAD
    q_cols = heads * HEAD
    qkv = _nmm_call(x_lat, norm_mix_pre[1], sh_ml, sc_ml, w_qkv, name="attn_qkv_lat")
    qkv = _qk_rope_call(qkv, head_gain[:, :qk_cols], cos, sin, name="attn_qk_rope_lat")
    kv_ctx = _nmm_call(x_ctx, norm_mix_pre[1], sh_mc, sc_mc, w_qkv, col0=q_cols, name="attn_kv_ctx")
    kv_ctx = _qk_rope_call(kv_ctx, head_gain[:, q_cols:qk_cols], jnp.ones((tc, HEAD), F32),
                           jnp.zeros((tc, HEAD), F32), name="attn_k_norm_ctx")
    qkv = qkv.reshape(b, t, n_qkv)
    kv_ctx = kv_ctx.reshape(b, tc, 2 * kvh * HEAD)
    o = _flash_call(qkv, kv_ctx, heads=heads)
    x_lat = _proj_res_call(o.reshape(b * t, d), attn_w_out, x_lat, norm_mix_post[1], gt_ml, name="attn_out_lat")
    a = _ffn_in_call(x_lat, norm_ffn_pre[1], sh_fl, sc_fl, ffn_w_in, ffn_conv_w[1], ffn_conv_b[1],
                     layer=1, seq_len=t, name="ffn1_in_lat")
    x_lat = _proj_res_call(a, ffn_w_out, x_lat, norm_ffn_post[1], gt_fl, layer=1, name="ffn1_out_lat")
    return x_lat.reshape(b, t, d)
```

```python
import functools
import math

import jax
import jax.numpy as jnp
from jax import lax
from jax.experimental import pallas as pl
from jax.experimental.pallas import tpu as pltpu

F32 = jnp.float32
BF16 = jnp.bfloat16

EPS = 1e-6
GRID_W = 64
ROPE_THETA = 10000.0
LOG2_E = math.log2(math.e)
N_MOD = 6
HEAD = 128
ROW_GROUP = 16
GLA_CHUNK = 128
GLA_RUN = 32
GLA_SAFE_LB = math.exp(-80.0 / (GLA_CHUNK // 2))
F32_SUBLANES = 8
BF16_SUBLANES = 16
ADA_ROWS = F32_SUBLANES
V7X_VMEM_BYTES = 64 * 1024 * 1024
VMEM_LIMIT = V7X_VMEM_BYTES - 6 * 1024 * 1024


def _pick(n, target, align):
    if n <= target:
        return n
    for t in range(target - target % align, 0, -align):
        if n % t == 0:
            return t
    return n


def _params(sem):
    return pltpu.CompilerParams(dimension_semantics=sem, vmem_limit_bytes=VMEM_LIMIT)


def _sigmoid(x):
    return 0.5 * jnp.tanh(0.5 * x) + 0.5


def _silu(x):
    return x * _sigmoid(x)


def _dot(a, b):
    return jnp.dot(a, b, preferred_element_type=F32)


def _dot_nt(a, b):
    return lax.dot_general(a, b, (((1,), (1,)), ((), ())), preferred_element_type=F32)


def _dot_tn(a, b):
    return lax.dot_general(a, b, (((0,), (0,)), ((), ())), preferred_element_type=F32)


def _ada_kernel(c_ref, w_ref, b_ref, o_ref):
    sc = _silu(c_ref[...]).astype(BF16)
    o_ref[...] = _dot(sc, w_ref[...].astype(BF16)) + b_ref[...]


def _ada_call(cond, ada_w, ada_b):
    depth, d, n = ada_w.shape
    tn = _pick(n, 1024, HEAD)
    return pl.pallas_call(
        _ada_kernel,
        grid=(depth, n // tn),
        in_specs=[
            pl.BlockSpec((ADA_ROWS, d), lambda l, j: (0, 0)),
            pl.BlockSpec((None, d, tn), lambda l, j: (l, 0, j)),
            pl.BlockSpec((None, 1, tn), lambda l, j: (l, 0, j)),
        ],
        out_specs=pl.BlockSpec((None, ADA_ROWS, tn), lambda l, j: (l, 0, j)),
        out_shape=jax.ShapeDtypeStruct((depth, ADA_ROWS, n), F32),
        compiler_params=_params(("arbitrary", "arbitrary")),
        name="ada_ln",
    )(cond, ada_w, ada_b.reshape(depth, 1, n))


def _norm_mod(x, gain, shift, scale):
    ms = jnp.mean(x * x, axis=-1, keepdims=True)
    return (x * lax.rsqrt(ms + EPS) * gain) * (1.0 + scale) + shift


def _norm_mod_rows(x_ref, n_rows, h_ref, h_row0, gain, shift, scale):
    mult = gain * (1.0 + scale)

    def body(r, carry):
        start = pl.multiple_of(r * ROW_GROUP, ROW_GROUP)
        x = x_ref[pl.ds(start, ROW_GROUP), :]
        ms = jnp.mean(x * x, axis=-1, keepdims=True)
        dst = pl.ds(pl.multiple_of(h_row0 + start, ROW_GROUP), ROW_GROUP)
        h_ref[dst, :] = (x * lax.rsqrt(ms + EPS) * mult + shift).astype(BF16)
        return carry

    groups = n_rows // ROW_GROUP
    lax.fori_loop(0, groups, body, 0, unroll=math.gcd(groups, 4))


def _split_rows(tm):
    half = tm // 2
    return half if half % ROW_GROUP == 0 else tm


def _nmm_kernel(x_ref, gain_ref, shift_ref, scale_ref, w_ref, o_ref, h_ref):
    j = pl.program_id(1)
    tm = x_ref.shape[0]
    lead = _split_rows(tm)

    @pl.when(j == 0)
    def _():
        gain, shift, scale = gain_ref[...], shift_ref[...], scale_ref[...]
        _norm_mod_rows(x_ref, lead, h_ref, 0, gain, shift, scale)
        if lead < tm:
            h_ref[lead:, :] = _norm_mod(x_ref[lead:, :], gain, shift, scale).astype(BF16)
        o_ref[:lead, :] = _dot(h_ref[:lead, :], w_ref[...]).astype(o_ref.dtype)
        if lead < tm:
            o_ref[lead:, :] = _dot(h_ref[lead:, :], w_ref[...]).astype(o_ref.dtype)

    @pl.when(j > 0)
    def _():
        o_ref[...] = _dot(h_ref[...], w_ref[...]).astype(o_ref.dtype)


def _nmm_call(x, gain, shift, scale, w, *, layer=0, col0=0, name):
    m, d = x.shape
    n = w.shape[2] - col0
    bm = shift.shape[0]
    rows_per_mod = m // bm
    tm = _pick(rows_per_mod, 1024, BF16_SUBLANES)
    tn = _pick(math.gcd(n, col0) if col0 else n, 2048, HEAD)
    mod_blocks = rows_per_mod // tm
    col_block0 = col0 // tn
    return pl.pallas_call(
        _nmm_kernel,
        grid=(m // tm, n // tn),
        in_specs=[
            pl.BlockSpec((tm, d), lambda i, j: (i, 0)),
            pl.BlockSpec((1, d), lambda i, j: (0, 0)),
            pl.BlockSpec((None, 1, d), lambda i, j: (i // mod_blocks, 0, 0)),
            pl.BlockSpec((None, 1, d), lambda i, j: (i // mod_blocks, 0, 0)),
            pl.BlockSpec((None, d, tn), lambda i, j: (layer, 0, col_block0 + j)),
        ],
        out_specs=pl.BlockSpec((tm, tn), lambda i, j: (i, j)),
        out_shape=jax.ShapeDtypeStruct((m, n), BF16),
        scratch_shapes=[pltpu.VMEM((tm, d), BF16)],
        compiler_params=_params(("arbitrary", "arbitrary")),
        name=name,
    )(x, gain.reshape(1, d), shift, scale, w)


def _qk_rope_kernel(x_ref, hg_ref, cos_ref, sin_ref, o_ref):
    cos = cos_ref[...]
    sin = sin_ref[...]
    mean_lanes = jnp.full((HEAD, HEAD), 1.0 / HEAD, BF16)
    for hh in range(x_ref.shape[1] // HEAD):
        cols = slice(hh * HEAD, (hh + 1) * HEAD)
        a = x_ref[:, cols].astype(F32)
        ms = _dot((a * a).astype(BF16), mean_lanes)
        a = a * lax.rsqrt(ms + EPS) * hg_ref[:, cols]
        a = a * cos + pltpu.roll(a, HEAD // 2, axis=1) * sin
        o_ref[:, cols] = a.astype(o_ref.dtype)


def _qk_rope_call(p, head_gain, cos, sin, *, name):
    m, n = p.shape
    nn = head_gain.shape[1]
    tm = _pick(math.gcd(m, cos.shape[0]), 2048, BF16_SUBLANES)
    tn = _pick(math.gcd(n, nn), 1024, HEAD)
    table_blocks = cos.shape[0] // tm
    return pl.pallas_call(
        _qk_rope_kernel,
        grid=(m // tm, nn // tn),
        in_specs=[
            pl.BlockSpec((tm, tn), lambda i, j: (i, j)),
            pl.BlockSpec((1, tn), lambda i, j: (0, j)),
            pl.BlockSpec((tm, HEAD), lambda i, j: (i % table_blocks, 0)),
            pl.BlockSpec((tm, HEAD), lambda i, j: (i % table_blocks, 0)),
        ],
        out_specs=pl.BlockSpec((tm, tn), lambda i, j: (i, j)),
        out_shape=jax.ShapeDtypeStruct((m, n), p.dtype),
        input_output_aliases={0: 0},
        compiler_params=_params(("arbitrary", "arbitrary")),
        name=name,
    )(p, head_gain, cos, sin)


def _residual(x, y, gain, gate):
    ms = jnp.mean(y * y, axis=-1, keepdims=True)
    return x + gate * (y * lax.rsqrt(ms + EPS) * gain)


def _proj_res_kernel(a_ref, w_ref, x_ref, gain_ref, gate_ref, o_ref):
    y = _dot(a_ref[...], w_ref[...])
    o_ref[...] = _residual(x_ref[...], y, gain_ref[...], gate_ref[...])


def _proj_res_ktiled_kernel(a_ref, w_ref, x_ref, gain_ref, gate_ref, o_ref):
    k = pl.program_id(1)
    last = pl.num_programs(1) - 1
    tm = o_ref.shape[0]
    lead = _split_rows(tm)

    @pl.when(k == 0)
    def _():
        o_ref[...] = _dot(a_ref[...], w_ref[...])

    @pl.when(jnp.logical_and(k > 0, k < last))
    def _():
        o_ref[...] += _dot(a_ref[...], w_ref[...])

    @pl.when(k == last)
    def _():
        gain, gate = gain_ref[...], gate_ref[...]
        parts = [slice(0, lead), slice(lead, tm)] if lead < tm else [slice(0, tm)]
        ys = [o_ref[r, :] + _dot(a_ref[r, :], w_ref[...]) for r in parts]
        for r, y in zip(parts, ys):
            o_ref[r, :] = _residual(x_ref[r, :], y, gain, gate)


def _proj_res_call(a, w, x, gain, gate, *, layer=0, name):
    m, k = a.shape
    d = w.shape[2]
    bm = gate.shape[0]
    rows_per_mod = m // bm
    tk = k if k <= d else _pick(k, 512, 2 * HEAD)
    kb = k // tk
    tm = _pick(rows_per_mod, 512 if kb == 1 else 1024, BF16_SUBLANES)
    mod_blocks = rows_per_mod // tm
    return pl.pallas_call(
        _proj_res_kernel if kb == 1 else _proj_res_ktiled_kernel,
        grid=(m // tm, kb),
        in_specs=[
            pl.BlockSpec((tm, tk), lambda i, kk: (i, kk)),
            pl.BlockSpec((None, tk, d), lambda i, kk: (layer, kk, 0)),
            pl.BlockSpec((tm, d), lambda i, kk: (i, 0)),
            pl.BlockSpec((1, d), lambda i, kk: (0, 0)),
            pl.BlockSpec((None, 1, d), lambda i, kk: (i // mod_blocks, 0, 0)),
        ],
        out_specs=pl.BlockSpec((tm, d), lambda i, kk: (i, 0)),
        out_shape=jax.ShapeDtypeStruct((m, d), F32),
        compiler_params=_params(("arbitrary", "arbitrary")),
        name=name,
    )(a, w, x, gain.reshape(1, d), gate)


def _ffn_in_kernel(x_ref, xprev_ref, xnext_ref, gain_ref, shift_ref, scale_ref, wg_ref, wu_ref, cw_ref, cb_ref,
                   o_ref, h_ref, *, blocks_per_seq):
    i = pl.program_id(0)
    j = pl.program_id(1)
    tm = x_ref.shape[0]
    halo = BF16_SUBLANES
    lead = _split_rows(tm)

    def gated(gate, up):
        rows = gate.shape[0]
        g_prev = pltpu.roll(gate, 1, axis=0)[halo:halo + tm, :]
        g_next = pltpu.roll(gate, rows - 1, axis=0)[halo:halo + tm, :]
        conv = (g_prev * cw_ref[0:1, :] + gate[halo:halo + tm, :] * cw_ref[1:2, :] + g_next * cw_ref[2:3, :]
                + cb_ref[...])
        o_ref[...] = (_silu(conv) * up).astype(o_ref.dtype)

    @pl.when(j == 0)
    def _():
        gain, shift, scale = gain_ref[...], shift_ref[...], scale_ref[...]
        seq_block = i % blocks_per_seq
        h_prev = jnp.where(seq_block > 0, _norm_mod(xprev_ref[...], gain, shift, scale), 0.0)
        h_next = jnp.where(seq_block < blocks_per_seq - 1, _norm_mod(xnext_ref[...], gain, shift, scale), 0.0)
        zeros = jnp.zeros((halo - h_prev.shape[0], h_prev.shape[1]), F32)
        h_ref[0:halo, :] = jnp.concatenate([zeros, h_prev], axis=0).astype(BF16)
        h_ref[halo + tm:, :] = jnp.concatenate([h_next, zeros], axis=0).astype(BF16)
        _norm_mod_rows(x_ref, lead, h_ref, halo, gain, shift, scale)
        if lead == tm:
            gated(_dot(h_ref[...], wg_ref[...]), _dot(h_ref[halo:halo + tm, :], wu_ref[...]))
        else:
            cut = halo + lead
            h_ref[cut:halo + tm, :] = _norm_mod(x_ref[lead:, :], gain, shift, scale).astype(BF16)
            gate = jnp.concatenate([_dot(h_ref[:cut, :], wg_ref[...]), _dot(h_ref[cut:, :], wg_ref[...])], axis=0)
            up = jnp.concatenate([_dot(h_ref[halo:cut, :], wu_ref[...]),
                                  _dot(h_ref[cut:halo + tm, :], wu_ref[...])], axis=0)
            gated(gate, up)

    @pl.when(j > 0)
    def _():
        gated(_dot(h_ref[...], wg_ref[...]), _dot(h_ref[halo:halo + tm, :], wu_ref[...]))


def _ffn_in_call(x, gain, shift, scale, w, conv_w, conv_b, *, layer, seq_len, name):
    m, d = x.shape
    f = w.shape[2] // 2
    bm = shift.shape[0]
    rows_per_mod = m // bm
    tm = _pick(seq_len, 1024, BF16_SUBLANES)
    tn = _pick(f, 512, HEAD)
    assert rows_per_mod % tm == 0
    mod_blocks = rows_per_mod // tm
    nb = f // tn
    xh = F32_SUBLANES
    tiles_per_tm = tm // xh
    last_tile = m // xh - 1
    body = functools.partial(_ffn_in_kernel, blocks_per_seq=seq_len // tm)
    return pl.pallas_call(
        body,
        grid=(m // tm, nb),
        in_specs=[
            pl.BlockSpec((tm, d), lambda i, j: (i, 0)),
            pl.BlockSpec((xh, d), lambda i, j: (jnp.maximum(i * tiles_per_tm - 1, 0), 0)),
            pl.BlockSpec((xh, d), lambda i, j: (jnp.minimum((i + 1) * tiles_per_tm, last_tile), 0)),
            pl.BlockSpec((1, d), lambda i, j: (0, 0)),
            pl.BlockSpec((None, 1, d), lambda i, j: (i // mod_blocks, 0, 0)),
            pl.BlockSpec((None, 1, d), lambda i, j: (i // mod_blocks, 0, 0)),
            pl.BlockSpec((None, d, tn), lambda i, j: (layer, 0, j)),
            pl.BlockSpec((None, d, tn), lambda i, j: (layer, 0, nb + j)),
            pl.BlockSpec((3, tn), lambda i, j: (0, j)),
            pl.BlockSpec((1, tn), lambda i, j: (0, j)),
        ],
        out_specs=pl.BlockSpec((tm, tn), lambda i, j: (i, j)),
        out_shape=jax.ShapeDtypeStruct((m, f), BF16),
        scratch_shapes=[pltpu.VMEM((tm + 2 * BF16_SUBLANES, d), BF16)],
        compiler_params=_params(("arbitrary", "arbitrary")),
        name=name,
    )(x, x, x, gain.reshape(1, d), shift, scale, w, w, conv_w, conv_b.reshape(1, f))


def _gla_gates(f_raw, lb, within):
    half = 0.5 * (1.0 - lb)
    ht = half * jnp.tanh(0.5 * f_raw.astype(F32))
    lf = jnp.log((1.0 - half) + ht)
    ones = jnp.where(within, 1.0, 0.0).astype(BF16)
    lf_hi = lf.astype(BF16)
    lf_lo = (lf - lf_hi.astype(F32)).astype(BF16)
    return half - ht, _dot(ones, lf_hi) + _dot(ones, lf_lo)


def _gla_pairs(qs, kk, b, forward, within):
    c = qs.shape[0]
    a_row = c // 2 - 1 if forward else c // 2
    e_row = c - 1 if forward else 0
    anchor = b[a_row:a_row + 1, :]
    b_end = b[e_row:e_row + 1, :]
    qa = qs * jnp.exp(b - anchor)
    ka = kk * jnp.exp(anchor - b)
    scores = jnp.where(within, _dot_nt(qa.astype(BF16), ka.astype(BF16)), 0.0).astype(BF16)
    return (scores, (qa * jnp.exp(anchor)).astype(BF16), (ka * jnp.exp(b_end - anchor)).astype(BF16),
            jnp.exp(b_end))


def _gla_pairs_exact(qs, kk, b, forward, within, tmp):
    tb_ref, tq_ref, ts_ref = tmp
    c = qs.shape[0]
    b_end = b[c - 1:c, :] if forward else b[0:1, :]
    tb_ref[...] = b
    tq_ref[...] = qs
    ones = jnp.ones((F32_SUBLANES, qs.shape[1]), BF16)

    def row(t, carry):
        rel = jnp.minimum(tb_ref[pl.ds(t, 1), :] - b, 0.0)
        e = (tq_ref[pl.ds(t, 1), :] * kk) * jnp.exp(rel)
        ts_ref[pl.ds(t, 1), :] = _dot_nt(ones, e.astype(BF16))[0:1, :]
        return carry

    lax.fori_loop(0, c, row, 0)
    scores = jnp.where(within, ts_ref[...], 0.0).astype(BF16)
    return scores, (qs * jnp.exp(b)).astype(BF16), (kk * jnp.exp(b_end - b)).astype(BF16), jnp.exp(b_end)


def _gla_readout(o, g_raw, o_gain):
    ms = jnp.mean(o * o, axis=-1, keepdims=True)
    y = o * lax.rsqrt(ms + EPS) * o_gain
    return (y * _silu(g_raw.astype(F32))).astype(BF16)


def _gla_kernel(qc, ffc, fbc, vc, gc, ql, ffl, fbl, vl, gl, lb_ref, og_ref, yc_ref, yl_ref,
                o_ref, qb_ref, ke_ref, dec_ref, st_ref, tb_ref, tq_ref, ts_ref):
    c = GLA_CHUNK
    tc = qc.shape[0]
    n_ctx = tc // c
    n_lat = ql.shape[0] // c
    lb = (lb_ref[0], lb_ref[1])
    row = lax.broadcasted_iota(jnp.int32, (c, c), 0)
    col = lax.broadcasted_iota(jnp.int32, (c, c), 1)
    within = (col <= row, col >= row)

    def rows(start):
        return pl.ds(start if isinstance(start, int) else pl.multiple_of(start, c), c)

    def local(refs, src0, dst0, chunk0, n, exact):
        q, ff, fb, v = refs
        src = [rows(src0 + j * c) for j in range(n)]
        dst = [rows(dst0 + j * c) for j in range(n)]
        qs = [_silu(q[src[j], :].astype(F32)) * (HEAD ** -0.5) for j in range(n)]
        gates = [[_gla_gates(f[src[j], :], lb[d], within[d]) for d, f in enumerate((ff, fb))] for j in range(n)]
        scores = []
        for j in range(n):
            per_dir = []
            for d in range(2):
                kk, b = gates[j][d]
                if exact:
                    sc, qb, ke, dec = _gla_pairs_exact(qs[j], kk, b, d == 0, within[d], (tb_ref, tq_ref, ts_ref))
                else:
                    sc, qb, ke, dec = _gla_pairs(qs[j], kk, b, d == 0, within[d])
                qb_ref[d, dst[j], :] = qb
                ke_ref[d, dst[j], :] = ke
                dec_ref[d, chunk0 + j] = dec
                per_dir.append(sc)
            scores.append(per_dir)
        for j in range(n):
            vj = v[src[j], :]
            o_ref[dst[j], :] = _dot(scores[j][0], vj) + _dot(scores[j][1], vj)

    def phase1(exact):
        run = 1 if exact else math.gcd(n_lat, GLA_RUN)
        run_ctx = math.gcd(n_ctx, run)
        for i in range(0, n_ctx, run_ctx):
            local((qc, ffc, fbc, vc), i * c, i * c, i, run_ctx, exact)

        def body(i, carry):
            src0 = pl.multiple_of(i * (run * c), c)
            local((ql, ffl, fbl, vl), src0, tc + src0, n_ctx + i * run, run, exact)
            return carry

        lax.fori_loop(0, n_lat // run, body, 0)

    safe = jnp.min(jnp.minimum(lb[0], lb[1])) >= GLA_SAFE_LB

    @pl.when(safe)
    def _():
        phase1(False)

    @pl.when(jnp.logical_not(safe))
    def _():
        phase1(True)

    st_ref[...] = jnp.zeros_like(st_ref)

    o_gain = og_ref[...]

    def steps(v, first, n, n_seg, row0, chunk0, finish=None):
        order = [[first + s for s in range(n)], [n_seg - 1 - first - s for s in range(n)]]
        src = [[rows(j * c) for j in order[d]] for d in range(2)]
        dst = [[rows(row0 + j * c) for j in order[d]] for d in range(2)]
        kv = [[_dot_tn(v[src[d][s], :], ke_ref[d, dst[d][s], :]) for s in range(n)] for d in range(2)]
        for d in range(2):
            st = st_ref[d]
            for s in range(n):
                o = o_ref[dst[d][s], :] + _dot_nt(qb_ref[d, dst[d][s], :], st.astype(BF16))
                if finish is None:
                    o_ref[dst[d][s], :] = o
                else:
                    g, y = finish
                    y[src[d][s], :] = _gla_readout(o, g[src[d][s], :], o_gain)
                st = st * dec_ref[d, chunk0 + order[d][s]] + kv[d][s]
            st_ref[d] = st

    steps(vc, 0, n_ctx, n_ctx, 0, 0)
    for i in range(n_ctx):
        r = slice(i * c, (i + 1) * c)
        yc_ref[r, :] = _gla_readout(o_ref[r, :], gc[r, :], o_gain)

    run = math.gcd(n_lat, GLA_RUN)
    n_runs = n_lat // run
    meet = (n_runs + 1) // 2

    def steps_body(i, carry):
        steps(vl, i * run, run, n_lat, tc, n_ctx)
        return carry

    def last_steps_body(i, carry):
        steps(vl, i * run, run, n_lat, tc, n_ctx, finish=(gl, yl_ref))
        return carry

    lax.fori_loop(0, meet, steps_body, 0)
    if n_runs % 2:
        for j in range((n_runs // 2) * run, (n_runs // 2 + 1) * run):
            yl_ref[rows(j * c), :] = _gla_readout(o_ref[rows(tc + j * c), :], gl[rows(j * c), :], o_gain)
    lax.fori_loop(meet, n_runs, last_steps_body, 0)


def _gla_call(p_ctx, p_lat, lower_bound, o_gain):
    b, tc, d5 = p_ctx.shape
    t = p_lat.shape[1]
    d = d5 // 5
    heads = d // HEAD
    assert tc % GLA_CHUNK == 0 and t % GLA_CHUNK == 0

    def col(tt, part):
        return pl.BlockSpec((None, tt, HEAD), lambda bi, h: (bi, 0, part * heads + h))

    out_spec = lambda tt: pl.BlockSpec((None, tt, HEAD), lambda bi, h: (bi, 0, h))
    return pl.pallas_call(
        _gla_kernel,
        grid=(b, heads),
        in_specs=[col(tc, p) for p in range(5)] + [col(t, p) for p in range(5)] + [
            pl.BlockSpec((2, None, 1, HEAD), lambda bi, h: (0, h, 0, 0)),
            pl.BlockSpec((1, HEAD), lambda bi, h: (0, 0)),
        ],
        out_specs=[out_spec(tc), out_spec(t)],
        out_shape=[jax.ShapeDtypeStruct((b, tc, d), BF16), jax.ShapeDtypeStruct((b, t, d), BF16)],
        scratch_shapes=[
            pltpu.VMEM((tc + t, HEAD), F32),
            pltpu.VMEM((2, tc + t, HEAD), BF16),
            pltpu.VMEM((2, tc + t, HEAD), BF16),
            pltpu.VMEM((2, (tc + t) // GLA_CHUNK, 1, HEAD), F32),
            pltpu.VMEM((2, HEAD, HEAD), F32),
            pltpu.VMEM((GLA_CHUNK, HEAD), F32),
            pltpu.VMEM((GLA_CHUNK, HEAD), F32),
            pltpu.VMEM((GLA_CHUNK, GLA_CHUNK), F32),
        ],
        compiler_params=_params(("arbitrary", "arbitrary")),
        name="hgrn_scan",
    )(*([p_ctx] * 5), *([p_lat] * 5), lower_bound.reshape(2, heads, 1, HEAD), o_gain.reshape(1, HEAD))


def _flash_kernel(q_ref, kc_ref, vc_ref, kl_ref, vl_ref, o_ref, *, tk):
    tq = q_ref.shape[0]
    group = q_ref.shape[1] // HEAD
    qs = [q_ref[:, g * HEAD:(g + 1) * HEAD] for g in range(group)]

    def attend(carry, k, v):
        v_ext = jnp.concatenate([v, jnp.ones_like(v)], axis=1)
        scores = [_dot_nt(qs[g], k) for g in range(group)]
        out = []
        for g in range(group):
            m, acc = carry[g]
            s = scores[g]
            m_new = jnp.maximum(m, jnp.max(s, axis=-1, keepdims=True))
            p = jnp.exp2(s - m_new).astype(BF16)
            out.append((m_new, jnp.exp2(m - m_new) * acc + _dot(p, v_ext)))
        return tuple(out)

    carry = tuple((jnp.full((tq, 1), -jnp.inf, F32), jnp.zeros((tq, 2 * HEAD), F32)) for _ in range(group))
    for start in range(0, kl_ref.shape[0], tk):
        carry = attend(carry, kl_ref[start:start + tk, :], vl_ref[start:start + tk, :])
    carry = attend(carry, kc_ref[...], vc_ref[...])
    for g in range(group):
        acc = carry[g][1]
        o_ref[:, g * HEAD:(g + 1) * HEAD] = (acc[:, :HEAD] / acc[:, HEAD:]).astype(o_ref.dtype)


def _flash_call(qkv, kv_ctx, *, heads):
    b, t, n = qkv.shape
    tc = kv_ctx.shape[1]
    kvh = (n // HEAD - heads) // 2
    group = heads // kvh
    tq = _pick(t, 512, BF16_SUBLANES)
    tk = _pick(t, 1024, 2 * HEAD)
    assert t // tk <= 16
    body = functools.partial(_flash_kernel, tk=tk)
    return pl.pallas_call(
        body,
        grid=(b, kvh, t // tq),
        in_specs=[
            pl.BlockSpec((None, tq, group * HEAD), lambda bi, h, i: (bi, i, h)),
            pl.BlockSpec((None, tc, HEAD), lambda bi, h, i: (bi, 0, h)),
            pl.BlockSpec((None, tc, HEAD), lambda bi, h, i: (bi, 0, kvh + h)),
            pl.BlockSpec((None, t, HEAD), lambda bi, h, i: (bi, 0, heads + h)),
            pl.BlockSpec((None, t, HEAD), lambda bi, h, i: (bi, 0, heads + kvh + h)),
        ],
        out_specs=pl.BlockSpec((None, tq, group * HEAD), lambda bi, h, i: (bi, i, h)),
        out_shape=jax.ShapeDtypeStruct((b, t, heads * HEAD), BF16),
        compiler_params=_params(("arbitrary", "arbitrary", "arbitrary")),
        name="gqa_flash",
    )(qkv, kv_ctx, kv_ctx, qkv, qkv)


def _rope_tables(t):
    pos = jnp.arange(t, dtype=jnp.int32)
    rows = (pos // GRID_W).astype(F32)
    cols = (pos % GRID_W).astype(F32)
    axis_dim = HEAD // 2
    inv_freq = ROPE_THETA ** (-jnp.arange(0, axis_dim, 2, dtype=F32) / axis_dim)
    ang = jnp.concatenate([rows[:, None] * inv_freq, cols[:, None] * inv_freq], axis=-1)
    cos, sin = jnp.cos(ang), jnp.sin(ang)
    return jnp.concatenate([cos, cos], axis=-1), jnp.concatenate([-sin, sin], axis=-1)


def _split_mods(mods, b):
    d = mods.shape[1] // N_MOD
    lat = [mods[:b, i * d:(i + 1) * d].reshape(b, 1, d) for i in range(N_MOD)]
    ctx = [mods[b:b + 1, i * d:(i + 1) * d].reshape(1, 1, d) for i in range(N_MOD)]
    return lat, ctx


def kernel(x, c, ctx, c_ctx, ada_w, ada_b, norm_mix_pre, norm_mix_post, norm_ffn_pre, norm_ffn_post, hgrn_w_in, hgrn_lb_logits, hgrn_o_norm, hgrn_w_out, attn_w_qkv, attn_q_norm, attn_k_norm, attn_w_out, ffn_w_in, ffn_conv_w, ffn_conv_b, ffn_w_out):
    b, t, d = x.shape
    tc = ctx.shape[1]
    heads = d // HEAD
    assert ada_w.shape[0] == 2 and b + 1 <= ADA_ROWS

    cond = jnp.zeros((ADA_ROWS, d), F32).at[:b].set(c).at[b].set(c_ctx)
    mods = _ada_call(cond, ada_w, ada_b)
    x_lat = x.reshape(b * t, d)
    x_ctx = ctx.reshape(b * tc, d)

    (sh_ml, sc_ml, gt_ml, sh_fl, sc_fl, gt_fl), (sh_mc, sc_mc, gt_mc, sh_fc, sc_fc, gt_fc) = _split_mods(mods[0], b)
    lower_bound = jnp.cumsum(jax.nn.softmax(hgrn_lb_logits.astype(F32), axis=1), axis=1)[:, 0]
    hgrn_w_in, hgrn_w_out, attn_w_out = hgrn_w_in.astype(BF16), hgrn_w_out.astype(BF16), attn_w_out.astype(BF16)
    ffn_w_in, ffn_w_out = ffn_w_in.astype(BF16), ffn_w_out.astype(BF16)
    p_lat = _nmm_call(x_lat, norm_mix_pre[0], sh_ml, sc_ml, hgrn_w_in, name="hgrn_in_lat")
    p_ctx = _nmm_call(x_ctx, norm_mix_pre[0], sh_mc, sc_mc, hgrn_w_in, name="hgrn_in_ctx")
    y_ctx, y_lat = _gla_call(p_ctx.reshape(b, tc, 5 * d), p_lat.reshape(b, t, 5 * d), lower_bound, hgrn_o_norm[0])
    x_lat = _proj_res_call(y_lat.reshape(b * t, d), hgrn_w_out, x_lat, norm_mix_post[0], gt_ml, name="hgrn_out_lat")
    x_ctx = _proj_res_call(y_ctx.reshape(b * tc, d), hgrn_w_out, x_ctx, norm_mix_post[0], gt_mc, name="hgrn_out_ctx")
    a = _ffn_in_call(x_lat, norm_ffn_pre[0], sh_fl, sc_fl, ffn_w_in, ffn_conv_w[0], ffn_conv_b[0],
                     layer=0, seq_len=t, name="ffn0_in_lat")
    x_lat = _proj_res_call(a, ffn_w_out, x_lat, norm_ffn_post[0], gt_fl, layer=0, name="ffn0_out_lat")
    a = _ffn_in_call(x_ctx, norm_ffn_pre[0], sh_fc, sc_fc, ffn_w_in, ffn_conv_w[0], ffn_conv_b[0],
                     layer=0, seq_len=tc, name="ffn0_in_ctx")
    x_ctx = _proj_res_call(a, ffn_w_out, x_ctx, norm_ffn_post[0], gt_fc, layer=0, name="ffn0_out_ctx")

    (sh_ml, sc_ml, gt_ml, sh_fl, sc_fl, gt_fl), (sh_mc, sc_mc, _, _, _, _) = _split_mods(mods[1], b)
    n_qkv = attn_w_qkv.shape[2]
    kvh = (n_qkv // HEAD - heads) // 2
    perm = jnp.concatenate([jnp.arange(0, HEAD, 2), jnp.arange(1, HEAD, 2)])
    head_perm = (jnp.arange(heads + kvh)[:, None] * HEAD + perm[None, :]).reshape(-1)
    col_perm = jnp.concatenate([head_perm, jnp.arange((heads + kvh) * HEAD, n_qkv)])
    w_qkv = attn_w_qkv[:, :, col_perm].astype(BF16)
    head_gain = jnp.concatenate([
        jnp.tile(attn_q_norm[0][perm] * (LOG2_E * HEAD ** -0.5), heads),
        jnp.tile(attn_k_norm[0][perm], kvh),
        jnp.ones((kvh * HEAD,), F32),
    ]).reshape(1, n_qkv)
    cos, sin = _rope_tables(t)
    qk_cols = (heads + kvh) * HEAD
    q_cols = heads * HEAD
    qkv = _nmm_call(x_lat, norm_mix_pre[1], sh_ml, sc_ml, w_qkv, name="attn_qkv_lat")
    qkv = _qk_rope_call(qkv, head_gain[:, :qk_cols], cos, sin, name="attn_qk_rope_lat")
    kv_ctx = _nmm_call(x_ctx, norm_mix_pre[1], sh_mc, sc_mc, w_qkv, col0=q_cols, name="attn_kv_ctx")
    kv_ctx = _qk_rope_call(kv_ctx, head_gain[:, q_cols:qk_cols], jnp.ones((tc, HEAD), F32),
                           jnp.zeros((tc, HEAD), F32), name="attn_k_norm_ctx")
    qkv = qkv.reshape(b, t, n_qkv)
    kv_ctx = kv_ctx.reshape(b, tc, 2 * kvh * HEAD)
    o = _flash_call(qkv, kv_ctx, heads=heads)
    x_lat = _proj_res_call(o.reshape(b * t, d), attn_w_out, x_lat, norm_mix_post[1], gt_ml, name="attn_out_lat")
    a = _ffn_in_call(x_lat, norm_ffn_pre[1], sh_fl, sc_fl, ffn_w_in, ffn_conv_w[1], ffn_conv_b[1],
                     layer=1, seq_len=t, name="ffn1_in_lat")
    x_lat = _proj_res_call(a, ffn_w_out, x_lat, norm_ffn_post[1], gt_fl, layer=1, name="ffn1_out_lat")
    return x_lat.reshape(b, t, d)
```

```python
import functools
import math

import jax
import jax.numpy as jnp
from jax import lax
from jax.experimental import pallas as pl
from jax.experimental.pallas import tpu as pltpu

F32 = jnp.float32
BF16 = jnp.bfloat16

EPS = 1e-6
GRID_W = 64
ROPE_THETA = 10000.0
LOG2_E = math.log2(math.e)
N_MOD = 6
HEAD = 128
ROW_GROUP = 16
GLA_CHUNK = 128
GLA_RUN = 32
GLA_SAFE_LB = math.exp(-80.0 / (GLA_CHUNK // 2))
F32_SUBLANES = 8
BF16_SUBLANES = 16
ADA_ROWS = F32_SUBLANES
V7X_VMEM_BYTES = 64 * 1024 * 1024
VMEM_LIMIT = V7X_VMEM_BYTES - 6 * 1024 * 1024


def _pick(n, target, align):
    if n <= target:
        return n
    for t in range(target - target % align, 0, -align):
        if n % t == 0:
            return t
    return n


def _params(sem):
    return pltpu.CompilerParams(dimension_semantics=sem, vmem_limit_bytes=VMEM_LIMIT)


def _sigmoid(x):
    return 0.5 * jnp.tanh(0.5 * x) + 0.5


def _silu(x):
    return x * _sigmoid(x)


def _dot(a, b):
    return jnp.dot(a, b, preferred_element_type=F32)


def _dot_nt(a, b):
    return lax.dot_general(a, b, (((1,), (1,)), ((), ())), preferred_element_type=F32)


def _dot_tn(a, b):
    return lax.dot_general(a, b, (((0,), (0,)), ((), ())), preferred_element_type=F32)


def _ada_kernel(c_ref, w_ref, b_ref, o_ref):
    sc = _silu(c_ref[...]).astype(BF16)
    o_ref[...] = _dot(sc, w_ref[...].astype(BF16)) + b_ref[...]


def _ada_call(cond, ada_w, ada_b):
    depth, d, n = ada_w.shape
    tn = _pick(n, 1024, HEAD)
    return pl.pallas_call(
        _ada_kernel,
        grid=(depth, n // tn),
        in_specs=[
            pl.BlockSpec((ADA_ROWS, d), lambda l, j: (0, 0)),
            pl.BlockSpec((None, d, tn), lambda l, j: (l, 0, j)),
            pl.BlockSpec((None, 1, tn), lambda l, j: (l, 0, j)),
        ],
        out_specs=pl.BlockSpec((None, ADA_ROWS, tn), lambda l, j: (l, 0, j)),
        out_shape=jax.ShapeDtypeStruct((depth, ADA_ROWS, n), F32),
        compiler_params=_params(("arbitrary", "arbitrary")),
        name="ada_ln",
    )(cond, ada_w, ada_b.reshape(depth, 1, n))


def _norm_mod(x, gain, shift, scale):
    ms = jnp.mean(x * x, axis=-1, keepdims=True)
    return (x * lax.rsqrt(ms + EPS) * gain) * (1.0 + scale) + shift


def _norm_mod_rows(x_ref, n_rows, h_ref, h_row0, gain, shift, scale):
    mult = gain * (1.0 + scale)

    def body(r, carry):
        start = pl.multiple_of(r * ROW_GROUP, ROW_GROUP)
        x = x_ref[pl.ds(start, ROW_GROUP), :]
        ms = jnp.mean(x * x, axis=-1, keepdims=True)
        dst = pl.ds(pl.multiple_of(h_row0 + start, ROW_GROUP), ROW_GROUP)
        h_ref[dst, :] = (x * lax.rsqrt(ms + EPS) * mult + shift).astype(BF16)
        return carry

    groups = n_rows // ROW_GROUP
    lax.fori_loop(0, groups, body, 0, unroll=math.gcd(groups, 4))


def _split_rows(tm):
    half = tm // 2
    return half if half % ROW_GROUP == 0 else tm


def _nmm_kernel(x_ref, gain_ref, shift_ref, scale_ref, w_ref, o_ref, h_ref):
    j = pl.program_id(1)
    tm = x_ref.shape[0]
    lead = _split_rows(tm)

    @pl.when(j == 0)
    def _():
        gain, shift, scale = gain_ref[...], shift_ref[...], scale_ref[...]
        _norm_mod_rows(x_ref, lead, h_ref, 0, gain, shift, scale)
        if lead < tm:
            h_ref[lead:, :] = _norm_mod(x_ref[lead:, :], gain, shift, scale).astype(BF16)
        o_ref[:lead, :] = _dot(h_ref[:lead, :], w_ref[...]).astype(o_ref.dtype)
        if lead < tm:
            o_ref[lead:, :] = _dot(h_ref[lead:, :], w_ref[...]).astype(o_ref.dtype)

    @pl.when(j > 0)
    def _():
        o_ref[...] = _dot(h_ref[...], w_ref[...]).astype(o_ref.dtype)


def _nmm_call(x, gain, shift, scale, w, *, layer=0, col0=0, name):
    m, d = x.shape
    n = w.shape[2] - col0
    bm = shift.shape[0]
    rows_per_mod = m // bm
    tm = _pick(rows_per_mod, 1024, BF16_SUBLANES)
    tn = _pick(math.gcd(n, col0) if col0 else n, 2048, HEAD)
    mod_blocks = rows_per_mod // tm
    col_block0 = col0 // tn
    return pl.pallas_call(
        _nmm_kernel,
        grid=(m // tm, n // tn),
        in_specs=[
            pl.BlockSpec((tm, d), lambda i, j: (i, 0)),
            pl.BlockSpec((1, d), lambda i, j: (0, 0)),
            pl.BlockSpec((None, 1, d), lambda i, j: (i // mod_blocks, 0, 0)),
            pl.BlockSpec((None, 1, d), lambda i, j: (i // mod_blocks, 0, 0)),
            pl.BlockSpec((None, d, tn), lambda i, j: (layer, 0, col_block0 + j)),
        ],
        out_specs=pl.BlockSpec((tm, tn), lambda i, j: (i, j)),
        out_shape=jax.ShapeDtypeStruct((m, n), BF16),
        scratch_shapes=[pltpu.VMEM((tm, d), BF16)],
        compiler_params=_params(("arbitrary", "arbitrary")),
        name=name,
    )(x, gain.reshape(1, d), shift, scale, w)


def _qk_rope_kernel(x_ref, hg_ref, cos_ref, sin_ref, o_ref):
    cos = cos_ref[...]
    sin = sin_ref[...]
    mean_lanes = jnp.full((HEAD, HEAD), 1.0 / HEAD, BF16)
    for hh in range(x_ref.shape[1] // HEAD):
        cols = slice(hh * HEAD, (hh + 1) * HEAD)
        a = x_ref[:, cols].astype(F32)
        ms = _dot((a * a).astype(BF16), mean_lanes)
        a = a * lax.rsqrt(ms + EPS) * hg_ref[:, cols]
        a = a * cos + pltpu.roll(a, HEAD // 2, axis=1) * sin
        o_ref[:, cols] = a.astype(o_ref.dtype)


def _qk_rope_call(p, head_gain, cos, sin, *, name):
    m, n = p.shape
    nn = head_gain.shape[1]
    tm = _pick(math.gcd(m, cos.shape[0]), 4096, BF16_SUBLANES)
    tn = _pick(math.gcd(n, nn), 1024, HEAD)
    table_blocks = cos.shape[0] // tm
    return pl.pallas_call(
        _qk_rope_kernel,
        grid=(m // tm, nn // tn),
        in_specs=[
            pl.BlockSpec((tm, tn), lambda i, j: (i, j)),
            pl.BlockSpec((1, tn), lambda i, j: (0, j)),
            pl.BlockSpec((tm, HEAD), lambda i, j: (i % table_blocks, 0)),
            pl.BlockSpec((tm, HEAD), lambda i, j: (i % table_blocks, 0)),
        ],
        out_specs=pl.BlockSpec((tm, tn), lambda i, j: (i, j)),
        out_shape=jax.ShapeDtypeStruct((m, n), p.dtype),
        input_output_aliases={0: 0},
        compiler_params=_params(("arbitrary", "arbitrary")),
        name=name,
    )(p, head_gain, cos, sin)


def _residual(x, y, gain, gate):
    ms = jnp.mean(y * y, axis=-1, keepdims=True)
    return x + gate * (y * lax.rsqrt(ms + EPS) * gain)


def _proj_res_kernel(a_ref, w_ref, x_ref, gain_ref, gate_ref, o_ref):
    y = _dot(a_ref[...], w_ref[...])
    o_ref[...] = _residual(x_ref[...], y, gain_ref[...], gate_ref[...])


def _proj_res_ktiled_kernel(a_ref, w_ref, x_ref, gain_ref, gate_ref, o_ref):
    k = pl.program_id(1)
    last = pl.num_programs(1) - 1
    tm = o_ref.shape[0]
    lead = _split_rows(tm)

    @pl.when(k == 0)
    def _():
        o_ref[...] = _dot(a_ref[...], w_ref[...])

    @pl.when(jnp.logical_and(k > 0, k < last))
    def _():
        o_ref[...] += _dot(a_ref[...], w_ref[...])

    @pl.when(k == last)
    def _():
        gain, gate = gain_ref[...], gate_ref[...]
        parts = [slice(0, lead), slice(lead, tm)] if lead < tm else [slice(0, tm)]
        ys = [o_ref[r, :] + _dot(a_ref[r, :], w_ref[...]) for r in parts]
        for r, y in zip(parts, ys):
            o_ref[r, :] = _residual(x_ref[r, :], y, gain, gate)


def _proj_res_call(a, w, x, gain, gate, *, layer=0, name):
    m, k = a.shape
    d = w.shape[2]
    bm = gate.shape[0]
    rows_per_mod = m // bm
    tk = k if k <= d else _pick(k, 512, 2 * HEAD)
    kb = k // tk
    tm = _pick(rows_per_mod, 512 if kb == 1 else 1024, BF16_SUBLANES)
    mod_blocks = rows_per_mod // tm
    return pl.pallas_call(
        _proj_res_kernel if kb == 1 else _proj_res_ktiled_kernel,
        grid=(m // tm, kb),
        in_specs=[
            pl.BlockSpec((tm, tk), lambda i, kk: (i, kk)),
            pl.BlockSpec((None, tk, d), lambda i, kk: (layer, kk, 0)),
            pl.BlockSpec((tm, d), lambda i, kk: (i, 0)),
            pl.BlockSpec((1, d), lambda i, kk: (0, 0)),
            pl.BlockSpec((None, 1, d), lambda i, kk: (i // mod_blocks, 0, 0)),
        ],
        out_specs=pl.BlockSpec((tm, d), lambda i, kk: (i, 0)),
        out_shape=jax.ShapeDtypeStruct((m, d), F32),
        compiler_params=_params(("arbitrary", "arbitrary")),
        name=name,
    )(a, w, x, gain.reshape(1, d), gate)


def _ffn_in_kernel(x_ref, xprev_ref, xnext_ref, gain_ref, shift_ref, scale_ref, wg_ref, wu_ref, cw_ref, cb_ref,
                   o_ref, h_ref, *, blocks_per_seq):
    i = pl.program_id(0)
    j = pl.program_id(1)
    tm = x_ref.shape[0]
    halo = BF16_SUBLANES
    lead = _split_rows(tm)

    def gated(gate, up):
        rows = gate.shape[0]
        g_prev = pltpu.roll(gate, 1, axis=0)[halo:halo + tm, :]
        g_next = pltpu.roll(gate, rows - 1, axis=0)[halo:halo + tm, :]
        conv = (g_prev * cw_ref[0:1, :] + gate[halo:halo + tm, :] * cw_ref[1:2, :] + g_next * cw_ref[2:3, :]
                + cb_ref[...])
        o_ref[...] = (_silu(conv) * up).astype(o_ref.dtype)

    @pl.when(j == 0)
    def _():
        gain, shift, scale = gain_ref[...], shift_ref[...], scale_ref[...]
        seq_block = i % blocks_per_seq
        h_prev = jnp.where(seq_block > 0, _norm_mod(xprev_ref[...], gain, shift, scale), 0.0)
        h_next = jnp.where(seq_block < blocks_per_seq - 1, _norm_mod(xnext_ref[...], gain, shift, scale), 0.0)
        zeros = jnp.zeros((halo - h_prev.shape[0], h_prev.shape[1]), F32)
        h_ref[0:halo, :] = jnp.concatenate([zeros, h_prev], axis=0).astype(BF16)
        h_ref[halo + tm:, :] = jnp.concatenate([h_next, zeros], axis=0).astype(BF16)
        _norm_mod_rows(x_ref, lead, h_ref, halo, gain, shift, scale)
        if lead == tm:
            gated(_dot(h_ref[...], wg_ref[...]), _dot(h_ref[halo:halo + tm, :], wu_ref[...]))
        else:
            cut = halo + lead
            h_ref[cut:halo + tm, :] = _norm_mod(x_ref[lead:, :], gain, shift, scale).astype(BF16)
            gate = jnp.concatenate([_dot(h_ref[:cut, :], wg_ref[...]), _dot(h_ref[cut:, :], wg_ref[...])], axis=0)
            up = jnp.concatenate([_dot(h_ref[halo:cut, :], wu_ref[...]),
                                  _dot(h_ref[cut:halo + tm, :], wu_ref[...])], axis=0)
            gated(gate, up)

    @pl.when(j > 0)
    def _():
        gated(_dot(h_ref[...], wg_ref[...]), _dot(h_ref[halo:halo + tm, :], wu_ref[...]))


def _ffn_in_call(x, gain, shift, scale, w, conv_w, conv_b, *, layer, seq_len, name):
    m, d = x.shape
    f = w.shape[2] // 2
    bm = shift.shape[0]
    rows_per_mod = m // bm
    tm = _pick(seq_len, 1024, BF16_SUBLANES)
    tn = _pick(f, 512, HEAD)
    assert rows_per_mod % tm == 0
    mod_blocks = rows_per_mod // tm
    nb = f // tn
    xh = F32_SUBLANES
    tiles_per_tm = tm // xh
    last_tile = m // xh - 1
    body = functools.partial(_ffn_in_kernel, blocks_per_seq=seq_len // tm)
    return pl.pallas_call(
        body,
        grid=(m // tm, nb),
        in_specs=[
            pl.BlockSpec((tm, d), lambda i, j: (i, 0)),
            pl.BlockSpec((xh, d), lambda i, j: (jnp.maximum(i * tiles_per_tm - 1, 0), 0)),
            pl.BlockSpec((xh, d), lambda i, j: (jnp.minimum((i + 1) * tiles_per_tm, last_tile), 0)),
            pl.BlockSpec((1, d), lambda i, j: (0, 0)),
            pl.BlockSpec((None, 1, d), lambda i, j: (i // mod_blocks, 0, 0)),
            pl.BlockSpec((None, 1, d), lambda i, j: (i // mod_blocks, 0, 0)),
            pl.BlockSpec((None, d, tn), lambda i, j: (layer, 0, j)),
            pl.BlockSpec((None, d, tn), lambda i, j: (layer, 0, nb + j)),
            pl.BlockSpec((3, tn), lambda i, j: (0, j)),
            pl.BlockSpec((1, tn), lambda i, j: (0, j)),
        ],
        out_specs=pl.BlockSpec((tm, tn), lambda i, j: (i, j)),
        out_shape=jax.ShapeDtypeStruct((m, f), BF16),
        scratch_shapes=[pltpu.VMEM((tm + 2 * BF16_SUBLANES, d), BF16)],
        compiler_params=_params(("arbitrary", "arbitrary")),
        name=name,
    )(x, x, x, gain.reshape(1, d), shift, scale, w, w, conv_w, conv_b.reshape(1, f))


def _gla_gates(f_raw, lb, within):
    half = 0.5 * (1.0 - lb)
    ht = half * jnp.tanh(0.5 * f_raw.astype(F32))
    lf = jnp.log((1.0 - half) + ht)
    ones = jnp.where(within, 1.0, 0.0).astype(BF16)
    lf_hi = lf.astype(BF16)
    lf_lo = (lf - lf_hi.astype(F32)).astype(BF16)
    return half - ht, _dot(ones, lf_hi) + _dot(ones, lf_lo)


def _gla_pairs(qs, kk, b, forward, within):
    c = qs.shape[0]
    a_row = c // 2 - 1 if forward else c // 2
    e_row = c - 1 if forward else 0
    anchor = b[a_row:a_row + 1, :]
    b_end = b[e_row:e_row + 1, :]
    qa = qs * jnp.exp(b - anchor)
    ka = kk * jnp.exp(anchor - b)
    scores = jnp.where(within, _dot_nt(qa.astype(BF16), ka.astype(BF16)), 0.0).astype(BF16)
    return (scores, (qa * jnp.exp(anchor)).astype(BF16), (ka * jnp.exp(b_end - anchor)).astype(BF16),
            jnp.exp(b_end))


def _gla_pairs_exact(qs, kk, b, forward, within, tmp):
    tb_ref, tq_ref, ts_ref = tmp
    c = qs.shape[0]
    b_end = b[c - 1:c, :] if forward else b[0:1, :]
    tb_ref[...] = b
    tq_ref[...] = qs
    ones = jnp.ones((F32_SUBLANES, qs.shape[1]), BF16)

    def row(t, carry):
        rel = jnp.minimum(tb_ref[pl.ds(t, 1), :] - b, 0.0)
        e = (tq_ref[pl.ds(t, 1), :] * kk) * jnp.exp(rel)
        ts_ref[pl.ds(t, 1), :] = _dot_nt(ones, e.astype(BF16))[0:1, :]
        return carry

    lax.fori_loop(0, c, row, 0)
    scores = jnp.where(within, ts_ref[...], 0.0).astype(BF16)
    return scores, (qs * jnp.exp(b)).astype(BF16), (kk * jnp.exp(b_end - b)).astype(BF16), jnp.exp(b_end)


def _gla_readout(o, g_raw, o_gain):
    ms = jnp.mean(o * o, axis=-1, keepdims=True)
    y = o * lax.rsqrt(ms + EPS) * o_gain
    return (y * _silu(g_raw.astype(F32))).astype(BF16)


def _gla_kernel(qc, ffc, fbc, vc, gc, ql, ffl, fbl, vl, gl, lb_ref, og_ref, yc_ref, yl_ref,
                o_ref, qb_ref, ke_ref, dec_ref, st_ref, tb_ref, tq_ref, ts_ref):
    c = GLA_CHUNK
    tc = qc.shape[0]
    n_ctx = tc // c
    n_lat = ql.shape[0] // c
    lb = (lb_ref[0], lb_ref[1])
    row = lax.broadcasted_iota(jnp.int32, (c, c), 0)
    col = lax.broadcasted_iota(jnp.int32, (c, c), 1)
    within = (col <= row, col >= row)

    def rows(start):
        return pl.ds(start if isinstance(start, int) else pl.multiple_of(start, c), c)

    def local(refs, src0, dst0, chunk0, n, exact):
        q, ff, fb, v = refs
        src = [rows(src0 + j * c) for j in range(n)]
        dst = [rows(dst0 + j * c) for j in range(n)]
        qs = [_silu(q[src[j], :].astype(F32)) * (HEAD ** -0.5) for j in range(n)]
        gates = [[_gla_gates(f[src[j], :], lb[d], within[d]) for d, f in enumerate((ff, fb))] for j in range(n)]
        scores = []
        for j in range(n):
            per_dir = []
            for d in range(2):
                kk, b = gates[j][d]
                if exact:
                    sc, qb, ke, dec = _gla_pairs_exact(qs[j], kk, b, d == 0, within[d], (tb_ref, tq_ref, ts_ref))
                else:
                    sc, qb, ke, dec = _gla_pairs(qs[j], kk, b, d == 0, within[d])
                qb_ref[d, dst[j], :] = qb
                ke_ref[d, dst[j], :] = ke
                dec_ref[d, chunk0 + j] = dec
                per_dir.append(sc)
            scores.append(per_dir)
        for j in range(n):
            vj = v[src[j], :]
            o_ref[dst[j], :] = _dot(scores[j][0], vj) + _dot(scores[j][1], vj)

    def phase1(exact):
        run = 1 if exact else math.gcd(n_lat, GLA_RUN)
        run_ctx = math.gcd(n_ctx, run)
        for i in range(0, n_ctx, run_ctx):
            local((qc, ffc, fbc, vc), i * c, i * c, i, run_ctx, exact)

        def body(i, carry):
            src0 = pl.multiple_of(i * (run * c), c)
            local((ql, ffl, fbl, vl), src0, tc + src0, n_ctx + i * run, run, exact)
            return carry

        lax.fori_loop(0, n_lat // run, body, 0)

    safe = jnp.min(jnp.minimum(lb[0], lb[1])) >= GLA_SAFE_LB

    @pl.when(safe)
    def _():
        phase1(False)

    @pl.when(jnp.logical_not(safe))
    def _():
        phase1(True)

    st_ref[...] = jnp.zeros_like(st_ref)

    o_gain = og_ref[...]

    def steps(v, first, n, n_seg, row0, chunk0, finish=None):
        order = [[first + s for s in range(n)], [n_seg - 1 - first - s for s in range(n)]]
        src = [[rows(j * c) for j in order[d]] for d in range(2)]
        dst = [[rows(row0 + j * c) for j in order[d]] for d in range(2)]
        kv = [[_dot_tn(v[src[d][s], :], ke_ref[d, dst[d][s], :]) for s in range(n)] for d in range(2)]
        for d in range(2):
            st = st_ref[d]
            for s in range(n):
                o = o_ref[dst[d][s], :] + _dot_nt(qb_ref[d, dst[d][s], :], st.astype(BF16))
                if finish is None:
                    o_ref[dst[d][s], :] = o
                else:
                    g, y = finish
                    y[src[d][s], :] = _gla_readout(o, g[src[d][s], :], o_gain)
                st = st * dec_ref[d, chunk0 + order[d][s]] + kv[d][s]
            st_ref[d] = st

    steps(vc, 0, n_ctx, n_ctx, 0, 0)
    for i in range(n_ctx):
        r = slice(i * c, (i + 1) * c)
        yc_ref[r, :] = _gla_readout(o_ref[r, :], gc[r, :], o_gain)

    run = math.gcd(n_lat, GLA_RUN)
    n_runs = n_lat // run
    meet = (n_runs + 1) // 2

    def steps_body(i, carry):
        steps(vl, i * run, run, n_lat, tc, n_ctx)
        return carry

    def last_steps_body(i, carry):
        steps(vl, i * run, run, n_lat, tc, n_ctx, finish=(gl, yl_ref))
        return carry

    lax.fori_loop(0, meet, steps_body, 0)
    if n_runs % 2:
        for j in range((n_runs // 2) * run, (n_runs // 2 + 1) * run):
            yl_ref[rows(j * c), :] = _gla_readout(o_ref[rows(tc + j * c), :], gl[rows(j * c), :], o_gain)
    lax.fori_loop(meet, n_runs, last_steps_body, 0)


def _gla_call(p_ctx, p_lat, lower_bound, o_gain):
    b, tc, d5 = p_ctx.shape
    t = p_lat.shape[1]
    d = d5 // 5
    heads = d // HEAD
    assert tc % GLA_CHUNK == 0 and t % GLA_CHUNK == 0

    def col(tt, part):
        return pl.BlockSpec((None, tt, HEAD), lambda bi, h: (bi, 0, part * heads + h))

    out_spec = lambda tt: pl.BlockSpec((None, tt, HEAD), lambda bi, h: (bi, 0, h))
    return pl.pallas_call(
        _gla_kernel,
        grid=(b, heads),
        in_specs=[col(tc, p) for p in range(5)] + [col(t, p) for p in range(5)] + [
            pl.BlockSpec((2, None, 1, HEAD), lambda bi, h: (0, h, 0, 0)),
            pl.BlockSpec((1, HEAD), lambda bi, h: (0, 0)),
        ],
        out_specs=[out_spec(tc), out_spec(t)],
        out_shape=[jax.ShapeDtypeStruct((b, tc, d), BF16), jax.ShapeDtypeStruct((b, t, d), BF16)],
        scratch_shapes=[
            pltpu.VMEM((tc + t, HEAD), F32),
            pltpu.VMEM((2, tc + t, HEAD), BF16),
            pltpu.VMEM((2, tc + t, HEAD), BF16),
            pltpu.VMEM((2, (tc + t) // GLA_CHUNK, 1, HEAD), F32),
            pltpu.VMEM((2, HEAD, HEAD), F32),
            pltpu.VMEM((GLA_CHUNK, HEAD), F32),
            pltpu.VMEM((GLA_CHUNK, HEAD), F32),
            pltpu.VMEM((GLA_CHUNK, GLA_CHUNK), F32),
        ],
        compiler_params=_params(("arbitrary", "arbitrary")),
        name="hgrn_scan",
    )(*([p_ctx] * 5), *([p_lat] * 5), lower_bound.reshape(2, heads, 1, HEAD), o_gain.reshape(1, HEAD))


def _flash_kernel(q_ref, kc_ref, vc_ref, kl_ref, vl_ref, o_ref, *, tk):
    tq = q_ref.shape[0]
    group = q_ref.shape[1] // HEAD
    qs = [q_ref[:, g * HEAD:(g + 1) * HEAD] for g in range(group)]

    def attend(carry, k, v):
        v_ext = jnp.concatenate([v, jnp.ones_like(v)], axis=1)
        out = []
        for g in range(group):
            m, acc = carry[g]
            s = _dot_nt(qs[g], k)
            m_new = jnp.maximum(m, jnp.max(s, axis=-1, keepdims=True))
            p = jnp.exp2(s - m_new).astype(BF16)
            out.append((m_new, jnp.exp2(m - m_new) * acc + _dot(p, v_ext)))
        return tuple(out)

    carry = tuple((jnp.full((tq, 1), -jnp.inf, F32), jnp.zeros((tq, 2 * HEAD), F32)) for _ in range(group))
    for start in range(0, kl_ref.shape[0], tk):
        carry = attend(carry, kl_ref[start:start + tk, :], vl_ref[start:start + tk, :])
    carry = attend(carry, kc_ref[...], vc_ref[...])
    for g in range(group):
        acc = carry[g][1]
        o_ref[:, g * HEAD:(g + 1) * HEAD] = (acc[:, :HEAD] / acc[:, HEAD:]).astype(o_ref.dtype)


def _flash_call(qkv, kv_ctx, *, heads):
    b, t, n = qkv.shape
    tc = kv_ctx.shape[1]
    kvh = (n // HEAD - heads) // 2
    group = heads // kvh
    tq = _pick(t, 512, BF16_SUBLANES)
    tk = _pick(t, 2048, 2 * HEAD)
    assert t // tk <= 16
    body = functools.partial(_flash_kernel, tk=tk)
    return pl.pallas_call(
        body,
        grid=(b, kvh, t // tq),
        in_specs=[
            pl.BlockSpec((None, tq, group * HEAD), lambda bi, h, i: (bi, i, h)),
            pl.BlockSpec((None, tc, HEAD), lambda bi, h, i: (bi, 0, h)),
            pl.BlockSpec((None, tc, HEAD), lambda bi, h, i: (bi, 0, kvh + h)),
            pl.BlockSpec((None, t, HEAD), lambda bi, h, i: (bi, 0, heads + h)),
            pl.BlockSpec((None, t, HEAD), lambda bi, h, i: (bi, 0, heads + kvh + h)),
        ],
        out_specs=pl.BlockSpec((None, tq, group * HEAD), lambda bi, h, i: (bi, i, h)),
        out_shape=jax.ShapeDtypeStruct((b, t, heads * HEAD), BF16),
        compiler_params=_params(("arbitrary", "arbitrary", "arbitrary")),
        name="gqa_flash",
    )(qkv, kv_ctx, kv_ctx, qkv, qkv)


def _rope_tables(t):
    pos = jnp.arange(t, dtype=jnp.int32)
    rows = (pos // GRID_W).astype(F32)
    cols = (pos % GRID_W).astype(F32)
    axis_dim = HEAD // 2
    inv_freq = ROPE_THETA ** (-jnp.arange(0, axis_dim, 2, dtype=F32) / axis_dim)
    ang = jnp.concatenate([rows[:, None] * inv_freq, cols[:, None] * inv_freq], axis=-1)
    cos, sin = jnp.cos(ang), jnp.sin(ang)
    return jnp.concatenate([cos, cos], axis=-1), jnp.concatenate([-sin, sin], axis=-1)


def _split_mods(mods, b):
    d = mods.shape[1] // N_MOD
    lat = [mods[:b, i * d:(i + 1) * d].reshape(b, 1, d) for i in range(N_MOD)]
    ctx = [mods[b:b + 1, i * d:(i + 1) * d].reshape(1, 1, d) for i in range(N_MOD)]
    return lat, ctx


def kernel(x, c, ctx, c_ctx, ada_w, ada_b, norm_mix_pre, norm_mix_post, norm_ffn_pre, norm_ffn_post, hgrn_w_in, hgrn_lb_logits, hgrn_o_norm, hgrn_w_out, attn_w_qkv, attn_q_norm, attn_k_norm, attn_w_out, ffn_w_in, ffn_conv_w, ffn_conv_b, ffn_w_out):
    b, t, d = x.shape
    tc = ctx.shape[1]
    heads = d // HEAD
    assert ada_w.shape[0] == 2 and b + 1 <= ADA_ROWS

    cond = jnp.zeros((ADA_ROWS, d), F32).at[:b].set(c).at[b].set(c_ctx)
    mods = _ada_call(cond, ada_w, ada_b)
    x_lat = x.reshape(b * t, d)
    x_ctx = ctx.reshape(b * tc, d)

    (sh_ml, sc_ml, gt_ml, sh_fl, sc_fl, gt_fl), (sh_mc, sc_mc, gt_mc, sh_fc, sc_fc, gt_fc) = _split_mods(mods[0], b)
    lower_bound = jnp.cumsum(jax.nn.softmax(hgrn_lb_logits.astype(F32), axis=1), axis=1)[:, 0]
    hgrn_w_in, hgrn_w_out, attn_w_out = hgrn_w_in.astype(BF16), hgrn_w_out.astype(BF16), attn_w_out.astype(BF16)
    ffn_w_in, ffn_w_out = ffn_w_in.astype(BF16), ffn_w_out.astype(BF16)
    p_lat = _nmm_call(x_lat, norm_mix_pre[0], sh_ml, sc_ml, hgrn_w_in, name="hgrn_in_lat")
    p_ctx = _nmm_call(x_ctx, norm_mix_pre[0], sh_mc, sc_mc, hgrn_w_in, name="hgrn_in_ctx")
    y_ctx, y_lat = _gla_call(p_ctx.reshape(b, tc, 5 * d), p_lat.reshape(b, t, 5 * d), lower_bound, hgrn_o_norm[0])
    x_lat = _proj_res_call(y_lat.reshape(b * t, d), hgrn_w_out, x_lat, norm_mix_post[0], gt_ml, name="hgrn_out_lat")
    x_ctx = _proj_res_call(y_ctx.reshape(b * tc, d), hgrn_w_out, x_ctx, norm_mix_post[0], gt_mc, name="hgrn_out_ctx")
    a = _ffn_in_call(x_lat, norm_ffn_pre[0], sh_fl, sc_fl, ffn_w_in, ffn_conv_w[0], ffn_conv_b[0],
                     layer=0, seq_len=t, name="ffn0_in_lat")
    x_lat = _proj_res_call(a, ffn_w_out, x_lat, norm_ffn_post[0], gt_fl, layer=0, name="ffn0_out_lat")
    a = _ffn_in_call(x_ctx, norm_ffn_pre[0], sh_fc, sc_fc, ffn_w_in, ffn_conv_w[0], ffn_conv_b[0],
                     layer=0, seq_len=tc, name="ffn0_in_ctx")
    x_ctx = _proj_res_call(a, ffn_w_out, x_ctx, norm_ffn_post[0], gt_fc, layer=0, name="ffn0_out_ctx")

    (sh_ml, sc_ml, gt_ml, sh_fl, sc_fl, gt_fl), (sh_mc, sc_mc, _, _, _, _) = _split_mods(mods[1], b)
    n_qkv = attn_w_qkv.shape[2]
    kvh = (n_qkv // HEAD - heads) // 2
    perm = jnp.concatenate([jnp.arange(0, HEAD, 2), jnp.arange(1, HEAD, 2)])
    head_perm = (jnp.arange(heads + kvh)[:, None] * HEAD + perm[None, :]).reshape(-1)
    col_perm = jnp.concatenate([head_perm, jnp.arange((heads + kvh) * HEAD, n_qkv)])
    w_qkv = attn_w_qkv[:, :, col_perm].astype(BF16)
    head_gain = jnp.concatenate([
        jnp.tile(attn_q_norm[0][perm] * (LOG2_E * HEAD ** -0.5), heads),
        jnp.tile(attn_k_norm[0][perm], kvh),
        jnp.ones((kvh * HEAD,), F32),
    ]).reshape(1, n_qkv)
    cos, sin = _rope_tables(t)
    qk_cols = (heads + kvh) * HEAD
    q_cols = heads * HEAD
    qkv = _nmm_call(x_lat, norm_mix_pre[1], sh_ml, sc_ml, w_qkv, name="attn_qkv_lat")
    qkv = _qk_rope_call(qkv, head_gain[:, :qk_cols], cos, sin, name="attn_qk_rope_lat")
    kv_ctx = _nmm_call(x_ctx, norm_mix_pre[1], sh_mc, sc_mc, w_qkv, col0=q_cols, name="attn_kv_ctx")
    kv_ctx = _qk_rope_call(kv_ctx, head_gain[:, q_cols:qk_cols], jnp.ones((tc, HEAD), F32),
                           jnp.zeros((tc, HEAD), F32), name="attn_k_norm_ctx")
    qkv = qkv.reshape(b, t, n_qkv)
    kv_ctx = kv_ctx.reshape(b, tc, 2 * kvh * HEAD)
    o = _flash_call(qkv, kv_ctx, heads=heads)
    x_lat = _proj_res_call(o.reshape(b * t, d), attn_w_out, x_lat, norm_mix_post[1], gt_ml, name="attn_out_lat")
    a = _ffn_in_call(x_lat, norm_ffn_pre[1], sh_fl, sc_fl, ffn_w_in, ffn_conv_w[1], ffn_conv_b[1],
                     layer=1, seq_len=t, name="ffn1_in_lat")
    x_lat = _proj_res_call(a, ffn_w_out, x_lat, norm_ffn_post[1], gt_fl, layer=1, name="ffn1_out_lat")
    return x_lat.reshape(b, t, d)
```

```python
import functools
import math

import jax
import jax.numpy as jnp
from jax import lax
from jax.experimental import pallas as pl
from jax.experimental.pallas import tpu as pltpu

F32 = jnp.float32
BF16 = jnp.bfloat16

EPS = 1e-6
GRID_W = 64
ROPE_THETA = 10000.0
LOG2_E = math.log2(math.e)
N_MOD = 6
HEAD = 128
ROW_GROUP = 16
GLA_CHUNK = 128
GLA_RUN = 32
GLA_SAFE_LB = math.exp(-80.0 / (GLA_CHUNK // 2))
F32_SUBLANES = 8
BF16_SUBLANES = 16
ADA_ROWS = F32_SUBLANES
V7X_VMEM_BYTES = 64 * 1024 * 1024
VMEM_LIMIT = V7X_VMEM_BYTES - 6 * 1024 * 1024


def _pick(n, target, align):
    if n <= target:
        return n
    for t in range(target - target % align, 0, -align):
        if n % t == 0:
            return t
    return n


def _params(sem):
    return pltpu.CompilerParams(dimension_semantics=sem, vmem_limit_bytes=VMEM_LIMIT)


def _sigmoid(x):
    return 0.5 * jnp.tanh(0.5 * x) + 0.5


def _silu(x):
    return x * _sigmoid(x)


def _dot(a, b):
    return jnp.dot(a, b, preferred_element_type=F32)


def _dot_nt(a, b):
    return lax.dot_general(a, b, (((1,), (1,)), ((), ())), preferred_element_type=F32)


def _dot_tn(a, b):
    return lax.dot_general(a, b, (((0,), (0,)), ((), ())), preferred_element_type=F32)


def _ada_kernel(c_ref, w_ref, b_ref, o_ref):
    sc = _silu(c_ref[...]).astype(BF16)
    o_ref[...] = _dot(sc, w_ref[...].astype(BF16)) + b_ref[...]


def _ada_call(cond, ada_w, ada_b):
    depth, d, n = ada_w.shape
    tn = _pick(n, 1024, HEAD)
    return pl.pallas_call(
        _ada_kernel,
        grid=(depth, n // tn),
        in_specs=[
            pl.BlockSpec((ADA_ROWS, d), lambda l, j: (0, 0)),
            pl.BlockSpec((None, d, tn), lambda l, j: (l, 0, j)),
            pl.BlockSpec((None, 1, tn), lambda l, j: (l, 0, j)),
        ],
        out_specs=pl.BlockSpec((None, ADA_ROWS, tn), lambda l, j: (l, 0, j)),
        out_shape=jax.ShapeDtypeStruct((depth, ADA_ROWS, n), F32),
        compiler_params=_params(("arbitrary", "arbitrary")),
        name="ada_ln",
    )(cond, ada_w, ada_b.reshape(depth, 1, n))


def _norm_mod(x, gain, shift, scale):
    ms = jnp.mean(x * x, axis=-1, keepdims=True)
    return (x * lax.rsqrt(ms + EPS) * gain) * (1.0 + scale) + shift


def _norm_mod_rows(x_ref, n_rows, h_ref, h_row0, gain, shift, scale):
    mult = gain * (1.0 + scale)

    def body(r, carry):
        start = pl.multiple_of(r * ROW_GROUP, ROW_GROUP)
        x = x_ref[pl.ds(start, ROW_GROUP), :]
        ms = jnp.mean(x * x, axis=-1, keepdims=True)
        dst = pl.ds(pl.multiple_of(h_row0 + start, ROW_GROUP), ROW_GROUP)
        h_ref[dst, :] = (x * lax.rsqrt(ms + EPS) * mult + shift).astype(BF16)
        return carry

    groups = n_rows // ROW_GROUP
    lax.fori_loop(0, groups, body, 0, unroll=math.gcd(groups, 4))


def _split_rows(tm):
    half = tm // 2
    return half if half % ROW_GROUP == 0 else tm


def _nmm_kernel(x_ref, gain_ref, shift_ref, scale_ref, w_ref, o_ref, h_ref):
    j = pl.program_id(1)
    tm = x_ref.shape[0]
    lead = _split_rows(tm)

    @pl.when(j == 0)
    def _():
        gain, shift, scale = gain_ref[...], shift_ref[...], scale_ref[...]
        _norm_mod_rows(x_ref, lead, h_ref, 0, gain, shift, scale)
        if lead < tm:
            h_ref[lead:, :] = _norm_mod(x_ref[lead:, :], gain, shift, scale).astype(BF16)
        o_ref[:lead, :] = _dot(h_ref[:lead, :], w_ref[...]).astype(o_ref.dtype)
        if lead < tm:
            o_ref[lead:, :] = _dot(h_ref[lead:, :], w_ref[...]).astype(o_ref.dtype)

    @pl.when(j > 0)
    def _():
        o_ref[...] = _dot(h_ref[...], w_ref[...]).astype(o_ref.dtype)


def _nmm_call(x, gain, shift, scale, w, *, layer=0, col0=0, name):
    m, d = x.shape
    n = w.shape[2] - col0
    bm = shift.shape[0]
    rows_per_mod = m // bm
    tm = _pick(rows_per_mod, 1024, BF16_SUBLANES)
    tn = _pick(math.gcd(n, col0) if col0 else n, 2048, HEAD)
    mod_blocks = rows_per_mod // tm
    col_block0 = col0 // tn
    return pl.pallas_call(
        _nmm_kernel,
        grid=(m // tm, n // tn),
        in_specs=[
            pl.BlockSpec((tm, d), lambda i, j: (i, 0)),
            pl.BlockSpec((1, d), lambda i, j: (0, 0)),
            pl.BlockSpec((None, 1, d), lambda i, j: (i // mod_blocks, 0, 0)),
            pl.BlockSpec((None, 1, d), lambda i, j: (i // mod_blocks, 0, 0)),
            pl.BlockSpec((None, d, tn), lambda i, j: (layer, 0, col_block0 + j)),
        ],
        out_specs=pl.BlockSpec((tm, tn), lambda i, j: (i, j)),
        out_shape=jax.ShapeDtypeStruct((m, n), BF16),
        scratch_shapes=[pltpu.VMEM((tm, d), BF16)],
        compiler_params=_params(("arbitrary", "arbitrary")),
        name=name,
    )(x, gain.reshape(1, d), shift, scale, w)


def _qk_rope_kernel(x_ref, hg_ref, cos_ref, sin_ref, o_ref):
    cos = cos_ref[...]
    sin = sin_ref[...]
    mean_lanes = jnp.full((HEAD, HEAD), 1.0 / HEAD, BF16)
    for hh in range(x_ref.shape[1] // HEAD):
        cols = slice(hh * HEAD, (hh + 1) * HEAD)
        a = x_ref[:, cols].astype(F32)
        ms = _dot((a * a).astype(BF16), mean_lanes)
        a = a * lax.rsqrt(ms + EPS) * hg_ref[:, cols]
        a = a * cos + pltpu.roll(a, HEAD // 2, axis=1) * sin
        o_ref[:, cols] = a.astype(o_ref.dtype)


def _qk_rope_call(p, head_gain, cos, sin, *, name):
    m, n = p.shape
    nn = head_gain.shape[1]
    tm = _pick(math.gcd(m, cos.shape[0]), 8192, BF16_SUBLANES)
    tn = _pick(math.gcd(n, nn), 1024, HEAD)
    table_blocks = cos.shape[0] // tm
    return pl.pallas_call(
        _qk_rope_kernel,
        grid=(m // tm, nn // tn),
        in_specs=[
            pl.BlockSpec((tm, tn), lambda i, j: (i, j)),
            pl.BlockSpec((1, tn), lambda i, j: (0, j)),
            pl.BlockSpec((tm, HEAD), lambda i, j: (i % table_blocks, 0)),
            pl.BlockSpec((tm, HEAD), lambda i, j: (i % table_blocks, 0)),
        ],
        out_specs=pl.BlockSpec((tm, tn), lambda i, j: (i, j)),
        out_shape=jax.ShapeDtypeStruct((m, n), p.dtype),
        input_output_aliases={0: 0},
        compiler_params=_params(("arbitrary", "arbitrary")),
        name=name,
    )(p, head_gain, cos, sin)


def _residual(x, y, gain, gate):
    ms = jnp.mean(y * y, axis=-1, keepdims=True)
    return x + gate * (y * lax.rsqrt(ms + EPS) * gain)


def _proj_res_kernel(a_ref, w_ref, x_ref, gain_ref, gate_ref, o_ref):
    y = _dot(a_ref[...], w_ref[...])
    o_ref[...] = _residual(x_ref[...], y, gain_ref[...], gate_ref[...])


def _proj_res_ktiled_kernel(a_ref, w_ref, x_ref, gain_ref, gate_ref, o_ref):
    k = pl.program_id(1)
    last = pl.num_programs(1) - 1
    tm = o_ref.shape[0]
    lead = _split_rows(tm)

    @pl.when(k == 0)
    def _():
        o_ref[...] = _dot(a_ref[...], w_ref[...])

    @pl.when(jnp.logical_and(k > 0, k < last))
    def _():
        o_ref[...] += _dot(a_ref[...], w_ref[...])

    @pl.when(k == last)
    def _():
        gain, gate = gain_ref[...], gate_ref[...]
        parts = [slice(0, lead), slice(lead, tm)] if lead < tm else [slice(0, tm)]
        ys = [o_ref[r, :] + _dot(a_ref[r, :], w_ref[...]) for r in parts]
        for r, y in zip(parts, ys):
            o_ref[r, :] = _residual(x_ref[r, :], y, gain, gate)


def _proj_res_call(a, w, x, gain, gate, *, layer=0, name):
    m, k = a.shape
    d = w.shape[2]
    bm = gate.shape[0]
    rows_per_mod = m // bm
    tk = k if k <= d else _pick(k, 512, 2 * HEAD)
    kb = k // tk
    tm = _pick(rows_per_mod, 512 if kb == 1 else 1024, BF16_SUBLANES)
    mod_blocks = rows_per_mod // tm
    return pl.pallas_call(
        _proj_res_kernel if kb == 1 else _proj_res_ktiled_kernel,
        grid=(m // tm, kb),
        in_specs=[
            pl.BlockSpec((tm, tk), lambda i, kk: (i, kk)),
            pl.BlockSpec((None, tk, d), lambda i, kk: (layer, kk, 0)),
            pl.BlockSpec((tm, d), lambda i, kk: (i, 0)),
            pl.BlockSpec((1, d), lambda i, kk: (0, 0)),
            pl.BlockSpec((None, 1, d), lambda i, kk: (i // mod_blocks, 0, 0)),
        ],
        out_specs=pl.BlockSpec((tm, d), lambda i, kk: (i, 0)),
        out_shape=jax.ShapeDtypeStruct((m, d), F32),
        compiler_params=_params(("arbitrary", "arbitrary")),
        name=name,
    )(a, w, x, gain.reshape(1, d), gate)


def _ffn_in_kernel(x_ref, xprev_ref, xnext_ref, gain_ref, shift_ref, scale_ref, wg_ref, wu_ref, cw_ref, cb_ref,
                   o_ref, h_ref, *, blocks_per_seq):
    i = pl.program_id(0)
    j = pl.program_id(1)
    tm = x_ref.shape[0]
    halo = BF16_SUBLANES
    lead = _split_rows(tm)

    def gated(gate, up):
        rows = gate.shape[0]
        g_prev = pltpu.roll(gate, 1, axis=0)[halo:halo + tm, :]
        g_next = pltpu.roll(gate, rows - 1, axis=0)[halo:halo + tm, :]
        conv = (g_prev * cw_ref[0:1, :] + gate[halo:halo + tm, :] * cw_ref[1:2, :] + g_next * cw_ref[2:3, :]
                + cb_ref[...])
        o_ref[...] = (_silu(conv) * up).astype(o_ref.dtype)

    @pl.when(j == 0)
    def _():
        gain, shift, scale = gain_ref[...], shift_ref[...], scale_ref[...]
        seq_block = i % blocks_per_seq
        h_prev = jnp.where(seq_block > 0, _norm_mod(xprev_ref[...], gain, shift, scale), 0.0)
        h_next = jnp.where(seq_block < blocks_per_seq - 1, _norm_mod(xnext_ref[...], gain, shift, scale), 0.0)
        zeros = jnp.zeros((halo - h_prev.shape[0], h_prev.shape[1]), F32)
        h_ref[0:halo, :] = jnp.concatenate([zeros, h_prev], axis=0).astype(BF16)
        h_ref[halo + tm:, :] = jnp.concatenate([h_next, zeros], axis=0).astype(BF16)
        _norm_mod_rows(x_ref, lead, h_ref, halo, gain, shift, scale)
        if lead == tm:
            gated(_dot(h_ref[...], wg_ref[...]), _dot(h_ref[halo:halo + tm, :], wu_ref[...]))
        else:
            cut = halo + lead
            h_ref[cut:halo + tm, :] = _norm_mod(x_ref[lead:, :], gain, shift, scale).astype(BF16)
            gate = jnp.concatenate([_dot(h_ref[:cut, :], wg_ref[...]), _dot(h_ref[cut:, :], wg_ref[...])], axis=0)
            up = jnp.concatenate([_dot(h_ref[halo:cut, :], wu_ref[...]),
                                  _dot(h_ref[cut:halo + tm, :], wu_ref[...])], axis=0)
            gated(gate, up)

    @pl.when(j > 0)
    def _():
        gated(_dot(h_ref[...], wg_ref[...]), _dot(h_ref[halo:halo + tm, :], wu_ref[...]))


def _ffn_in_call(x, gain, shift, scale, w, conv_w, conv_b, *, layer, seq_len, name):
    m, d = x.shape
    f = w.shape[2] // 2
    bm = shift.shape[0]
    rows_per_mod = m // bm
    tm = _pick(seq_len, 1024, BF16_SUBLANES)
    tn = _pick(f, 512, HEAD)
    assert rows_per_mod % tm == 0
    mod_blocks = rows_per_mod // tm
    nb = f // tn
    xh = F32_SUBLANES
    tiles_per_tm = tm // xh
    last_tile = m // xh - 1
    body = functools.partial(_ffn_in_kernel, blocks_per_seq=seq_len // tm)
    return pl.pallas_call(
        body,
        grid=(m // tm, nb),
        in_specs=[
            pl.BlockSpec((tm, d), lambda i, j: (i, 0)),
            pl.BlockSpec((xh, d), lambda i, j: (jnp.maximum(i * tiles_per_tm - 1, 0), 0)),
            pl.BlockSpec((xh, d), lambda i, j: (jnp.minimum((i + 1) * tiles_per_tm, last_tile), 0)),
            pl.BlockSpec((1, d), lambda i, j: (0, 0)),
            pl.BlockSpec((None, 1, d), lambda i, j: (i // mod_blocks, 0, 0)),
            pl.BlockSpec((None, 1, d), lambda i, j: (i // mod_blocks, 0, 0)),
            pl.BlockSpec((None, d, tn), lambda i, j: (layer, 0, j)),
            pl.BlockSpec((None, d, tn), lambda i, j: (layer, 0, nb + j)),
            pl.BlockSpec((3, tn), lambda i, j: (0, j)),
            pl.BlockSpec((1, tn), lambda i, j: (0, j)),
        ],
        out_specs=pl.BlockSpec((tm, tn), lambda i, j: (i, j)),
        out_shape=jax.ShapeDtypeStruct((m, f), BF16),
        scratch_shapes=[pltpu.VMEM((tm + 2 * BF16_SUBLANES, d), BF16)],
        compiler_params=_params(("arbitrary", "arbitrary")),
        name=name,
    )(x, x, x, gain.reshape(1, d), shift, scale, w, w, conv_w, conv_b.reshape(1, f))


def _gla_gates(f_raw, lb, within):
    half = 0.5 * (1.0 - lb)
    ht = half * jnp.tanh(0.5 * f_raw.astype(F32))
    lf = jnp.log((1.0 - half) + ht)
    ones = jnp.where(within, 1.0, 0.0).astype(BF16)
    lf_hi = lf.astype(BF16)
    lf_lo = (lf - lf_hi.astype(F32)).astype(BF16)
    return half - ht, _dot(ones, lf_hi) + _dot(ones, lf_lo)


def _gla_pairs(qs, kk, b, forward, within):
    c = qs.shape[0]
    a_row = c // 2 - 1 if forward else c // 2
    e_row = c - 1 if forward else 0
    anchor = b[a_row:a_row + 1, :]
    b_end = b[e_row:e_row + 1, :]
    qa = qs * jnp.exp(b - anchor)
    ka = kk * jnp.exp(anchor - b)
    scores = jnp.where(within, _dot_nt(qa.astype(BF16), ka.astype(BF16)), 0.0).astype(BF16)
    return (scores, (qa * jnp.exp(anchor)).astype(BF16), (ka * jnp.exp(b_end - anchor)).astype(BF16),
            jnp.exp(b_end))


def _gla_pairs_exact(qs, kk, b, forward, within, tmp):
    tb_ref, tq_ref, ts_ref = tmp
    c = qs.shape[0]
    b_end = b[c - 1:c, :] if forward else b[0:1, :]
    tb_ref[...] = b
    tq_ref[...] = qs
    ones = jnp.ones((F32_SUBLANES, qs.shape[1]), BF16)

    def row(t, carry):
        rel = jnp.minimum(tb_ref[pl.ds(t, 1), :] - b, 0.0)
        e = (tq_ref[pl.ds(t, 1), :] * kk) * jnp.exp(rel)
        ts_ref[pl.ds(t, 1), :] = _dot_nt(ones, e.astype(BF16))[0:1, :]
        return carry

    lax.fori_loop(0, c, row, 0)
    scores = jnp.where(within, ts_ref[...], 0.0).astype(BF16)
    return scores, (qs * jnp.exp(b)).astype(BF16), (kk * jnp.exp(b_end - b)).astype(BF16), jnp.exp(b_end)


def _gla_readout(o, g_raw, o_gain):
    ms = jnp.mean(o * o, axis=-1, keepdims=True)
    y = o * lax.rsqrt(ms + EPS) * o_gain
    return (y * _silu(g_raw.astype(F32))).astype(BF16)


def _gla_kernel(qc, ffc, fbc, vc, gc, ql, ffl, fbl, vl, gl, lb_ref, og_ref, yc_ref, yl_ref,
                o_ref, qb_ref, ke_ref, dec_ref, st_ref, tb_ref, tq_ref, ts_ref):
    c = GLA_CHUNK
    tc = qc.shape[0]
    n_ctx = tc // c
    n_lat = ql.shape[0] // c
    lb = (lb_ref[0], lb_ref[1])
    row = lax.broadcasted_iota(jnp.int32, (c, c), 0)
    col = lax.broadcasted_iota(jnp.int32, (c, c), 1)
    within = (col <= row, col >= row)

    def rows(start):
        return pl.ds(start if isinstance(start, int) else pl.multiple_of(start, c), c)

    def local(refs, src0, dst0, chunk0, n, exact):
        q, ff, fb, v = refs
        src = [rows(src0 + j * c) for j in range(n)]
        dst = [rows(dst0 + j * c) for j in range(n)]
        qs = [_silu(q[src[j], :].astype(F32)) * (HEAD ** -0.5) for j in range(n)]
        gates = [[_gla_gates(f[src[j], :], lb[d], within[d]) for d, f in enumerate((ff, fb))] for j in range(n)]
        scores = []
        for j in range(n):
            per_dir = []
            for d in range(2):
                kk, b = gates[j][d]
                if exact:
                    sc, qb, ke, dec = _gla_pairs_exact(qs[j], kk, b, d == 0, within[d], (tb_ref, tq_ref, ts_ref))
                else:
                    sc, qb, ke, dec = _gla_pairs(qs[j], kk, b, d == 0, within[d])
                qb_ref[d, dst[j], :] = qb
                ke_ref[d, dst[j], :] = ke
                dec_ref[d, chunk0 + j] = dec
                per_dir.append(sc)
            scores.append(per_dir)
        for j in range(n):
            vj = v[src[j], :]
            o_ref[dst[j], :] = _dot(scores[j][0], vj) + _dot(scores[j][1], vj)

    def phase1(exact):
        run = 1 if exact else math.gcd(n_lat, GLA_RUN)
        run_ctx = math.gcd(n_ctx, run)
        for i in range(0, n_ctx, run_ctx):
            local((qc, ffc, fbc, vc), i * c, i * c, i, run_ctx, exact)

        def body(i, carry):
            src0 = pl.multiple_of(i * (run * c), c)
            local((ql, ffl, fbl, vl), src0, tc + src0, n_ctx + i * run, run, exact)
            return carry

        lax.fori_loop(0, n_lat // run, body, 0)

    safe = jnp.min(jnp.minimum(lb[0], lb[1])) >= GLA_SAFE_LB

    @pl.when(safe)
    def _():
        phase1(False)

    @pl.when(jnp.logical_not(safe))
    def _():
        phase1(True)

    st_ref[...] = jnp.zeros_like(st_ref)

    o_gain = og_ref[...]

    def steps(v, first, n, n_seg, row0, chunk0, finish=None):
        order = [[first + s for s in range(n)], [n_seg - 1 - first - s for s in range(n)]]
        src = [[rows(j * c) for j in order[d]] for d in range(2)]
        dst = [[rows(row0 + j * c) for j in order[d]] for d in range(2)]
        kv = [[_dot_tn(v[src[d][s], :], ke_ref[d, dst[d][s], :]) for s in range(n)] for d in range(2)]
        for d in range(2):
            st = st_ref[d]
            for s in range(n):
                o = o_ref[dst[d][s], :] + _dot_nt(qb_ref[d, dst[d][s], :], st.astype(BF16))
                if finish is None:
                    o_ref[dst[d][s], :] = o
                else:
                    g, y = finish
                    y[src[d][s], :] = _gla_readout(o, g[src[d][s], :], o_gain)
                st = st * dec_ref[d, chunk0 + order[d][s]] + kv[d][s]
            st_ref[d] = st

    steps(vc, 0, n_ctx, n_ctx, 0, 0)
    for i in range(n_ctx):
        r = slice(i * c, (i + 1) * c)
        yc_ref[r, :] = _gla_readout(o_ref[r, :], gc[r, :], o_gain)

    run = math.gcd(n_lat, GLA_RUN)
    n_runs = n_lat // run
    meet = (n_runs + 1) // 2

    def steps_body(i, carry):
        steps(vl, i * run, run, n_lat, tc, n_ctx)
        return carry

    def last_steps_body(i, carry):
        steps(vl, i * run, run, n_lat, tc, n_ctx, finish=(gl, yl_ref))
        return carry

    lax.fori_loop(0, meet, steps_body, 0)
    if n_runs % 2:
        for j in range((n_runs // 2) * run, (n_runs // 2 + 1) * run):
            yl_ref[rows(j * c), :] = _gla_readout(o_ref[rows(tc + j * c), :], gl[rows(j * c), :], o_gain)
    lax.fori_loop(meet, n_runs, last_steps_body, 0)


def _gla_call(p_ctx, p_lat, lower_bound, o_gain):
    b, tc, d5 = p_ctx.shape
    t = p_lat.shape[1]
    d = d5 // 5
    heads = d // HEAD
    assert tc % GLA_CHUNK == 0 and t % GLA_CHUNK == 0

    def col(tt, part):
        return pl.BlockSpec((None, tt, HEAD), lambda bi, h: (bi, 0, part * heads + h))

    out_spec = lambda tt: pl.BlockSpec((None, tt, HEAD), lambda bi, h: (bi, 0, h))
    return pl.pallas_call(
        _gla_kernel,
        grid=(b, heads),
        in_specs=[col(tc, p) for p in range(5)] + [col(t, p) for p in range(5)] + [
            pl.BlockSpec((2, None, 1, HEAD), lambda bi, h: (0, h, 0, 0)),
            pl.BlockSpec((1, HEAD), lambda bi, h: (0, 0)),
        ],
        out_specs=[out_spec(tc), out_spec(t)],
        out_shape=[jax.ShapeDtypeStruct((b, tc, d), BF16), jax.ShapeDtypeStruct((b, t, d), BF16)],
        scratch_shapes=[
            pltpu.VMEM((tc + t, HEAD), F32),
            pltpu.VMEM((2, tc + t, HEAD), BF16),
            pltpu.VMEM((2, tc + t, HEAD), BF16),
            pltpu.VMEM((2, (tc + t) // GLA_CHUNK, 1, HEAD), F32),
            pltpu.VMEM((2, HEAD, HEAD), F32),
            pltpu.VMEM((GLA_CHUNK, HEAD), F32),
            pltpu.VMEM((GLA_CHUNK, HEAD), F32),
            pltpu.VMEM((GLA_CHUNK, GLA_CHUNK), F32),
        ],
        compiler_params=_params(("arbitrary", "arbitrary")),
        name="hgrn_scan",
    )(*([p_ctx] * 5), *([p_lat] * 5), lower_bound.reshape(2, heads, 1, HEAD), o_gain.reshape(1, HEAD))


def _flash_kernel(q_ref, kc_ref, vc_ref, kl_ref, vl_ref, o_ref, *, tk):
    tq = q_ref.shape[0]
    group = q_ref.shape[1] // HEAD
    qs = [q_ref[:, g * HEAD:(g + 1) * HEAD] for g in range(group)]

    def attend(carry, k, v):
        v_ext = jnp.concatenate([v, jnp.ones_like(v)], axis=1)
        out = []
        for g in range(group):
            m, acc = carry[g]
            s = _dot_nt(qs[g], k)
            m_new = jnp.maximum(m, jnp.max(s, axis=-1, keepdims=True))
            p = jnp.exp2(s - m_new).astype(BF16)
            out.append((m_new, jnp.exp2(m - m_new) * acc + _dot(p, v_ext)))
        return tuple(out)

    carry = tuple((jnp.full((tq, 1), -jnp.inf, F32), jnp.zeros((tq, 2 * HEAD), F32)) for _ in range(group))
    for start in range(0, kl_ref.shape[0], tk):
        carry = attend(carry, kl_ref[start:start + tk, :], vl_ref[start:start + tk, :])
    carry = attend(carry, kc_ref[...], vc_ref[...])
    for g in range(group):
        acc = carry[g][1]
        o_ref[:, g * HEAD:(g + 1) * HEAD] = (acc[:, :HEAD] / acc[:, HEAD:]).astype(o_ref.dtype)


def _flash_call(qkv, kv_ctx, *, heads):
    b, t, n = qkv.shape
    tc = kv_ctx.shape[1]
    kvh = (n // HEAD - heads) // 2
    group = heads // kvh
    tq = _pick(t, 512, BF16_SUBLANES)
    tk = _pick(t, 1024, 2 * HEAD)
    assert t // tk <= 16
    body = functools.partial(_flash_kernel, tk=tk)
    return pl.pallas_call(
        body,
        grid=(b, kvh, t // tq),
        in_specs=[
            pl.BlockSpec((None, tq, group * HEAD), lambda bi, h, i: (bi, i, h)),
            pl.BlockSpec((None, tc, HEAD), lambda bi, h, i: (bi, 0, h)),
            pl.BlockSpec((None, tc, HEAD), lambda bi, h, i: (bi, 0, kvh + h)),
            pl.BlockSpec((None, t, HEAD), lambda bi, h, i: (bi, 0, heads + h)),
            pl.BlockSpec((None, t, HEAD), lambda bi, h, i: (bi, 0, heads + kvh + h)),
        ],
        out_specs=pl.BlockSpec((None, tq, group * HEAD), lambda bi, h, i: (bi, i, h)),
        out_shape=jax.ShapeDtypeStruct((b, t, heads * HEAD), BF16),
        compiler_params=_params(("arbitrary", "arbitrary", "arbitrary")),
        name="gqa_flash",
    )(qkv, kv_ctx, kv_ctx, qkv, qkv)


def _rope_tables(t):
    pos = jnp.arange(t, dtype=jnp.int32)
    rows = (pos // GRID_W).astype(F32)
    cols = (pos % GRID_W).astype(F32)
    axis_dim = HEAD // 2
    inv_freq = ROPE_THETA ** (-jnp.arange(0, axis_dim, 2, dtype=F32) / axis_dim)
    ang = jnp.concatenate([rows[:, None] * inv_freq, cols[:, None] * inv_freq], axis=-1)
    cos, sin = jnp.cos(ang), jnp.sin(ang)
    return jnp.concatenate([cos, cos], axis=-1), jnp.concatenate([-sin, sin], axis=-1)


def _split_mods(mods, b):
    d = mods.shape[1] // N_MOD
    lat = [mods[:b, i * d:(i + 1) * d].reshape(b, 1, d) for i in range(N_MOD)]
    ctx = [mods[b:b + 1, i * d:(i + 1) * d].reshape(1, 1, d) for i in range(N_MOD)]
    return lat, ctx


def kernel(x, c, ctx, c_ctx, ada_w, ada_b, norm_mix_pre, norm_mix_post, norm_ffn_pre, norm_ffn_post, hgrn_w_in, hgrn_lb_logits, hgrn_o_norm, hgrn_w_out, attn_w_qkv, attn_q_norm, attn_k_norm, attn_w_out, ffn_w_in, ffn_conv_w, ffn_conv_b, ffn_w_out):
    b, t, d = x.shape
    tc = ctx.shape[1]
    heads = d // HEAD
    assert ada_w.shape[0] == 2 and b + 1 <= ADA_ROWS

    cond = jnp.zeros((ADA_ROWS, d), F32).at[:b].set(c).at[b].set(c_ctx)
    mods = _ada_call(cond, ada_w, ada_b)
    x_lat = x.reshape(b * t, d)
    x_ctx = ctx.reshape(b * tc, d)

    (sh_ml, sc_ml, gt_ml, sh_fl, sc_fl, gt_fl), (sh_mc, sc_mc, gt_mc, sh_fc, sc_fc, gt_fc) = _split_mods(mods[0], b)
    lower_bound = jnp.cumsum(jax.nn.softmax(hgrn_lb_logits.astype(F32), axis=1), axis=1)[:, 0]
    hgrn_w_in, hgrn_w_out, attn_w_out = hgrn_w_in.astype(BF16), hgrn_w_out.astype(BF16), attn_w_out.astype(BF16)
    ffn_w_in, ffn_w_out = ffn_w_in.astype(BF16), ffn_w_out.astype(BF16)
    p_lat = _nmm_call(x_lat, norm_mix_pre[0], sh_ml, sc_ml, hgrn_w_in, name="hgrn_in_lat")
    p_ctx = _nmm_call(x_ctx, norm_mix_pre[0], sh_mc, sc_mc, hgrn_w_in, name="hgrn_in_ctx")
    y_ctx, y_lat = _gla_call(p_ctx.reshape(b, tc, 5 * d), p_lat.reshape(b, t, 5 * d), lower_bound, hgrn_o_norm[0])
    x_lat = _proj_res_call(y_lat.reshape(b * t, d), hgrn_w_out, x_lat, norm_mix_post[0], gt_ml, name="hgrn_out_lat")
    x_ctx = _proj_res_call(y_ctx.reshape(b * tc, d), hgrn_w_out, x_ctx, norm_mix_post[0], gt_mc, name="hgrn_out_ctx")
    a = _ffn_in_call(x_lat, norm_ffn_pre[0], sh_fl, sc_fl, ffn_w_in, ffn_conv_w[0], ffn_conv_b[0],
                     layer=0, seq_len=t, name="ffn0_in_lat")
    x_lat = _proj_res_call(a, ffn_w_out, x_lat, norm_ffn_post[0], gt_fl, layer=0, name="ffn0_out_lat")
    a = _ffn_in_call(x_ctx, norm_ffn_pre[0], sh_fc, sc_fc, ffn_w_in, ffn_conv_w[0], ffn_conv_b[0],
                     layer=0, seq_len=tc, name="ffn0_in_ctx")
    x_ctx = _proj_res_call(a, ffn_w_out, x_ctx, norm_ffn_post[0], gt_fc, layer=0, name="ffn0_out_ctx")

    (sh_ml, sc_ml, gt_ml, sh_fl, sc_fl, gt_fl), (sh_mc, sc_mc, _, _, _, _) = _split_mods(mods[1], b)
    n_qkv = attn_w_qkv.shape[2]
    kvh = (n_qkv // HEAD - heads) // 2
    perm = jnp.concatenate([jnp.arange(0, HEAD, 2), jnp.arange(1, HEAD, 2)])
    head_perm = (jnp.arange(heads + kvh)[:, None] * HEAD + perm[None, :]).reshape(-1)
    col_perm = jnp.concatenate([head_perm, jnp.arange((heads + kvh) * HEAD, n_qkv)])
    w_qkv = attn_w_qkv[:, :, col_perm].astype(BF16)
    head_gain = jnp.concatenate([
        jnp.tile(attn_q_norm[0][perm] * (LOG2_E * HEAD ** -0.5), heads),
        jnp.tile(attn_k_norm[0][perm], kvh),
        jnp.ones((kvh * HEAD,), F32),
    ]).reshape(1, n_qkv)
    cos, sin = _rope_tables(t)
    qk_cols = (heads + kvh) * HEAD
    q_cols = heads * HEAD
    qkv = _nmm_call(x_lat, norm_mix_pre[1], sh_ml, sc_ml, w_qkv, name="attn_qkv_lat")
    qkv = _qk_rope_call(qkv, head_gain[:, :qk_cols], cos, sin, name="attn_qk_rope_lat")
    kv_ctx = _nmm_call(x_ctx, norm_mix_pre[1], sh_mc, sc_mc, w_qkv, col0=q_cols, name="attn_kv_ctx")
    kv_ctx = _qk_rope_call(kv_ctx, head_gain[:, q_cols:qk_cols], jnp.ones((tc, HEAD), F32),
                           jnp.zeros((tc, HEAD), F32), name="attn_k_norm_ctx")
    qkv = qkv.reshape(b, t, n_qkv)
    kv_ctx = kv_ctx.reshape(b, tc, 2 * kvh * HEAD)
    o = _flash_call(qkv, kv_ctx, heads=heads)
    x_lat = _proj_res_call(o.reshape(b * t, d), attn_w_out, x_lat, norm_mix_post[1], gt_ml, name="attn_out_lat")
    a = _ffn_in_call(x_lat, norm_ffn_pre[1], sh_fl, sc_fl, ffn_w_in, ffn_conv_w[1], ffn_conv_b[1],
                     layer=1, seq_len=t, name="ffn1_in_lat")
    x_lat = _proj_res_call(a, ffn_w_out, x_lat, norm_ffn_post[1], gt_fl, layer=1, name="ffn1_out_lat")
    return x_lat.reshape(b, t, d)
```

```python
import functools
import math

import jax
import jax.numpy as jnp
from jax import lax
from jax.experimental import pallas as pl
from jax.experimental.pallas import tpu as pltpu

F32 = jnp.float32
BF16 = jnp.bfloat16

EPS = 1e-6
GRID_W = 64
ROPE_THETA = 10000.0
LOG2_E = math.log2(math.e)
N_MOD = 6
HEAD = 128
ROW_GROUP = 16
GLA_CHUNK = 128
GLA_RUN = 32
GLA_SAFE_LB = math.exp(-80.0 / (GLA_CHUNK // 2))
F32_SUBLANES = 8
BF16_SUBLANES = 16
ADA_ROWS = F32_SUBLANES
V7X_VMEM_BYTES = 64 * 1024 * 1024
VMEM_LIMIT = V7X_VMEM_BYTES - 6 * 1024 * 1024


def _pick(n, target, align):
    if n <= target:
        return n
    for t in range(target - target % align, 0, -align):
        if n % t == 0:
            return t
    return n


def _params(sem):
    return pltpu.CompilerParams(dimension_semantics=sem, vmem_limit_bytes=VMEM_LIMIT)


def _sigmoid(x):
    return 0.5 * jnp.tanh(0.5 * x) + 0.5


def _silu(x):
    return x * _sigmoid(x)


def _dot(a, b):
    return jnp.dot(a, b, preferred_element_type=F32)


def _dot_nt(a, b):
    return lax.dot_general(a, b, (((1,), (1,)), ((), ())), preferred_element_type=F32)


def _dot_tn(a, b):
    return lax.dot_general(a, b, (((0,), (0,)), ((), ())), preferred_element_type=F32)


def _ada_kernel(c_ref, w_ref, b_ref, o_ref):
    sc = _silu(c_ref[...]).astype(BF16)
    o_ref[...] = _dot(sc, w_ref[...].astype(BF16)) + b_ref[...]


def _ada_call(cond, ada_w, ada_b):
    depth, d, n = ada_w.shape
    tn = _pick(n, 1024, HEAD)
    return pl.pallas_call(
        _ada_kernel,
        grid=(depth, n // tn),
        in_specs=[
            pl.BlockSpec((ADA_ROWS, d), lambda l, j: (0, 0)),
            pl.BlockSpec((None, d, tn), lambda l, j: (l, 0, j)),
            pl.BlockSpec((None, 1, tn), lambda l, j: (l, 0, j)),
        ],
        out_specs=pl.BlockSpec((None, ADA_ROWS, tn), lambda l, j: (l, 0, j)),
        out_shape=jax.ShapeDtypeStruct((depth, ADA_ROWS, n), F32),
        compiler_params=_params(("arbitrary", "arbitrary")),
        name="ada_ln",
    )(cond, ada_w, ada_b.reshape(depth, 1, n))


def _norm_mod(x, gain, shift, scale):
    ms = jnp.mean(x * x, axis=-1, keepdims=True)
    return (x * lax.rsqrt(ms + EPS) * gain) * (1.0 + scale) + shift


def _norm_mod_rows(x_ref, n_rows, h_ref, h_row0, gain, shift, scale):
    mult = gain * (1.0 + scale)

    def body(r, carry):
        start = pl.multiple_of(r * ROW_GROUP, ROW_GROUP)
        x = x_ref[pl.ds(start, ROW_GROUP), :]
        ms = jnp.mean(x * x, axis=-1, keepdims=True)
        dst = pl.ds(pl.multiple_of(h_row0 + start, ROW_GROUP), ROW_GROUP)
        h_ref[dst, :] = (x * lax.rsqrt(ms + EPS) * mult + shift).astype(BF16)
        return carry

    groups = n_rows // ROW_GROUP
    lax.fori_loop(0, groups, body, 0, unroll=math.gcd(groups, 8))


def _split_rows(tm):
    half = tm // 2
    return half if half % ROW_GROUP == 0 else tm


def _nmm_kernel(x_ref, gain_ref, shift_ref, scale_ref, w_ref, o_ref, h_ref):
    j = pl.program_id(1)
    tm = x_ref.shape[0]
    lead = _split_rows(tm)

    @pl.when(j == 0)
    def _():
        gain, shift, scale = gain_ref[...], shift_ref[...], scale_ref[...]
        _norm_mod_rows(x_ref, lead, h_ref, 0, gain, shift, scale)
        if lead < tm:
            h_ref[lead:, :] = _norm_mod(x_ref[lead:, :], gain, shift, scale).astype(BF16)
        o_ref[:lead, :] = _dot(h_ref[:lead, :], w_ref[...]).astype(o_ref.dtype)
        if lead < tm:
            o_ref[lead:, :] = _dot(h_ref[lead:, :], w_ref[...]).astype(o_ref.dtype)

    @pl.when(j > 0)
    def _():
        o_ref[...] = _dot(h_ref[...], w_ref[...]).astype(o_ref.dtype)


def _nmm_call(x, gain, shift, scale, w, *, layer=0, col0=0, name):
    m, d = x.shape
    n = w.shape[2] - col0
    bm = shift.shape[0]
    rows_per_mod = m // bm
    tm = _pick(rows_per_mod, 1024, BF16_SUBLANES)
    tn = _pick(math.gcd(n, col0) if col0 else n, 2048, HEAD)
    mod_blocks = rows_per_mod // tm
    col_block0 = col0 // tn
    return pl.pallas_call(
        _nmm_kernel,
        grid=(m // tm, n // tn),
        in_specs=[
            pl.BlockSpec((tm, d), lambda i, j: (i, 0)),
            pl.BlockSpec((1, d), lambda i, j: (0, 0)),
            pl.BlockSpec((None, 1, d), lambda i, j: (i // mod_blocks, 0, 0)),
            pl.BlockSpec((None, 1, d), lambda i, j: (i // mod_blocks, 0, 0)),
            pl.BlockSpec((None, d, tn), lambda i, j: (layer, 0, col_block0 + j)),
        ],
        out_specs=pl.BlockSpec((tm, tn), lambda i, j: (i, j)),
        out_shape=jax.ShapeDtypeStruct((m, n), BF16),
        scratch_shapes=[pltpu.VMEM((tm, d), BF16)],
        compiler_params=_params(("arbitrary", "arbitrary")),
        name=name,
    )(x, gain.reshape(1, d), shift, scale, w)


def _qk_rope_kernel(x_ref, hg_ref, cos_ref, sin_ref, o_ref):
    cos = cos_ref[...]
    sin = sin_ref[...]
    mean_lanes = jnp.full((HEAD, HEAD), 1.0 / HEAD, BF16)
    for hh in range(x_ref.shape[1] // HEAD):
        cols = slice(hh * HEAD, (hh + 1) * HEAD)
        a = x_ref[:, cols].astype(F32)
        ms = _dot((a * a).astype(BF16), mean_lanes)
        a = a * lax.rsqrt(ms + EPS) * hg_ref[:, cols]
        a = a * cos + pltpu.roll(a, HEAD // 2, axis=1) * sin
        o_ref[:, cols] = a.astype(o_ref.dtype)


def _qk_rope_call(p, head_gain, cos, sin, *, name):
    m, n = p.shape
    nn = head_gain.shape[1]
    tm = _pick(math.gcd(m, cos.shape[0]), 8192, BF16_SUBLANES)
    tn = _pick(math.gcd(n, nn), 1024, HEAD)
    table_blocks = cos.shape[0] // tm
    return pl.pallas_call(
        _qk_rope_kernel,
        grid=(m // tm, nn // tn),
        in_specs=[
            pl.BlockSpec((tm, tn), lambda i, j: (i, j)),
            pl.BlockSpec((1, tn), lambda i, j: (0, j)),
            pl.BlockSpec((tm, HEAD), lambda i, j: (i % table_blocks, 0)),
            pl.BlockSpec((tm, HEAD), lambda i, j: (i % table_blocks, 0)),
        ],
        out_specs=pl.BlockSpec((tm, tn), lambda i, j: (i, j)),
        out_shape=jax.ShapeDtypeStruct((m, n), p.dtype),
        input_output_aliases={0: 0},
        compiler_params=_params(("arbitrary", "arbitrary")),
        name=name,
    )(p, head_gain, cos, sin)


def _residual(x, y, gain, gate):
    ms = jnp.mean(y * y, axis=-1, keepdims=True)
    return x + gate * (y * lax.rsqrt(ms + EPS) * gain)


def _proj_res_kernel(a_ref, w_ref, x_ref, gain_ref, gate_ref, o_ref):
    y = _dot(a_ref[...], w_ref[...])
    o_ref[...] = _residual(x_ref[...], y, gain_ref[...], gate_ref[...])


def _proj_res_ktiled_kernel(a_ref, w_ref, x_ref, gain_ref, gate_ref, o_ref):
    k = pl.program_id(1)
    last = pl.num_programs(1) - 1
    tm = o_ref.shape[0]
    lead = _split_rows(tm)

    @pl.when(k == 0)
    def _():
        o_ref[...] = _dot(a_ref[...], w_ref[...])

    @pl.when(jnp.logical_and(k > 0, k < last))
    def _():
        o_ref[...] += _dot(a_ref[...], w_ref[...])

    @pl.when(k == last)
    def _():
        gain, gate = gain_ref[...], gate_ref[...]
        parts = [slice(0, lead), slice(lead, tm)] if lead < tm else [slice(0, tm)]
        ys = [o_ref[r, :] + _dot(a_ref[r, :], w_ref[...]) for r in parts]
        for r, y in zip(parts, ys):
            o_ref[r, :] = _residual(x_ref[r, :], y, gain, gate)


def _proj_res_call(a, w, x, gain, gate, *, layer=0, name):
    m, k = a.shape
    d = w.shape[2]
    bm = gate.shape[0]
    rows_per_mod = m // bm
    tk = k if k <= d else _pick(k, 512, 2 * HEAD)
    kb = k // tk
    tm = _pick(rows_per_mod, 512 if kb == 1 else 1024, BF16_SUBLANES)
    mod_blocks = rows_per_mod // tm
    return pl.pallas_call(
        _proj_res_kernel if kb == 1 else _proj_res_ktiled_kernel,
        grid=(m // tm, kb),
        in_specs=[
            pl.BlockSpec((tm, tk), lambda i, kk: (i, kk)),
            pl.BlockSpec((None, tk, d), lambda i, kk: (layer, kk, 0)),
            pl.BlockSpec((tm, d), lambda i, kk: (i, 0)),
            pl.BlockSpec((1, d), lambda i, kk: (0, 0)),
            pl.BlockSpec((None, 1, d), lambda i, kk: (i // mod_blocks, 0, 0)),
        ],
        out_specs=pl.BlockSpec((tm, d), lambda i, kk: (i, 0)),
        out_shape=jax.ShapeDtypeStruct((m, d), F32),
        compiler_params=_params(("arbitrary", "arbitrary")),
        name=name,
    )(a, w, x, gain.reshape(1, d), gate)


def _ffn_in_kernel(x_ref, xprev_ref, xnext_ref, gain_ref, shift_ref, scale_ref, wg_ref, wu_ref, cw_ref, cb_ref,
                   o_ref, h_ref, *, blocks_per_seq):
    i = pl.program_id(0)
    j = pl.program_id(1)
    tm = x_ref.shape[0]
    halo = BF16_SUBLANES
    lead = _split_rows(tm)

    def gated(gate, up):
        rows = gate.shape[0]
        g_prev = pltpu.roll(gate, 1, axis=0)[halo:halo + tm, :]
        g_next = pltpu.roll(gate, rows - 1, axis=0)[halo:halo + tm, :]
        conv = (g_prev * cw_ref[0:1, :] + gate[halo:halo + tm, :] * cw_ref[1:2, :] + g_next * cw_ref[2:3, :]
                + cb_ref[...])
        o_ref[...] = (_silu(conv) * up).astype(o_ref.dtype)

    @pl.when(j == 0)
    def _():
        gain, shift, scale = gain_ref[...], shift_ref[...], scale_ref[...]
        seq_block = i % blocks_per_seq
        h_prev = jnp.where(seq_block > 0, _norm_mod(xprev_ref[...], gain, shift, scale), 0.0)
        h_next = jnp.where(seq_block < blocks_per_seq - 1, _norm_mod(xnext_ref[...], gain, shift, scale), 0.0)
        zeros = jnp.zeros((halo - h_prev.shape[0], h_prev.shape[1]), F32)
        h_ref[0:halo, :] = jnp.concatenate([zeros, h_prev], axis=0).astype(BF16)
        h_ref[halo + tm:, :] = jnp.concatenate([h_next, zeros], axis=0).astype(BF16)
        _norm_mod_rows(x_ref, lead, h_ref, halo, gain, shift, scale)
        if lead == tm:
            gated(_dot(h_ref[...], wg_ref[...]), _dot(h_ref[halo:halo + tm, :], wu_ref[...]))
        else:
            cut = halo + lead
            h_ref[cut:halo + tm, :] = _norm_mod(x_ref[lead:, :], gain, shift, scale).astype(BF16)
            gate = jnp.concatenate([_dot(h_ref[:cut, :], wg_ref[...]), _dot(h_ref[cut:, :], wg_ref[...])], axis=0)
            up = jnp.concatenate([_dot(h_ref[halo:cut, :], wu_ref[...]),
                                  _dot(h_ref[cut:halo + tm, :], wu_ref[...])], axis=0)
            gated(gate, up)

    @pl.when(j > 0)
    def _():
        gated(_dot(h_ref[...], wg_ref[...]), _dot(h_ref[halo:halo + tm, :], wu_ref[...]))


def _ffn_in_call(x, gain, shift, scale, w, conv_w, conv_b, *, layer, seq_len, name):
    m, d = x.shape
    f = w.shape[2] // 2
    bm = shift.shape[0]
    rows_per_mod = m // bm
    tm = _pick(seq_len, 1024, BF16_SUBLANES)
    tn = _pick(f, 512, HEAD)
    assert rows_per_mod % tm == 0
    mod_blocks = rows_per_mod // tm
    nb = f // tn
    xh = F32_SUBLANES
    tiles_per_tm = tm // xh
    last_tile = m // xh - 1
    body = functools.partial(_ffn_in_kernel, blocks_per_seq=seq_len // tm)
    return pl.pallas_call(
        body,
        grid=(m // tm, nb),
        in_specs=[
            pl.BlockSpec((tm, d), lambda i, j: (i, 0)),
            pl.BlockSpec((xh, d), lambda i, j: (jnp.maximum(i * tiles_per_tm - 1, 0), 0)),
            pl.BlockSpec((xh, d), lambda i, j: (jnp.minimum((i + 1) * tiles_per_tm, last_tile), 0)),
            pl.BlockSpec((1, d), lambda i, j: (0, 0)),
            pl.BlockSpec((None, 1, d), lambda i, j: (i // mod_blocks, 0, 0)),
            pl.BlockSpec((None, 1, d), lambda i, j: (i // mod_blocks, 0, 0)),
            pl.BlockSpec((None, d, tn), lambda i, j: (layer, 0, j)),
            pl.BlockSpec((None, d, tn), lambda i, j: (layer, 0, nb + j)),
            pl.BlockSpec((3, tn), lambda i, j: (0, j)),
            pl.BlockSpec((1, tn), lambda i, j: (0, j)),
        ],
        out_specs=pl.BlockSpec((tm, tn), lambda i, j: (i, j)),
        out_shape=jax.ShapeDtypeStruct((m, f), BF16),
        scratch_shapes=[pltpu.VMEM((tm + 2 * BF16_SUBLANES, d), BF16)],
        compiler_params=_params(("arbitrary", "arbitrary")),
        name=name,
    )(x, x, x, gain.reshape(1, d), shift, scale, w, w, conv_w, conv_b.reshape(1, f))


def _gla_gates(f_raw, lb, within):
    half = 0.5 * (1.0 - lb)
    ht = half * jnp.tanh(0.5 * f_raw.astype(F32))
    lf = jnp.log((1.0 - half) + ht)
    ones = jnp.where(within, 1.0, 0.0).astype(BF16)
    lf_hi = lf.astype(BF16)
    lf_lo = (lf - lf_hi.astype(F32)).astype(BF16)
    return half - ht, _dot(ones, lf_hi) + _dot(ones, lf_lo)


def _gla_pairs(qs, kk, b, forward, within):
    c = qs.shape[0]
    a_row = c // 2 - 1 if forward else c // 2
    e_row = c - 1 if forward else 0
    anchor = b[a_row:a_row + 1, :]
    b_end = b[e_row:e_row + 1, :]
    qa = qs * jnp.exp(b - anchor)
    ka = kk * jnp.exp(anchor - b)
    scores = jnp.where(within, _dot_nt(qa.astype(BF16), ka.astype(BF16)), 0.0).astype(BF16)
    return (scores, (qa * jnp.exp(anchor)).astype(BF16), (ka * jnp.exp(b_end - anchor)).astype(BF16),
            jnp.exp(b_end))


def _gla_pairs_exact(qs, kk, b, forward, within, tmp):
    tb_ref, tq_ref, ts_ref = tmp
    c = qs.shape[0]
    b_end = b[c - 1:c, :] if forward else b[0:1, :]
    tb_ref[...] = b
    tq_ref[...] = qs
    ones = jnp.ones((F32_SUBLANES, qs.shape[1]), BF16)

    def row(t, carry):
        rel = jnp.minimum(tb_ref[pl.ds(t, 1), :] - b, 0.0)
        e = (tq_ref[pl.ds(t, 1), :] * kk) * jnp.exp(rel)
        ts_ref[pl.ds(t, 1), :] = _dot_nt(ones, e.astype(BF16))[0:1, :]
        return carry

    lax.fori_loop(0, c, row, 0)
    scores = jnp.where(within, ts_ref[...], 0.0).astype(BF16)
    return scores, (qs * jnp.exp(b)).astype(BF16), (kk * jnp.exp(b_end - b)).astype(BF16), jnp.exp(b_end)


def _gla_readout(o, g_raw, o_gain):
    ms = jnp.mean(o * o, axis=-1, keepdims=True)
    y = o * lax.rsqrt(ms + EPS) * o_gain
    return (y * _silu(g_raw.astype(F32))).astype(BF16)


def _gla_kernel(qc, ffc, fbc, vc, gc, ql, ffl, fbl, vl, gl, lb_ref, og_ref, yc_ref, yl_ref,
                o_ref, qb_ref, ke_ref, dec_ref, st_ref, tb_ref, tq_ref, ts_ref):
    c = GLA_CHUNK
    tc = qc.shape[0]
    n_ctx = tc // c
    n_lat = ql.shape[0] // c
    lb = (lb_ref[0], lb_ref[1])
    row = lax.broadcasted_iota(jnp.int32, (c, c), 0)
    col = lax.broadcasted_iota(jnp.int32, (c, c), 1)
    within = (col <= row, col >= row)

    def rows(start):
        return pl.ds(start if isinstance(start, int) else pl.multiple_of(start, c), c)

    def local(refs, src0, dst0, chunk0, n, exact):
        q, ff, fb, v = refs
        src = [rows(src0 + j * c) for j in range(n)]
        dst = [rows(dst0 + j * c) for j in range(n)]
        qs = [_silu(q[src[j], :].astype(F32)) * (HEAD ** -0.5) for j in range(n)]
        gates = [[_gla_gates(f[src[j], :], lb[d], within[d]) for d, f in enumerate((ff, fb))] for j in range(n)]
        scores = []
        for j in range(n):
            per_dir = []
            for d in range(2):
                kk, b = gates[j][d]
                if exact:
                    sc, qb, ke, dec = _gla_pairs_exact(qs[j], kk, b, d == 0, within[d], (tb_ref, tq_ref, ts_ref))
                else:
                    sc, qb, ke, dec = _gla_pairs(qs[j], kk, b, d == 0, within[d])
                qb_ref[d, dst[j], :] = qb
                ke_ref[d, dst[j], :] = ke
                dec_ref[d, chunk0 + j] = dec
                per_dir.append(sc)
            scores.append(per_dir)
        for j in range(n):
            vj = v[src[j], :]
            o_ref[dst[j], :] = _dot(scores[j][0], vj) + _dot(scores[j][1], vj)

    def phase1(exact):
        run = 1 if exact else math.gcd(n_lat, GLA_RUN)
        run_ctx = math.gcd(n_ctx, run)
        for i in range(0, n_ctx, run_ctx):
            local((qc, ffc, fbc, vc), i * c, i * c, i, run_ctx, exact)

        def body(i, carry):
            src0 = pl.multiple_of(i * (run * c), c)
            local((ql, ffl, fbl, vl), src0, tc + src0, n_ctx + i * run, run, exact)
            return carry

        lax.fori_loop(0, n_lat // run, body, 0)

    safe = jnp.min(jnp.minimum(lb[0], lb[1])) >= GLA_SAFE_LB

    @pl.when(safe)
    def _():
        phase1(False)

    @pl.when(jnp.logical_not(safe))
    def _():
        phase1(True)

    st_ref[...] = jnp.zeros_like(st_ref)

    o_gain = og_ref[...]

    def steps(v, first, n, n_seg, row0, chunk0, finish=None):
        order = [[first + s for s in range(n)], [n_seg - 1 - first - s for s in range(n)]]
        src = [[rows(j * c) for j in order[d]] for d in range(2)]
        dst = [[rows(row0 + j * c) for j in order[d]] for d in range(2)]
        kv = [[_dot_tn(v[src[d][s], :], ke_ref[d, dst[d][s], :]) for s in range(n)] for d in range(2)]
        for d in range(2):
            st = st_ref[d]
            for s in range(n):
                o = o_ref[dst[d][s], :] + _dot_nt(qb_ref[d, dst[d][s], :], st.astype(BF16))
                if finish is None:
                    o_ref[dst[d][s], :] = o
                else:
                    g, y = finish
                    y[src[d][s], :] = _gla_readout(o, g[src[d][s], :], o_gain)
                st = st * dec_ref[d, chunk0 + order[d][s]] + kv[d][s]
            st_ref[d] = st

    steps(vc, 0, n_ctx, n_ctx, 0, 0)
    for i in range(n_ctx):
        r = slice(i * c, (i + 1) * c)
        yc_ref[r, :] = _gla_readout(o_ref[r, :], gc[r, :], o_gain)

    run = math.gcd(n_lat, GLA_RUN)
    n_runs = n_lat // run
    meet = (n_runs + 1) // 2

    def steps_body(i, carry):
        steps(vl, i * run, run, n_lat, tc, n_ctx)
        return carry

    def last_steps_body(i, carry):
        steps(vl, i * run, run, n_lat, tc, n_ctx, finish=(gl, yl_ref))
        return carry

    lax.fori_loop(0, meet, steps_body, 0)
    if n_runs % 2:
        for j in range((n_runs // 2) * run, (n_runs // 2 + 1) * run):
            yl_ref[rows(j * c), :] = _gla_readout(o_ref[rows(tc + j * c), :], gl[rows(j * c), :], o_gain)
    lax.fori_loop(meet, n_runs, last_steps_body, 0)


def _gla_call(p_ctx, p_lat, lower_bound, o_gain):
    b, tc, d5 = p_ctx.shape
    t = p_lat.shape[1]
    d = d5 // 5
    heads = d // HEAD
    assert tc % GLA_CHUNK == 0 and t % GLA_CHUNK == 0

    def col(tt, part):
        return pl.BlockSpec((None, tt, HEAD), lambda bi, h: (bi, 0, part * heads + h))

    out_spec = lambda tt: pl.BlockSpec((None, tt, HEAD), lambda bi, h: (bi, 0, h))
    return pl.pallas_call(
        _gla_kernel,
        grid=(b, heads),
        in_specs=[col(tc, p) for p in range(5)] + [col(t, p) for p in range(5)] + [
            pl.BlockSpec((2, None, 1, HEAD), lambda bi, h: (0, h, 0, 0)),
            pl.BlockSpec((1, HEAD), lambda bi, h: (0, 0)),
        ],
        out_specs=[out_spec(tc), out_spec(t)],
        out_shape=[jax.ShapeDtypeStruct((b, tc, d), BF16), jax.ShapeDtypeStruct((b, t, d), BF16)],
        scratch_shapes=[
            pltpu.VMEM((tc + t, HEAD), F32),
            pltpu.VMEM((2, tc + t, HEAD), BF16),
            pltpu.VMEM((2, tc + t, HEAD), BF16),
            pltpu.VMEM((2, (tc + t) // GLA_CHUNK, 1, HEAD), F32),
            pltpu.VMEM((2, HEAD, HEAD), F32),
            pltpu.VMEM((GLA_CHUNK, HEAD), F32),
            pltpu.VMEM((GLA_CHUNK, HEAD), F32),
            pltpu.VMEM((GLA_CHUNK, GLA_CHUNK), F32),
        ],
        compiler_params=_params(("arbitrary", "arbitrary")),
        name="hgrn_scan",
    )(*([p_ctx] * 5), *([p_lat] * 5), lower_bound.reshape(2, heads, 1, HEAD), o_gain.reshape(1, HEAD))


def _flash_kernel(q_ref, kc_ref, vc_ref, kl_ref, vl_ref, o_ref, *, tk):
    tq = q_ref.shape[0]
    group = q_ref.shape[1] // HEAD
    qs = [q_ref[:, g * HEAD:(g + 1) * HEAD] for g in range(group)]

    def attend(carry, k, v):
        v_ext = jnp.concatenate([v, jnp.ones_like(v)], axis=1)
        out = []
        for g in range(group):
            m, acc = carry[g]
            s = _dot_nt(qs[g], k)
            m_new = jnp.maximum(m, jnp.max(s, axis=-1, keepdims=True))
            p = jnp.exp2(s - m_new).astype(BF16)
            out.append((m_new, jnp.exp2(m - m_new) * acc + _dot(p, v_ext)))
        return tuple(out)

    carry = tuple((jnp.full((tq, 1), -jnp.inf, F32), jnp.zeros((tq, 2 * HEAD), F32)) for _ in range(group))
    for start in range(0, kl_ref.shape[0], tk):
        carry = attend(carry, kl_ref[start:start + tk, :], vl_ref[start:start + tk, :])
    carry = attend(carry, kc_ref[...], vc_ref[...])
    for g in range(group):
        acc = carry[g][1]
        o_ref[:, g * HEAD:(g + 1) * HEAD] = (acc[:, :HEAD] / acc[:, HEAD:]).astype(o_ref.dtype)


def _flash_call(qkv, kv_ctx, *, heads):
    b, t, n = qkv.shape
    tc = kv_ctx.shape[1]
    kvh = (n // HEAD - heads) // 2
    group = heads // kvh
    tq = _pick(t, 512, BF16_SUBLANES)
    tk = _pick(t, 1024, 2 * HEAD)
    assert t // tk <= 16
    body = functools.partial(_flash_kernel, tk=tk)
    return pl.pallas_call(
        body,
        grid=(b, kvh, t // tq),
        in_specs=[
            pl.BlockSpec((None, tq, group * HEAD), lambda bi, h, i: (bi, i, h)),
            pl.BlockSpec((None, tc, HEAD), lambda bi, h, i: (bi, 0, h)),
            pl.BlockSpec((None, tc, HEAD), lambda bi, h, i: (bi, 0, kvh + h)),
            pl.BlockSpec((None, t, HEAD), lambda bi, h, i: (bi, 0, heads + h)),
            pl.BlockSpec((None, t, HEAD), lambda bi, h, i: (bi, 0, heads + kvh + h)),
        ],
        out_specs=pl.BlockSpec((None, tq, group * HEAD), lambda bi, h, i: (bi, i, h)),
        out_shape=jax.ShapeDtypeStruct((b, t, heads * HEAD), BF16),
        compiler_params=_params(("arbitrary", "arbitrary", "arbitrary")),
        name="gqa_flash",
    )(qkv, kv_ctx, kv_ctx, qkv, qkv)


def _rope_tables(t):
    pos = jnp.arange(t, dtype=jnp.int32)
    rows = (pos // GRID_W).astype(F32)
    cols = (pos % GRID_W).astype(F32)
    axis_dim = HEAD // 2
    inv_freq = ROPE_THETA ** (-jnp.arange(0, axis_dim, 2, dtype=F32) / axis_dim)
    ang = jnp.concatenate([rows[:, None] * inv_freq, cols[:, None] * inv_freq], axis=-1)
    cos, sin = jnp.cos(ang), jnp.sin(ang)
    return jnp.concatenate([cos, cos], axis=-1), jnp.concatenate([-sin, sin], axis=-1)


def _split_mods(mods, b):
    d = mods.shape[1] // N_MOD
    lat = [mods[:b, i * d:(i + 1) * d].reshape(b, 1, d) for i in range(N_MOD)]
    ctx = [mods[b:b + 1, i * d:(i + 1) * d].reshape(1, 1, d) for i in range(N_MOD)]
    return lat, ctx


def kernel(x, c, ctx, c_ctx, ada_w, ada_b, norm_mix_pre, norm_mix_post, norm_ffn_pre, norm_ffn_post, hgrn_w_in, hgrn_lb_logits, hgrn_o_norm, hgrn_w_out, attn_w_qkv, attn_q_norm, attn_k_norm, attn_w_out, ffn_w_in, ffn_conv_w, ffn_conv_b, ffn_w_out):
    b, t, d = x.shape
    tc = ctx.shape[1]
    heads = d // HEAD
    assert ada_w.shape[0] == 2 and b + 1 <= ADA_ROWS

    cond = jnp.zeros((ADA_ROWS, d), F32).at[:b].set(c).at[b].set(c_ctx)
    mods = _ada_call(cond, ada_w, ada_b)
    x_lat = x.reshape(b * t, d)
    x_ctx = ctx.reshape(b * tc, d)

    (sh_ml, sc_ml, gt_ml, sh_fl, sc_fl, gt_fl), (sh_mc, sc_mc, gt_mc, sh_fc, sc_fc, gt_fc) = _split_mods(mods[0], b)
    lower_bound = jnp.cumsum(jax.nn.softmax(hgrn_lb_logits.astype(F32), axis=1), axis=1)[:, 0]
    hgrn_w_in, hgrn_w_out, attn_w_out = hgrn_w_in.astype(BF16), hgrn_w_out.astype(BF16), attn_w_out.astype(BF16)
    ffn_w_in, ffn_w_out = ffn_w_in.astype(BF16), ffn_w_out.astype(BF16)
    p_lat = _nmm_call(x_lat, norm_mix_pre[0], sh_ml, sc_ml, hgrn_w_in, name="hgrn_in_lat")
    p_ctx = _nmm_call(x_ctx, norm_mix_pre[0], sh_mc, sc_mc, hgrn_w_in, name="hgrn_in_ctx")
    y_ctx, y_lat = _gla_call(p_ctx.reshape(b, tc, 5 * d), p_lat.reshape(b, t, 5 * d), lower_bound, hgrn_o_norm[0])
    x_lat = _proj_res_call(y_lat.reshape(b * t, d), hgrn_w_out, x_lat, norm_mix_post[0], gt_ml, name="hgrn_out_lat")
    x_ctx = _proj_res_call(y_ctx.reshape(b * tc, d), hgrn_w_out, x_ctx, norm_mix_post[0], gt_mc, name="hgrn_out_ctx")
    a = _ffn_in_call(x_lat, norm_ffn_pre[0], sh_fl, sc_fl, ffn_w_in, ffn_conv_w[0], ffn_conv_b[0],
                     layer=0, seq_len=t, name="ffn0_in_lat")
    x_lat = _proj_res_call(a, ffn_w_out, x_lat, norm_ffn_post[0], gt_fl, layer=0, name="ffn0_out_lat")
    a = _ffn_in_call(x_ctx, norm_ffn_pre[0], sh_fc, sc_fc, ffn_w_in, ffn_conv_w[0], ffn_conv_b[0],
                     layer=0, seq_len=tc, name="ffn0_in_ctx")
    x_ctx = _proj_res_call(a, ffn_w_out, x_ctx, norm_ffn_post[0], gt_fc, layer=0, name="ffn0_out_ctx")

    (sh_ml, sc_ml, gt_ml, sh_fl, sc_fl, gt_fl), (sh_mc, sc_mc, _, _, _, _) = _split_mods(mods[1], b)
    n_qkv = attn_w_qkv.shape[2]
    kvh = (n_qkv // HEAD - heads) // 2
    perm = jnp.concatenate([jnp.arange(0, HEAD, 2), jnp.arange(1, HEAD, 2)])
    head_perm = (jnp.arange(heads + kvh)[:, None] * HEAD + perm[None, :]).reshape(-1)
    col_perm = jnp.concatenate([head_perm, jnp.arange((heads + kvh) * HEAD, n_qkv)])
    w_qkv = attn_w_qkv[:, :, col_perm].astype(BF16)
    head_gain = jnp.concatenate([
        jnp.tile(attn_q_norm[0][perm] * (LOG2_E * HEAD ** -0.5), heads),
        jnp.tile(attn_k_norm[0][perm], kvh),
        jnp.ones((kvh * HEAD,), F32),
    ]).reshape(1, n_qkv)
    cos, sin = _rope_tables(t)
    qk_cols = (heads + kvh) * HEAD
    q_cols = heads * HEAD
    qkv = _nmm_call(x_lat, norm_mix_pre[1], sh_ml, sc_ml, w_qkv, name="attn_qkv_lat")
    qkv = _qk_rope_call(qkv, head_gain[:, :qk_cols], cos, sin, name="attn_qk_rope_lat")
    kv_ctx = _nmm_call(x_ctx, norm_mix_pre[1], sh_mc, sc_mc, w_qkv, col0=q_cols, name="attn_kv_ctx")
    kv_ctx = _qk_rope_call(kv_ctx, head_gain[:, q_cols:qk_cols], jnp.ones((tc, HEAD), F32),
                           jnp.zeros((tc, HEAD), F32), name="attn_k_norm_ctx")
    qkv = qkv.reshape(b, t, n_qkv)
    kv_ctx = kv_ctx.reshape(b, tc, 2 * kvh * HEAD)
    o = _flash_call(qkv, kv_ctx, heads=heads)
    x_lat = _proj_res_call(o.reshape(b * t, d), attn_w_out, x_lat, norm_mix_post[1], gt_ml, name="attn_out_lat")
    a = _ffn_in_call(x_lat, norm_ffn_pre[1], sh_fl, sc_fl, ffn_w_in, ffn_conv_w[1], ffn_conv_b[1],
                     layer=1, seq_len=t, name="ffn1_in_lat")
    x_lat = _proj_res_call(a, ffn_w_out, x_lat, norm_ffn_post[1], gt_fl, layer=1, name="ffn1_out_lat")
    return x_lat.reshape(b, t, d)
```
